```python
import jax
import jax.numpy as jnp
from jax import lax
import numpy as np

D_MODEL = 1024
BATCH = 8
SEQ = 2048
DEPTH = 2

CTX_LEN = 256
GRID_W = 64
N_HEADS = 4
HEAD_DIM = 64
MIX_W = N_HEADS * HEAD_DIM
N_KV = 2
KV_W = N_KV * HEAD_DIM
Q_PER_KV = N_HEADS // N_KV
RET_CHUNK = 128
Q_LORA = 256
KV_LORA = 128
MLA_NOPE = 64
MLA_ROPE = 32
MLA_V = 64
WINDOW = 128
Q_BLOCK = 128
N_BRANCH = 4
N_EXPERTS = 32
N_GROUPS = 4
EXPERTS_PER_GROUP = N_EXPERTS // N_GROUPS
TOP_K = 2
D_EXPERT = 1024
MOE_BLOCK = 128
ROPE_BASE = 10000.0
EPS = 1e-6
NEG_INF = -1e30

RET_COLS = 4 * MIX_W
MLA_COLS = Q_LORA + KV_LORA + MLA_ROPE
GQA_COLS = MIX_W + 2 * KV_W
WIN_COLS = MIX_W + 2 * KV_W
N_IN = RET_COLS + MLA_COLS + GQA_COLS + WIN_COLS

kernel_name = 'hybrid_prefix_diffusion_block'

F32 = jnp.float32


def rmsnorm(x, g):
    xf = x.astype(F32)
    y = xf * lax.rsqrt(jnp.mean(xf * xf, axis=-1, keepdims=True) + EPS)
    return (y * g.astype(F32)).astype(x.dtype)


def axial_rope_tables(n_tokens, rot_dim):
    rows = n_tokens // GRID_W
    row = jnp.broadcast_to(jnp.arange(rows)[:, None], (rows, GRID_W)).reshape(-1).astype(F32)
    col = jnp.broadcast_to(jnp.arange(GRID_W)[None, :], (rows, GRID_W)).reshape(-1).astype(F32)
    n_f = rot_dim // 4
    inv = ROPE_BASE ** (-jnp.arange(n_f, dtype=F32) / n_f)
    ang = jnp.concatenate([row[:, None] * inv, col[:, None] * inv], axis=-1)
    return jnp.cos(ang), jnp.sin(ang)


def apply_rope(x, cos, sin):
    half = x.shape[-1] // 2
    shp = (x.shape[1],) + (1,) * (x.ndim - 3) + (half,)
    c = cos.reshape(shp)
    s = sin.reshape(shp)
    xf = x.astype(F32)
    x1, x2 = xf[..., :half], xf[..., half:]
    return jnp.concatenate([x1 * c - x2 * s, x2 * c + x1 * s], axis=-1).astype(x.dtype)


def attend(q, k, v, scale, sink=None):
    s = jnp.einsum('bqgrd,bkgd->bgrqk', q, k).astype(F32) * scale
    if sink is not None:
        sk = jnp.broadcast_to(sink.astype(F32)[None, :, :, None, None], s.shape[:-1] + (1,))
        s = jnp.concatenate([s, sk], axis=-1)
    p = jax.nn.softmax(s, axis=-1)
    if sink is not None:
        p = p[..., :-1]
    return jnp.einsum('bgrqk,bkge->bqgre', p.astype(v.dtype), v)


def dense_block_attend(q, k, v, scale):
    B_, L_ = q.shape[:2]
    n = L_ // Q_BLOCK
    qb = jnp.swapaxes(q.reshape((B_, n, Q_BLOCK) + q.shape[2:]), 0, 1)
    ob = lax.map(lambda qc: attend(qc, k, v, scale), qb)
    return jnp.swapaxes(ob, 0, 1).reshape((B_, L_) + ob.shape[3:])


def retention_scan(q, k, v, log_g, s0, strict):
    B_, H_, L_, _ = q.shape
    dv = v.shape[-1]
    C = RET_CHUNK
    n = L_ // C
    idx = jnp.arange(C, dtype=F32)
    diff = idx[:, None] - idx[None, :]
    mask = (diff > 0) if strict else (diff >= 0)
    lg = log_g.astype(F32)
    dmat = jnp.where(mask, jnp.exp(lg[:, None, None] * jnp.where(mask, diff, 0.0)), 0.0).astype(q.dtype)
    xi = jnp.exp(lg[:, None] * (idx + 1.0))[..., None].astype(q.dtype)
    zeta = jnp.exp(lg[:, None] * (C - 1.0 - idx))[..., None].astype(q.dtype)
    g_chunk = jnp.exp(lg * C)[:, None, None].astype(q.dtype)

    def chunks(a):
        return jnp.moveaxis(a.reshape(B_, H_, n, C, a.shape[-1]), 2, 0)

    def step(state, qkv):
        qc, kc, vc = qkv
        inner = jnp.einsum('bhid,bhjd->bhij', qc, kc) * dmat
        o = jnp.einsum('bhij,bhje->bhie', inner, vc) + jnp.einsum('bhid,bhde->bhie', qc * xi, state)
        state = g_chunk * state + jnp.einsum('bhjd,bhje->bhde', kc * zeta, vc)
        return state, o

    s_final, o = lax.scan(step, s0, (chunks(q), chunks(k), chunks(v)))
    return jnp.moveaxis(o, 0, 2).reshape(B_, H_, L_, dv), s_final


def split_ret(z, rope):
    B_, T_ = z.shape[:2]
    q, k, v, g = (z[..., i * MIX_W:(i + 1) * MIX_W].reshape(B_, T_, N_HEADS, HEAD_DIM) for i in range(4))
    if rope is not None:
        q = apply_rope(q, *rope)
        k = apply_rope(k, *rope)
    tr = lambda a: jnp.transpose(a, (0, 2, 1, 3))
    return tr(q) * (HEAD_DIM ** -0.5), tr(k), tr(v), g.reshape(B_, T_, MIX_W)


def retention_out(o, g):
    of = o.astype(F32)
    mu = jnp.mean(of, axis=-1, keepdims=True)
    var = jnp.mean(jnp.square(of - mu), axis=-1, keepdims=True)
    y = (of - mu) * lax.rsqrt(var + EPS)
    B_, H_, T_, dv = o.shape
    y = jnp.transpose(y, (0, 2, 1, 3)).reshape(B_, T_, H_ * dv)
    return (y * jax.nn.silu(g.astype(F32))).astype(g.dtype)


def retention_mixer(p, pc, rope, decay, ctx_out):
    q, k, v, g = split_ret(p, rope)
    qc, kc, vc, gc = split_ret(pc, None)
    lg = -jnp.exp(decay.astype(F32))
    flip = lambda a: a[:, :, ::-1]
    s0 = jnp.zeros(q.shape[:2] + (HEAD_DIM, HEAD_DIM), q.dtype)
    oc_f, s_f = retention_scan(qc, kc, vc, lg[0], s0, False)
    oc_b, s_b = retention_scan(flip(qc), flip(kc), flip(vc), lg[1], s0, True)
    o_f, _ = retention_scan(q, k, v, lg[0], s_f, False)
    o_b, _ = retention_scan(flip(q), flip(k), flip(v), lg[1], s_b, True)
    y = retention_out(o_f + flip(o_b), g)
    yc = retention_out(oc_f + flip(oc_b), gc) if ctx_out else None
    return y, yc


def mla_mixer(p, pc, rope, qn_g, w_uq, kvn_g, w_ukv, ctx_out):
    def project(z, rp):
        B_, T_ = z.shape[:2]
        q = (rmsnorm(z[..., :Q_LORA], qn_g) @ w_uq).reshape(B_, T_, N_HEADS, MLA_NOPE + MLA_ROPE)
        kv = (rmsnorm(z[..., Q_LORA:Q_LORA + KV_LORA], kvn_g) @ w_ukv).reshape(B_, T_, N_HEADS, MLA_NOPE + MLA_V)
        q_nope, q_rope = q[..., :MLA_NOPE], q[..., MLA_NOPE:]
        k_nope, v = kv[..., :MLA_NOPE], kv[..., MLA_NOPE:]
        k_rope = z[..., Q_LORA + KV_LORA:][:, :, None, :]
        if rp is not None:
            q_rope = apply_rope(q_rope, *rp)
            k_rope = apply_rope(k_rope, *rp)
        qf = jnp.concatenate([q_nope, q_rope], axis=-1)[:, :, :, None, :]
        kf = jnp.concatenate([k_nope, jnp.broadcast_to(k_rope, k_nope.shape[:-1] + (MLA_ROPE,))], axis=-1)
        return qf, kf, v

    scale = (MLA_NOPE + MLA_ROPE) ** -0.5
    q, k, v = project(p, rope)
    qc, kc, vc = project(pc, None)
    o = dense_block_attend(q, jnp.concatenate([k, kc], axis=1), jnp.concatenate([v, vc], axis=1), scale)
    y = o.reshape(o.shape[:2] + (N_HEADS * MLA_V,))
    yc = attend(qc, kc, vc, scale).reshape(pc.shape[:2] + (N_HEADS * MLA_V,)) if ctx_out else None
    return y, yc


def split_gqa(z):
    B_, T_ = z.shape[:2]
    q = z[..., :MIX_W].reshape(B_, T_, N_KV, Q_PER_KV, HEAD_DIM)
    k = z[..., MIX_W:MIX_W + KV_W].reshape(B_, T_, N_KV, HEAD_DIM)
    v = z[..., MIX_W + KV_W:].reshape(B_, T_, N_KV, HEAD_DIM)
    return q, k, v


def gqa_mixer(p, pc, rope, qn_g, kn_g, ctx_out):
    scale = HEAD_DIM ** -0.5
    q, k, v = split_gqa(p)
    q = apply_rope(rmsnorm(q, qn_g), *rope)
    k = apply_rope(rmsnorm(k, kn_g), *rope)
    qc, kc, vc = split_gqa(pc)
    qc = rmsnorm(qc, qn_g)
    kc = rmsnorm(kc, kn_g)
    o = dense_block_attend(q, jnp.concatenate([k, kc], axis=1), jnp.concatenate([v, vc], axis=1), scale)
    y = o.reshape(p.shape[:2] + (MIX_W,))
    yc = attend(qc, kc, vc, scale).reshape(pc.shape[:2] + (MIX_W,)) if ctx_out else None
    return y, yc


def window_mixer(p, pc, rope, sink, ctx_out):
    scale = HEAD_DIM ** -0.5
    C = Q_BLOCK
    q, k, v = split_gqa(p)
    q = apply_rope(q, *rope)
    k = apply_rope(k, *rope)
    qc, kc, vc = split_gqa(pc)
    B_, L_ = p.shape[:2]
    n = L_ // C
    qb = q.reshape(B_, n, C, N_KV, Q_PER_KV, HEAD_DIM)

    def band(a):
        ap = jnp.pad(a, ((0, 0), (C, C), (0, 0), (0, 0))).reshape(B_, n + 2, C, N_KV, a.shape[-1])
        return jnp.concatenate([ap[:, :-2], ap[:, 1:-1], ap[:, 2:]], axis=2)

    kb, vb = band(k), band(v)
    s_loc = jnp.einsum('bnqgrd,bnkgd->bngrqk', qb, kb).astype(F32) * scale
    qi = jnp.arange(C)
    kj = jnp.arange(3 * C) - C
    kabs = jnp.arange(n)[:, None, None] * C + kj[None, None, :]
    valid = (jnp.abs(kj[None, :] - qi[:, None]) <= WINDOW)[None] & (kabs >= 0) & (kabs < L_)
    s_loc = jnp.where(valid[None, :, None, None], s_loc, NEG_INF)
    s_ctx = jnp.einsum('bnqgrd,bkgd->bngrqk', qb, kc).astype(F32) * scale
    sink_gr = sink.reshape(N_KV, Q_PER_KV)
    sink_col = jnp.broadcast_to(sink_gr.astype(F32)[None, None, :, :, None, None], s_loc.shape[:-1] + (1,))
    pr = jax.nn.softmax(jnp.concatenate([s_loc, s_ctx, sink_col], axis=-1), axis=-1).astype(v.dtype)
    n_ctx = kc.shape[1]
    o = (jnp.einsum('bngrqk,bnkge->bnqgre', pr[..., :3 * C], vb)
         + jnp.einsum('bngrqk,bkge->bnqgre', pr[..., 3 * C:3 * C + n_ctx], vc))
    y = o.reshape(B_, L_, MIX_W)
    yc = attend(qc, kc, vc, scale, sink=sink_gr).reshape(pc.shape[:2] + (MIX_W,)) if ctx_out else None
    return y, yc


def merge_branches(h, ys, w_gate, w_branch, w_out):
    gates = jax.nn.sigmoid(h @ w_gate).reshape(h.shape[:2] + (N_BRANCH, D_MODEL))
    proj = jnp.einsum('btim,imd->btid', jnp.stack(ys, axis=2), w_branch)
    return jnp.sum(gates * proj, axis=2) @ w_out


def moe_ffn(t, w_router, b_router, w1, w3, w2):
    N = t.shape[0]
    s = jax.nn.sigmoid((t @ w_router).astype(F32))
    sel = s + b_router.astype(F32)
    grp_score = jnp.sum(lax.top_k(sel.reshape(N, N_GROUPS, EXPERTS_PER_GROUP), TOP_K)[0], axis=-1)
    grp = jnp.argmax(grp_score, axis=-1)
    in_grp = (jnp.arange(N_EXPERTS) // EXPERTS_PER_GROUP)[None, :] == grp[:, None]
    _, e_idx = lax.top_k(jnp.where(in_grp, sel, NEG_INF), TOP_K)
    gw = jnp.take_along_axis(s, e_idx, axis=-1)
    gw = gw / jnp.sum(gw, axis=-1, keepdims=True)

    A = N * TOP_K
    flat_e = e_idx.reshape(-1)
    flat_t = jnp.repeat(jnp.arange(N), TOP_K)
    flat_w = gw.reshape(-1)
    order = jnp.argsort(flat_e)
    e_sorted = flat_e[order]
    counts = jnp.bincount(flat_e, length=N_EXPERTS)
    padded = (counts + MOE_BLOCK - 1) // MOE_BLOCK * MOE_BLOCK
    start_s = jnp.cumsum(counts) - counts
    end_p = jnp.cumsum(padded)
    start_p = end_p - padded
    dest = start_p[e_sorted] + jnp.arange(A) - start_s[e_sorted]
    n_blocks = -(-A // MOE_BLOCK) + N_EXPERTS
    P = n_blocks * MOE_BLOCK
    tok_buf = jnp.full((P,), N, dtype=flat_t.dtype).at[dest].set(flat_t[order])
    w_buf = jnp.zeros((P,), t.dtype).at[dest].set(flat_w[order].astype(t.dtype))
    blk_e = jnp.minimum(jnp.searchsorted(end_p, jnp.arange(n_blocks) * MOE_BLOCK, side='right'), N_EXPERTS - 1)
    t_pad = jnp.concatenate([t, jnp.zeros((1, t.shape[1]), t.dtype)], axis=0)
    xb = t_pad[tok_buf].reshape(n_blocks, MOE_BLOCK, t.shape[1])

    def expert_block(args):
        xblk, e = args
        hid = jax.nn.silu(xblk @ w1[e]) * (xblk @ w3[e])
        return hid @ w2[e]

    yb = lax.map(expert_block, (xb, blk_e))
    y = yb.reshape(P, t.shape[1]) * w_buf[:, None]
    return jax.ops.segment_sum(y, tok_buf, num_segments=N + 1)[:N]


def setup_inputs(seed: int = 0) -> dict:
    key = jax.random.key(seed)
    ks = jax.random.split(key, 32)
    nrm = lambda k, shape, sc: jax.random.normal(k, shape, F32) * sc
    D = D_MODEL
    base_decay = jnp.log(-jnp.log(1.0 - 2.0 ** (-5.0 - jnp.arange(N_HEADS, dtype=F32))))
    return {
        'x': nrm(ks[0], (BATCH, SEQ, D), 1.0),
        'c': nrm(ks[1], (BATCH, D), 1.0),
        'ctx': nrm(ks[2], (BATCH, CTX_LEN, D), 1.0),
        'c_ctx': nrm(ks[3], (D,), 1.0),
        'w_ada': nrm(ks[4], (DEPTH, D, 6 * D), 0.5 * D ** -0.5),
        'b_ada': nrm(ks[5], (DEPTH, 6 * D), 0.02),
        'norm1_g': 1.0 + nrm(ks[6], (DEPTH, D), 0.02),
        'norm2_g': 1.0 + nrm(ks[7], (DEPTH, D), 0.02),
        'w_in': nrm(ks[8], (DEPTH, D, N_IN), D ** -0.5),
        'w_gate': nrm(ks[9], (DEPTH, D, N_BRANCH * D), D ** -0.5),
        'w_branch': nrm(ks[10], (DEPTH, N_BRANCH, MIX_W, D), MIX_W ** -0.5),
        'w_out': nrm(ks[11], (DEPTH, D, D), D ** -0.5),
        'ret_decay': base_decay + nrm(ks[12], (DEPTH, 2, N_HEADS), 0.1),
        'mla_qn_g': 1.0 + nrm(ks[13], (DEPTH, Q_LORA), 0.02),
        'mla_w_uq': nrm(ks[14], (DEPTH, Q_LORA, N_HEADS * (MLA_NOPE + MLA_ROPE)), Q_LORA ** -0.5),
        'mla_kvn_g': 1.0 + nrm(ks[15], (DEPTH, KV_LORA), 0.02),
        'mla_w_ukv': nrm(ks[16], (DEPTH, KV_LORA, N_HEADS * (MLA_NOPE + MLA_V)), KV_LORA ** -0.5),
        'gqa_qn_g': 1.0 + nrm(ks[17], (DEPTH, HEAD_DIM), 0.02),
        'gqa_kn_g': 1.0 + nrm(ks[18], (DEPTH, HEAD_DIM), 0.02),
        'win_sink': nrm(ks[19], (DEPTH, N_HEADS), 0.5),
        'w_router': nrm(ks[20], (D, N_EXPERTS), D ** -0.5),
        'b_router': nrm(ks[21], (N_EXPERTS,), 0.01),
        'w1': nrm(ks[22], (DEPTH, N_EXPERTS, D, D_EXPERT), D ** -0.5),
        'w3': nrm(ks[23], (DEPTH, N_EXPERTS, D, D_EXPERT), D ** -0.5),
        'w2': nrm(ks[24], (DEPTH, N_EXPERTS, D_EXPERT, D), D_EXPERT ** -0.5),
        'final_norm_g': 1.0 + nrm(ks[25], (D,), 0.02),
    }


def reference(x, c, ctx, c_ctx, w_ada, b_ada, norm1_g, norm2_g, w_in, w_gate, w_branch, w_out,
              ret_decay, mla_qn_g, mla_w_uq, mla_kvn_g, mla_w_ukv, gqa_qn_g, gqa_kn_g, win_sink,
              w_router, b_router, w1, w3, w2, final_norm_g):
    B_, L_, D = x.shape
    rope64 = axial_rope_tables(L_, HEAD_DIM)
    rope32 = axial_rope_tables(L_, MLA_ROPE)
    silu_c = jax.nn.silu(c)
    silu_cc = jax.nn.silu(c_ctx)
    o_mla = RET_COLS
    o_gqa = o_mla + MLA_COLS
    o_win = o_gqa + GQA_COLS
    xc = ctx
    for l in range(DEPTH):
        need_ctx = l < DEPTH - 1
        sh1, sc1, g1, sh2, sc2, g2 = jnp.split((silu_c @ w_ada[l] + b_ada[l])[:, None, :], 6, axis=-1)
        csh1, csc1, cg1, csh2, csc2, cg2 = jnp.split(silu_cc @ w_ada[l] + b_ada[l], 6, axis=-1)
        h = rmsnorm(x, norm1_g[l]) * (1.0 + sc1) + sh1
        hc = rmsnorm(xc, norm1_g[l]) * (1.0 + csc1) + csh1
        p = h @ w_in[l]
        pc = hc @ w_in[l]
        y_ret, yc_ret = retention_mixer(p[..., :o_mla], pc[..., :o_mla], rope64, ret_decay[l], need_ctx)
        y_mla, yc_mla = mla_mixer(p[..., o_mla:o_gqa], pc[..., o_mla:o_gqa], rope32, mla_qn_g[l],
                                  mla_w_uq[l], mla_kvn_g[l], mla_w_ukv[l], need_ctx)
        y_gqa, yc_gqa = gqa_mixer(p[..., o_gqa:o_win], pc[..., o_gqa:o_win], rope64, gqa_qn_g[l],
                                  gqa_kn_g[l], need_ctx)
        y_win, yc_win = window_mixer(p[..., o_win:], pc[..., o_win:], rope64, win_sink[l], need_ctx)
        x = x + g1 * merge_branches(h, [y_ret, y_mla, y_gqa, y_win], w_gate[l], w_branch[l], w_out[l])
        if need_ctx:
            xc = xc + cg1 * merge_branches(hc, [yc_ret, yc_mla, yc_gqa, yc_win], w_gate[l], w_branch[l], w_out[l])
        h2 = rmsnorm(x, norm2_g[l]) * (1.0 + sc2) + sh2
        if need_ctx:
            hc2 = rmsnorm(xc, norm2_g[l]) * (1.0 + csc2) + csh2
            toks = jnp.concatenate([h2.reshape(-1, D), hc2.reshape(-1, D)], axis=0)
            f = moe_ffn(toks, w_router, b_router, w1[l], w3[l], w2[l])
            x = x + g2 * f[:B_ * L_].reshape(x.shape)
            xc = xc + cg2 * f[B_ * L_:].reshape(xc.shape)
        else:
            f = moe_ffn(h2.reshape(-1, D), w_router, b_router, w1[l], w3[l], w2[l])
            x = x + g2 * f.reshape(x.shape)
    return rmsnorm(x, final_norm_g)
```

```python
import functools

import jax
import jax.numpy as jnp
from jax import lax
from jax.experimental import pallas as pl
from jax.experimental.pallas import tpu as pltpu

F32 = jnp.float32
BF16 = jnp.bfloat16

D_MODEL = 1024
BATCH = 8
SEQ = 2048
DEPTH = 2
CTX_LEN = 256
TOK = SEQ + CTX_LEN
GRID_W = 64
N_HEADS = 4
HEAD_DIM = 64
MIX_W = N_HEADS * HEAD_DIM
RET_CHUNK = 128
Q_LORA = 256
KV_LORA = 128
MLA_NOPE = 64
MLA_ROPE = 32
MLA_V = 64
WINDOW = 128
N_EXPERTS = 32
N_GROUPS = 4
EXPERTS_PER_GROUP = N_EXPERTS // N_GROUPS
TOP_K = 2
D_EXPERT = 1024
ROPE_BASE = 10000.0
EPS = 1e-6
NEG_INF = -1e30

LANES = 128
TM = 256
NT = TOK // TM
NT_LAT = SEQ // TM
WIN_KEYS = TM + 2 * WINDOW
N_CHUNK = TOK // RET_CHUNK
MOE_ROWS = 128
N_IN_P = 3072
MLA_QK = 4 * LANES

C_RET = 0
C_MQ = 1024
C_MKV = 1280
C_GQ = 1408
C_GK = 1664
C_GV = 1920
C_WQ = 2176
C_WK = 2432
C_WV = 2688
C_KR = 2944

VMEM_LIMIT = 56 * 1024 * 1024


def _params(sem, vmem=VMEM_LIMIT):
    return pltpu.CompilerParams(dimension_semantics=sem, vmem_limit_bytes=vmem)


def _const_spec(shape):
    nd = len(shape)
    return pl.BlockSpec(shape, lambda *_: (0,) * nd, pipeline_mode=pl.Buffered(1))


def _bdot(a, b):
    return jnp.dot(a.astype(BF16), b.astype(BF16), preferred_element_type=F32)


def _split(a):
    hi = a.astype(BF16)
    lo = (a - hi.astype(F32)).astype(BF16)
    return hi, lo


def _dot_split_lhs(a, b):
    hi, lo = _split(a)
    return (jnp.dot(hi, b, preferred_element_type=F32)
            + jnp.dot(lo, b, preferred_element_type=F32))


def _dot3(a, b):
    ah, al = _split(a)
    bh, bl = _split(b)
    return (jnp.dot(ah, bh, preferred_element_type=F32)
            + jnp.dot(ah, bl, preferred_element_type=F32)
            + jnp.dot(al, bh, preferred_element_type=F32))


def _modnorm(x, g, sc, sh):
    ms = jnp.mean(x * x, axis=-1, keepdims=True)
    return x * lax.rsqrt(ms + EPS) * g * (1.0 + sc) + sh


def _rmsnorm(x, g):
    ms = jnp.mean(x * x, axis=-1, keepdims=True)
    return x * lax.rsqrt(ms + EPS) * g


def _rope(x, c, s_up, s_dn, half):
    outs = []
    for j in range(x.shape[1] // LANES):
        xc = x[:, j * LANES:(j + 1) * LANES]
        outs.append(xc * c + pltpu.roll(xc, LANES - half, 1) * s_up + pltpu.roll(xc, half, 1) * s_dn)
    return outs[0] if len(outs) == 1 else jnp.concatenate(outs, axis=1)


def _head_mean(x, gmat):
    return _dot_split_lhs(x, gmat) * (1.0 / HEAD_DIM)


def _ada_kernel(c_ref, w_ref, b_ref, o_ref):
    c = c_ref[...]
    sc = c * jax.nn.sigmoid(c)
    o_ref[0] = _dot3(sc, w_ref[0]) + b_ref[0]


def _ada_call(cvec, w_ada, b_ada):
    tn = 1536
    return pl.pallas_call(
        _ada_kernel,
        grid=(DEPTH, 6 * D_MODEL // tn),
        in_specs=[
            pl.BlockSpec((16, D_MODEL), lambda l, j: (0, 0)),
            pl.BlockSpec((1, D_MODEL, tn), lambda l, j: (l, 0, j)),
            pl.BlockSpec((1, 1, tn), lambda l, j: (l, 0, j)),
        ],
        out_specs=pl.BlockSpec((1, 16, tn), lambda l, j: (l, 0, j)),
        out_shape=jax.ShapeDtypeStruct((DEPTH, 16, 6 * D_MODEL), F32),
        compiler_params=_params(("arbitrary", "arbitrary")),
        name="ada",
    )(cvec, w_ada, b_ada.reshape(DEPTH, 1, 6 * D_MODEL))


def _prep_kernel(x_ref, mod_ref, g1_ref, win_ref, wuq_ref, wuk_ref, wuv_ref, ekr_ref,
                 qng_ref, kvng_ref, gqg_ref, gkg_ref, gmat_ref,
                 c64_ref, u64_ref, d64_ref, cm_ref, um_ref, dm_ref,
                 rq_ref, rkt_ref, rv_ref, rg_ref, mq_ref, mkt_ref, mv_ref,
                 gq_ref, gkt_ref, gv_ref, wq_ref, wkt_ref, wv_ref):
    x = x_ref[0]
    md = mod_ref[0, 0]
    h = _modnorm(x, g1_ref[...], md[1:2], md[0:1])
    p = jnp.dot(h.astype(BF16), win_ref[...], preferred_element_type=F32)

    c64, u64, d64 = c64_ref[...], u64_ref[...], d64_ref[...]
    cm, um, dm = cm_ref[...], um_ref[...], dm_ref[...]
    gmat = gmat_ref[...]
    qk_scale = HEAD_DIM ** -0.5
    rope64 = lambda a: _rope(a, c64, u64, d64, HEAD_DIM // 2)
    ropem = lambda a: _rope(a, cm, um, dm, MLA_ROPE // 2)

    rq_ref[0] = (rope64(p[:, C_RET:C_RET + 256]) * qk_scale).astype(BF16)
    rkt_ref[0] = rope64(p[:, C_RET + 256:C_RET + 512]).T.astype(BF16)
    rv_ref[0] = p[:, C_RET + 512:C_RET + 768].astype(BF16)
    rg_ref[0] = p[:, C_RET + 768:C_RET + 1024].astype(BF16)

    qn = _rmsnorm(p[:, C_MQ:C_MQ + Q_LORA], qng_ref[...])
    q2 = ropem(_bdot(qn, wuq_ref[...])) * ((MLA_NOPE + MLA_ROPE) ** -0.5)
    mq_ref[0] = q2.astype(BF16)
    kvn = _rmsnorm(p[:, C_MKV:C_MKV + KV_LORA], kvng_ref[...]).astype(BF16)
    k2 = (jnp.dot(kvn, wuk_ref[...], preferred_element_type=F32)
          + _dot_split_lhs(p[:, C_KR:C_KR + LANES], ekr_ref[...]))
    mkt_ref[0] = ropem(k2).T.astype(BF16)
    mv_ref[0] = jnp.dot(kvn, wuv_ref[...], preferred_element_type=F32).astype(BF16)

    gq = p[:, C_GQ:C_GQ + 256]
    gq = gq * lax.rsqrt(_head_mean(gq * gq, gmat) + EPS) * gqg_ref[...]
    gq_ref[0] = (rope64(gq) * qk_scale).astype(BF16)
    gk = p[:, C_GK:C_GK + 256]
    gk = gk * lax.rsqrt(_head_mean(gk * gk, gmat) + EPS) * gkg_ref[...]
    gkt_ref[0] = rope64(gk).T.astype(BF16)
    gv_ref[0] = p[:, C_GV:C_GV + 256].astype(BF16)

    wq_ref[0] = (rope64(p[:, C_WQ:C_WQ + 256]) * qk_scale).astype(BF16)
    wkt_ref[0] = rope64(p[:, C_WK:C_WK + 256]).T.astype(BF16)
    wv_ref[0] = p[:, C_WV:C_WV + 256].astype(BF16)


def _prep_call(xx, mod, g1, win_p, wuq, wuk, wuv, ekr, qng, kvng, gqg, gkg, gmat, tabs):
    tok = lambda w: pl.BlockSpec((1, TM, w), lambda b, t: (b, t, 0))
    tokt = lambda w: pl.BlockSpec((1, w, TM), lambda b, t: (b, 0, t))
    tab = pl.BlockSpec((TM, LANES), lambda b, t: (t, 0))
    sd = lambda w: jax.ShapeDtypeStruct((BATCH, TOK, w), BF16)
    sdt = lambda w: jax.ShapeDtypeStruct((BATCH, w, TOK), BF16)
    in_specs = [
        tok(D_MODEL),
        pl.BlockSpec((1, 1, 8, D_MODEL), lambda b, t: (b, t // NT_LAT, 0, 0)),
        _const_spec((1, D_MODEL)),
        _const_spec((D_MODEL, N_IN_P)),
        _const_spec((Q_LORA, MLA_QK)),
        _const_spec((KV_LORA, MLA_QK)),
        _const_spec((KV_LORA, MIX_W)),
        _const_spec((LANES, MLA_QK)),
        _const_spec((1, Q_LORA)),
        _const_spec((1, KV_LORA)),
        _const_spec((1, MIX_W)),
        _const_spec((1, MIX_W)),
        _const_spec((MIX_W, MIX_W)),
        tab, tab, tab, tab, tab, tab,
    ]
    out_specs = [tok(256), tokt(256), tok(256), tok(256),
                 tok(MLA_QK), tokt(MLA_QK), tok(256),
                 tok(256), tokt(256), tok(256),
                 tok(256), tokt(256), tok(256)]
    out_shape = [sd(256), sdt(256), sd(256), sd(256),
                 sd(MLA_QK), sdt(MLA_QK), sd(256),
                 sd(256), sdt(256), sd(256),
                 sd(256), sdt(256), sd(256)]
    return pl.pallas_call(
        _prep_kernel,
        grid=(BATCH, NT),
        in_specs=in_specs,
        out_specs=out_specs,
        out_shape=out_shape,
        compiler_params=_params(("parallel", "parallel")),
        name="prep",
    )(xx, mod, g1, win_p, wuq, wuk, wuv, ekr, qng, kvng, gqg, gkg, gmat, *tabs)


def _ret_kernel(qf_ref, ktf_ref, vf_ref, qb_ref, ktb_ref, vb_ref,
                dmat_ref, xi_ref, zt_ref, gc_ref, of_ref, ob_ref, sf_ref, sb_ref):
    i = pl.program_id(0)

    @pl.when(i == 0)
    def _():
        sf_ref[...] = jnp.zeros_like(sf_ref)
        sb_ref[...] = jnp.zeros_like(sb_ref)

    lane_head = lax.broadcasted_iota(jnp.int32, (RET_CHUNK, MIX_W), 1) // HEAD_DIM
    r_head = lax.broadcasted_iota(jnp.int32, (MIX_W, MIX_W), 0) // HEAD_DIM
    c_head = lax.broadcasted_iota(jnp.int32, (MIX_W, MIX_W), 1) // HEAD_DIM
    block_diag = r_head == c_head

    def one(b, d, q_ref, kt_ref, v_ref, o_ref, s_ref):
        q = q_ref[b].astype(F32)
        kt = kt_ref[b]
        v = v_ref[b]
        s_old = s_ref[b]
        o = _bdot(q * xi_ref[d], s_old)
        for hd in range(N_HEADS):
            qm = jnp.where(lane_head == hd, q, 0.0).astype(BF16)
            inner = jnp.dot(qm, kt, preferred_element_type=F32) * dmat_ref[d, hd]
            oh = jnp.dot(inner.astype(BF16), v, preferred_element_type=F32)
            o = o + jnp.where(lane_head == hd, oh, 0.0)
        o_ref[b] = o
        kz = (kt.astype(F32) * zt_ref[d]).astype(BF16)
        upd = jnp.dot(kz, v, preferred_element_type=F32)
        s_ref[b] = gc_ref[d] * s_old + jnp.where(block_diag, upd, 0.0)

    def body(b, carry):
        one(b, 0, qf_ref, ktf_ref, vf_ref, of_ref, sf_ref)
        one(b, 1, qb_ref, ktb_ref, vb_ref, ob_ref, sb_ref)
        return carry

    lax.fori_loop(0, BATCH, body, 0)


def _ret_call(rq, rkt, rv, dmat, xi, zt, gc):
    cf = lambda i: (i + SEQ // RET_CHUNK) % N_CHUNK
    cb = lambda i: N_CHUNK - 1 - i
    rows = lambda f: pl.BlockSpec((BATCH, RET_CHUNK, MIX_W), lambda i: (0, f(i), 0))
    cols = lambda f: pl.BlockSpec((BATCH, MIX_W, RET_CHUNK), lambda i: (0, 0, f(i)))
    return pl.pallas_call(
        _ret_kernel,
        grid=(N_CHUNK,),
        in_specs=[rows(cf), cols(cf), rows(cf), rows(cb), cols(cb), rows(cb),
                  _const_spec((2, N_HEADS, RET_CHUNK, RET_CHUNK)),
                  _const_spec((2, RET_CHUNK, MIX_W)),
                  _const_spec((2, MIX_W, RET_CHUNK)),
                  _const_spec((2, MIX_W, MIX_W))],
        out_specs=[rows(cf), rows(cb)],
        out_shape=[jax.ShapeDtypeStruct((BATCH, TOK, MIX_W), F32)] * 2,
        scratch_shapes=[pltpu.VMEM((BATCH, MIX_W, MIX_W), F32),
                        pltpu.VMEM((BATCH, MIX_W, MIX_W), F32)],
        compiler_params=_params(("arbitrary",)),
        name="retention",
    )(rq, rkt, rv, rq, rkt, rv, dmat, xi, zt, gc)


def _head_q(q_ref, hd, pair):
    if not pair:
        return q_ref[0, :, hd * LANES:(hd + 1) * LANES], hd * LANES
    c = hd // 2
    qc = q_ref[0, :, c * LANES:(c + 1) * LANES]
    half = lax.broadcasted_iota(jnp.int32, qc.shape, 1) // HEAD_DIM
    return jnp.where(half == hd % 2, qc, jnp.zeros_like(qc)), c * LANES


def _dense_kernel(q_ref, kt_ref, v_ref, o_ref, *, pair):
    t = pl.program_id(1)
    lane_head = lax.broadcasted_iota(jnp.int32, (TM, MIX_W), 1) // HEAD_DIM

    def run(k_lo, k_hi):
        v = v_ref[0, k_lo:k_hi, :]
        acc = jnp.zeros((TM, MIX_W), F32)
        for hd in range(N_HEADS):
            qm, r0 = _head_q(q_ref, hd, pair)
            s = jnp.dot(qm, kt_ref[0, r0:r0 + LANES, k_lo:k_hi], preferred_element_type=F32)
            m = jnp.max(s, axis=-1, keepdims=True)
            p = jnp.exp(s - m)
            l = jnp.sum(p, axis=-1, keepdims=True)
            oh = jnp.dot(p.astype(BF16), v, preferred_element_type=F32)
            acc = jnp.where(lane_head == hd, oh * (1.0 / l), acc)
        o_ref[0] = acc.astype(BF16)

    @pl.when(t < NT_LAT)
    def _():
        run(0, TOK)

    @pl.when(t >= NT_LAT)
    def _():
        run(SEQ, TOK)


def _dense_call(q, kt, v, nq, pair, name):
    wq = q.shape[-1]
    return pl.pallas_call(
        functools.partial(_dense_kernel, pair=pair),
        grid=(BATCH, nq),
        in_specs=[pl.BlockSpec((1, TM, wq), lambda b, t: (b, t, 0)),
                  pl.BlockSpec((1, kt.shape[1], TOK), lambda b, t: (b, 0, 0)),
                  pl.BlockSpec((1, TOK, MIX_W), lambda b, t: (b, 0, 0))],
        out_specs=pl.BlockSpec((1, TM, MIX_W), lambda b, t: (b, t, 0)),
        out_shape=jax.ShapeDtypeStruct((BATCH, nq * TM, MIX_W), BF16),
        compiler_params=_params(("parallel", "arbitrary")),
        name=name,
    )(q, kt, v)


def _win_kernel(sink_ref, q_ref, kt_ref, v_ref, o_ref):
    t = pl.program_id(1)
    lane_head = lax.broadcasted_iota(jnp.int32, (TM, MIX_W), 1) // HEAD_DIM
    v_ctx = v_ref[0, SEQ:TOK, :]

    @pl.when(t < NT_LAT)
    def _():
        start = pl.multiple_of(jnp.clip(t * TM - WINDOW, 0, SEQ - WIN_KEYS), LANES)
        qpos = t * TM + lax.broadcasted_iota(jnp.int32, (TM, WIN_KEYS), 0)
        kpos = start + lax.broadcasted_iota(jnp.int32, (TM, WIN_KEYS), 1)
        valid = jnp.abs(kpos - qpos) <= WINDOW
        v_loc = v_ref[0, pl.ds(start, WIN_KEYS), :]
        acc = jnp.zeros((TM, MIX_W), F32)
        for hd in range(N_HEADS):
            qm, r0 = _head_q(q_ref, hd, True)
            s_loc = jnp.dot(qm, kt_ref[0, r0:r0 + LANES, pl.ds(start, WIN_KEYS)],
                            preferred_element_type=F32)
            s_loc = jnp.where(valid, s_loc, NEG_INF)
            s_ctx = jnp.dot(qm, kt_ref[0, r0:r0 + LANES, SEQ:TOK], preferred_element_type=F32)
            sk = sink_ref[hd]
            m = jnp.maximum(jnp.maximum(jnp.max(s_loc, axis=-1, keepdims=True),
                                        jnp.max(s_ctx, axis=-1, keepdims=True)), sk)
            p_loc = jnp.exp(s_loc - m)
            p_ctx = jnp.exp(s_ctx - m)
            l = (jnp.sum(p_loc, axis=-1, keepdims=True) + jnp.sum(p_ctx, axis=-1, keepdims=True)
                 + jnp.exp(sk - m))
            oh = (jnp.dot(p_loc.astype(BF16), v_loc, preferred_element_type=F32)
                  + jnp.dot(p_ctx.astype(BF16), v_ctx, preferred_element_type=F32))
            acc = jnp.where(lane_head == hd, oh * (1.0 / l), acc)
        o_ref[0] = acc.astype(BF16)

    @pl.when(t >= NT_LAT)
    def _():
        acc = jnp.zeros((TM, MIX_W), F32)
        for hd in range(N_HEADS):
            qm, r0 = _head_q(q_ref, hd, True)
            s = jnp.dot(qm, kt_ref[0, r0:r0 + LANES, SEQ:TOK], preferred_element_type=F32)
            sk = sink_ref[hd]
            m = jnp.maximum(jnp.max(s, axis=-1, keepdims=True), sk)
            p = jnp.exp(s - m)
            l = jnp.sum(p, axis=-1, keepdims=True) + jnp.exp(sk - m)
            oh = jnp.dot(p.astype(BF16), v_ctx, preferred_element_type=F32)
            acc = jnp.where(lane_head == hd, oh * (1.0 / l), acc)
        o_ref[0] = acc.astype(BF16)


def _win_call(sink, q, kt, v, nq):
    return pl.pallas_call(
        _win_kernel,
        grid=(BATCH, nq),
        in_specs=[pl.BlockSpec(memory_space=pltpu.SMEM),
                  pl.BlockSpec((1, TM, MIX_W), lambda b, t: (b, t, 0)),
                  pl.BlockSpec((1, MIX_W, TOK), lambda b, t: (b, 0, 0)),
                  pl.BlockSpec((1, TOK, MIX_W), lambda b, t: (b, 0, 0))],
        out_specs=pl.BlockSpec((1, TM, MIX_W), lambda b, t: (b, t, 0)),
        out_shape=jax.ShapeDtypeStruct((BATCH, nq * TM, MIX_W), BF16),
        compiler_params=_params(("parallel", "arbitrary")),
        name="window",
    )(sink, q, kt, v)


def _merge_kernel(x_ref, mod_ref, g1_ref, g2_ref, of_ref, ob_ref, rg_ref, ym_ref, yg_ref, yw_ref,
                  wg_ref, wb_ref, wo_ref, wr_ref, gmat_ref, xo_ref, h2_ref, lg_ref):
    x = x_ref[0]
    md = mod_ref[0, 0]
    hb = _modnorm(x, g1_ref[...], md[1:2], md[0:1]).astype(BF16)

    gmat = gmat_ref[...]
    o = of_ref[0] + ob_ref[0]
    dlt = o - _head_mean(o, gmat)
    var = _head_mean(dlt * dlt, gmat)
    g = rg_ref[0].astype(F32)
    y_ret = dlt * lax.rsqrt(var + EPS) * (g * jax.nn.sigmoid(g))

    ys = (y_ret.astype(BF16), ym_ref[0], yg_ref[0], yw_ref[0])
    acc = jnp.zeros((TM, D_MODEL), F32)
    for i in range(4):
        gate = jax.nn.sigmoid(jnp.dot(hb, wg_ref[:, i * D_MODEL:(i + 1) * D_MODEL],
                                      preferred_element_type=F32))
        acc = acc + gate * jnp.dot(ys[i], wb_ref[i], preferred_element_type=F32)
    out = jnp.dot(acc.astype(BF16), wo_ref[...], preferred_element_type=F32)
    xm = x + md[2:3] * out
    xo_ref[0] = xm
    h2 = _modnorm(xm, g2_ref[...], md[4:5], md[3:4])
    h2_ref[0] = h2
    lg_ref[0] = _dot3(h2, wr_ref[...])


def _merge_call(xx, mod, g1, g2, of, ob, rg, ym, yg, yw, wg, wb, wo, wr, gmat, nq):
    tok = lambda w: pl.BlockSpec((1, TM, w), lambda b, t: (b, t, 0))
    return pl.pallas_call(
        _merge_kernel,
        grid=(BATCH, nq),
        in_specs=[tok(D_MODEL),
                  pl.BlockSpec((1, 1, 8, D_MODEL), lambda b, t: (b, t // NT_LAT, 0, 0)),
                  _const_spec((1, D_MODEL)), _const_spec((1, D_MODEL)),
                  tok(MIX_W), tok(MIX_W), tok(MIX_W), tok(MIX_W), tok(MIX_W), tok(MIX_W),
                  _const_spec((D_MODEL, 4 * D_MODEL)),
                  _const_spec((4, MIX_W, D_MODEL)),
                  _const_spec((D_MODEL, D_MODEL)),
                  _const_spec((D_MODEL, N_EXPERTS)),
                  _const_spec((MIX_W, MIX_W))],
        out_specs=[tok(D_MODEL), tok(D_MODEL), tok(N_EXPERTS)],
        out_shape=[jax.ShapeDtypeStruct((BATCH, nq * TM, D_MODEL), F32),
                   jax.ShapeDtypeStruct((BATCH, nq * TM, D_MODEL), F32),
                   jax.ShapeDtypeStruct((BATCH, nq * TM, N_EXPERTS), F32)],
        compiler_params=_params(("parallel", "parallel")),
        name="merge",
    )(xx, mod, g1, g2, of, ob, rg, ym, yg, yw, wg, wb, wo, wr, gmat)


def _moe_kernel(row0_ref, nblk_ref, aid_ref, h_hbm, w1_ref, w3_ref, w2_ref, y_hbm,
                xbuf, ybuf, w1b, w3b, w2b, gsem, ssem, *, n_tok):
    e = pl.program_id(0)
    nb = nblk_ref[e]
    r0 = row0_ref[e]

    def gather_copy(src_row, slot, j):
        return pltpu.make_async_copy(h_hbm.at[pl.ds(src_row, 1)], xbuf.at[slot, pl.ds(j, 1)],
                                     gsem.at[slot])

    def scatter_copy(dst_row, slot, j):
        return pltpu.make_async_copy(ybuf.at[slot, pl.ds(j, 1)], y_hbm.at[pl.ds(dst_row, 1)],
                                     ssem.at[slot])

    def issue_gather(r, slot):
        base = r0 + r * MOE_ROWS

        def one(j, carry):
            a = aid_ref[base + j]
            gather_copy(jnp.minimum(a >> 1, n_tok - 1), slot, j).start()
            return carry

        lax.fori_loop(0, MOE_ROWS, one, 0, unroll=8)

    def issue_scatter(r, slot):
        base = r0 + r * MOE_ROWS

        def one(j, carry):
            scatter_copy(aid_ref[base + j], slot, j).start()
            return carry

        lax.fori_loop(0, MOE_ROWS, one, 0, unroll=8)

    def wait_gather(slot):
        pltpu.make_async_copy(h_hbm.at[pl.ds(0, MOE_ROWS)], xbuf.at[slot], gsem.at[slot]).wait()

    def wait_scatter(slot):
        pltpu.make_async_copy(ybuf.at[slot], y_hbm.at[pl.ds(0, MOE_ROWS)], ssem.at[slot]).wait()

    @pl.when(e == 0)
    def _():
        ybuf[...] = jnp.zeros_like(ybuf)
        for slot in range(2):
            tail = pltpu.make_async_copy(
                ybuf.at[slot], y_hbm.at[pl.ds(TOP_K * n_tok + slot * MOE_ROWS, MOE_ROWS)], ssem.at[slot])
            tail.start()
            tail.wait()

    @pl.when(nb > 0)
    def _():
        w1b[...] = w1_ref[0].astype(BF16)
        w3b[...] = w3_ref[0].astype(BF16)
        w2b[...] = w2_ref[0].astype(BF16)
        issue_gather(0, 0)

        def block(r, carry):
            slot = r % 2

            @pl.when(r + 1 < nb)
            def _():
                issue_gather(r + 1, 1 - slot)

            wait_gather(slot)

            @pl.when(r >= 2)
            def _():
                wait_scatter(slot)

            xb = xbuf[slot].astype(BF16)
            h1 = jnp.dot(xb, w1b[...], preferred_element_type=F32)
            h3 = jnp.dot(xb, w3b[...], preferred_element_type=F32)
            hid = (h1 * jax.nn.sigmoid(h1) * h3).astype(BF16)
            ybuf[slot] = jnp.dot(hid, w2b[...], preferred_element_type=F32)
            issue_scatter(r, slot)
            return carry

        lax.fori_loop(0, nb, block, 0)

        @pl.when(nb >= 2)
        def _():
            wait_scatter(nb % 2)

        wait_scatter((nb - 1) % 2)


def _moe_call(row0, nblk, aid, h2, w1, w3, w2):
    n_tok = h2.shape[0]
    wspec = pl.BlockSpec((1, D_MODEL, D_EXPERT), lambda e, *_: (e, 0, 0))
    grid_spec = pltpu.PrefetchScalarGridSpec(
        num_scalar_prefetch=3,
        grid=(N_EXPERTS,),
        in_specs=[pl.BlockSpec(memory_space=pl.ANY), wspec, wspec,
                  pl.BlockSpec((1, D_EXPERT, D_MODEL), lambda e, *_: (e, 0, 0))],
        out_specs=pl.BlockSpec(memory_space=pl.ANY),
        scratch_shapes=[pltpu.VMEM((2, MOE_ROWS, D_MODEL), F32),
                        pltpu.VMEM((2, MOE_ROWS, D_MODEL), F32),
                        pltpu.VMEM((D_MODEL, D_EXPERT), BF16),
                        pltpu.VMEM((D_MODEL, D_EXPERT), BF16),
                        pltpu.VMEM((D_EXPERT, D_MODEL), BF16),
                        pltpu.SemaphoreType.DMA((2,)),
                        pltpu.SemaphoreType.DMA((2,))])
    return pl.pallas_call(
        functools.partial(_moe_kernel, n_tok=n_tok),
        grid_spec=grid_spec,
        out_shape=jax.ShapeDtypeStruct((TOP_K * n_tok + 2 * MOE_ROWS, D_MODEL), F32),
        compiler_params=_params(("arbitrary",)),
        name="moe",
    )(row0, nblk, aid, h2, w1, w3, w2)


def _route(logits, b_router):
    n = logits.shape[0]
    s = jax.nn.sigmoid(logits)
    sel = s + b_router.astype(F32)
    grp_score = jnp.sum(lax.top_k(sel.reshape(n, N_GROUPS, EXPERTS_PER_GROUP), TOP_K)[0], axis=-1)
    grp = jnp.argmax(grp_score, axis=-1)
    in_grp = (jnp.arange(N_EXPERTS) // EXPERTS_PER_GROUP)[None, :] == grp[:, None]
    _, e_idx = lax.top_k(jnp.where(in_grp, sel, NEG_INF), TOP_K)
    gw = jnp.take_along_axis(s, e_idx, axis=-1)
    gw = gw / jnp.sum(gw, axis=-1, keepdims=True)

    n_asg = n * TOP_K
    flat_e = e_idx.reshape(-1).astype(jnp.int32)
    order = jnp.argsort(flat_e).astype(jnp.int32)
    e_sorted = flat_e[order]
    counts = jnp.bincount(flat_e, length=N_EXPERTS).astype(jnp.int32)
    nblk = (counts + MOE_ROWS - 1) // MOE_ROWS
    padded = nblk * MOE_ROWS
    start_s = jnp.cumsum(counts) - counts
    end_p = jnp.cumsum(padded)
    start_p = end_p - padded
    dest = start_p[e_sorted] + jnp.arange(n_asg, dtype=jnp.int32) - start_s[e_sorted]
    n_blocks = n_asg // MOE_ROWS + N_EXPERTS
    blk = jnp.arange(n_blocks, dtype=jnp.int32)
    blk_e = jnp.minimum(jnp.searchsorted(end_p, blk * MOE_ROWS, side='right'), N_EXPERTS - 1)
    parity = (blk - start_p[blk_e] // MOE_ROWS) % 2
    dump = (n_asg + parity[:, None] * MOE_ROWS + jnp.arange(MOE_ROWS, dtype=jnp.int32)[None, :])
    aid = dump.reshape(-1).astype(jnp.int32).at[dest].set(order)
    return gw, start_p.astype(jnp.int32), nblk, aid


def _combine_kernel(x_ref, mod_ref, y_ref, gw_ref, fg_ref, o_ref, *, final):
    md = mod_ref[0, 0]
    gw = gw_ref[...]
    y = y_ref[...]
    f = gw[:, 0:1] * y[:, :D_MODEL] + gw[:, 1:2] * y[:, D_MODEL:]
    xn = x_ref[0] + md[5:6] * f
    o_ref[0] = _rmsnorm(xn, fg_ref[...]) if final else xn


def _combine_call(xm, mod, y2, gw, fg, nq, final):
    return pl.pallas_call(
        functools.partial(_combine_kernel, final=final),
        grid=(BATCH, nq),
        in_specs=[pl.BlockSpec((1, TM, D_MODEL), lambda b, t: (b, t, 0)),
                  pl.BlockSpec((1, 1, 8, D_MODEL), lambda b, t: (b, t // NT_LAT, 0, 0)),
                  pl.BlockSpec((TM, TOP_K * D_MODEL), lambda b, t: (b * nq + t, 0)),
                  pl.BlockSpec((TM, TOP_K), lambda b, t: (b * nq + t, 0)),
                  _const_spec((1, D_MODEL))],
        out_specs=pl.BlockSpec((1, TM, D_MODEL), lambda b, t: (b, t, 0)),
        out_shape=jax.ShapeDtypeStruct((BATCH, nq * TM, D_MODEL), F32),
        compiler_params=_params(("parallel", "parallel")),
        name="combine",
    )(xm, mod, y2, gw, fg)


def _rope_tables():
    rows = SEQ // GRID_W
    row = jnp.broadcast_to(jnp.arange(rows)[:, None], (rows, GRID_W)).reshape(-1).astype(F32)
    col = jnp.broadcast_to(jnp.arange(GRID_W)[None, :], (rows, GRID_W)).reshape(-1).astype(F32)

    def cs(rot_dim):
        n_f = rot_dim // 4
        inv = ROPE_BASE ** (-jnp.arange(n_f, dtype=F32) / n_f)
        ang = jnp.concatenate([row[:, None] * inv, col[:, None] * inv], axis=-1)
        return jnp.cos(ang), jnp.sin(ang)

    def with_ctx(c, u, d):
        one = jnp.ones((CTX_LEN, LANES), F32)
        zero = jnp.zeros((CTX_LEN, LANES), F32)
        return (jnp.concatenate([c, one]), jnp.concatenate([u, zero]), jnp.concatenate([d, zero]))

    cos, sin = cs(HEAD_DIM)
    z = jnp.zeros_like(sin)
    t64 = with_ctx(jnp.tile(jnp.concatenate([cos, cos], -1), (1, 2)),
                   jnp.tile(jnp.concatenate([-sin, z], -1), (1, 2)),
                   jnp.tile(jnp.concatenate([z, sin], -1), (1, 2)))
    cos, sin = cs(MLA_ROPE)
    z = jnp.zeros_like(sin)
    one_n = jnp.ones((SEQ, MLA_NOPE), F32)
    zero_n = jnp.zeros((SEQ, MLA_NOPE), F32)
    one_p = jnp.ones((SEQ, LANES - MLA_NOPE - MLA_ROPE), F32)
    zero_p = jnp.zeros((SEQ, LANES - MLA_NOPE - MLA_ROPE), F32)
    tm = with_ctx(jnp.concatenate([one_n, cos, cos, one_p], -1),
                  jnp.concatenate([zero_n, -sin, z, zero_p], -1),
                  jnp.concatenate([zero_n, z, sin, zero_p], -1))
    return t64 + tm


def _ret_tables(decay):
    lg = -jnp.exp(decay.astype(F32))
    idx = jnp.arange(RET_CHUNK, dtype=F32)
    diff = idx[:, None] - idx[None, :]
    fwd = diff >= 0
    bwd = diff < 0
    dm_f = jnp.where(fwd, jnp.exp(lg[0][:, None, None] * jnp.where(fwd, diff, 0.0)), 0.0)
    dm_b = jnp.where(bwd, jnp.exp(lg[1][:, None, None] * jnp.where(bwd, -diff, 0.0)), 0.0)
    dmat = jnp.stack([dm_f, dm_b])
    xi = jnp.stack([jnp.exp(lg[0][:, None] * (idx + 1.0)),
                    jnp.exp(lg[1][:, None] * (RET_CHUNK - idx))])
    zeta = jnp.stack([jnp.exp(lg[0][:, None] * (RET_CHUNK - 1.0 - idx)),
                      jnp.exp(lg[1][:, None] * idx)])
    gch = jnp.exp(lg * RET_CHUNK)
    xi_t = jnp.repeat(jnp.transpose(xi, (0, 2, 1)), HEAD_DIM, axis=2)
    zt_t = jnp.repeat(zeta, HEAD_DIM, axis=1)
    gc_t = jnp.broadcast_to(jnp.repeat(gch, HEAD_DIM, axis=1)[:, :, None], (2, MIX_W, MIX_W))
    return dmat, xi_t, zt_t, gc_t


def _in_proj_columns():
    o_mla = 4 * MIX_W
    o_gqa = o_mla + Q_LORA + KV_LORA + MLA_ROPE
    kv_w = (N_HEADS // 2) * HEAD_DIM
    o_win = o_gqa + MIX_W + 2 * kv_w
    ar = jnp.arange
    dup = jnp.concatenate([ar(HEAD_DIM), ar(HEAD_DIM), HEAD_DIM + ar(HEAD_DIM), HEAD_DIM + ar(HEAD_DIM)])

    def gqa_cols(o):
        return [o + ar(MIX_W), o + MIX_W + dup, o + MIX_W + kv_w + dup]

    return jnp.concatenate([ar(o_mla), o_mla + ar(Q_LORA), o_mla + Q_LORA + ar(KV_LORA)]
                           + gqa_cols(o_gqa) + gqa_cols(o_win)
                           + [o_mla + Q_LORA + KV_LORA + ar(MLA_ROPE)])


def _layer_weights(l, w_in, mla_w_uq, mla_w_ukv):
    cols = _in_proj_columns()
    win_p = jnp.pad(w_in[l][:, cols], ((0, 0), (0, N_IN_P - cols.shape[0]))).astype(BF16)
    uq = mla_w_uq[l].reshape(Q_LORA, N_HEADS, MLA_NOPE + MLA_ROPE)
    wuq = jnp.pad(uq, ((0, 0), (0, 0), (0, LANES - MLA_NOPE - MLA_ROPE))).reshape(Q_LORA, MLA_QK)
    ukv = mla_w_ukv[l].reshape(KV_LORA, N_HEADS, MLA_NOPE + MLA_V)
    wuk = jnp.pad(ukv[:, :, :MLA_NOPE], ((0, 0), (0, 0), (0, LANES - MLA_NOPE))).reshape(KV_LORA, MLA_QK)
    wuv = ukv[:, :, MLA_NOPE:].reshape(KV_LORA, MIX_W)
    return win_p, wuq.astype(BF16), wuk.astype(BF16), wuv.astype(BF16)


def _krope_placement():
    r = jnp.arange(LANES)[:, None]
    c = jnp.arange(MLA_QK)[None, :]
    return ((r < MLA_ROPE) & (c % LANES == MLA_NOPE + r)).astype(BF16)


def _head_block_matrix():
    r = jnp.arange(MIX_W)
    return (r[:, None] // HEAD_DIM == r[None, :] // HEAD_DIM).astype(BF16)


def kernel(x, c, ctx, c_ctx, w_ada, b_ada, norm1_g, norm2_g, w_in, w_gate, w_branch, w_out, ret_decay,
           mla_qn_g, mla_w_uq, mla_kvn_g, mla_w_ukv, gqa_qn_g, gqa_kn_g, win_sink, w_router, b_router,
           w1, w3, w2, final_norm_g):
    cvec = jnp.concatenate([c, c_ctx[None, :], jnp.zeros((7, D_MODEL), F32)], axis=0)
    ada = _ada_call(cvec, w_ada, b_ada)
    tabs = _rope_tables()
    gmat = _head_block_matrix()
    ekr = _krope_placement()
    xx = jnp.concatenate([x, ctx], axis=1)
    out = None
    for l in range(DEPTH):
        last = l == DEPTH - 1
        nq = NT_LAT if last else NT
        m = ada[l].reshape(16, 6, D_MODEL)
        m_lat = m[:BATCH]
        m_ctx = jnp.broadcast_to(m[BATCH][None], (BATCH, 6, D_MODEL))
        mod = jnp.pad(jnp.stack([m_lat, m_ctx], axis=1), ((0, 0), (0, 0), (0, 2), (0, 0)))
        g1 = norm1_g[l][None, :]
        g2 = norm2_g[l][None, :]
        win_p, wuq, wuk, wuv = _layer_weights(l, w_in, mla_w_uq, mla_w_ukv)
        (rq, rkt, rv, rg, mq, mkt, mv, gq, gkt, gv, wq, wkt, wv) = _prep_call(
            xx, mod, g1, win_p, wuq, wuk, wuv, ekr,
            mla_qn_g[l][None, :], mla_kvn_g[l][None, :],
            jnp.tile(gqa_qn_g[l], N_HEADS)[None, :], jnp.tile(gqa_kn_g[l], N_HEADS)[None, :],
            gmat, tabs)
        of, ob = _ret_call(rq, rkt, rv, *_ret_tables(ret_decay[l]))
        ym = _dense_call(mq, mkt, mv, nq, False, "mla")
        yg = _dense_call(gq, gkt, gv, nq, True, "gqa")
        yw = _win_call(win_sink[l], wq, wkt, wv, nq)
        xm, h2, logits = _merge_call(
            xx, mod, g1, g2, of, ob, rg, ym, yg, yw,
            w_gate[l].astype(BF16), w_branch[l].astype(BF16), w_out[l].astype(BF16), w_router, gmat, nq)
        n_tok = BATCH * nq * TM
        gw, row0, nblk, aid = _route(logits.reshape(n_tok, N_EXPERTS), b_router)
        y2 = _moe_call(row0, nblk, aid, h2.reshape(n_tok, D_MODEL), w1[l], w3[l], w2[l])
        y2 = y2.reshape(n_tok + MOE_ROWS, TOP_K * D_MODEL)
        res = _combine_call(xm, mod, y2, gw, final_norm_g[None, :], nq, last)
        if last:
            out = res
        else:
            xx = res
    return out
```

```python
import functools

import jax
import jax.numpy as jnp
from jax import lax
from jax.experimental import pallas as pl
from jax.experimental.pallas import tpu as pltpu

F32 = jnp.float32
BF16 = jnp.bfloat16

D_MODEL = 1024
BATCH = 8
SEQ = 2048
DEPTH = 2
CTX_LEN = 256
TOK = SEQ + CTX_LEN
GRID_W = 64
N_HEADS = 4
HEAD_DIM = 64
MIX_W = N_HEADS * HEAD_DIM
RET_CHUNK = 128
Q_LORA = 256
KV_LORA = 128
MLA_NOPE = 64
MLA_ROPE = 32
MLA_V = 64
WINDOW = 128
N_EXPERTS = 32
N_GROUPS = 4
EXPERTS_PER_GROUP = N_EXPERTS // N_GROUPS
TOP_K = 2
D_EXPERT = 1024
ROPE_BASE = 10000.0
EPS = 1e-6
NEG_INF = -1e30

LANES = 128
TM = 256
NT = TOK // TM
NT_LAT = SEQ // TM
WIN_KEYS = TM + 2 * WINDOW
N_CHUNK = TOK // RET_CHUNK
MOE_ROWS = 128
N_IN_P = 3072
MLA_QK = 4 * LANES

C_RET = 0
C_MQ = 1024
C_MKV = 1280
C_GQ = 1408
C_GK = 1664
C_GV = 1920
C_WQ = 2176
C_WK = 2432
C_WV = 2688
C_KR = 2944

VMEM_LIMIT = 56 * 1024 * 1024


def _params(sem, vmem=VMEM_LIMIT):
    return pltpu.CompilerParams(dimension_semantics=sem, vmem_limit_bytes=vmem)


def _const_spec(shape):
    nd = len(shape)
    return pl.BlockSpec(shape, lambda *_: (0,) * nd, pipeline_mode=pl.Buffered(1))


def _bdot(a, b):
    return jnp.dot(a.astype(BF16), b.astype(BF16), preferred_element_type=F32)


def _split(a):
    hi = a.astype(BF16)
    lo = (a - hi.astype(F32)).astype(BF16)
    return hi, lo


def _dot_split_lhs(a, b):
    hi, lo = _split(a)
    return (jnp.dot(hi, b, preferred_element_type=F32)
            + jnp.dot(lo, b, preferred_element_type=F32))


def _dot3(a, b):
    ah, al = _split(a)
    bh, bl = _split(b)
    return (jnp.dot(ah, bh, preferred_element_type=F32)
            + jnp.dot(ah, bl, preferred_element_type=F32)
            + jnp.dot(al, bh, preferred_element_type=F32))


def _modnorm(x, g, sc, sh):
    ms = jnp.mean(x * x, axis=-1, keepdims=True)
    return x * lax.rsqrt(ms + EPS) * g * (1.0 + sc) + sh


def _rmsnorm(x, g):
    ms = jnp.mean(x * x, axis=-1, keepdims=True)
    return x * lax.rsqrt(ms + EPS) * g


def _rope(x, c, s_up, s_dn, half):
    outs = []
    for j in range(x.shape[1] // LANES):
        xc = x[:, j * LANES:(j + 1) * LANES]
        outs.append(xc * c + pltpu.roll(xc, LANES - half, 1) * s_up + pltpu.roll(xc, half, 1) * s_dn)
    return outs[0] if len(outs) == 1 else jnp.concatenate(outs, axis=1)


def _head_mean(x, gmat):
    return _dot_split_lhs(x, gmat) * (1.0 / HEAD_DIM)


def _ada_kernel(c_ref, w_ref, b_ref, o_ref):
    c = c_ref[...]
    sc = c * jax.nn.sigmoid(c)
    o_ref[0] = _dot3(sc, w_ref[0]) + b_ref[0]


def _ada_call(cvec, w_ada, b_ada):
    tn = 1536
    return pl.pallas_call(
        _ada_kernel,
        grid=(DEPTH, 6 * D_MODEL // tn),
        in_specs=[
            pl.BlockSpec((16, D_MODEL), lambda l, j: (0, 0)),
            pl.BlockSpec((1, D_MODEL, tn), lambda l, j: (l, 0, j)),
            pl.BlockSpec((1, 1, tn), lambda l, j: (l, 0, j)),
        ],
        out_specs=pl.BlockSpec((1, 16, tn), lambda l, j: (l, 0, j)),
        out_shape=jax.ShapeDtypeStruct((DEPTH, 16, 6 * D_MODEL), F32),
        compiler_params=_params(("arbitrary", "arbitrary")),
        name="ada",
    )(cvec, w_ada, b_ada.reshape(DEPTH, 1, 6 * D_MODEL))


def _prep_kernel(x_ref, mod_ref, g1_ref, win_ref, wuq_ref, wuk_ref, wuv_ref, ekr_ref,
                 qng_ref, kvng_ref, gqg_ref, gkg_ref, gmat_ref,
                 c64_ref, u64_ref, d64_ref, cm_ref, um_ref, dm_ref,
                 rq_ref, rkt_ref, rv_ref, rg_ref, mq_ref, mkt_ref, mv_ref,
                 gq_ref, gkt_ref, gv_ref, wq_ref, wkt_ref, wv_ref):
    x = x_ref[0]
    md = mod_ref[0, 0]
    h = _modnorm(x, g1_ref[...], md[1:2], md[0:1])
    p = jnp.dot(h.astype(BF16), win_ref[...], preferred_element_type=F32)

    c64, u64, d64 = c64_ref[...], u64_ref[...], d64_ref[...]
    cm, um, dm = cm_ref[...], um_ref[...], dm_ref[...]
    gmat = gmat_ref[...]
    qk_scale = HEAD_DIM ** -0.5
    rope64 = lambda a: _rope(a, c64, u64, d64, HEAD_DIM // 2)
    ropem = lambda a: _rope(a, cm, um, dm, MLA_ROPE // 2)

    rq_ref[0] = (rope64(p[:, C_RET:C_RET + 256]) * qk_scale).astype(BF16)
    rkt_ref[0] = rope64(p[:, C_RET + 256:C_RET + 512]).T.astype(BF16)
    rv_ref[0] = p[:, C_RET + 512:C_RET + 768].astype(BF16)
    rg_ref[0] = p[:, C_RET + 768:C_RET + 1024].astype(BF16)

    qn = _rmsnorm(p[:, C_MQ:C_MQ + Q_LORA], qng_ref[...])
    q2 = ropem(_bdot(qn, wuq_ref[...])) * ((MLA_NOPE + MLA_ROPE) ** -0.5)
    mq_ref[0] = q2.astype(BF16)
    kvn = _rmsnorm(p[:, C_MKV:C_MKV + KV_LORA], kvng_ref[...]).astype(BF16)
    k2 = (jnp.dot(kvn, wuk_ref[...], preferred_element_type=F32)
          + _dot_split_lhs(p[:, C_KR:C_KR + LANES], ekr_ref[...]))
    mkt_ref[0] = ropem(k2).T.astype(BF16)
    mv_ref[0] = jnp.dot(kvn, wuv_ref[...], preferred_element_type=F32).astype(BF16)

    gq = p[:, C_GQ:C_GQ + 256]
    gq = gq * lax.rsqrt(_head_mean(gq * gq, gmat) + EPS) * gqg_ref[...]
    gq_ref[0] = (rope64(gq) * qk_scale).astype(BF16)
    gk = p[:, C_GK:C_GK + 256]
    gk = gk * lax.rsqrt(_head_mean(gk * gk, gmat) + EPS) * gkg_ref[...]
    gkt_ref[0] = rope64(gk).T.astype(BF16)
    gv_ref[0] = p[:, C_GV:C_GV + 256].astype(BF16)

    wq_ref[0] = (rope64(p[:, C_WQ:C_WQ + 256]) * qk_scale).astype(BF16)
    wkt_ref[0] = rope64(p[:, C_WK:C_WK + 256]).T.astype(BF16)
    wv_ref[0] = p[:, C_WV:C_WV + 256].astype(BF16)


def _prep_call(xx, mod, g1, win_p, wuq, wuk, wuv, ekr, qng, kvng, gqg, gkg, gmat, tabs):
    tok = lambda w: pl.BlockSpec((1, TM, w), lambda b, t: (b, t, 0))
    tokt = lambda w: pl.BlockSpec((1, w, TM), lambda b, t: (b, 0, t))
    tab = pl.BlockSpec((TM, LANES), lambda b, t: (t, 0))
    sd = lambda w: jax.ShapeDtypeStruct((BATCH, TOK, w), BF16)
    sdt = lambda w: jax.ShapeDtypeStruct((BATCH, w, TOK), BF16)
    in_specs = [
        tok(D_MODEL),
        pl.BlockSpec((1, 1, 8, D_MODEL), lambda b, t: (b, t // NT_LAT, 0, 0)),
        _const_spec((1, D_MODEL)),
        _const_spec((D_MODEL, N_IN_P)),
        _const_spec((Q_LORA, MLA_QK)),
        _const_spec((KV_LORA, MLA_QK)),
        _const_spec((KV_LORA, MIX_W)),
        _const_spec((LANES, MLA_QK)),
        _const_spec((1, Q_LORA)),
        _const_spec((1, KV_LORA)),
        _const_spec((1, MIX_W)),
        _const_spec((1, MIX_W)),
        _const_spec((MIX_W, MIX_W)),
        tab, tab, tab, tab, tab, tab,
    ]
    out_specs = [tok(256), tokt(256), tok(256), tok(256),
                 tok(MLA_QK), tokt(MLA_QK), tok(256),
                 tok(256), tokt(256), tok(256),
                 tok(256), tokt(256), tok(256)]
    out_shape = [sd(256), sdt(256), sd(256), sd(256),
                 sd(MLA_QK), sdt(MLA_QK), sd(256),
                 sd(256), sdt(256), sd(256),
                 sd(256), sdt(256), sd(256)]
    return pl.pallas_call(
        _prep_kernel,
        grid=(BATCH, NT),
        in_specs=in_specs,
        out_specs=out_specs,
        out_shape=out_shape,
        compiler_params=_params(("parallel", "parallel")),
        name="prep",
    )(xx, mod, g1, win_p, wuq, wuk, wuv, ekr, qng, kvng, gqg, gkg, gmat, *tabs)


def _ret_kernel(qf_ref, ktf_ref, vf_ref, qb_ref, ktb_ref, vb_ref,
                dmat_ref, xi_ref, zt_ref, gc_ref, of_ref, ob_ref, sf_ref, sb_ref):
    i = pl.program_id(0)

    @pl.when(i == 0)
    def _():
        sf_ref[...] = jnp.zeros_like(sf_ref)
        sb_ref[...] = jnp.zeros_like(sb_ref)

    lane_head = lax.broadcasted_iota(jnp.int32, (RET_CHUNK, MIX_W), 1) // HEAD_DIM
    r_head = lax.broadcasted_iota(jnp.int32, (MIX_W, MIX_W), 0) // HEAD_DIM
    c_head = lax.broadcasted_iota(jnp.int32, (MIX_W, MIX_W), 1) // HEAD_DIM
    block_diag = r_head == c_head

    def one(b, d, q_ref, kt_ref, v_ref, o_ref, s_ref):
        q = q_ref[b].astype(F32)
        kt = kt_ref[b]
        v = v_ref[b]
        s_old = s_ref[b]
        o = _bdot(q * xi_ref[d], s_old)
        for hd in range(N_HEADS):
            qm = jnp.where(lane_head == hd, q, 0.0).astype(BF16)
            inner = jnp.dot(qm, kt, preferred_element_type=F32) * dmat_ref[d, hd]
            oh = jnp.dot(inner.astype(BF16), v, preferred_element_type=F32)
            o = o + jnp.where(lane_head == hd, oh, 0.0)
        o_ref[b] = o
        kz = (kt.astype(F32) * zt_ref[d]).astype(BF16)
        upd = jnp.dot(kz, v, preferred_element_type=F32)
        s_ref[b] = gc_ref[d] * s_old + jnp.where(block_diag, upd, 0.0)

    def body(b, carry):
        one(b, 0, qf_ref, ktf_ref, vf_ref, of_ref, sf_ref)
        one(b, 1, qb_ref, ktb_ref, vb_ref, ob_ref, sb_ref)
        return carry

    lax.fori_loop(0, BATCH, body, 0)


def _ret_call(rq, rkt, rv, dmat, xi, zt, gc):
    cf = lambda i: (i + SEQ // RET_CHUNK) % N_CHUNK
    cb = lambda i: N_CHUNK - 1 - i
    rows = lambda f: pl.BlockSpec((BATCH, RET_CHUNK, MIX_W), lambda i: (0, f(i), 0))
    cols = lambda f: pl.BlockSpec((BATCH, MIX_W, RET_CHUNK), lambda i: (0, 0, f(i)))
    return pl.pallas_call(
        _ret_kernel,
        grid=(N_CHUNK,),
        in_specs=[rows(cf), cols(cf), rows(cf), rows(cb), cols(cb), rows(cb),
                  _const_spec((2, N_HEADS, RET_CHUNK, RET_CHUNK)),
                  _const_spec((2, RET_CHUNK, MIX_W)),
                  _const_spec((2, MIX_W, RET_CHUNK)),
                  _const_spec((2, MIX_W, MIX_W))],
        out_specs=[rows(cf), rows(cb)],
        out_shape=[jax.ShapeDtypeStruct((BATCH, TOK, MIX_W), F32)] * 2,
        scratch_shapes=[pltpu.VMEM((BATCH, MIX_W, MIX_W), F32),
                        pltpu.VMEM((BATCH, MIX_W, MIX_W), F32)],
        compiler_params=_params(("arbitrary",)),
        name="retention",
    )(rq, rkt, rv, rq, rkt, rv, dmat, xi, zt, gc)


def _head_q(q_ref, hd, pair):
    if not pair:
        return q_ref[0, :, hd * LANES:(hd + 1) * LANES], hd * LANES
    c = hd // 2
    qc = q_ref[0, :, c * LANES:(c + 1) * LANES]
    half = lax.broadcasted_iota(jnp.int32, qc.shape, 1) // HEAD_DIM
    return jnp.where(half == hd % 2, qc, jnp.zeros_like(qc)), c * LANES


def _dense_kernel(q_ref, kt_ref, v_ref, o_ref, *, pair):
    t = pl.program_id(1)
    lane_head = lax.broadcasted_iota(jnp.int32, (TM, MIX_W), 1) // HEAD_DIM

    def run(k_lo, k_hi):
        v = v_ref[0, k_lo:k_hi, :]
        acc = jnp.zeros((TM, MIX_W), F32)
        for hd in range(N_HEADS):
            qm, r0 = _head_q(q_ref, hd, pair)
            s = jnp.dot(qm, kt_ref[0, r0:r0 + LANES, k_lo:k_hi], preferred_element_type=F32)
            m = jnp.max(s, axis=-1, keepdims=True)
            p = jnp.exp(s - m)
            l = jnp.sum(p, axis=-1, keepdims=True)
            oh = jnp.dot(p.astype(BF16), v, preferred_element_type=F32)
            acc = jnp.where(lane_head == hd, oh * (1.0 / l), acc)
        o_ref[0] = acc.astype(BF16)

    @pl.when(t < NT_LAT)
    def _():
        run(0, TOK)

    @pl.when(t >= NT_LAT)
    def _():
        run(SEQ, TOK)


def _dense_call(q, kt, v, nq, pair, name):
    wq = q.shape[-1]
    return pl.pallas_call(
        functools.partial(_dense_kernel, pair=pair),
        grid=(BATCH, nq),
        in_specs=[pl.BlockSpec((1, TM, wq), lambda b, t: (b, t, 0)),
                  pl.BlockSpec((1, kt.shape[1], TOK), lambda b, t: (b, 0, 0)),
                  pl.BlockSpec((1, TOK, MIX_W), lambda b, t: (b, 0, 0))],
        out_specs=pl.BlockSpec((1, TM, MIX_W), lambda b, t: (b, t, 0)),
        out_shape=jax.ShapeDtypeStruct((BATCH, nq * TM, MIX_W), BF16),
        compiler_params=_params(("parallel", "arbitrary")),
        name=name,
    )(q, kt, v)


def _win_kernel(sink_ref, q_ref, kt_ref, v_ref, o_ref):
    t = pl.program_id(1)
    lane_head = lax.broadcasted_iota(jnp.int32, (TM, MIX_W), 1) // HEAD_DIM
    v_ctx = v_ref[0, SEQ:TOK, :]

    @pl.when(t < NT_LAT)
    def _():
        start = pl.multiple_of(jnp.clip(t * TM - WINDOW, 0, SEQ - WIN_KEYS), LANES)
        qpos = t * TM + lax.broadcasted_iota(jnp.int32, (TM, WIN_KEYS), 0)
        kpos = start + lax.broadcasted_iota(jnp.int32, (TM, WIN_KEYS), 1)
        valid = jnp.abs(kpos - qpos) <= WINDOW
        v_loc = v_ref[0, pl.ds(start, WIN_KEYS), :]
        acc = jnp.zeros((TM, MIX_W), F32)
        for hd in range(N_HEADS):
            qm, r0 = _head_q(q_ref, hd, True)
            s_loc = jnp.dot(qm, kt_ref[0, r0:r0 + LANES, pl.ds(start, WIN_KEYS)],
                            preferred_element_type=F32)
            s_loc = jnp.where(valid, s_loc, NEG_INF)
            s_ctx = jnp.dot(qm, kt_ref[0, r0:r0 + LANES, SEQ:TOK], preferred_element_type=F32)
            sk = sink_ref[hd]
            m = jnp.maximum(jnp.maximum(jnp.max(s_loc, axis=-1, keepdims=True),
                                        jnp.max(s_ctx, axis=-1, keepdims=True)), sk)
            p_loc = jnp.exp(s_loc - m)
            p_ctx = jnp.exp(s_ctx - m)
            l = (jnp.sum(p_loc, axis=-1, keepdims=True) + jnp.sum(p_ctx, axis=-1, keepdims=True)
                 + jnp.exp(sk - m))
            oh = (jnp.dot(p_loc.astype(BF16), v_loc, preferred_element_type=F32)
                  + jnp.dot(p_ctx.astype(BF16), v_ctx, preferred_element_type=F32))
            acc = jnp.where(lane_head == hd, oh * (1.0 / l), acc)
        o_ref[0] = acc.astype(BF16)

    @pl.when(t >= NT_LAT)
    def _():
        acc = jnp.zeros((TM, MIX_W), F32)
        for hd in range(N_HEADS):
            qm, r0 = _head_q(q_ref, hd, True)
            s = jnp.dot(qm, kt_ref[0, r0:r0 + LANES, SEQ:TOK], preferred_element_type=F32)
            sk = sink_ref[hd]
            m = jnp.maximum(jnp.max(s, axis=-1, keepdims=True), sk)
            p = jnp.exp(s - m)
            l = jnp.sum(p, axis=-1, keepdims=True) + jnp.exp(sk - m)
            oh = jnp.dot(p.astype(BF16), v_ctx, preferred_element_type=F32)
            acc = jnp.where(lane_head == hd, oh * (1.0 / l), acc)
        o_ref[0] = acc.astype(BF16)


def _win_call(sink, q, kt, v, nq):
    return pl.pallas_call(
        _win_kernel,
        grid=(BATCH, nq),
        in_specs=[pl.BlockSpec(memory_space=pltpu.SMEM),
                  pl.BlockSpec((1, TM, MIX_W), lambda b, t: (b, t, 0)),
                  pl.BlockSpec((1, MIX_W, TOK), lambda b, t: (b, 0, 0)),
                  pl.BlockSpec((1, TOK, MIX_W), lambda b, t: (b, 0, 0))],
        out_specs=pl.BlockSpec((1, TM, MIX_W), lambda b, t: (b, t, 0)),
        out_shape=jax.ShapeDtypeStruct((BATCH, nq * TM, MIX_W), BF16),
        compiler_params=_params(("parallel", "arbitrary")),
        name="window",
    )(sink, q, kt, v)


def _merge_kernel(x_ref, mod_ref, g1_ref, g2_ref, of_ref, ob_ref, rg_ref, ym_ref, yg_ref, yw_ref,
                  wg_ref, wb_ref, wo_ref, wr_ref, gmat_ref, xo_ref, h2_ref, lg_ref):
    x = x_ref[0]
    md = mod_ref[0, 0]
    hb = _modnorm(x, g1_ref[...], md[1:2], md[0:1]).astype(BF16)

    gmat = gmat_ref[...]
    o = of_ref[0] + ob_ref[0]
    dlt = o - _head_mean(o, gmat)
    var = _head_mean(dlt * dlt, gmat)
    g = rg_ref[0].astype(F32)
    y_ret = dlt * lax.rsqrt(var + EPS) * (g * jax.nn.sigmoid(g))

    ys = (y_ret.astype(BF16), ym_ref[0], yg_ref[0], yw_ref[0])
    acc = jnp.zeros((TM, D_MODEL), F32)
    for i in range(4):
        gate = jax.nn.sigmoid(jnp.dot(hb, wg_ref[:, i * D_MODEL:(i + 1) * D_MODEL],
                                      preferred_element_type=F32))
        acc = acc + gate * jnp.dot(ys[i], wb_ref[i], preferred_element_type=F32)
    out = jnp.dot(acc.astype(BF16), wo_ref[...], preferred_element_type=F32)
    xm = x + md[2:3] * out
    xo_ref[0] = xm
    h2 = _modnorm(xm, g2_ref[...], md[4:5], md[3:4])
    h2_ref[0] = h2
    hh, hl = _split(h2)
    wh, wl = _split(wr_ref[...])
    nt = lambda a, b: lax.dot_general(a, b, (((1,), (1,)), ((), ())), preferred_element_type=F32)
    lg_ref[0] = nt(wh, hh) + nt(wh, hl) + nt(wl, hh)


def _merge_call(xx, mod, g1, g2, of, ob, rg, ym, yg, yw, wg, wb, wo, wr, gmat, nq):
    tok = lambda w: pl.BlockSpec((1, TM, w), lambda b, t: (b, t, 0))
    return pl.pallas_call(
        _merge_kernel,
        grid=(BATCH, nq),
        in_specs=[tok(D_MODEL),
                  pl.BlockSpec((1, 1, 8, D_MODEL), lambda b, t: (b, t // NT_LAT, 0, 0)),
                  _const_spec((1, D_MODEL)), _const_spec((1, D_MODEL)),
                  tok(MIX_W), tok(MIX_W), tok(MIX_W), tok(MIX_W), tok(MIX_W), tok(MIX_W),
                  _const_spec((D_MODEL, 4 * D_MODEL)),
                  _const_spec((4, MIX_W, D_MODEL)),
                  _const_spec((D_MODEL, D_MODEL)),
                  _const_spec((N_EXPERTS, D_MODEL)),
                  _const_spec((MIX_W, MIX_W))],
        out_specs=[tok(D_MODEL), tok(D_MODEL),
                   pl.BlockSpec((1, N_EXPERTS, TM), lambda b, t: (b, 0, t))],
        out_shape=[jax.ShapeDtypeStruct((BATCH, nq * TM, D_MODEL), F32),
                   jax.ShapeDtypeStruct((BATCH, nq * TM, D_MODEL), F32),
                   jax.ShapeDtypeStruct((BATCH, N_EXPERTS, nq * TM), F32)],
        compiler_params=_params(("parallel", "parallel")),
        name="merge",
    )(xx, mod, g1, g2, of, ob, rg, ym, yg, yw, wg, wb, wo, wr, gmat)


RANK_BITS = 20
RANK_MASK = (1 << RANK_BITS) - 1


def _route_kernel(lg_ref, b_ref, tri_ref, pk_ref, gw_ref, cnt_ref, base_ref):
    i = pl.program_id(0)

    @pl.when(i == 0)
    def _():
        base_ref[...] = jnp.zeros_like(base_ref)

    s = jax.nn.sigmoid(lg_ref[0])
    sel = s + b_ref[:, 0:1]
    sub = lax.broadcasted_iota(jnp.int32, (EXPERTS_PER_GROUP, TM), 0)
    best = e1 = e2 = s1 = s2 = None
    for g in range(N_GROUPS):
        rows = slice(g * EXPERTS_PER_GROUP, (g + 1) * EXPERTS_PER_GROUP)
        blk, sb = sel[rows], s[rows]
        m1 = jnp.max(blk, axis=0, keepdims=True)
        i1 = jnp.min(jnp.where(blk == m1, sub, EXPERTS_PER_GROUP), axis=0, keepdims=True)
        hit1 = sub == i1
        blk2 = jnp.where(hit1, -jnp.inf, blk)
        m2 = jnp.max(blk2, axis=0, keepdims=True)
        i2 = jnp.min(jnp.where(blk2 == m2, sub, EXPERTS_PER_GROUP), axis=0, keepdims=True)
        hit2 = sub == i2
        score = m1 + m2
        s1g = jnp.sum(jnp.where(hit1, sb, 0.0), axis=0, keepdims=True)
        s2g = jnp.sum(jnp.where(hit2, sb, 0.0), axis=0, keepdims=True)
        e1g = g * EXPERTS_PER_GROUP + i1
        e2g = g * EXPERTS_PER_GROUP + i2
        if g == 0:
            best, e1, e2, s1, s2 = score, e1g, e2g, s1g, s2g
        else:
            better = score > best
            best = jnp.where(better, score, best)
            e1 = jnp.where(better, e1g, e1)
            e2 = jnp.where(better, e2g, e2)
            s1 = jnp.where(better, s1g, s1)
            s2 = jnp.where(better, s2g, s2)

    eid = lax.broadcasted_iota(jnp.int32, (N_EXPERTS, TM), 0)
    oh1 = eid == e1
    oh2 = eid == e2
    oh = jnp.where(oh1 | oh2, 1.0, 0.0)
    before = jnp.dot(oh.astype(BF16), tri_ref[...], preferred_element_type=F32) + base_ref[:, 0:1]
    r1 = jnp.sum(jnp.where(oh1, before, 0.0), axis=0, keepdims=True).astype(jnp.int32)
    r2 = jnp.sum(jnp.where(oh2, before, 0.0), axis=0, keepdims=True).astype(jnp.int32)
    total = base_ref[...] + jnp.sum(oh, axis=1, keepdims=True)
    base_ref[...] = total
    cnt_ref[...] = total

    pk_ref[...] = jnp.concatenate([(e1 << RANK_BITS) + r1, (e2 << RANK_BITS) + r2], axis=0)
    den = s1 + s2
    row = lax.broadcasted_iota(jnp.int32, (8, TM), 0)
    gw_ref[...] = jnp.where(row == 0, s1 / den, jnp.where(row == 1, s2 / den, 0.0))


def _route_call(lgt, b_router, nq):
    n_tiles = BATCH * nq
    n_tok = n_tiles * TM
    r = jnp.arange(TM)
    tri = (r[:, None] < r[None, :]).astype(BF16)
    bcol = jnp.broadcast_to(b_router.astype(F32)[:, None], (N_EXPERTS, LANES))
    return pl.pallas_call(
        _route_kernel,
        grid=(n_tiles,),
        in_specs=[pl.BlockSpec((1, N_EXPERTS, TM), lambda i: (i // nq, 0, i % nq)),
                  _const_spec((N_EXPERTS, LANES)),
                  _const_spec((TM, TM))],
        out_specs=[pl.BlockSpec((TOP_K, TM), lambda i: (0, i)),
                   pl.BlockSpec((8, TM), lambda i: (0, i)),
                   pl.BlockSpec((N_EXPERTS, LANES), lambda i: (0, 0))],
        out_shape=[jax.ShapeDtypeStruct((TOP_K, n_tok), jnp.int32),
                   jax.ShapeDtypeStruct((8, n_tok), F32),
                   jax.ShapeDtypeStruct((N_EXPERTS, LANES), F32)],
        scratch_shapes=[pltpu.VMEM((N_EXPERTS, LANES), F32)],
        compiler_params=_params(("arbitrary",)),
        name="route",
    )(lgt, bcol, tri)


def _slot_row(pk_ref, row0_ref, k, n, n_tok):
    p = pk_ref[k * n_tok + n]
    return row0_ref[p >> RANK_BITS] + (p & RANK_MASK)


def _dispatch_kernel(pk_ref, row0_ref, nblk_ref, h_hbm, xs_hbm, zbuf, sem, zsem, *, n_tok, n_blocks):
    i = pl.program_id(0)
    n_tiles = pl.num_programs(0)

    def zero_copy(row):
        return pltpu.make_async_copy(zbuf, xs_hbm.at[pl.ds(pl.multiple_of(row, MOE_ROWS), MOE_ROWS)], zsem)

    @pl.when(i == 0)
    def _():
        zbuf[...] = jnp.zeros_like(zbuf)
        used = (row0_ref[N_EXPERTS - 1] // MOE_ROWS) + nblk_ref[N_EXPERTS - 1]

        def last_row(e):
            return row0_ref[e] + (nblk_ref[e] - 1) * MOE_ROWS

        def start_e(e, c):
            @pl.when(nblk_ref[e] > 0)
            def _():
                zero_copy(last_row(e)).start()
            return c

        def wait_e(e, c):
            @pl.when(nblk_ref[e] > 0)
            def _():
                zero_copy(last_row(e)).wait()
            return c

        lax.fori_loop(0, N_EXPERTS, start_e, 0)
        lax.fori_loop(used, n_blocks, lambda bk, c: (zero_copy(bk * MOE_ROWS).start(), c)[1], 0)
        lax.fori_loop(0, N_EXPERTS, wait_e, 0)
        lax.fori_loop(used, n_blocks, lambda bk, c: (zero_copy(bk * MOE_ROWS).wait(), c)[1], 0)

    def one(j, c):
        n = i * TM + j
        for k in range(TOP_K):
            pltpu.make_async_copy(h_hbm.at[pl.ds(n, 1)],
                                  xs_hbm.at[pl.ds(_slot_row(pk_ref, row0_ref, k, n, n_tok), 1)],
                                  sem.at[i % 2]).start()
        return c

    lax.fori_loop(0, TM, one, 0, unroll=4)

    def wait_tile(slot):
        pltpu.make_async_copy(h_hbm.at[pl.ds(0, TOP_K * TM)], xs_hbm.at[pl.ds(0, TOP_K * TM)],
                              sem.at[slot]).wait()

    @pl.when(i > 0)
    def _():
        wait_tile((i - 1) % 2)

    @pl.when(i == n_tiles - 1)
    def _():
        wait_tile(i % 2)


def _dispatch_call(pk, row0, nblk, h2):
    n_tok = h2.shape[0]
    n_blocks = TOP_K * n_tok // MOE_ROWS + N_EXPERTS
    grid_spec = pltpu.PrefetchScalarGridSpec(
        num_scalar_prefetch=3,
        grid=(n_tok // TM,),
        in_specs=[pl.BlockSpec(memory_space=pl.ANY)],
        out_specs=pl.BlockSpec(memory_space=pl.ANY),
        scratch_shapes=[pltpu.VMEM((MOE_ROWS, D_MODEL), F32),
                        pltpu.SemaphoreType.DMA((2,)),
                        pltpu.SemaphoreType.DMA(())])
    return pl.pallas_call(
        functools.partial(_dispatch_kernel, n_tok=n_tok, n_blocks=n_blocks),
        grid_spec=grid_spec,
        out_shape=jax.ShapeDtypeStruct((n_blocks * MOE_ROWS, D_MODEL), F32),
        compiler_params=_params(("arbitrary",)),
        name="dispatch",
    )(pk, row0, nblk, h2)


def _moe_kernel(row0_ref, nblk_ref, xs_hbm, w1_ref, w3_ref, w2_ref, ys_hbm,
                xbuf, ybuf, w1b, w3b, w2b, isem, osem, *, n_blocks):
    e = pl.program_id(0)
    nb = nblk_ref[e]
    r0 = row0_ref[e]

    def rows(r):
        return pl.ds(pl.multiple_of(r0 + r * MOE_ROWS, MOE_ROWS), MOE_ROWS)

    def in_copy(r, slot):
        return pltpu.make_async_copy(xs_hbm.at[rows(r)], xbuf.at[slot], isem.at[slot])

    def out_copy(r, slot):
        return pltpu.make_async_copy(ybuf.at[slot], ys_hbm.at[rows(r)], osem.at[slot])

    @pl.when(nb > 0)
    def _():
        w1b[...] = w1_ref[0, 0].astype(BF16)
        w3b[...] = w3_ref[0, 0].astype(BF16)
        w2b[...] = w2_ref[0, 0].astype(BF16)
        in_copy(0, 0).start()

        def block(r, carry):
            slot = r % 2

            @pl.when(r + 1 < nb)
            def _():
                in_copy(r + 1, 1 - slot).start()

            in_copy(r, slot).wait()

            @pl.when(r >= 2)
            def _():
                out_copy(r - 2, slot).wait()

            xb = xbuf[slot].astype(BF16)
            h1 = jnp.dot(xb, w1b[...], preferred_element_type=F32)
            h3 = jnp.dot(xb, w3b[...], preferred_element_type=F32)
            hid = (h1 * jax.nn.sigmoid(h1) * h3).astype(BF16)
            ybuf[slot] = jnp.dot(hid, w2b[...], preferred_element_type=F32)
            out_copy(r, slot).start()
            return carry

        lax.fori_loop(0, nb, block, 0)

        @pl.when(nb >= 2)
        def _():
            out_copy(nb - 2, nb % 2).wait()

        out_copy(nb - 1, (nb - 1) % 2).wait()

    @pl.when(e == N_EXPERTS - 1)
    def _():
        used = r0 // MOE_ROWS + nb
        ybuf[0] = jnp.zeros((MOE_ROWS, D_MODEL), F32)

        def tail_copy(bk):
            return pltpu.make_async_copy(
                ybuf.at[0], ys_hbm.at[pl.ds(pl.multiple_of(bk * MOE_ROWS, MOE_ROWS), MOE_ROWS)], osem.at[0])

        lax.fori_loop(used, n_blocks, lambda bk, c: (tail_copy(bk).start(), c)[1], 0)
        lax.fori_loop(used, n_blocks, lambda bk, c: (tail_copy(bk).wait(), c)[1], 0)


def _moe_call(row0, nblk, xs, w1, w3, w2, l):
    n_blocks = xs.shape[0] // MOE_ROWS
    wspec = pl.BlockSpec((1, 1, D_MODEL, D_EXPERT), lambda e, *_: (l, e, 0, 0))
    grid_spec = pltpu.PrefetchScalarGridSpec(
        num_scalar_prefetch=2,
        grid=(N_EXPERTS,),
        in_specs=[pl.BlockSpec(memory_space=pl.ANY), wspec, wspec,
                  pl.BlockSpec((1, 1, D_EXPERT, D_MODEL), lambda e, *_: (l, e, 0, 0))],
        out_specs=pl.BlockSpec(memory_space=pl.ANY),
        scratch_shapes=[pltpu.VMEM((2, MOE_ROWS, D_MODEL), F32),
                        pltpu.VMEM((2, MOE_ROWS, D_MODEL), F32),
                        pltpu.VMEM((D_MODEL, D_EXPERT), BF16),
                        pltpu.VMEM((D_MODEL, D_EXPERT), BF16),
                        pltpu.VMEM((D_EXPERT, D_MODEL), BF16),
                        pltpu.SemaphoreType.DMA((2,)),
                        pltpu.SemaphoreType.DMA((2,))])
    return pl.pallas_call(
        functools.partial(_moe_kernel, n_blocks=n_blocks),
        grid_spec=grid_spec,
        out_shape=jax.ShapeDtypeStruct(xs.shape, F32),
        compiler_params=_params(("arbitrary",)),
        name="moe",
    )(row0, nblk, xs, w1, w3, w2)


def _combine_kernel(pk_ref, row0_ref, x_ref, mod_ref, gw_ref, fg_ref, ys_hbm, o_ref, ybuf, sem,
                    *, n_tok, final):
    i = pl.program_id(0)
    n_tiles = pl.num_programs(0)
    slot = i % 2

    def issue(tile, sl):
        def one(j, c):
            n = tile * TM + j
            for k in range(TOP_K):
                pltpu.make_async_copy(ys_hbm.at[pl.ds(_slot_row(pk_ref, row0_ref, k, n, n_tok), 1)],
                                      ybuf.at[sl, k, pl.ds(j, 1)], sem.at[sl]).start()
            return c

        lax.fori_loop(0, TM, one, 0, unroll=4)

    @pl.when(i == 0)
    def _():
        issue(0, 0)

    @pl.when(i + 1 < n_tiles)
    def _():
        issue(i + 1, 1 - slot)

    for k in range(TOP_K):
        pltpu.make_async_copy(ys_hbm.at[pl.ds(0, TM)], ybuf.at[slot, k], sem.at[slot]).wait()

    md = mod_ref[0, 0]
    gw = gw_ref[...].T
    f = gw[:, 0:1] * ybuf[slot, 0] + gw[:, 1:2] * ybuf[slot, 1]
    xn = x_ref[0] + md[5:6] * f
    o_ref[0] = _rmsnorm(xn, fg_ref[...]) if final else xn


def _combine_call(pk, row0, xm, mod, gw, fg, ys, nq, final):
    n_tok = BATCH * nq * TM
    grid_spec = pltpu.PrefetchScalarGridSpec(
        num_scalar_prefetch=2,
        grid=(BATCH * nq,),
        in_specs=[pl.BlockSpec((1, TM, D_MODEL), lambda i, *_: (i // nq, i % nq, 0)),
                  pl.BlockSpec((1, 1, 8, D_MODEL), lambda i, *_: (i // nq, (i % nq) // NT_LAT, 0, 0)),
                  pl.BlockSpec((8, TM), lambda i, *_: (0, i)),
                  pl.BlockSpec((1, D_MODEL), lambda i, *_: (0, 0)),
                  pl.BlockSpec(memory_space=pl.ANY)],
        out_specs=pl.BlockSpec((1, TM, D_MODEL), lambda i, *_: (i // nq, i % nq, 0)),
        scratch_shapes=[pltpu.VMEM((2, TOP_K, TM, D_MODEL), F32),
                        pltpu.SemaphoreType.DMA((2,))])
    return pl.pallas_call(
        functools.partial(_combine_kernel, n_tok=n_tok, final=final),
        grid_spec=grid_spec,
        out_shape=jax.ShapeDtypeStruct((BATCH, nq * TM, D_MODEL), F32),
        compiler_params=_params(("arbitrary",)),
        name="combine",
    )(pk, row0, xm, mod, gw, fg, ys)


def _rope_tables():
    rows = SEQ // GRID_W
    row = jnp.broadcast_to(jnp.arange(rows)[:, None], (rows, GRID_W)).reshape(-1).astype(F32)
    col = jnp.broadcast_to(jnp.arange(GRID_W)[None, :], (rows, GRID_W)).reshape(-1).astype(F32)

    def cs(rot_dim):
        n_f = rot_dim // 4
        inv = ROPE_BASE ** (-jnp.arange(n_f, dtype=F32) / n_f)
        ang = jnp.concatenate([row[:, None] * inv, col[:, None] * inv], axis=-1)
        return jnp.cos(ang), jnp.sin(ang)

    def with_ctx(c, u, d):
        one = jnp.ones((CTX_LEN, LANES), F32)
        zero = jnp.zeros((CTX_LEN, LANES), F32)
        return (jnp.concatenate([c, one]), jnp.concatenate([u, zero]), jnp.concatenate([d, zero]))

    cos, sin = cs(HEAD_DIM)
    z = jnp.zeros_like(sin)
    t64 = with_ctx(jnp.tile(jnp.concatenate([cos, cos], -1), (1, 2)),
                   jnp.tile(jnp.concatenate([-sin, z], -1), (1, 2)),
                   jnp.tile(jnp.concatenate([z, sin], -1), (1, 2)))
    cos, sin = cs(MLA_ROPE)
    z = jnp.zeros_like(sin)
    one_n = jnp.ones((SEQ, MLA_NOPE), F32)
    zero_n = jnp.zeros((SEQ, MLA_NOPE), F32)
    one_p = jnp.ones((SEQ, LANES - MLA_NOPE - MLA_ROPE), F32)
    zero_p = jnp.zeros((SEQ, LANES - MLA_NOPE - MLA_ROPE), F32)
    tm = with_ctx(jnp.concatenate([one_n, cos, cos, one_p], -1),
                  jnp.concatenate([zero_n, -sin, z, zero_p], -1),
                  jnp.concatenate([zero_n, z, sin, zero_p], -1))
    return t64 + tm


def _ret_tables(decay):
    lg = -jnp.exp(decay.astype(F32))
    idx = jnp.arange(RET_CHUNK, dtype=F32)
    diff = idx[:, None] - idx[None, :]
    fwd = diff >= 0
    bwd = diff < 0
    dm_f = jnp.where(fwd, jnp.exp(lg[0][:, None, None] * jnp.where(fwd, diff, 0.0)), 0.0)
    dm_b = jnp.where(bwd, jnp.exp(lg[1][:, None, None] * jnp.where(bwd, -diff, 0.0)), 0.0)
    dmat = jnp.stack([dm_f, dm_b])
    xi = jnp.stack([jnp.exp(lg[0][:, None] * (idx + 1.0)),
                    jnp.exp(lg[1][:, None] * (RET_CHUNK - idx))])
    zeta = jnp.stack([jnp.exp(lg[0][:, None] * (RET_CHUNK - 1.0 - idx)),
                      jnp.exp(lg[1][:, None] * idx)])
    gch = jnp.exp(lg * RET_CHUNK)
    xi_t = jnp.repeat(jnp.transpose(xi, (0, 2, 1)), HEAD_DIM, axis=2)
    zt_t = jnp.repeat(zeta, HEAD_DIM, axis=1)
    gc_t = jnp.broadcast_to(jnp.repeat(gch, HEAD_DIM, axis=1)[:, :, None], (2, MIX_W, MIX_W))
    return dmat, xi_t, zt_t, gc_t


def _in_proj_columns():
    o_mla = 4 * MIX_W
    o_gqa = o_mla + Q_LORA + KV_LORA + MLA_ROPE
    kv_w = (N_HEADS // 2) * HEAD_DIM
    o_win = o_gqa + MIX_W + 2 * kv_w
    ar = jnp.arange
    dup = jnp.concatenate([ar(HEAD_DIM), ar(HEAD_DIM), HEAD_DIM + ar(HEAD_DIM), HEAD_DIM + ar(HEAD_DIM)])

    def gqa_cols(o):
        return [o + ar(MIX_W), o + MIX_W + dup, o + MIX_W + kv_w + dup]

    return jnp.concatenate([ar(o_mla), o_mla + ar(Q_LORA), o_mla + Q_LORA + ar(KV_LORA)]
                           + gqa_cols(o_gqa) + gqa_cols(o_win)
                           + [o_mla + Q_LORA + KV_LORA + ar(MLA_ROPE)])


def _layer_weights(l, w_in, mla_w_uq, mla_w_ukv):
    cols = _in_proj_columns()
    win_p = jnp.pad(w_in[l][:, cols], ((0, 0), (0, N_IN_P - cols.shape[0]))).astype(BF16)
    uq = mla_w_uq[l].reshape(Q_LORA, N_HEADS, MLA_NOPE + MLA_ROPE)
    wuq = jnp.pad(uq, ((0, 0), (0, 0), (0, LANES - MLA_NOPE - MLA_ROPE))).reshape(Q_LORA, MLA_QK)
    ukv = mla_w_ukv[l].reshape(KV_LORA, N_HEADS, MLA_NOPE + MLA_V)
    wuk = jnp.pad(ukv[:, :, :MLA_NOPE], ((0, 0), (0, 0), (0, LANES - MLA_NOPE))).reshape(KV_LORA, MLA_QK)
    wuv = ukv[:, :, MLA_NOPE:].reshape(KV_LORA, MIX_W)
    return win_p, wuq.astype(BF16), wuk.astype(BF16), wuv.astype(BF16)


def _krope_placement():
    r = jnp.arange(LANES)[:, None]
    c = jnp.arange(MLA_QK)[None, :]
    return ((r < MLA_ROPE) & (c % LANES == MLA_NOPE + r)).astype(BF16)


def _head_block_matrix():
    r = jnp.arange(MIX_W)
    return (r[:, None] // HEAD_DIM == r[None, :] // HEAD_DIM).astype(BF16)


def kernel(x, c, ctx, c_ctx, w_ada, b_ada, norm1_g, norm2_g, w_in, w_gate, w_branch, w_out, ret_decay,
           mla_qn_g, mla_w_uq, mla_kvn_g, mla_w_ukv, gqa_qn_g, gqa_kn_g, win_sink, w_router, b_router,
           w1, w3, w2, final_norm_g):
    cvec = jnp.concatenate([c, c_ctx[None, :], jnp.zeros((7, D_MODEL), F32)], axis=0)
    ada = _ada_call(cvec, w_ada, b_ada)
    tabs = _rope_tables()
    gmat = _head_block_matrix()
    ekr = _krope_placement()
    xx = jnp.concatenate([x, ctx], axis=1)
    out = None
    for l in range(DEPTH):
        last = l == DEPTH - 1
        nq = NT_LAT if last else NT
        m = ada[l].reshape(16, 6, D_MODEL)
        m_lat = m[:BATCH]
        m_ctx = jnp.broadcast_to(m[BATCH][None], (BATCH, 6, D_MODEL))
        mod = jnp.pad(jnp.stack([m_lat, m_ctx], axis=1), ((0, 0), (0, 0), (0, 2), (0, 0)))
        g1 = norm1_g[l][None, :]
        g2 = norm2_g[l][None, :]
        win_p, wuq, wuk, wuv = _layer_weights(l, w_in, mla_w_uq, mla_w_ukv)
        (rq, rkt, rv, rg, mq, mkt, mv, gq, gkt, gv, wq, wkt, wv) = _prep_call(
            xx, mod, g1, win_p, wuq, wuk, wuv, ekr,
            mla_qn_g[l][None, :], mla_kvn_g[l][None, :],
            jnp.tile(gqa_qn_g[l], N_HEADS)[None, :], jnp.tile(gqa_kn_g[l], N_HEADS)[None, :],
            gmat, tabs)
        of, ob = _ret_call(rq, rkt, rv, *_ret_tables(ret_decay[l]))
        ym = _dense_call(mq, mkt, mv, nq, False, "mla")
        yg = _dense_call(gq, gkt, gv, nq, True, "gqa")
        yw = _win_call(win_sink[l], wq, wkt, wv, nq)
        xm, h2, lgt = _merge_call(
            xx, mod, g1, g2, of, ob, rg, ym, yg, yw,
            w_gate[l].astype(BF16), w_branch[l].astype(BF16), w_out[l].astype(BF16), w_router.T, gmat, nq)
        n_tok = BATCH * nq * TM
        pk, gw, cnt = _route_call(lgt, b_router, nq)
        pk = pk.reshape(TOP_K * n_tok)
        nblk = (cnt[:, 0].astype(jnp.int32) + MOE_ROWS - 1) // MOE_ROWS
        row0 = (jnp.cumsum(nblk) - nblk) * MOE_ROWS
        xs = _dispatch_call(pk, row0, nblk, h2.reshape(n_tok, D_MODEL))
        ys = _moe_call(row0, nblk, xs, w1, w3, w2, l)
        res = _combine_call(pk, row0, xm, mod, gw, final_norm_g[None, :], ys, nq, last)
        if last:
            out = res
        else:
            xx = res
    return out
```

```python
import functools

import jax
import jax.numpy as jnp
from jax import lax
from jax.experimental import pallas as pl
from jax.experimental.pallas import tpu as pltpu

F32 = jnp.float32
BF16 = jnp.bfloat16

D_MODEL = 1024
BATCH = 8
SEQ = 2048
DEPTH = 2
CTX_LEN = 256
TOK = SEQ + CTX_LEN
GRID_W = 64
N_HEADS = 4
HEAD_DIM = 64
MIX_W = N_HEADS * HEAD_DIM
RET_CHUNK = 128
Q_LORA = 256
KV_LORA = 128
MLA_NOPE = 64
MLA_ROPE = 32
MLA_V = 64
WINDOW = 128
N_EXPERTS = 32
N_GROUPS = 4
EXPERTS_PER_GROUP = N_EXPERTS // N_GROUPS
TOP_K = 2
D_EXPERT = 1024
ROPE_BASE = 10000.0
EPS = 1e-6
NEG_INF = -1e30

LANES = 128
TM = 256
NT = TOK // TM
NT_LAT = SEQ // TM
WIN_KEYS = TM + 2 * WINDOW
N_CHUNK = TOK // RET_CHUNK
MOE_ROWS = 128
N_IN_P = 3072
MLA_QK = 4 * LANES

C_RET = 0
C_MQ = 1024
C_MKV = 1280
C_GQ = 1408
C_GK = 1664
C_GV = 1920
C_WQ = 2176
C_WK = 2432
C_WV = 2688
C_KR = 2944

VMEM_LIMIT = 56 * 1024 * 1024


def _params(sem, vmem=VMEM_LIMIT):
    return pltpu.CompilerParams(dimension_semantics=sem, vmem_limit_bytes=vmem)


def _const_spec(shape):
    nd = len(shape)
    return pl.BlockSpec(shape, lambda *_: (0,) * nd, pipeline_mode=pl.Buffered(1))


def _bdot(a, b):
    return jnp.dot(a.astype(BF16), b.astype(BF16), preferred_element_type=F32)


def _split(a):
    hi = a.astype(BF16)
    lo = (a - hi.astype(F32)).astype(BF16)
    return hi, lo


def _dot_split_lhs(a, b):
    hi, lo = _split(a)
    return (jnp.dot(hi, b, preferred_element_type=F32)
            + jnp.dot(lo, b, preferred_element_type=F32))


def _dot3(a, b):
    ah, al = _split(a)
    bh, bl = _split(b)
    return (jnp.dot(ah, bh, preferred_element_type=F32)
            + jnp.dot(ah, bl, preferred_element_type=F32)
            + jnp.dot(al, bh, preferred_element_type=F32))


def _modnorm(x, g, sc, sh):
    ms = jnp.mean(x * x, axis=-1, keepdims=True)
    return x * lax.rsqrt(ms + EPS) * g * (1.0 + sc) + sh


def _rmsnorm(x, g):
    ms = jnp.mean(x * x, axis=-1, keepdims=True)
    return x * lax.rsqrt(ms + EPS) * g


def _rope(x, c, s_up, s_dn, half):
    outs = []
    for j in range(x.shape[1] // LANES):
        xc = x[:, j * LANES:(j + 1) * LANES]
        outs.append(xc * c + pltpu.roll(xc, LANES - half, 1) * s_up + pltpu.roll(xc, half, 1) * s_dn)
    return outs[0] if len(outs) == 1 else jnp.concatenate(outs, axis=1)


def _head_mean(x, gmat):
    return _dot_split_lhs(x, gmat) * (1.0 / HEAD_DIM)


def _ada_kernel(c_ref, w_ref, b_ref, o_ref):
    c = c_ref[...]
    sc = c * jax.nn.sigmoid(c)
    o_ref[0] = _dot3(sc, w_ref[0]) + b_ref[0]


def _ada_call(cvec, w_ada, b_ada):
    tn = 1536
    return pl.pallas_call(
        _ada_kernel,
        grid=(DEPTH, 6 * D_MODEL // tn),
        in_specs=[
            pl.BlockSpec((16, D_MODEL), lambda l, j: (0, 0)),
            pl.BlockSpec((1, D_MODEL, tn), lambda l, j: (l, 0, j)),
            pl.BlockSpec((1, 1, tn), lambda l, j: (l, 0, j)),
        ],
        out_specs=pl.BlockSpec((1, 16, tn), lambda l, j: (l, 0, j)),
        out_shape=jax.ShapeDtypeStruct((DEPTH, 16, 6 * D_MODEL), F32),
        compiler_params=_params(("arbitrary", "arbitrary")),
        name="ada",
    )(cvec, w_ada, b_ada.reshape(DEPTH, 1, 6 * D_MODEL))


def _prep_kernel(x_ref, mod_ref, g1_ref, win_ref, wuq_ref, wuk_ref, wuv_ref, ekr_ref,
                 qng_ref, kvng_ref, gqg_ref, gkg_ref, gmat_ref,
                 c64_ref, u64_ref, d64_ref, cm_ref, um_ref, dm_ref,
                 rq_ref, rkt_ref, rv_ref, rg_ref, mq_ref, mkt_ref, mv_ref,
                 gq_ref, gkt_ref, gv_ref, wq_ref, wkt_ref, wv_ref):
    x = x_ref[0]
    md = mod_ref[0, 0]
    h = _modnorm(x, g1_ref[...], md[1:2], md[0:1])
    p = jnp.dot(h.astype(BF16), win_ref[...], preferred_element_type=F32)

    c64, u64, d64 = c64_ref[...], u64_ref[...], d64_ref[...]
    cm, um, dm = cm_ref[...], um_ref[...], dm_ref[...]
    gmat = gmat_ref[...]
    qk_scale = HEAD_DIM ** -0.5
    rope64 = lambda a: _rope(a, c64, u64, d64, HEAD_DIM // 2)
    ropem = lambda a: _rope(a, cm, um, dm, MLA_ROPE // 2)

    rq_ref[0] = (rope64(p[:, C_RET:C_RET + 256]) * qk_scale).astype(BF16)
    rkt_ref[0] = rope64(p[:, C_RET + 256:C_RET + 512]).T.astype(BF16)
    rv_ref[0] = p[:, C_RET + 512:C_RET + 768].astype(BF16)
    rg_ref[0] = p[:, C_RET + 768:C_RET + 1024].astype(BF16)

    qn = _rmsnorm(p[:, C_MQ:C_MQ + Q_LORA], qng_ref[...])
    q2 = ropem(_bdot(qn, wuq_ref[...])) * ((MLA_NOPE + MLA_ROPE) ** -0.5)
    mq_ref[0] = q2.astype(BF16)
    kvn = _rmsnorm(p[:, C_MKV:C_MKV + KV_LORA], kvng_ref[...]).astype(BF16)
    k2 = (jnp.dot(kvn, wuk_ref[...], preferred_element_type=F32)
          + _dot_split_lhs(p[:, C_KR:C_KR + LANES], ekr_ref[...]))
    mkt_ref[0] = ropem(k2).T.astype(BF16)
    mv_ref[0] = jnp.dot(kvn, wuv_ref[...], preferred_element_type=F32).astype(BF16)

    gq = p[:, C_GQ:C_GQ + 256]
    gq = gq * lax.rsqrt(_head_mean(gq * gq, gmat) + EPS) * gqg_ref[...]
    gq_ref[0] = (rope64(gq) * qk_scale).astype(BF16)
    gk = p[:, C_GK:C_GK + 256]
    gk = gk * lax.rsqrt(_head_mean(gk * gk, gmat) + EPS) * gkg_ref[...]
    gkt_ref[0] = rope64(gk).T.astype(BF16)
    gv_ref[0] = p[:, C_GV:C_GV + 256].astype(BF16)

    wq_ref[0] = (rope64(p[:, C_WQ:C_WQ + 256]) * qk_scale).astype(BF16)
    wkt_ref[0] = rope64(p[:, C_WK:C_WK + 256]).T.astype(BF16)
    wv_ref[0] = p[:, C_WV:C_WV + 256].astype(BF16)


def _prep_call(xx, mod, g1, win_p, wuq, wuk, wuv, ekr, qng, kvng, gqg, gkg, gmat, tabs):
    tok = lambda w: pl.BlockSpec((1, TM, w), lambda b, t: (b, t, 0))
    tokt = lambda w: pl.BlockSpec((1, w, TM), lambda b, t: (b, 0, t))
    tab = pl.BlockSpec((TM, LANES), lambda b, t: (t, 0))
    sd = lambda w: jax.ShapeDtypeStruct((BATCH, TOK, w), BF16)
    sdt = lambda w: jax.ShapeDtypeStruct((BATCH, w, TOK), BF16)
    in_specs = [
        tok(D_MODEL),
        pl.BlockSpec((1, 1, 8, D_MODEL), lambda b, t: (b, t // NT_LAT, 0, 0)),
        _const_spec((1, D_MODEL)),
        _const_spec((D_MODEL, N_IN_P)),
        _const_spec((Q_LORA, MLA_QK)),
        _const_spec((KV_LORA, MLA_QK)),
        _const_spec((KV_LORA, MIX_W)),
        _const_spec((LANES, MLA_QK)),
        _const_spec((1, Q_LORA)),
        _const_spec((1, KV_LORA)),
        _const_spec((1, MIX_W)),
        _const_spec((1, MIX_W)),
        _const_spec((MIX_W, MIX_W)),
        tab, tab, tab, tab, tab, tab,
    ]
    out_specs = [tok(256), tokt(256), tok(256), tok(256),
                 tok(MLA_QK), tokt(MLA_QK), tok(256),
                 tok(256), tokt(256), tok(256),
                 tok(256), tokt(256), tok(256)]
    out_shape = [sd(256), sdt(256), sd(256), sd(256),
                 sd(MLA_QK), sdt(MLA_QK), sd(256),
                 sd(256), sdt(256), sd(256),
                 sd(256), sdt(256), sd(256)]
    return pl.pallas_call(
        _prep_kernel,
        grid=(BATCH, NT),
        in_specs=in_specs,
        out_specs=out_specs,
        out_shape=out_shape,
        compiler_params=_params(("parallel", "parallel")),
        name="prep",
    )(xx, mod, g1, win_p, wuq, wuk, wuv, ekr, qng, kvng, gqg, gkg, gmat, *tabs)


def _ret_kernel(qf_ref, ktf_ref, vf_ref, qb_ref, ktb_ref, vb_ref,
                dmat_ref, xi_ref, zt_ref, gc_ref, of_ref, ob_ref, sf_ref, sb_ref):
    i = pl.program_id(0)

    @pl.when(i == 0)
    def _():
        sf_ref[...] = jnp.zeros_like(sf_ref)
        sb_ref[...] = jnp.zeros_like(sb_ref)

    lane_head = lax.broadcasted_iota(jnp.int32, (RET_CHUNK, MIX_W), 1) // HEAD_DIM
    r_head = lax.broadcasted_iota(jnp.int32, (MIX_W, MIX_W), 0) // HEAD_DIM
    c_head = lax.broadcasted_iota(jnp.int32, (MIX_W, MIX_W), 1) // HEAD_DIM
    block_diag = r_head == c_head

    def one(b, d, q_ref, kt_ref, v_ref, o_ref, s_ref):
        q = q_ref[b].astype(F32)
        kt = kt_ref[b]
        v = v_ref[b]
        s_old = s_ref[b]
        o = _bdot(q * xi_ref[d], s_old)
        for hd in range(N_HEADS):
            qm = jnp.where(lane_head == hd, q, 0.0).astype(BF16)
            inner = jnp.dot(qm, kt, preferred_element_type=F32) * dmat_ref[d, hd]
            oh = jnp.dot(inner.astype(BF16), v, preferred_element_type=F32)
            o = o + jnp.where(lane_head == hd, oh, 0.0)
        o_ref[b] = o
        kz = (kt.astype(F32) * zt_ref[d]).astype(BF16)
        upd = jnp.dot(kz, v, preferred_element_type=F32)
        s_ref[b] = gc_ref[d] * s_old + jnp.where(block_diag, upd, 0.0)

    def body(b, carry):
        one(b, 0, qf_ref, ktf_ref, vf_ref, of_ref, sf_ref)
        one(b, 1, qb_ref, ktb_ref, vb_ref, ob_ref, sb_ref)
        return carry

    lax.fori_loop(0, BATCH, body, 0)


def _ret_call(rq, rkt, rv, dmat, xi, zt, gc):
    cf = lambda i: (i + SEQ // RET_CHUNK) % N_CHUNK
    cb = lambda i: N_CHUNK - 1 - i
    rows = lambda f: pl.BlockSpec((BATCH, RET_CHUNK, MIX_W), lambda i: (0, f(i), 0))
    cols = lambda f: pl.BlockSpec((BATCH, MIX_W, RET_CHUNK), lambda i: (0, 0, f(i)))
    return pl.pallas_call(
        _ret_kernel,
        grid=(N_CHUNK,),
        in_specs=[rows(cf), cols(cf), rows(cf), rows(cb), cols(cb), rows(cb),
                  _const_spec((2, N_HEADS, RET_CHUNK, RET_CHUNK)),
                  _const_spec((2, RET_CHUNK, MIX_W)),
                  _const_spec((2, MIX_W, RET_CHUNK)),
                  _const_spec((2, MIX_W, MIX_W))],
        out_specs=[rows(cf), rows(cb)],
        out_shape=[jax.ShapeDtypeStruct((BATCH, TOK, MIX_W), F32)] * 2,
        scratch_shapes=[pltpu.VMEM((BATCH, MIX_W, MIX_W), F32),
                        pltpu.VMEM((BATCH, MIX_W, MIX_W), F32)],
        compiler_params=_params(("arbitrary",)),
        name="retention",
    )(rq, rkt, rv, rq, rkt, rv, dmat, xi, zt, gc)


def _head_q(q_ref, hd, pair):
    if not pair:
        return q_ref[0, :, hd * LANES:(hd + 1) * LANES], hd * LANES
    c = hd // 2
    qc = q_ref[0, :, c * LANES:(c + 1) * LANES]
    half = lax.broadcasted_iota(jnp.int32, qc.shape, 1) // HEAD_DIM
    return jnp.where(half == hd % 2, qc, jnp.zeros_like(qc)), c * LANES


def _dense_kernel(q_ref, kt_ref, v_ref, o_ref, *, pair):
    t = pl.program_id(1)
    lane_head = lax.broadcasted_iota(jnp.int32, (TM, MIX_W), 1) // HEAD_DIM

    def run(k_lo, k_hi):
        v = v_ref[0, k_lo:k_hi, :]
        acc = jnp.zeros((TM, MIX_W), F32)
        for hd in range(N_HEADS):
            qm, r0 = _head_q(q_ref, hd, pair)
            s = jnp.dot(qm, kt_ref[0, r0:r0 + LANES, k_lo:k_hi], preferred_element_type=F32)
            m = jnp.max(s, axis=-1, keepdims=True)
            p = jnp.exp(s - m)
            l = jnp.sum(p, axis=-1, keepdims=True)
            oh = jnp.dot(p.astype(BF16), v, preferred_element_type=F32)
            acc = jnp.where(lane_head == hd, oh * (1.0 / l), acc)
        o_ref[0] = acc.astype(BF16)

    @pl.when(t < NT_LAT)
    def _():
        run(0, TOK)

    @pl.when(t >= NT_LAT)
    def _():
        run(SEQ, TOK)


def _dense_call(q, kt, v, nq, pair, name):
    wq = q.shape[-1]
    return pl.pallas_call(
        functools.partial(_dense_kernel, pair=pair),
        grid=(BATCH, nq),
        in_specs=[pl.BlockSpec((1, TM, wq), lambda b, t: (b, t, 0)),
                  pl.BlockSpec((1, kt.shape[1], TOK), lambda b, t: (b, 0, 0)),
                  pl.BlockSpec((1, TOK, MIX_W), lambda b, t: (b, 0, 0))],
        out_specs=pl.BlockSpec((1, TM, MIX_W), lambda b, t: (b, t, 0)),
        out_shape=jax.ShapeDtypeStruct((BATCH, nq * TM, MIX_W), BF16),
        compiler_params=_params(("parallel", "arbitrary")),
        name=name,
    )(q, kt, v)


def _win_kernel(sink_ref, q_ref, kt_ref, v_ref, o_ref):
    t = pl.program_id(1)
    lane_head = lax.broadcasted_iota(jnp.int32, (TM, MIX_W), 1) // HEAD_DIM
    v_ctx = v_ref[0, SEQ:TOK, :]

    @pl.when(t < NT_LAT)
    def _():
        start = pl.multiple_of(jnp.clip(t * TM - WINDOW, 0, SEQ - WIN_KEYS), LANES)
        qpos = t * TM + lax.broadcasted_iota(jnp.int32, (TM, WIN_KEYS), 0)
        kpos = start + lax.broadcasted_iota(jnp.int32, (TM, WIN_KEYS), 1)
        valid = jnp.abs(kpos - qpos) <= WINDOW
        v_loc = v_ref[0, pl.ds(start, WIN_KEYS), :]
        acc = jnp.zeros((TM, MIX_W), F32)
        for hd in range(N_HEADS):
            qm, r0 = _head_q(q_ref, hd, True)
            s_loc = jnp.dot(qm, kt_ref[0, r0:r0 + LANES, pl.ds(start, WIN_KEYS)],
                            preferred_element_type=F32)
            s_loc = jnp.where(valid, s_loc, NEG_INF)
            s_ctx = jnp.dot(qm, kt_ref[0, r0:r0 + LANES, SEQ:TOK], preferred_element_type=F32)
            sk = sink_ref[hd]
            m = jnp.maximum(jnp.maximum(jnp.max(s_loc, axis=-1, keepdims=True),
                                        jnp.max(s_ctx, axis=-1, keepdims=True)), sk)
            p_loc = jnp.exp(s_loc - m)
            p_ctx = jnp.exp(s_ctx - m)
            l = (jnp.sum(p_loc, axis=-1, keepdims=True) + jnp.sum(p_ctx, axis=-1, keepdims=True)
                 + jnp.exp(sk - m))
            oh = (jnp.dot(p_loc.astype(BF16), v_loc, preferred_element_type=F32)
                  + jnp.dot(p_ctx.astype(BF16), v_ctx, preferred_element_type=F32))
            acc = jnp.where(lane_head == hd, oh * (1.0 / l), acc)
        o_ref[0] = acc.astype(BF16)

    @pl.when(t >= NT_LAT)
    def _():
        acc = jnp.zeros((TM, MIX_W), F32)
        for hd in range(N_HEADS):
            qm, r0 = _head_q(q_ref, hd, True)
            s = jnp.dot(qm, kt_ref[0, r0:r0 + LANES, SEQ:TOK], preferred_element_type=F32)
            sk = sink_ref[hd]
            m = jnp.maximum(jnp.max(s, axis=-1, keepdims=True), sk)
            p = jnp.exp(s - m)
            l = jnp.sum(p, axis=-1, keepdims=True) + jnp.exp(sk - m)
            oh = jnp.dot(p.astype(BF16), v_ctx, preferred_element_type=F32)
            acc = jnp.where(lane_head == hd, oh * (1.0 / l), acc)
        o_ref[0] = acc.astype(BF16)


def _win_call(sink, q, kt, v, nq):
    return pl.pallas_call(
        _win_kernel,
        grid=(BATCH, nq),
        in_specs=[pl.BlockSpec(memory_space=pltpu.SMEM),
                  pl.BlockSpec((1, TM, MIX_W), lambda b, t: (b, t, 0)),
                  pl.BlockSpec((1, MIX_W, TOK), lambda b, t: (b, 0, 0)),
                  pl.BlockSpec((1, TOK, MIX_W), lambda b, t: (b, 0, 0))],
        out_specs=pl.BlockSpec((1, TM, MIX_W), lambda b, t: (b, t, 0)),
        out_shape=jax.ShapeDtypeStruct((BATCH, nq * TM, MIX_W), BF16),
        compiler_params=_params(("parallel", "arbitrary")),
        name="window",
    )(sink, q, kt, v)


def _merge_kernel(x_ref, mod_ref, g1_ref, g2_ref, of_ref, ob_ref, rg_ref, ym_ref, yg_ref, yw_ref,
                  wg_ref, wb_ref, wo_ref, wr_ref, gmat_ref, xo_ref, h2_ref, lg_ref):
    x = x_ref[0]
    md = mod_ref[0, 0]
    hb = _modnorm(x, g1_ref[...], md[1:2], md[0:1]).astype(BF16)

    gmat = gmat_ref[...]
    o = of_ref[0] + ob_ref[0]
    dlt = o - _head_mean(o, gmat)
    var = _head_mean(dlt * dlt, gmat)
    g = rg_ref[0].astype(F32)
    y_ret = dlt * lax.rsqrt(var + EPS) * (g * jax.nn.sigmoid(g))

    ys = (y_ret.astype(BF16), ym_ref[0], yg_ref[0], yw_ref[0])
    acc = jnp.zeros((TM, D_MODEL), F32)
    for i in range(4):
        gate = jax.nn.sigmoid(jnp.dot(hb, wg_ref[:, i * D_MODEL:(i + 1) * D_MODEL],
                                      preferred_element_type=F32))
        acc = acc + gate * jnp.dot(ys[i], wb_ref[i], preferred_element_type=F32)
    out = jnp.dot(acc.astype(BF16), wo_ref[...], preferred_element_type=F32)
    xm = x + md[2:3] * out
    xo_ref[0] = xm
    h2 = _modnorm(xm, g2_ref[...], md[4:5], md[3:4])
    h2_ref[0] = h2
    hh, hl = _split(h2)
    wh, wl = _split(wr_ref[...])
    nt = lambda a, b: lax.dot_general(a, b, (((1,), (1,)), ((), ())), preferred_element_type=F32)
    lg_ref[0] = nt(wh, hh) + nt(wh, hl) + nt(wl, hh)


def _merge_call(xx, mod, g1, g2, of, ob, rg, ym, yg, yw, wg, wb, wo, wr, gmat, nq):
    tok = lambda w: pl.BlockSpec((1, TM, w), lambda b, t: (b, t, 0))
    return pl.pallas_call(
        _merge_kernel,
        grid=(BATCH, nq),
        in_specs=[tok(D_MODEL),
                  pl.BlockSpec((1, 1, 8, D_MODEL), lambda b, t: (b, t // NT_LAT, 0, 0)),
                  _const_spec((1, D_MODEL)), _const_spec((1, D_MODEL)),
                  tok(MIX_W), tok(MIX_W), tok(MIX_W), tok(MIX_W), tok(MIX_W), tok(MIX_W),
                  _const_spec((D_MODEL, 4 * D_MODEL)),
                  _const_spec((4, MIX_W, D_MODEL)),
                  _const_spec((D_MODEL, D_MODEL)),
                  _const_spec((N_EXPERTS, D_MODEL)),
                  _const_spec((MIX_W, MIX_W))],
        out_specs=[tok(D_MODEL), tok(D_MODEL),
                   pl.BlockSpec((1, N_EXPERTS, TM), lambda b, t: (b, 0, t))],
        out_shape=[jax.ShapeDtypeStruct((BATCH, nq * TM, D_MODEL), F32),
                   jax.ShapeDtypeStruct((BATCH, nq * TM, D_MODEL), F32),
                   jax.ShapeDtypeStruct((BATCH, N_EXPERTS, nq * TM), F32)],
        compiler_params=_params(("parallel", "parallel")),
        name="merge",
    )(xx, mod, g1, g2, of, ob, rg, ym, yg, yw, wg, wb, wo, wr, gmat)


RANK_BITS = 20
RANK_MASK = (1 << RANK_BITS) - 1


def _route_kernel(lg_ref, b_ref, tri_ref, pk_ref, gw_ref, cnt_ref, base_ref):
    i = pl.program_id(0)

    @pl.when(i == 0)
    def _():
        base_ref[...] = jnp.zeros_like(base_ref)

    s = jax.nn.sigmoid(lg_ref[0])
    sel = s + b_ref[:, 0:1]
    sub = lax.broadcasted_iota(jnp.int32, (EXPERTS_PER_GROUP, TM), 0)
    best = e1 = e2 = s1 = s2 = None
    for g in range(N_GROUPS):
        rows = slice(g * EXPERTS_PER_GROUP, (g + 1) * EXPERTS_PER_GROUP)
        blk, sb = sel[rows], s[rows]
        m1 = jnp.max(blk, axis=0, keepdims=True)
        i1 = jnp.min(jnp.where(blk == m1, sub, EXPERTS_PER_GROUP), axis=0, keepdims=True)
        hit1 = sub == i1
        blk2 = jnp.where(hit1, -jnp.inf, blk)
        m2 = jnp.max(blk2, axis=0, keepdims=True)
        i2 = jnp.min(jnp.where(blk2 == m2, sub, EXPERTS_PER_GROUP), axis=0, keepdims=True)
        hit2 = sub == i2
        score = m1 + m2
        s1g = jnp.sum(jnp.where(hit1, sb, 0.0), axis=0, keepdims=True)
        s2g = jnp.sum(jnp.where(hit2, sb, 0.0), axis=0, keepdims=True)
        e1g = g * EXPERTS_PER_GROUP + i1
        e2g = g * EXPERTS_PER_GROUP + i2
        if g == 0:
            best, e1, e2, s1, s2 = score, e1g, e2g, s1g, s2g
        else:
            better = score > best
            best = jnp.where(better, score, best)
            e1 = jnp.where(better, e1g, e1)
            e2 = jnp.where(better, e2g, e2)
            s1 = jnp.where(better, s1g, s1)
            s2 = jnp.where(better, s2g, s2)

    eid = lax.broadcasted_iota(jnp.int32, (N_EXPERTS, TM), 0)
    oh1 = eid == e1
    oh2 = eid == e2
    oh = jnp.where(oh1 | oh2, 1.0, 0.0)
    before = jnp.dot(oh.astype(BF16), tri_ref[...], preferred_element_type=F32) + base_ref[:, 0:1]
    r1 = jnp.sum(jnp.where(oh1, before, 0.0), axis=0, keepdims=True).astype(jnp.int32)
    r2 = jnp.sum(jnp.where(oh2, before, 0.0), axis=0, keepdims=True).astype(jnp.int32)
    total = base_ref[...] + jnp.sum(oh, axis=1, keepdims=True)
    base_ref[...] = total
    cnt_ref[...] = total

    pk_ref[...] = jnp.concatenate([(e1 << RANK_BITS) + r1, (e2 << RANK_BITS) + r2], axis=0)
    den = s1 + s2
    row = lax.broadcasted_iota(jnp.int32, (8, TM), 0)
    gw_ref[...] = jnp.where(row == 0, s1 / den, jnp.where(row == 1, s2 / den, 0.0))


def _route_call(lgt, b_router, nq):
    n_tiles = BATCH * nq
    n_tok = n_tiles * TM
    r = jnp.arange(TM)
    tri = (r[:, None] < r[None, :]).astype(BF16)
    bcol = jnp.broadcast_to(b_router.astype(F32)[:, None], (N_EXPERTS, LANES))
    return pl.pallas_call(
        _route_kernel,
        grid=(n_tiles,),
        in_specs=[pl.BlockSpec((1, N_EXPERTS, TM), lambda i: (i // nq, 0, i % nq)),
                  _const_spec((N_EXPERTS, LANES)),
                  _const_spec((TM, TM))],
        out_specs=[pl.BlockSpec((TOP_K, TM), lambda i: (0, i)),
                   pl.BlockSpec((8, TM), lambda i: (0, i)),
                   pl.BlockSpec((N_EXPERTS, LANES), lambda i: (0, 0))],
        out_shape=[jax.ShapeDtypeStruct((TOP_K, n_tok), jnp.int32),
                   jax.ShapeDtypeStruct((8, n_tok), F32),
                   jax.ShapeDtypeStruct((N_EXPERTS, LANES), F32)],
        scratch_shapes=[pltpu.VMEM((N_EXPERTS, LANES), F32)],
        compiler_params=_params(("arbitrary",)),
        name="route",
    )(lgt, bcol, tri)


def _slot_row(pk_ref, row0_ref, k, n, n_tok):
    p = pk_ref[k * n_tok + n]
    return row0_ref[p >> RANK_BITS] + (p & RANK_MASK)


def _dispatch_kernel(pk_ref, row0_ref, nblk_ref, h_ref, xs_hbm, zbuf, hbuf, sem, zsem, *, n_tok, n_blocks):
    i = pl.program_id(0)
    n_tiles = pl.num_programs(0)

    def zero_copy(row):
        return pltpu.make_async_copy(zbuf, xs_hbm.at[pl.ds(pl.multiple_of(row, MOE_ROWS), MOE_ROWS)], zsem)

    @pl.when(i == 0)
    def _():
        zbuf[...] = jnp.zeros_like(zbuf)
        used = (row0_ref[N_EXPERTS - 1] // MOE_ROWS) + nblk_ref[N_EXPERTS - 1]

        def last_row(e):
            return row0_ref[e] + (nblk_ref[e] - 1) * MOE_ROWS

        def start_e(e, c):
            @pl.when(nblk_ref[e] > 0)
            def _():
                zero_copy(last_row(e)).start()
            return c

        def wait_e(e, c):
            @pl.when(nblk_ref[e] > 0)
            def _():
                zero_copy(last_row(e)).wait()
            return c

        lax.fori_loop(0, N_EXPERTS, start_e, 0)
        lax.fori_loop(used, n_blocks, lambda bk, c: (zero_copy(bk * MOE_ROWS).start(), c)[1], 0)
        lax.fori_loop(0, N_EXPERTS, wait_e, 0)
        lax.fori_loop(used, n_blocks, lambda bk, c: (zero_copy(bk * MOE_ROWS).wait(), c)[1], 0)

    slot = i % 2

    def wait_tile(sl):
        for _ in range(TOP_K):
            pltpu.make_async_copy(hbuf.at[sl], xs_hbm.at[pl.ds(0, TM)], sem.at[sl]).wait()

    @pl.when(i >= 2)
    def _():
        wait_tile(slot)

    hbuf[slot] = h_ref[...]

    def one(j, c):
        n = i * TM + j
        for k in range(TOP_K):
            pltpu.make_async_copy(hbuf.at[slot, pl.ds(j, 1)],
                                  xs_hbm.at[pl.ds(_slot_row(pk_ref, row0_ref, k, n, n_tok), 1)],
                                  sem.at[slot]).start()
        return c

    lax.fori_loop(0, TM, one, 0, unroll=4)

    @pl.when(i == n_tiles - 1)
    def _():
        @pl.when(n_tiles >= 2)
        def _():
            wait_tile(1 - slot)

        wait_tile(slot)


def _dispatch_call(pk, row0, nblk, h2):
    n_tok = h2.shape[0]
    n_blocks = TOP_K * n_tok // MOE_ROWS + N_EXPERTS
    grid_spec = pltpu.PrefetchScalarGridSpec(
        num_scalar_prefetch=3,
        grid=(n_tok // TM,),
        in_specs=[pl.BlockSpec((TM, D_MODEL), lambda i, *_: (i, 0))],
        out_specs=pl.BlockSpec(memory_space=pl.ANY),
        scratch_shapes=[pltpu.VMEM((MOE_ROWS, D_MODEL), F32),
                        pltpu.VMEM((2, TM, D_MODEL), F32),
                        pltpu.SemaphoreType.DMA((2,)),
                        pltpu.SemaphoreType.DMA(())])
    return pl.pallas_call(
        functools.partial(_dispatch_kernel, n_tok=n_tok, n_blocks=n_blocks),
        grid_spec=grid_spec,
        out_shape=jax.ShapeDtypeStruct((n_blocks * MOE_ROWS, D_MODEL), F32),
        compiler_params=_params(("arbitrary",)),
        name="dispatch",
    )(pk, row0, nblk, h2)


def _moe_kernel(row0_ref, nblk_ref, xs_hbm, w1_ref, w3_ref, w2_ref, ys_hbm,
                xbuf, ybuf, w1b, w3b, w2b, isem, osem, *, n_blocks):
    e = pl.program_id(0)
    nb = nblk_ref[e]
    r0 = row0_ref[e]

    def rows(r):
        return pl.ds(pl.multiple_of(r0 + r * MOE_ROWS, MOE_ROWS), MOE_ROWS)

    def in_copy(r, slot):
        return pltpu.make_async_copy(xs_hbm.at[rows(r)], xbuf.at[slot], isem.at[slot])

    def out_copy(r, slot):
        return pltpu.make_async_copy(ybuf.at[slot], ys_hbm.at[rows(r)], osem.at[slot])

    @pl.when(nb > 0)
    def _():
        w1b[...] = w1_ref[0, 0].astype(BF16)
        w3b[...] = w3_ref[0, 0].astype(BF16)
        w2b[...] = w2_ref[0, 0].astype(BF16)
        in_copy(0, 0).start()

        def block(r, carry):
            slot = r % 2

            @pl.when(r + 1 < nb)
            def _():
                in_copy(r + 1, 1 - slot).start()

            in_copy(r, slot).wait()

            @pl.when(r >= 2)
            def _():
                out_copy(r - 2, slot).wait()

            xb = xbuf[slot].astype(BF16)
            h1 = jnp.dot(xb, w1b[...], preferred_element_type=F32)
            h3 = jnp.dot(xb, w3b[...], preferred_element_type=F32)
            hid = (h1 * jax.nn.sigmoid(h1) * h3).astype(BF16)
            ybuf[slot] = jnp.dot(hid, w2b[...], preferred_element_type=F32)
            out_copy(r, slot).start()
            return carry

        lax.fori_loop(0, nb, block, 0)

        @pl.when(nb >= 2)
        def _():
            out_copy(nb - 2, nb % 2).wait()

        out_copy(nb - 1, (nb - 1) % 2).wait()

    @pl.when(e == N_EXPERTS - 1)
    def _():
        used = r0 // MOE_ROWS + nb
        ybuf[0] = jnp.zeros((MOE_ROWS, D_MODEL), F32)

        def tail_copy(bk):
            return pltpu.make_async_copy(
                ybuf.at[0], ys_hbm.at[pl.ds(pl.multiple_of(bk * MOE_ROWS, MOE_ROWS), MOE_ROWS)], osem.at[0])

        lax.fori_loop(used, n_blocks, lambda bk, c: (tail_copy(bk).start(), c)[1], 0)
        lax.fori_loop(used, n_blocks, lambda bk, c: (tail_copy(bk).wait(), c)[1], 0)


def _moe_call(row0, nblk, xs, w1, w3, w2, l):
    n_blocks = xs.shape[0] // MOE_ROWS
    wspec = pl.BlockSpec((1, 1, D_MODEL, D_EXPERT), lambda e, *_: (l, e, 0, 0))
    grid_spec = pltpu.PrefetchScalarGridSpec(
        num_scalar_prefetch=2,
        grid=(N_EXPERTS,),
        in_specs=[pl.BlockSpec(memory_space=pl.ANY), wspec, wspec,
                  pl.BlockSpec((1, 1, D_EXPERT, D_MODEL), lambda e, *_: (l, e, 0, 0))],
        out_specs=pl.BlockSpec(memory_space=pl.ANY),
        scratch_shapes=[pltpu.VMEM((2, MOE_ROWS, D_MODEL), F32),
                        pltpu.VMEM((2, MOE_ROWS, D_MODEL), F32),
                        pltpu.VMEM((D_MODEL, D_EXPERT), BF16),
                        pltpu.VMEM((D_MODEL, D_EXPERT), BF16),
                        pltpu.VMEM((D_EXPERT, D_MODEL), BF16),
                        pltpu.SemaphoreType.DMA((2,)),
                        pltpu.SemaphoreType.DMA((2,))])
    return pl.pallas_call(
        functools.partial(_moe_kernel, n_blocks=n_blocks),
        grid_spec=grid_spec,
        out_shape=jax.ShapeDtypeStruct(xs.shape, F32),
        compiler_params=_params(("arbitrary",)),
        name="moe",
    )(row0, nblk, xs, w1, w3, w2)


def _combine_kernel(pk_ref, row0_ref, x_ref, mod_ref, gw_ref, fg_ref, ys_hbm, o_ref, ybuf, sem,
                    *, n_tok, final):
    i = pl.program_id(0)
    n_tiles = pl.num_programs(0)
    slot = i % 2

    def issue(tile, sl):
        def one(j, c):
            n = tile * TM + j
            for k in range(TOP_K):
                pltpu.make_async_copy(ys_hbm.at[pl.ds(_slot_row(pk_ref, row0_ref, k, n, n_tok), 1)],
                                      ybuf.at[sl, k, pl.ds(j, 1)], sem.at[sl]).start()
            return c

        lax.fori_loop(0, TM, one, 0, unroll=4)

    @pl.when(i == 0)
    def _():
        issue(0, 0)

    @pl.when(i + 1 < n_tiles)
    def _():
        issue(i + 1, 1 - slot)

    for k in range(TOP_K):
        pltpu.make_async_copy(ys_hbm.at[pl.ds(0, TM)], ybuf.at[slot, k], sem.at[slot]).wait()

    md = mod_ref[0, 0]
    gw = gw_ref[...].T
    f = gw[:, 0:1] * ybuf[slot, 0] + gw[:, 1:2] * ybuf[slot, 1]
    xn = x_ref[0] + md[5:6] * f
    o_ref[0] = _rmsnorm(xn, fg_ref[...]) if final else xn


def _combine_call(pk, row0, xm, mod, gw, fg, ys, nq, final):
    n_tok = BATCH * nq * TM
    grid_spec = pltpu.PrefetchScalarGridSpec(
        num_scalar_prefetch=2,
        grid=(BATCH * nq,),
        in_specs=[pl.BlockSpec((1, TM, D_MODEL), lambda i, *_: (i // nq, i % nq, 0)),
                  pl.BlockSpec((1, 1, 8, D_MODEL), lambda i, *_: (i // nq, (i % nq) // NT_LAT, 0, 0)),
                  pl.BlockSpec((8, TM), lambda i, *_: (0, i)),
                  pl.BlockSpec((1, D_MODEL), lambda i, *_: (0, 0)),
                  pl.BlockSpec(memory_space=pl.ANY)],
        out_specs=pl.BlockSpec((1, TM, D_MODEL), lambda i, *_: (i // nq, i % nq, 0)),
        scratch_shapes=[pltpu.VMEM((2, TOP_K, TM, D_MODEL), F32),
                        pltpu.SemaphoreType.DMA((2,))])
    return pl.pallas_call(
        functools.partial(_combine_kernel, n_tok=n_tok, final=final),
        grid_spec=grid_spec,
        out_shape=jax.ShapeDtypeStruct((BATCH, nq * TM, D_MODEL), F32),
        compiler_params=_params(("arbitrary",)),
        name="combine",
    )(pk, row0, xm, mod, gw, fg, ys)


def _rope_tables():
    rows = SEQ // GRID_W
    row = jnp.broadcast_to(jnp.arange(rows)[:, None], (rows, GRID_W)).reshape(-1).astype(F32)
    col = jnp.broadcast_to(jnp.arange(GRID_W)[None, :], (rows, GRID_W)).reshape(-1).astype(F32)

    def cs(rot_dim):
        n_f = rot_dim // 4
        inv = ROPE_BASE ** (-jnp.arange(n_f, dtype=F32) / n_f)
        ang = jnp.concatenate([row[:, None] * inv, col[:, None] * inv], axis=-1)
        return jnp.cos(ang), jnp.sin(ang)

    def with_ctx(c, u, d):
        one = jnp.ones((CTX_LEN, LANES), F32)
        zero = jnp.zeros((CTX_LEN, LANES), F32)
        return (jnp.concatenate([c, one]), jnp.concatenate([u, zero]), jnp.concatenate([d, zero]))

    cos, sin = cs(HEAD_DIM)
    z = jnp.zeros_like(sin)
    t64 = with_ctx(jnp.tile(jnp.concatenate([cos, cos], -1), (1, 2)),
                   jnp.tile(jnp.concatenate([-sin, z], -1), (1, 2)),
                   jnp.tile(jnp.concatenate([z, sin], -1), (1, 2)))
    cos, sin = cs(MLA_ROPE)
    z = jnp.zeros_like(sin)
    one_n = jnp.ones((SEQ, MLA_NOPE), F32)
    zero_n = jnp.zeros((SEQ, MLA_NOPE), F32)
    one_p = jnp.ones((SEQ, LANES - MLA_NOPE - MLA_ROPE), F32)
    zero_p = jnp.zeros((SEQ, LANES - MLA_NOPE - MLA_ROPE), F32)
    tm = with_ctx(jnp.concatenate([one_n, cos, cos, one_p], -1),
                  jnp.concatenate([zero_n, -sin, z, zero_p], -1),
                  jnp.concatenate([zero_n, z, sin, zero_p], -1))
    return t64 + tm


def _ret_tables(decay):
    lg = -jnp.exp(decay.astype(F32))
    idx = jnp.arange(RET_CHUNK, dtype=F32)
    diff = idx[:, None] - idx[None, :]
    fwd = diff >= 0
    bwd = diff < 0
    dm_f = jnp.where(fwd, jnp.exp(lg[0][:, None, None] * jnp.where(fwd, diff, 0.0)), 0.0)
    dm_b = jnp.where(bwd, jnp.exp(lg[1][:, None, None] * jnp.where(bwd, -diff, 0.0)), 0.0)
    dmat = jnp.stack([dm_f, dm_b])
    xi = jnp.stack([jnp.exp(lg[0][:, None] * (idx + 1.0)),
                    jnp.exp(lg[1][:, None] * (RET_CHUNK - idx))])
    zeta = jnp.stack([jnp.exp(lg[0][:, None] * (RET_CHUNK - 1.0 - idx)),
                      jnp.exp(lg[1][:, None] * idx)])
    gch = jnp.exp(lg * RET_CHUNK)
    xi_t = jnp.repeat(jnp.transpose(xi, (0, 2, 1)), HEAD_DIM, axis=2)
    zt_t = jnp.repeat(zeta, HEAD_DIM, axis=1)
    gc_t = jnp.broadcast_to(jnp.repeat(gch, HEAD_DIM, axis=1)[:, :, None], (2, MIX_W, MIX_W))
    return dmat, xi_t, zt_t, gc_t


def _in_proj_columns():
    o_mla = 4 * MIX_W
    o_gqa = o_mla + Q_LORA + KV_LORA + MLA_ROPE
    kv_w = (N_HEADS // 2) * HEAD_DIM
    o_win = o_gqa + MIX_W + 2 * kv_w
    ar = jnp.arange
    dup = jnp.concatenate([ar(HEAD_DIM), ar(HEAD_DIM), HEAD_DIM + ar(HEAD_DIM), HEAD_DIM + ar(HEAD_DIM)])

    def gqa_cols(o):
        return [o + ar(MIX_W), o + MIX_W + dup, o + MIX_W + kv_w + dup]

    return jnp.concatenate([ar(o_mla), o_mla + ar(Q_LORA), o_mla + Q_LORA + ar(KV_LORA)]
                           + gqa_cols(o_gqa) + gqa_cols(o_win)
                           + [o_mla + Q_LORA + KV_LORA + ar(MLA_ROPE)])


def _layer_weights(l, w_in, mla_w_uq, mla_w_ukv):
    cols = _in_proj_columns()
    win_p = jnp.pad(w_in[l][:, cols], ((0, 0), (0, N_IN_P - cols.shape[0]))).astype(BF16)
    uq = mla_w_uq[l].reshape(Q_LORA, N_HEADS, MLA_NOPE + MLA_ROPE)
    wuq = jnp.pad(uq, ((0, 0), (0, 0), (0, LANES - MLA_NOPE - MLA_ROPE))).reshape(Q_LORA, MLA_QK)
    ukv = mla_w_ukv[l].reshape(KV_LORA, N_HEADS, MLA_NOPE + MLA_V)
    wuk = jnp.pad(ukv[:, :, :MLA_NOPE], ((0, 0), (0, 0), (0, LANES - MLA_NOPE))).reshape(KV_LORA, MLA_QK)
    wuv = ukv[:, :, MLA_NOPE:].reshape(KV_LORA, MIX_W)
    return win_p, wuq.astype(BF16), wuk.astype(BF16), wuv.astype(BF16)


def _krope_placement():
    r = jnp.arange(LANES)[:, None]
    c = jnp.arange(MLA_QK)[None, :]
    return ((r < MLA_ROPE) & (c % LANES == MLA_NOPE + r)).astype(BF16)


def _head_block_matrix():
    r = jnp.arange(MIX_W)
    return (r[:, None] // HEAD_DIM == r[None, :] // HEAD_DIM).astype(BF16)


def kernel(x, c, ctx, c_ctx, w_ada, b_ada, norm1_g, norm2_g, w_in, w_gate, w_branch, w_out, ret_decay,
           mla_qn_g, mla_w_uq, mla_kvn_g, mla_w_ukv, gqa_qn_g, gqa_kn_g, win_sink, w_router, b_router,
           w1, w3, w2, final_norm_g):
    cvec = jnp.concatenate([c, c_ctx[None, :], jnp.zeros((7, D_MODEL), F32)], axis=0)
    ada = _ada_call(cvec, w_ada, b_ada)
    tabs = _rope_tables()
    gmat = _head_block_matrix()
    ekr = _krope_placement()
    xx = jnp.concatenate([x, ctx], axis=1)
    out = None
    for l in range(DEPTH):
        last = l == DEPTH - 1
        nq = NT_LAT if last else NT
        m = ada[l].reshape(16, 6, D_MODEL)
        m_lat = m[:BATCH]
        m_ctx = jnp.broadcast_to(m[BATCH][None], (BATCH, 6, D_MODEL))
        mod = jnp.pad(jnp.stack([m_lat, m_ctx], axis=1), ((0, 0), (0, 0), (0, 2), (0, 0)))
        g1 = norm1_g[l][None, :]
        g2 = norm2_g[l][None, :]
        win_p, wuq, wuk, wuv = _layer_weights(l, w_in, mla_w_uq, mla_w_ukv)
        (rq, rkt, rv, rg, mq, mkt, mv, gq, gkt, gv, wq, wkt, wv) = _prep_call(
            xx, mod, g1, win_p, wuq, wuk, wuv, ekr,
            mla_qn_g[l][None, :], mla_kvn_g[l][None, :],
            jnp.tile(gqa_qn_g[l], N_HEADS)[None, :], jnp.tile(gqa_kn_g[l], N_HEADS)[None, :],
            gmat, tabs)
        of, ob = _ret_call(rq, rkt, rv, *_ret_tables(ret_decay[l]))
        ym = _dense_call(mq, mkt, mv, nq, False, "mla")
        yg = _dense_call(gq, gkt, gv, nq, True, "gqa")
        yw = _win_call(win_sink[l], wq, wkt, wv, nq)
        xm, h2, lgt = _merge_call(
            xx, mod, g1, g2, of, ob, rg, ym, yg, yw,
            w_gate[l].astype(BF16), w_branch[l].astype(BF16), w_out[l].astype(BF16), w_router.T, gmat, nq)
        n_tok = BATCH * nq * TM
        pk, gw, cnt = _route_call(lgt, b_router, nq)
        pk = pk.reshape(TOP_K * n_tok)
        nblk = (cnt[:, 0].astype(jnp.int32) + MOE_ROWS - 1) // MOE_ROWS
        row0 = (jnp.cumsum(nblk) - nblk) * MOE_ROWS
        xs = _dispatch_call(pk, row0, nblk, h2.reshape(n_tok, D_MODEL))
        ys = _moe_call(row0, nblk, xs, w1, w3, w2, l)
        res = _combine_call(pk, row0, xm, mod, gw, final_norm_g[None, :], ys, nq, last)
        if last:
            out = res
        else:
            xx = res
    return out
```

```python
import functools

import jax
import jax.numpy as jnp
from jax import lax
from jax.experimental import pallas as pl
from jax.experimental.pallas import tpu as pltpu

F32 = jnp.float32
BF16 = jnp.bfloat16

D_MODEL = 1024
BATCH = 8
SEQ = 2048
DEPTH = 2
CTX_LEN = 256
TOK = SEQ + CTX_LEN
GRID_W = 64
N_HEADS = 4
HEAD_DIM = 64
MIX_W = N_HEADS * HEAD_DIM
RET_CHUNK = 128
Q_LORA = 256
KV_LORA = 128
MLA_NOPE = 64
MLA_ROPE = 32
MLA_V = 64
WINDOW = 128
N_EXPERTS = 32
N_GROUPS = 4
EXPERTS_PER_GROUP = N_EXPERTS // N_GROUPS
TOP_K = 2
D_EXPERT = 1024
ROPE_BASE = 10000.0
EPS = 1e-6
NEG_INF = -1e30

LANES = 128
TM = 256
NT = TOK // TM
NT_LAT = SEQ // TM
WIN_KEYS = TM + 2 * WINDOW
N_CHUNK = TOK // RET_CHUNK
LOG2E = 1.4426950408889634
MOE_ROWS = 128
MOE_SLOTS = 4
N_IN_P = 3072
MLA_QK = 4 * LANES

C_RET = 0
C_MQ = 1024
C_MKV = 1280
C_GQ = 1408
C_GK = 1664
C_GV = 1920
C_WQ = 2176
C_WK = 2432
C_WV = 2688
C_KR = 2944

VMEM_LIMIT = 56 * 1024 * 1024


def _params(sem, vmem=VMEM_LIMIT):
    return pltpu.CompilerParams(dimension_semantics=sem, vmem_limit_bytes=vmem)


def _const_spec(shape):
    nd = len(shape)
    return pl.BlockSpec(shape, lambda *_: (0,) * nd, pipeline_mode=pl.Buffered(1))


def _bdot(a, b):
    return jnp.dot(a.astype(BF16), b.astype(BF16), preferred_element_type=F32)


def _split(a):
    hi = a.astype(BF16)
    lo = (a - hi.astype(F32)).astype(BF16)
    return hi, lo


def _dot_split_lhs(a, b):
    hi, lo = _split(a)
    return (jnp.dot(hi, b, preferred_element_type=F32)
            + jnp.dot(lo, b, preferred_element_type=F32))


def _dot3(a, b):
    ah, al = _split(a)
    bh, bl = _split(b)
    return (jnp.dot(ah, bh, preferred_element_type=F32)
            + jnp.dot(ah, bl, preferred_element_type=F32)
            + jnp.dot(al, bh, preferred_element_type=F32))


def _modnorm(x, g, sc, sh):
    ms = jnp.mean(x * x, axis=-1, keepdims=True)
    return x * lax.rsqrt(ms + EPS) * g * (1.0 + sc) + sh


def _rmsnorm(x, g):
    ms = jnp.mean(x * x, axis=-1, keepdims=True)
    return x * lax.rsqrt(ms + EPS) * g


def _rope(x, c, s_up, s_dn, half):
    outs = []
    for j in range(x.shape[1] // LANES):
        xc = x[:, j * LANES:(j + 1) * LANES]
        outs.append(xc * c + pltpu.roll(xc, LANES - half, 1) * s_up + pltpu.roll(xc, half, 1) * s_dn)
    return outs[0] if len(outs) == 1 else jnp.concatenate(outs, axis=1)


def _head_mean(x, gmat):
    return _dot_split_lhs(x, gmat) * (1.0 / HEAD_DIM)


def _ada_kernel(c_ref, w_ref, b_ref, o_ref):
    c = c_ref[...]
    sc = c * jax.nn.sigmoid(c)
    o_ref[0] = _dot3(sc, w_ref[0]) + b_ref[0]


def _ada_call(cvec, w_ada, b_ada):
    tn = 1536
    return pl.pallas_call(
        _ada_kernel,
        grid=(DEPTH, 6 * D_MODEL // tn),
        in_specs=[
            pl.BlockSpec((16, D_MODEL), lambda l, j: (0, 0)),
            pl.BlockSpec((1, D_MODEL, tn), lambda l, j: (l, 0, j)),
            pl.BlockSpec((1, 1, tn), lambda l, j: (l, 0, j)),
        ],
        out_specs=pl.BlockSpec((1, 16, tn), lambda l, j: (l, 0, j)),
        out_shape=jax.ShapeDtypeStruct((DEPTH, 16, 6 * D_MODEL), F32),
        compiler_params=_params(("arbitrary", "arbitrary")),
        name="ada",
    )(cvec, w_ada, b_ada.reshape(DEPTH, 1, 6 * D_MODEL))


def _prep_kernel(x_ref, mod_ref, g1_ref, win_ref, wuq_ref, wuk_ref, wuv_ref, ekr_ref,
                 qng_ref, kvng_ref, gqg_ref, gkg_ref, gmat_ref,
                 c64_ref, u64_ref, d64_ref, cm_ref, um_ref, dm_ref,
                 rq_ref, rkt_ref, rv_ref, rg_ref, mq_ref, mkt_ref, mv_ref,
                 gq_ref, gkt_ref, gv_ref, wq_ref, wkt_ref, wv_ref):
    x = x_ref[0]
    md = mod_ref[0, 0]
    h = _modnorm(x, g1_ref[...], md[1:2], md[0:1])
    p = jnp.dot(h.astype(BF16), win_ref[...], preferred_element_type=F32)

    c64, u64, d64 = c64_ref[...], u64_ref[...], d64_ref[...]
    cm, um, dm = cm_ref[...], um_ref[...], dm_ref[...]
    gmat = gmat_ref[...]
    qk_scale = HEAD_DIM ** -0.5
    rope64 = lambda a: _rope(a, c64, u64, d64, HEAD_DIM // 2)
    ropem = lambda a: _rope(a, cm, um, dm, MLA_ROPE // 2)

    rq_ref[0] = (rope64(p[:, C_RET:C_RET + 256]) * qk_scale).astype(BF16)
    rkt_ref[0] = rope64(p[:, C_RET + 256:C_RET + 512]).T.astype(BF16)
    rv_ref[0] = p[:, C_RET + 512:C_RET + 768].astype(BF16)
    rg_ref[0] = p[:, C_RET + 768:C_RET + 1024].astype(BF16)

    qn = _rmsnorm(p[:, C_MQ:C_MQ + Q_LORA], qng_ref[...])
    q2 = ropem(_bdot(qn, wuq_ref[...])) * ((MLA_NOPE + MLA_ROPE) ** -0.5 * LOG2E)
    mq_ref[0] = q2.astype(BF16)
    kvn = _rmsnorm(p[:, C_MKV:C_MKV + KV_LORA], kvng_ref[...]).astype(BF16)
    k2 = (jnp.dot(kvn, wuk_ref[...], preferred_element_type=F32)
          + _dot_split_lhs(p[:, C_KR:C_KR + LANES], ekr_ref[...]))
    mkt_ref[0] = ropem(k2).T.astype(BF16)
    mv_ref[0] = jnp.dot(kvn, wuv_ref[...], preferred_element_type=F32).astype(BF16)

    gq = p[:, C_GQ:C_GQ + 256]
    gq = gq * lax.rsqrt(_head_mean(gq * gq, gmat) + EPS) * gqg_ref[...]
    gq_ref[0] = (rope64(gq) * (qk_scale * LOG2E)).astype(BF16)
    gk = p[:, C_GK:C_GK + 256]
    gk = gk * lax.rsqrt(_head_mean(gk * gk, gmat) + EPS) * gkg_ref[...]
    gkt_ref[0] = rope64(gk).T.astype(BF16)
    gv_ref[0] = p[:, C_GV:C_GV + 256].astype(BF16)

    wq_ref[0] = (rope64(p[:, C_WQ:C_WQ + 256]) * qk_scale).astype(BF16)
    wkt_ref[0] = rope64(p[:, C_WK:C_WK + 256]).T.astype(BF16)
    wv_ref[0] = p[:, C_WV:C_WV + 256].astype(BF16)


def _prep_call(xx, mod, g1, win_p, wuq, wuk, wuv, ekr, qng, kvng, gqg, gkg, gmat, tabs):
    tok = lambda w: pl.BlockSpec((1, TM, w), lambda b, t: (b, t, 0))
    tokt = lambda w: pl.BlockSpec((1, w, TM), lambda b, t: (b, 0, t))
    tab = pl.BlockSpec((TM, LANES), lambda b, t: (t, 0))
    sd = lambda w: jax.ShapeDtypeStruct((BATCH, TOK, w), BF16)
    sdt = lambda w: jax.ShapeDtypeStruct((BATCH, w, TOK), BF16)
    in_specs = [
        tok(D_MODEL),
        pl.BlockSpec((1, 1, 8, D_MODEL), lambda b, t: (b, t // NT_LAT, 0, 0)),
        _const_spec((1, D_MODEL)),
        _const_spec((D_MODEL, N_IN_P)),
        _const_spec((Q_LORA, MLA_QK)),
        _const_spec((KV_LORA, MLA_QK)),
        _const_spec((KV_LORA, MIX_W)),
        _const_spec((LANES, MLA_QK)),
        _const_spec((1, Q_LORA)),
        _const_spec((1, KV_LORA)),
        _const_spec((1, MIX_W)),
        _const_spec((1, MIX_W)),
        _const_spec((MIX_W, MIX_W)),
        tab, tab, tab, tab, tab, tab,
    ]
    out_specs = [tok(256), tokt(256), tok(256), tok(256),
                 tok(MLA_QK), tokt(MLA_QK), tok(256),
                 tok(256), tokt(256), tok(256),
                 tok(256), tokt(256), tok(256)]
    out_shape = [sd(256), sdt(256), sd(256), sd(256),
                 sd(MLA_QK), sdt(MLA_QK), sd(256),
                 sd(256), sdt(256), sd(256),
                 sd(256), sdt(256), sd(256)]
    return pl.pallas_call(
        _prep_kernel,
        grid=(BATCH, NT),
        in_specs=in_specs,
        out_specs=out_specs,
        out_shape=out_shape,
        compiler_params=_params(("parallel", "parallel")),
        name="prep",
    )(xx, mod, g1, win_p, wuq, wuk, wuv, ekr, qng, kvng, gqg, gkg, gmat, *tabs)


def _ret_kernel(qf_ref, ktf_ref, vf_ref, qb_ref, ktb_ref, vb_ref,
                dmat_ref, xi_ref, zt_ref, gc_ref, of_ref, ob_ref, sf_ref, sb_ref):
    i = pl.program_id(0)

    @pl.when(i == 0)
    def _():
        sf_ref[...] = jnp.zeros_like(sf_ref)
        sb_ref[...] = jnp.zeros_like(sb_ref)

    lane_head = lax.broadcasted_iota(jnp.int32, (RET_CHUNK, MIX_W), 1) // HEAD_DIM
    r_head = lax.broadcasted_iota(jnp.int32, (MIX_W, MIX_W), 0) // HEAD_DIM
    c_head = lax.broadcasted_iota(jnp.int32, (MIX_W, MIX_W), 1) // HEAD_DIM
    block_diag = r_head == c_head

    def one(b, d, q_ref, kt_ref, v_ref, o_ref, s_ref):
        q = q_ref[b].astype(F32)
        kt = kt_ref[b]
        v = v_ref[b]
        s_old = s_ref[b]
        o = _bdot(q * xi_ref[d], s_old)
        for hd in range(N_HEADS):
            qm = jnp.where(lane_head == hd, q, 0.0).astype(BF16)
            inner = jnp.dot(qm, kt, preferred_element_type=F32) * dmat_ref[d, hd]
            oh = jnp.dot(inner.astype(BF16), v, preferred_element_type=F32)
            o = o + jnp.where(lane_head == hd, oh, 0.0)
        o_ref[b] = o
        kz = (kt.astype(F32) * zt_ref[d]).astype(BF16)
        upd = jnp.dot(kz, v, preferred_element_type=F32)
        s_ref[b] = gc_ref[d] * s_old + jnp.where(block_diag, upd, 0.0)

    def body(b, carry):
        one(b, 0, qf_ref, ktf_ref, vf_ref, of_ref, sf_ref)
        one(b, 1, qb_ref, ktb_ref, vb_ref, ob_ref, sb_ref)
        return carry

    lax.fori_loop(0, BATCH, body, 0)


def _ret_call(rq, rkt, rv, dmat, xi, zt, gc):
    cf = lambda i: (i + SEQ // RET_CHUNK) % N_CHUNK
    cb = lambda i: N_CHUNK - 1 - i
    rows = lambda f: pl.BlockSpec((BATCH, RET_CHUNK, MIX_W), lambda i: (0, f(i), 0))
    cols = lambda f: pl.BlockSpec((BATCH, MIX_W, RET_CHUNK), lambda i: (0, 0, f(i)))
    return pl.pallas_call(
        _ret_kernel,
        grid=(N_CHUNK,),
        in_specs=[rows(cf), cols(cf), rows(cf), rows(cb), cols(cb), rows(cb),
                  _const_spec((2, N_HEADS, RET_CHUNK, RET_CHUNK)),
                  _const_spec((2, RET_CHUNK, MIX_W)),
                  _const_spec((2, MIX_W, RET_CHUNK)),
                  _const_spec((2, MIX_W, MIX_W))],
        out_specs=[rows(cf), rows(cb)],
        out_shape=[jax.ShapeDtypeStruct((BATCH, TOK, MIX_W), F32)] * 2,
        scratch_shapes=[pltpu.VMEM((BATCH, MIX_W, MIX_W), F32),
                        pltpu.VMEM((BATCH, MIX_W, MIX_W), F32)],
        compiler_params=_params(("arbitrary",)),
        name="retention",
    )(rq, rkt, rv, rq, rkt, rv, dmat, xi, zt, gc)


def _head_q(q_ref, hd, pair, rows=slice(None)):
    if not pair:
        return q_ref[0, rows, hd * LANES:(hd + 1) * LANES], hd * LANES
    c = hd // 2
    qc = q_ref[0, rows, c * LANES:(c + 1) * LANES]
    half = lax.broadcasted_iota(jnp.int32, qc.shape, 1) // HEAD_DIM
    return jnp.where(half == hd % 2, qc, jnp.zeros_like(qc)), c * LANES


def _dense_kernel(q_ref, kt_ref, v_ref, o_ref, *, pair):
    t = pl.program_id(1)
    lane_head = lax.broadcasted_iota(jnp.int32, (TM, MIX_W), 1) // HEAD_DIM

    def run(k_lo, k_hi):
        v = v_ref[0, k_lo:k_hi, :]
        acc = jnp.zeros((TM, MIX_W), F32)
        for hd in range(N_HEADS):
            qm, r0 = _head_q(q_ref, hd, pair)
            s = jnp.dot(qm, kt_ref[0, r0:r0 + LANES, k_lo:k_hi], preferred_element_type=F32)
            m = jnp.max(s, axis=-1, keepdims=True)
            p = jnp.exp2(s - m)
            l = jnp.sum(p, axis=-1, keepdims=True)
            oh = jnp.dot(p.astype(BF16), v, preferred_element_type=F32)
            acc = jnp.where(lane_head == hd, oh * (1.0 / l), acc)
        o_ref[0] = acc.astype(BF16)

    @pl.when(t < NT_LAT)
    def _():
        run(0, TOK)

    @pl.when(t >= NT_LAT)
    def _():
        run(SEQ, TOK)


def _dense_call(q, kt, v, nq, pair, name):
    wq = q.shape[-1]
    return pl.pallas_call(
        functools.partial(_dense_kernel, pair=pair),
        grid=(BATCH, nq),
        in_specs=[pl.BlockSpec((1, TM, wq), lambda b, t: (b, t, 0)),
                  pl.BlockSpec((1, kt.shape[1], TOK), lambda b, t: (b, 0, 0)),
                  pl.BlockSpec((1, TOK, MIX_W), lambda b, t: (b, 0, 0))],
        out_specs=pl.BlockSpec((1, TM, MIX_W), lambda b, t: (b, t, 0)),
        out_shape=jax.ShapeDtypeStruct((BATCH, nq * TM, MIX_W), BF16),
        compiler_params=_params(("parallel", "arbitrary")),
        name=name,
    )(q, kt, v)


def _win_kernel(sink_ref, q_ref, kt_ref, v_ref, o_ref):
    t = pl.program_id(1)
    lane_head = lax.broadcasted_iota(jnp.int32, (TM, MIX_W), 1) // HEAD_DIM
    v_ctx = v_ref[0, SEQ:TOK, :]

    @pl.when(t < NT_LAT)
    def _():
        start = pl.multiple_of(jnp.clip(t * TM - WINDOW, 0, SEQ - WIN_KEYS), LANES)
        qpos = t * TM + lax.broadcasted_iota(jnp.int32, (TM, WIN_KEYS), 0)
        kpos = start + lax.broadcasted_iota(jnp.int32, (TM, WIN_KEYS), 1)
        valid = jnp.abs(kpos - qpos) <= WINDOW
        v_loc = v_ref[0, pl.ds(start, WIN_KEYS), :]
        acc = jnp.zeros((TM, MIX_W), F32)
        for hd in range(N_HEADS):
            qm, r0 = _head_q(q_ref, hd, True)
            s_loc = jnp.dot(qm, kt_ref[0, r0:r0 + LANES, pl.ds(start, WIN_KEYS)],
                            preferred_element_type=F32)
            s_loc = jnp.where(valid, s_loc, NEG_INF)
            s_ctx = jnp.dot(qm, kt_ref[0, r0:r0 + LANES, SEQ:TOK], preferred_element_type=F32)
            sk = sink_ref[hd]
            m = jnp.maximum(jnp.maximum(jnp.max(s_loc, axis=-1, keepdims=True),
                                        jnp.max(s_ctx, axis=-1, keepdims=True)), sk)
            p_loc = jnp.exp(s_loc - m)
            p_ctx = jnp.exp(s_ctx - m)
            l = (jnp.sum(p_loc, axis=-1, keepdims=True) + jnp.sum(p_ctx, axis=-1, keepdims=True)
                 + jnp.exp(sk - m))
            oh = (jnp.dot(p_loc.astype(BF16), v_loc, preferred_element_type=F32)
                  + jnp.dot(p_ctx.astype(BF16), v_ctx, preferred_element_type=F32))
            acc = jnp.where(lane_head == hd, oh * (1.0 / l), acc)
        o_ref[0] = acc.astype(BF16)

    @pl.when(t >= NT_LAT)
    def _():
        acc = jnp.zeros((TM, MIX_W), F32)
        for hd in range(N_HEADS):
            qm, r0 = _head_q(q_ref, hd, True)
            s = jnp.dot(qm, kt_ref[0, r0:r0 + LANES, SEQ:TOK], preferred_element_type=F32)
            sk = sink_ref[hd]
            m = jnp.maximum(jnp.max(s, axis=-1, keepdims=True), sk)
            p = jnp.exp(s - m)
            l = jnp.sum(p, axis=-1, keepdims=True) + jnp.exp(sk - m)
            oh = jnp.dot(p.astype(BF16), v_ctx, preferred_element_type=F32)
            acc = jnp.where(lane_head == hd, oh * (1.0 / l), acc)
        o_ref[0] = acc.astype(BF16)


def _win_call(sink, q, kt, v, nq):
    return pl.pallas_call(
        _win_kernel,
        grid=(BATCH, nq),
        in_specs=[pl.BlockSpec(memory_space=pltpu.SMEM),
                  pl.BlockSpec((1, TM, MIX_W), lambda b, t: (b, t, 0)),
                  pl.BlockSpec((1, MIX_W, TOK), lambda b, t: (b, 0, 0)),
                  pl.BlockSpec((1, TOK, MIX_W), lambda b, t: (b, 0, 0))],
        out_specs=pl.BlockSpec((1, TM, MIX_W), lambda b, t: (b, t, 0)),
        out_shape=jax.ShapeDtypeStruct((BATCH, nq * TM, MIX_W), BF16),
        compiler_params=_params(("parallel", "arbitrary")),
        name="window",
    )(sink, q, kt, v)


def _merge_kernel(x_ref, mod_ref, g1_ref, g2_ref, of_ref, ob_ref, rg_ref, ym_ref, yg_ref, yw_ref,
                  wg_ref, wb_ref, wo_ref, wr_ref, gmat_ref, xo_ref, h2_ref, lg_ref):
    x = x_ref[0]
    md = mod_ref[0, 0]
    hb = _modnorm(x, g1_ref[...], md[1:2], md[0:1]).astype(BF16)

    gmat = gmat_ref[...]
    o = of_ref[0] + ob_ref[0]
    dlt = o - _head_mean(o, gmat)
    var = _head_mean(dlt * dlt, gmat)
    g = rg_ref[0].astype(F32)
    y_ret = dlt * lax.rsqrt(var + EPS) * (g * jax.nn.sigmoid(g))

    ys = (y_ret.astype(BF16), ym_ref[0], yg_ref[0], yw_ref[0])
    acc = jnp.zeros((TM, D_MODEL), F32)
    for i in range(4):
        gate = jax.nn.sigmoid(jnp.dot(hb, wg_ref[:, i * D_MODEL:(i + 1) * D_MODEL],
                                      preferred_element_type=F32))
        acc = acc + gate * jnp.dot(ys[i], wb_ref[i], preferred_element_type=F32)
    out = jnp.dot(acc.astype(BF16), wo_ref[...], preferred_element_type=F32)
    xm = x + md[2:3] * out
    xo_ref[0] = xm
    h2 = _modnorm(xm, g2_ref[...], md[4:5], md[3:4])
    h2_ref[0] = h2
    hh, hl = _split(h2)
    wh, wl = _split(wr_ref[...])
    nt = lambda a, b: lax.dot_general(a, b, (((1,), (1,)), ((), ())), preferred_element_type=F32)
    lg_ref[0] = nt(wh, hh) + nt(wh, hl) + nt(wl, hh)


def _merge_call(xx, mod, g1, g2, of, ob, rg, ym, yg, yw, wg, wb, wo, wr, gmat, nq):
    tok = lambda w: pl.BlockSpec((1, TM, w), lambda b, t: (b, t, 0))
    return pl.pallas_call(
        _merge_kernel,
        grid=(BATCH, nq),
        in_specs=[tok(D_MODEL),
                  pl.BlockSpec((1, 1, 8, D_MODEL), lambda b, t: (b, t // NT_LAT, 0, 0)),
                  _const_spec((1, D_MODEL)), _const_spec((1, D_MODEL)),
                  tok(MIX_W), tok(MIX_W), tok(MIX_W), tok(MIX_W), tok(MIX_W), tok(MIX_W),
                  _const_spec((D_MODEL, 4 * D_MODEL)),
                  _const_spec((4, MIX_W, D_MODEL)),
                  _const_spec((D_MODEL, D_MODEL)),
                  _const_spec((N_EXPERTS, D_MODEL)),
                  _const_spec((MIX_W, MIX_W))],
        out_specs=[tok(D_MODEL), tok(D_MODEL),
                   pl.BlockSpec((1, N_EXPERTS, TM), lambda b, t: (b, 0, t))],
        out_shape=[jax.ShapeDtypeStruct((BATCH, nq * TM, D_MODEL), F32),
                   jax.ShapeDtypeStruct((BATCH, nq * TM, D_MODEL), F32),
                   jax.ShapeDtypeStruct((BATCH, N_EXPERTS, nq * TM), F32)],
        compiler_params=_params(("parallel", "parallel")),
        name="merge",
    )(xx, mod, g1, g2, of, ob, rg, ym, yg, yw, wg, wb, wo, wr, gmat)


RANK_BITS = 20
RANK_MASK = (1 << RANK_BITS) - 1


def _route_kernel(lg_ref, b_ref, tri_ref, pk_ref, gw_ref, cnt_ref, base_ref):
    i = pl.program_id(0)

    @pl.when(i == 0)
    def _():
        base_ref[...] = jnp.zeros_like(base_ref)

    s = jax.nn.sigmoid(lg_ref[0])
    sel = s + b_ref[:, 0:1]
    sub = lax.broadcasted_iota(jnp.int32, (EXPERTS_PER_GROUP, TM), 0)
    best = e1 = e2 = s1 = s2 = None
    for g in range(N_GROUPS):
        rows = slice(g * EXPERTS_PER_GROUP, (g + 1) * EXPERTS_PER_GROUP)
        blk, sb = sel[rows], s[rows]
        m1 = jnp.max(blk, axis=0, keepdims=True)
        i1 = jnp.min(jnp.where(blk == m1, sub, EXPERTS_PER_GROUP), axis=0, keepdims=True)
        hit1 = sub == i1
        blk2 = jnp.where(hit1, -jnp.inf, blk)
        m2 = jnp.max(blk2, axis=0, keepdims=True)
        i2 = jnp.min(jnp.where(blk2 == m2, sub, EXPERTS_PER_GROUP), axis=0, keepdims=True)
        hit2 = sub == i2
        score = m1 + m2
        s1g = jnp.sum(jnp.where(hit1, sb, 0.0), axis=0, keepdims=True)
        s2g = jnp.sum(jnp.where(hit2, sb, 0.0), axis=0, keepdims=True)
        e1g = g * EXPERTS_PER_GROUP + i1
        e2g = g * EXPERTS_PER_GROUP + i2
        if g == 0:
            best, e1, e2, s1, s2 = score, e1g, e2g, s1g, s2g
        else:
            better = score > best
            best = jnp.where(better, score, best)
            e1 = jnp.where(better, e1g, e1)
            e2 = jnp.where(better, e2g, e2)
            s1 = jnp.where(better, s1g, s1)
            s2 = jnp.where(better, s2g, s2)

    eid = lax.broadcasted_iota(jnp.int32, (N_EXPERTS, TM), 0)
    oh1 = eid == e1
    oh2 = eid == e2
    oh = jnp.where(oh1 | oh2, 1.0, 0.0)
    before = jnp.dot(oh.astype(BF16), tri_ref[...], preferred_element_type=F32) + base_ref[:, 0:1]
    r1 = jnp.sum(jnp.where(oh1, before, 0.0), axis=0, keepdims=True).astype(jnp.int32)
    r2 = jnp.sum(jnp.where(oh2, before, 0.0), axis=0, keepdims=True).astype(jnp.int32)
    total = base_ref[...] + jnp.sum(oh, axis=1, keepdims=True)
    base_ref[...] = total
    cnt_ref[...] = total

    pk_ref[...] = jnp.concatenate([(e1 << RANK_BITS) + r1, (e2 << RANK_BITS) + r2], axis=0)
    den = s1 + s2
    row = lax.broadcasted_iota(jnp.int32, (8, TM), 0)
    gw_ref[...] = jnp.where(row == 0, s1 / den, jnp.where(row == 1, s2 / den, 0.0))


def _route_call(lgt, b_router, nq):
    n_tiles = BATCH * nq
    n_tok = n_tiles * TM
    r = jnp.arange(TM)
    tri = (r[:, None] < r[None, :]).astype(BF16)
    bcol = jnp.broadcast_to(b_router.astype(F32)[:, None], (N_EXPERTS, LANES))
    return pl.pallas_call(
        _route_kernel,
        grid=(n_tiles,),
        in_specs=[pl.BlockSpec((1, N_EXPERTS, TM), lambda i: (i // nq, 0, i % nq)),
                  _const_spec((N_EXPERTS, LANES)),
                  _const_spec((TM, TM))],
        out_specs=[pl.BlockSpec((TOP_K, TM), lambda i: (0, i)),
                   pl.BlockSpec((8, TM), lambda i: (0, i)),
                   pl.BlockSpec((N_EXPERTS, LANES), lambda i: (0, 0))],
        out_shape=[jax.ShapeDtypeStruct((TOP_K, n_tok), jnp.int32),
                   jax.ShapeDtypeStruct((8, n_tok), F32),
                   jax.ShapeDtypeStruct((N_EXPERTS, LANES), F32)],
        scratch_shapes=[pltpu.VMEM((N_EXPERTS, LANES), F32)],
        compiler_params=_params(("arbitrary",)),
        name="route",
    )(lgt, bcol, tri)


def _slot_row(pk_ref, row0_ref, k, n, n_tok):
    p = pk_ref[k * n_tok + n]
    return row0_ref[p >> RANK_BITS] + (p & RANK_MASK)


def _dispatch_kernel(pk_ref, row0_ref, nblk_ref, h_ref, xs_hbm, zbuf, hbuf, sem, zsem, *, n_tok, n_blocks):
    i = pl.program_id(0)
    n_tiles = pl.num_programs(0)

    def zero_copy(row):
        return pltpu.make_async_copy(zbuf, xs_hbm.at[pl.ds(pl.multiple_of(row, MOE_ROWS), MOE_ROWS)], zsem)

    @pl.when(i == 0)
    def _():
        zbuf[...] = jnp.zeros_like(zbuf)
        used = (row0_ref[N_EXPERTS - 1] // MOE_ROWS) + nblk_ref[N_EXPERTS - 1]

        def last_row(e):
            return row0_ref[e] + (nblk_ref[e] - 1) * MOE_ROWS

        def start_e(e, c):
            @pl.when(nblk_ref[e] > 0)
            def _():
                zero_copy(last_row(e)).start()
            return c

        def wait_e(e, c):
            @pl.when(nblk_ref[e] > 0)
            def _():
                zero_copy(last_row(e)).wait()
            return c

        lax.fori_loop(0, N_EXPERTS, start_e, 0)
        lax.fori_loop(used, n_blocks, lambda bk, c: (zero_copy(bk * MOE_ROWS).start(), c)[1], 0)
        lax.fori_loop(0, N_EXPERTS, wait_e, 0)
        lax.fori_loop(used, n_blocks, lambda bk, c: (zero_copy(bk * MOE_ROWS).wait(), c)[1], 0)

    slot = i % 2

    def wait_tile(sl):
        for _ in range(TOP_K):
            pltpu.make_async_copy(hbuf.at[sl], xs_hbm.at[pl.ds(0, TM)], sem.at[sl]).wait()

    @pl.when(i >= 2)
    def _():
        wait_tile(slot)

    hbuf[slot] = h_ref[...]

    def one(j, c):
        n = i * TM + j
        for k in range(TOP_K):
            pltpu.make_async_copy(hbuf.at[slot, pl.ds(j, 1)],
                                  xs_hbm.at[pl.ds(_slot_row(pk_ref, row0_ref, k, n, n_tok), 1)],
                                  sem.at[slot]).start()
        return c

    lax.fori_loop(0, TM, one, 0, unroll=4)

    @pl.when(i == n_tiles - 1)
    def _():
        @pl.when(n_tiles >= 2)
        def _():
            wait_tile(1 - slot)

        wait_tile(slot)


def _dispatch_call(pk, row0, nblk, h2):
    n_tok = h2.shape[0]
    n_blocks = TOP_K * n_tok // MOE_ROWS + N_EXPERTS
    grid_spec = pltpu.PrefetchScalarGridSpec(
        num_scalar_prefetch=3,
        grid=(n_tok // TM,),
        in_specs=[pl.BlockSpec((TM, D_MODEL), lambda i, *_: (i, 0))],
        out_specs=pl.BlockSpec(memory_space=pl.ANY),
        scratch_shapes=[pltpu.VMEM((MOE_ROWS, D_MODEL), F32),
                        pltpu.VMEM((2, TM, D_MODEL), F32),
                        pltpu.SemaphoreType.DMA((2,)),
                        pltpu.SemaphoreType.DMA(())])
    return pl.pallas_call(
        functools.partial(_dispatch_kernel, n_tok=n_tok, n_blocks=n_blocks),
        grid_spec=grid_spec,
        out_shape=jax.ShapeDtypeStruct((n_blocks * MOE_ROWS, D_MODEL), F32),
        compiler_params=_params(("arbitrary",)),
        name="dispatch",
    )(pk, row0, nblk, h2)


def _moe_kernel(row0_ref, nblk_ref, xs_hbm, w1_ref, w3_ref, w2_ref, ys_hbm,
                xbuf, ybuf, w1b, w3b, w2b, isem, osem, *, n_blocks):
    e = pl.program_id(0)
    nb = nblk_ref[e]
    b0 = row0_ref[e] // MOE_ROWS
    used = row0_ref[N_EXPERTS - 1] // MOE_ROWS + nblk_ref[N_EXPERTS - 1]

    def rows(g):
        return pl.ds(pl.multiple_of(g * MOE_ROWS, MOE_ROWS), MOE_ROWS)

    def in_copy(g):
        return pltpu.make_async_copy(xs_hbm.at[rows(g)], xbuf.at[g % MOE_SLOTS], isem.at[g % MOE_SLOTS])

    def out_copy(g):
        return pltpu.make_async_copy(ybuf.at[g % 2], ys_hbm.at[rows(g)], osem.at[g % 2])

    @pl.when(e == 0)
    def _():
        for g in range(MOE_SLOTS - 1):
            @pl.when(g < used)
            def _():
                in_copy(g).start()

    @pl.when(nb > 0)
    def _():
        w1b[...] = w1_ref[0, 0].astype(BF16)
        w3b[...] = w3_ref[0, 0].astype(BF16)
        w2b[...] = w2_ref[0, 0].astype(BF16)

        def block(r, carry):
            g = b0 + r

            @pl.when(g + MOE_SLOTS - 1 < used)
            def _():
                in_copy(g + MOE_SLOTS - 1).start()

            in_copy(g).wait()

            @pl.when(g >= 2)
            def _():
                out_copy(g - 2).wait()

            xb = xbuf[g % MOE_SLOTS].astype(BF16)
            h1 = jnp.dot(xb, w1b[...], preferred_element_type=F32)
            h3 = jnp.dot(xb, w3b[...], preferred_element_type=F32)
            hid = (h1 * jax.nn.sigmoid(h1) * h3).astype(BF16)
            ybuf[g % 2] = jnp.dot(hid, w2b[...], preferred_element_type=F32)
            out_copy(g).start()
            return carry

        lax.fori_loop(0, nb, block, 0)

    @pl.when(e == N_EXPERTS - 1)
    def _():
        @pl.when(used >= 2)
        def _():
            out_copy(used - 2).wait()

        @pl.when(used >= 1)
        def _():
            out_copy(used - 1).wait()

        ybuf[0] = jnp.zeros((MOE_ROWS, D_MODEL), F32)

        def tail_copy(bk):
            return pltpu.make_async_copy(
                ybuf.at[0], ys_hbm.at[pl.ds(pl.multiple_of(bk * MOE_ROWS, MOE_ROWS), MOE_ROWS)], osem.at[0])

        lax.fori_loop(used, n_blocks, lambda bk, c: (tail_copy(bk).start(), c)[1], 0)
        lax.fori_loop(used, n_blocks, lambda bk, c: (tail_copy(bk).wait(), c)[1], 0)


def _moe_call(row0, nblk, xs, w1, w3, w2, l):
    n_blocks = xs.shape[0] // MOE_ROWS
    wspec = pl.BlockSpec((1, 1, D_MODEL, D_EXPERT), lambda e, *_: (l, e, 0, 0))
    grid_spec = pltpu.PrefetchScalarGridSpec(
        num_scalar_prefetch=2,
        grid=(N_EXPERTS,),
        in_specs=[pl.BlockSpec(memory_space=pl.ANY), wspec, wspec,
                  pl.BlockSpec((1, 1, D_EXPERT, D_MODEL), lambda e, *_: (l, e, 0, 0))],
        out_specs=pl.BlockSpec(memory_space=pl.ANY),
        scratch_shapes=[pltpu.VMEM((MOE_SLOTS, MOE_ROWS, D_MODEL), F32),
                        pltpu.VMEM((2, MOE_ROWS, D_MODEL), F32),
                        pltpu.VMEM((D_MODEL, D_EXPERT), BF16),
                        pltpu.VMEM((D_MODEL, D_EXPERT), BF16),
                        pltpu.VMEM((D_EXPERT, D_MODEL), BF16),
                        pltpu.SemaphoreType.DMA((MOE_SLOTS,)),
                        pltpu.SemaphoreType.DMA((2,))])
    return pl.pallas_call(
        functools.partial(_moe_kernel, n_blocks=n_blocks),
        grid_spec=grid_spec,
        out_shape=jax.ShapeDtypeStruct(xs.shape, F32),
        compiler_params=_params(("arbitrary",)),
        name="moe",
    )(row0, nblk, xs, w1, w3, w2)


def _combine_kernel(pk_ref, row0_ref, x_ref, mod_ref, gw_ref, fg_ref, ys_hbm, o_ref, ybuf, sem,
                    *, n_tok, final):
    i = pl.program_id(0)
    n_tiles = pl.num_programs(0)
    slot = i % 2

    def issue(tile, sl):
        def one(j, c):
            n = tile * TM + j
            for k in range(TOP_K):
                pltpu.make_async_copy(ys_hbm.at[pl.ds(_slot_row(pk_ref, row0_ref, k, n, n_tok), 1)],
                                      ybuf.at[sl, k, pl.ds(j, 1)], sem.at[sl]).start()
            return c

        lax.fori_loop(0, TM, one, 0, unroll=4)

    @pl.when(i == 0)
    def _():
        issue(0, 0)

    @pl.when(i + 1 < n_tiles)
    def _():
        issue(i + 1, 1 - slot)

    for k in range(TOP_K):
        pltpu.make_async_copy(ys_hbm.at[pl.ds(0, TM)], ybuf.at[slot, k], sem.at[slot]).wait()

    md = mod_ref[0, 0]
    gw = gw_ref[...].T
    f = gw[:, 0:1] * ybuf[slot, 0] + gw[:, 1:2] * ybuf[slot, 1]
    xn = x_ref[0] + md[5:6] * f
    o_ref[0] = _rmsnorm(xn, fg_ref[...]) if final else xn


def _combine_call(pk, row0, xm, mod, gw, fg, ys, nq, final):
    n_tok = BATCH * nq * TM
    grid_spec = pltpu.PrefetchScalarGridSpec(
        num_scalar_prefetch=2,
        grid=(BATCH * nq,),
        in_specs=[pl.BlockSpec((1, TM, D_MODEL), lambda i, *_: (i // nq, i % nq, 0)),
                  pl.BlockSpec((1, 1, 8, D_MODEL), lambda i, *_: (i // nq, (i % nq) // NT_LAT, 0, 0)),
                  pl.BlockSpec((8, TM), lambda i, *_: (0, i)),
                  pl.BlockSpec((1, D_MODEL), lambda i, *_: (0, 0)),
                  pl.BlockSpec(memory_space=pl.ANY)],
        out_specs=pl.BlockSpec((1, TM, D_MODEL), lambda i, *_: (i // nq, i % nq, 0)),
        scratch_shapes=[pltpu.VMEM((2, TOP_K, TM, D_MODEL), F32),
                        pltpu.SemaphoreType.DMA((2,))])
    return pl.pallas_call(
        functools.partial(_combine_kernel, n_tok=n_tok, final=final),
        grid_spec=grid_spec,
        out_shape=jax.ShapeDtypeStruct((BATCH, nq * TM, D_MODEL), F32),
        compiler_params=_params(("arbitrary",)),
        name="combine",
    )(pk, row0, xm, mod, gw, fg, ys)


def _rope_tables():
    rows = SEQ // GRID_W
    row = jnp.broadcast_to(jnp.arange(rows)[:, None], (rows, GRID_W)).reshape(-1).astype(F32)
    col = jnp.broadcast_to(jnp.arange(GRID_W)[None, :], (rows, GRID_W)).reshape(-1).astype(F32)

    def cs(rot_dim):
        n_f = rot_dim // 4
        inv = ROPE_BASE ** (-jnp.arange(n_f, dtype=F32) / n_f)
        ang = jnp.concatenate([row[:, None] * inv, col[:, None] * inv], axis=-1)
        return jnp.cos(ang), jnp.sin(ang)

    def with_ctx(c, u, d):
        one = jnp.ones((CTX_LEN, LANES), F32)
        zero = jnp.zeros((CTX_LEN, LANES), F32)
        return (jnp.concatenate([c, one]), jnp.concatenate([u, zero]), jnp.concatenate([d, zero]))

    cos, sin = cs(HEAD_DIM)
    z = jnp.zeros_like(sin)
    t64 = with_ctx(jnp.tile(jnp.concatenate([cos, cos], -1), (1, 2)),
                   jnp.tile(jnp.concatenate([-sin, z], -1), (1, 2)),
                   jnp.tile(jnp.concatenate([z, sin], -1), (1, 2)))
    cos, sin = cs(MLA_ROPE)
    z = jnp.zeros_like(sin)
    one_n = jnp.ones((SEQ, MLA_NOPE), F32)
    zero_n = jnp.zeros((SEQ, MLA_NOPE), F32)
    one_p = jnp.ones((SEQ, LANES - MLA_NOPE - MLA_ROPE), F32)
    zero_p = jnp.zeros((SEQ, LANES - MLA_NOPE - MLA_ROPE), F32)
    tm = with_ctx(jnp.concatenate([one_n, cos, cos, one_p], -1),
                  jnp.concatenate([zero_n, -sin, z, zero_p], -1),
                  jnp.concatenate([zero_n, z, sin, zero_p], -1))
    return t64 + tm


def _ret_tables(decay):
    lg = -jnp.exp(decay.astype(F32))
    idx = jnp.arange(RET_CHUNK, dtype=F32)
    diff = idx[:, None] - idx[None, :]
    fwd = diff >= 0
    bwd = diff < 0
    dm_f = jnp.where(fwd, jnp.exp(lg[0][:, None, None] * jnp.where(fwd, diff, 0.0)), 0.0)
    dm_b = jnp.where(bwd, jnp.exp(lg[1][:, None, None] * jnp.where(bwd, -diff, 0.0)), 0.0)
    dmat = jnp.stack([dm_f, dm_b])
    xi = jnp.stack([jnp.exp(lg[0][:, None] * (idx + 1.0)),
                    jnp.exp(lg[1][:, None] * (RET_CHUNK - idx))])
    zeta = jnp.stack([jnp.exp(lg[0][:, None] * (RET_CHUNK - 1.0 - idx)),
                      jnp.exp(lg[1][:, None] * idx)])
    gch = jnp.exp(lg * RET_CHUNK)
    xi_t = jnp.repeat(jnp.transpose(xi, (0, 2, 1)), HEAD_DIM, axis=2)
    zt_t = jnp.repeat(zeta, HEAD_DIM, axis=1)
    gc_t = jnp.broadcast_to(jnp.repeat(gch, HEAD_DIM, axis=1)[:, :, None], (2, MIX_W, MIX_W))
    return dmat, xi_t, zt_t, gc_t


def _in_proj_columns():
    o_mla = 4 * MIX_W
    o_gqa = o_mla + Q_LORA + KV_LORA + MLA_ROPE
    kv_w = (N_HEADS // 2) * HEAD_DIM
    o_win = o_gqa + MIX_W + 2 * kv_w
    ar = jnp.arange
    dup = jnp.concatenate([ar(HEAD_DIM), ar(HEAD_DIM), HEAD_DIM + ar(HEAD_DIM), HEAD_DIM + ar(HEAD_DIM)])

    def gqa_cols(o):
        return [o + ar(MIX_W), o + MIX_W + dup, o + MIX_W + kv_w + dup]

    return jnp.concatenate([ar(o_mla), o_mla + ar(Q_LORA), o_mla + Q_LORA + ar(KV_LORA)]
                           + gqa_cols(o_gqa) + gqa_cols(o_win)
                           + [o_mla + Q_LORA + KV_LORA + ar(MLA_ROPE)])


def _layer_weights(l, w_in, mla_w_uq, mla_w_ukv):
    cols = _in_proj_columns()
    win_p = jnp.pad(w_in[l][:, cols], ((0, 0), (0, N_IN_P - cols.shape[0]))).astype(BF16)
    uq = mla_w_uq[l].reshape(Q_LORA, N_HEADS, MLA_NOPE + MLA_ROPE)
    wuq = jnp.pad(uq, ((0, 0), (0, 0), (0, LANES - MLA_NOPE - MLA_ROPE))).reshape(Q_LORA, MLA_QK)
    ukv = mla_w_ukv[l].reshape(KV_LORA, N_HEADS, MLA_NOPE + MLA_V)
    wuk = jnp.pad(ukv[:, :, :MLA_NOPE], ((0, 0), (0, 0), (0, LANES - MLA_NOPE))).reshape(KV_LORA, MLA_QK)
    wuv = ukv[:, :, MLA_NOPE:].reshape(KV_LORA, MIX_W)
    return win_p, wuq.astype(BF16), wuk.astype(BF16), wuv.astype(BF16)


def _krope_placement():
    r = jnp.arange(LANES)[:, None]
    c = jnp.arange(MLA_QK)[None, :]
    return ((r < MLA_ROPE) & (c % LANES == MLA_NOPE + r)).astype(BF16)


def _head_block_matrix():
    r = jnp.arange(MIX_W)
    return (r[:, None] // HEAD_DIM == r[None, :] // HEAD_DIM).astype(BF16)


def kernel(x, c, ctx, c_ctx, w_ada, b_ada, norm1_g, norm2_g, w_in, w_gate, w_branch, w_out, ret_decay,
           mla_qn_g, mla_w_uq, mla_kvn_g, mla_w_ukv, gqa_qn_g, gqa_kn_g, win_sink, w_router, b_router,
           w1, w3, w2, final_norm_g):
    cvec = jnp.concatenate([c, c_ctx[None, :], jnp.zeros((7, D_MODEL), F32)], axis=0)
    ada = _ada_call(cvec, w_ada, b_ada)
    tabs = _rope_tables()
    gmat = _head_block_matrix()
    ekr = _krope_placement()
    xx = jnp.concatenate([x, ctx], axis=1)
    out = None
    for l in range(DEPTH):
        last = l == DEPTH - 1
        nq = NT_LAT if last else NT
        m = ada[l].reshape(16, 6, D_MODEL)
        m_lat = m[:BATCH]
        m_ctx = jnp.broadcast_to(m[BATCH][None], (BATCH, 6, D_MODEL))
        mod = jnp.pad(jnp.stack([m_lat, m_ctx], axis=1), ((0, 0), (0, 0), (0, 2), (0, 0)))
        g1 = norm1_g[l][None, :]
        g2 = norm2_g[l][None, :]
        win_p, wuq, wuk, wuv = _layer_weights(l, w_in, mla_w_uq, mla_w_ukv)
        (rq, rkt, rv, rg, mq, mkt, mv, gq, gkt, gv, wq, wkt, wv) = _prep_call(
            xx, mod, g1, win_p, wuq, wuk, wuv, ekr,
            mla_qn_g[l][None, :], mla_kvn_g[l][None, :],
            jnp.tile(gqa_qn_g[l], N_HEADS)[None, :], jnp.tile(gqa_kn_g[l], N_HEADS)[None, :],
            gmat, tabs)
        of, ob = _ret_call(rq, rkt, rv, *_ret_tables(ret_decay[l]))
        ym = _dense_call(mq, mkt, mv, nq, False, "mla")
        yg = _dense_call(gq, gkt, gv, nq, True, "gqa")
        yw = _win_call(win_sink[l], wq, wkt, wv, nq)
        xm, h2, lgt = _merge_call(
            xx, mod, g1, g2, of, ob, rg, ym, yg, yw,
            w_gate[l].astype(BF16), w_branch[l].astype(BF16), w_out[l].astype(BF16), w_router.T, gmat, nq)
        n_tok = BATCH * nq * TM
        pk, gw, cnt = _route_call(lgt, b_router, nq)
        pk = pk.reshape(TOP_K * n_tok)
        nblk = (cnt[:, 0].astype(jnp.int32) + MOE_ROWS - 1) // MOE_ROWS
        row0 = (jnp.cumsum(nblk) - nblk) * MOE_ROWS
        xs = _dispatch_call(pk, row0, nblk, h2.reshape(n_tok, D_MODEL))
        ys = _moe_call(row0, nblk, xs, w1, w3, w2, l)
        res = _combine_call(pk, row0, xm, mod, gw, final_norm_g[None, :], ys, nq, last)
        if last:
            out = res
        else:
            xx = res
    return out
```

```python
import functools

import jax
import jax.numpy as jnp
from jax import lax
from jax.experimental import pallas as pl
from jax.experimental.pallas import tpu as pltpu

F32 = jnp.float32
BF16 = jnp.bfloat16

D_MODEL = 1024
BATCH = 8
SEQ = 2048
DEPTH = 2
CTX_LEN = 256
TOK = SEQ + CTX_LEN
GRID_W = 64
N_HEADS = 4
HEAD_DIM = 64
MIX_W = N_HEADS * HEAD_DIM
RET_CHUNK = 128
Q_LORA = 256
KV_LORA = 128
MLA_NOPE = 64
MLA_ROPE = 32
MLA_V = 64
WINDOW = 128
N_EXPERTS = 32
N_GROUPS = 4
EXPERTS_PER_GROUP = N_EXPERTS // N_GROUPS
TOP_K = 2
D_EXPERT = 1024
ROPE_BASE = 10000.0
EPS = 1e-6
NEG_INF = -1e30

LANES = 128
TM = 256
NT = TOK // TM
NT_LAT = SEQ // TM
WIN_KEYS = TM + 2 * WINDOW
N_CHUNK = TOK // RET_CHUNK
LOG2E = 1.4426950408889634
MOE_ROWS = 128
MOE_SLOTS = 4
N_IN_P = 3072
MLA_QK = 4 * LANES

C_RET = 0
C_MQ = 1024
C_MKV = 1280
C_GQ = 1408
C_GK = 1664
C_GV = 1920
C_WQ = 2176
C_WK = 2432
C_WV = 2688
C_KR = 2944

VMEM_LIMIT = 56 * 1024 * 1024


def _params(sem, vmem=VMEM_LIMIT):
    return pltpu.CompilerParams(dimension_semantics=sem, vmem_limit_bytes=vmem)


def _const_spec(shape):
    nd = len(shape)
    return pl.BlockSpec(shape, lambda *_: (0,) * nd, pipeline_mode=pl.Buffered(1))


def _bdot(a, b):
    return jnp.dot(a.astype(BF16), b.astype(BF16), preferred_element_type=F32)


def _split(a):
    hi = a.astype(BF16)
    lo = (a - hi.astype(F32)).astype(BF16)
    return hi, lo


def _dot_split_lhs(a, b):
    hi, lo = _split(a)
    return (jnp.dot(hi, b, preferred_element_type=F32)
            + jnp.dot(lo, b, preferred_element_type=F32))


def _dot3(a, b):
    ah, al = _split(a)
    bh, bl = _split(b)
    return (jnp.dot(ah, bh, preferred_element_type=F32)
            + jnp.dot(ah, bl, preferred_element_type=F32)
            + jnp.dot(al, bh, preferred_element_type=F32))


def _modnorm(x, g, sc, sh):
    ms = jnp.mean(x * x, axis=-1, keepdims=True)
    return x * lax.rsqrt(ms + EPS) * g * (1.0 + sc) + sh


def _rmsnorm(x, g):
    ms = jnp.mean(x * x, axis=-1, keepdims=True)
    return x * lax.rsqrt(ms + EPS) * g


def _rope(x, c, s_up, s_dn, half):
    outs = []
    for j in range(x.shape[1] // LANES):
        xc = x[:, j * LANES:(j + 1) * LANES]
        outs.append(xc * c + pltpu.roll(xc, LANES - half, 1) * s_up + pltpu.roll(xc, half, 1) * s_dn)
    return outs[0] if len(outs) == 1 else jnp.concatenate(outs, axis=1)


def _sigmoid(x):
    return 0.5 * jnp.tanh(0.5 * x) + 0.5


def _head_mean(x, gmat):
    return _dot_split_lhs(x, gmat) * (1.0 / HEAD_DIM)


def _ada_kernel(c_ref, w_ref, b_ref, o_ref):
    c = c_ref[...]
    sc = c * jax.nn.sigmoid(c)
    o_ref[0] = _dot3(sc, w_ref[0]) + b_ref[0]


def _ada_call(cvec, w_ada, b_ada):
    tn = 1536
    return pl.pallas_call(
        _ada_kernel,
        grid=(DEPTH, 6 * D_MODEL // tn),
        in_specs=[
            pl.BlockSpec((16, D_MODEL), lambda l, j: (0, 0)),
            pl.BlockSpec((1, D_MODEL, tn), lambda l, j: (l, 0, j)),
            pl.BlockSpec((1, 1, tn), lambda l, j: (l, 0, j)),
        ],
        out_specs=pl.BlockSpec((1, 16, tn), lambda l, j: (l, 0, j)),
        out_shape=jax.ShapeDtypeStruct((DEPTH, 16, 6 * D_MODEL), F32),
        compiler_params=_params(("arbitrary", "arbitrary")),
        name="ada",
    )(cvec, w_ada, b_ada.reshape(DEPTH, 1, 6 * D_MODEL))


def _prep_kernel(x_ref, mod_ref, g1_ref, win_ref, wuq_ref, wuk_ref, wuv_ref, ekr_ref,
                 qng_ref, kvng_ref, gqg_ref, gkg_ref, gmat_ref,
                 c64_ref, u64_ref, d64_ref, cm_ref, um_ref, dm_ref,
                 rq_ref, rkt_ref, rv_ref, rg_ref, mq_ref, mkt_ref, mv_ref,
                 gq_ref, gkt_ref, gv_ref, wq_ref, wkt_ref, wv_ref):
    x = x_ref[0]
    md = mod_ref[0, 0]
    h = _modnorm(x, g1_ref[...], md[1:2], md[0:1])
    p = jnp.dot(h.astype(BF16), win_ref[...], preferred_element_type=F32)

    c64, u64, d64 = c64_ref[...], u64_ref[...], d64_ref[...]
    cm, um, dm = cm_ref[...], um_ref[...], dm_ref[...]
    gmat = gmat_ref[...]
    qk_scale = HEAD_DIM ** -0.5
    rope64 = lambda a: _rope(a, c64, u64, d64, HEAD_DIM // 2)
    ropem = lambda a: _rope(a, cm, um, dm, MLA_ROPE // 2)

    rq_ref[0] = (rope64(p[:, C_RET:C_RET + 256]) * qk_scale).astype(BF16)
    rkt_ref[0] = rope64(p[:, C_RET + 256:C_RET + 512]).T.astype(BF16)
    rv_ref[0] = p[:, C_RET + 512:C_RET + 768].astype(BF16)
    rg_ref[0] = p[:, C_RET + 768:C_RET + 1024].astype(BF16)

    qn = _rmsnorm(p[:, C_MQ:C_MQ + Q_LORA], qng_ref[...])
    q2 = ropem(_bdot(qn, wuq_ref[...])) * ((MLA_NOPE + MLA_ROPE) ** -0.5 * LOG2E)
    mq_ref[0] = q2.astype(BF16)
    kvn = _rmsnorm(p[:, C_MKV:C_MKV + KV_LORA], kvng_ref[...]).astype(BF16)
    k2 = (jnp.dot(kvn, wuk_ref[...], preferred_element_type=F32)
          + _dot_split_lhs(p[:, C_KR:C_KR + LANES], ekr_ref[...]))
    mkt_ref[0] = ropem(k2).T.astype(BF16)
    mv_ref[0] = jnp.dot(kvn, wuv_ref[...], preferred_element_type=F32).astype(BF16)

    gq = p[:, C_GQ:C_GQ + 256]
    gq = gq * lax.rsqrt(_head_mean(gq * gq, gmat) + EPS) * gqg_ref[...]
    gq_ref[0] = (rope64(gq) * (qk_scale * LOG2E)).astype(BF16)
    gk = p[:, C_GK:C_GK + 256]
    gk = gk * lax.rsqrt(_head_mean(gk * gk, gmat) + EPS) * gkg_ref[...]
    gkt_ref[0] = rope64(gk).T.astype(BF16)
    gv_ref[0] = p[:, C_GV:C_GV + 256].astype(BF16)

    wq_ref[0] = (rope64(p[:, C_WQ:C_WQ + 256]) * qk_scale).astype(BF16)
    wkt_ref[0] = rope64(p[:, C_WK:C_WK + 256]).T.astype(BF16)
    wv_ref[0] = p[:, C_WV:C_WV + 256].astype(BF16)


def _prep_call(xx, mod, g1, win_p, wuq, wuk, wuv, ekr, qng, kvng, gqg, gkg, gmat, tabs):
    tok = lambda w: pl.BlockSpec((1, TM, w), lambda b, t: (b, t, 0))
    tokt = lambda w: pl.BlockSpec((1, w, TM), lambda b, t: (b, 0, t))
    tab = pl.BlockSpec((TM, LANES), lambda b, t: (t, 0))
    sd = lambda w: jax.ShapeDtypeStruct((BATCH, TOK, w), BF16)
    sdt = lambda w: jax.ShapeDtypeStruct((BATCH, w, TOK), BF16)
    in_specs = [
        tok(D_MODEL),
        pl.BlockSpec((1, 1, 8, D_MODEL), lambda b, t: (b, t // NT_LAT, 0, 0)),
        _const_spec((1, D_MODEL)),
        _const_spec((D_MODEL, N_IN_P)),
        _const_spec((Q_LORA, MLA_QK)),
        _const_spec((KV_LORA, MLA_QK)),
        _const_spec((KV_LORA, MIX_W)),
        _const_spec((LANES, MLA_QK)),
        _const_spec((1, Q_LORA)),
        _const_spec((1, KV_LORA)),
        _const_spec((1, MIX_W)),
        _const_spec((1, MIX_W)),
        _const_spec((MIX_W, MIX_W)),
        tab, tab, tab, tab, tab, tab,
    ]
    out_specs = [tok(256), tokt(256), tok(256), tok(256),
                 tok(MLA_QK), tokt(MLA_QK), tok(256),
                 tok(256), tokt(256), tok(256),
                 tok(256), tokt(256), tok(256)]
    out_shape = [sd(256), sdt(256), sd(256), sd(256),
                 sd(MLA_QK), sdt(MLA_QK), sd(256),
                 sd(256), sdt(256), sd(256),
                 sd(256), sdt(256), sd(256)]
    return pl.pallas_call(
        _prep_kernel,
        grid=(BATCH, NT),
        in_specs=in_specs,
        out_specs=out_specs,
        out_shape=out_shape,
        compiler_params=_params(("parallel", "parallel")),
        name="prep",
    )(xx, mod, g1, win_p, wuq, wuk, wuv, ekr, qng, kvng, gqg, gkg, gmat, *tabs)


def _ret_kernel(qf_ref, ktf_ref, vf_ref, qb_ref, ktb_ref, vb_ref,
                dmat_ref, xi_ref, zt_ref, gc_ref, of_ref, ob_ref, sf_ref, sb_ref):
    i = pl.program_id(0)

    @pl.when(i == 0)
    def _():
        sf_ref[...] = jnp.zeros_like(sf_ref)
        sb_ref[...] = jnp.zeros_like(sb_ref)

    lane_head = lax.broadcasted_iota(jnp.int32, (RET_CHUNK, MIX_W), 1) // HEAD_DIM
    r_head = lax.broadcasted_iota(jnp.int32, (MIX_W, MIX_W), 0) // HEAD_DIM
    c_head = lax.broadcasted_iota(jnp.int32, (MIX_W, MIX_W), 1) // HEAD_DIM
    block_diag = r_head == c_head

    def one(b, d, q_ref, kt_ref, v_ref, o_ref, s_ref):
        q = q_ref[b].astype(F32)
        kt = kt_ref[b]
        v = v_ref[b]
        s_old = s_ref[b]
        o = _bdot(q * xi_ref[d], s_old)
        for hd in range(N_HEADS):
            qm = jnp.where(lane_head == hd, q, 0.0).astype(BF16)
            inner = jnp.dot(qm, kt, preferred_element_type=F32) * dmat_ref[d, hd]
            oh = jnp.dot(inner.astype(BF16), v, preferred_element_type=F32)
            o = o + jnp.where(lane_head == hd, oh, 0.0)
        o_ref[b] = o
        kz = (kt.astype(F32) * zt_ref[d]).astype(BF16)
        upd = jnp.dot(kz, v, preferred_element_type=F32)
        s_ref[b] = gc_ref[d] * s_old + jnp.where(block_diag, upd, 0.0)

    def body(b, carry):
        one(b, 0, qf_ref, ktf_ref, vf_ref, of_ref, sf_ref)
        one(b, 1, qb_ref, ktb_ref, vb_ref, ob_ref, sb_ref)
        return carry

    lax.fori_loop(0, BATCH, body, 0)


def _ret_call(rq, rkt, rv, dmat, xi, zt, gc):
    cf = lambda i: (i + SEQ // RET_CHUNK) % N_CHUNK
    cb = lambda i: N_CHUNK - 1 - i
    rows = lambda f: pl.BlockSpec((BATCH, RET_CHUNK, MIX_W), lambda i: (0, f(i), 0))
    cols = lambda f: pl.BlockSpec((BATCH, MIX_W, RET_CHUNK), lambda i: (0, 0, f(i)))
    return pl.pallas_call(
        _ret_kernel,
        grid=(N_CHUNK,),
        in_specs=[rows(cf), cols(cf), rows(cf), rows(cb), cols(cb), rows(cb),
                  _const_spec((2, N_HEADS, RET_CHUNK, RET_CHUNK)),
                  _const_spec((2, RET_CHUNK, MIX_W)),
                  _const_spec((2, MIX_W, RET_CHUNK)),
                  _const_spec((2, MIX_W, MIX_W))],
        out_specs=[rows(cf), rows(cb)],
        out_shape=[jax.ShapeDtypeStruct((BATCH, TOK, MIX_W), F32)] * 2,
        scratch_shapes=[pltpu.VMEM((BATCH, MIX_W, MIX_W), F32),
                        pltpu.VMEM((BATCH, MIX_W, MIX_W), F32)],
        compiler_params=_params(("arbitrary",)),
        name="retention",
    )(rq, rkt, rv, rq, rkt, rv, dmat, xi, zt, gc)


def _head_q(q_ref, hd, pair, rows=slice(None)):
    if not pair:
        return q_ref[0, rows, hd * LANES:(hd + 1) * LANES], hd * LANES
    c = hd // 2
    qc = q_ref[0, rows, c * LANES:(c + 1) * LANES]
    half = lax.broadcasted_iota(jnp.int32, qc.shape, 1) // HEAD_DIM
    return jnp.where(half == hd % 2, qc, jnp.zeros_like(qc)), c * LANES


def _dense_kernel(q_ref, kt_ref, v_ref, o_ref, *, pair):
    t = pl.program_id(1)
    lane_head = lax.broadcasted_iota(jnp.int32, (TM, MIX_W), 1) // HEAD_DIM

    def run(k_lo, k_hi):
        v = v_ref[0, k_lo:k_hi, :]
        acc = jnp.zeros((TM, MIX_W), F32)
        for hd in range(N_HEADS):
            qm, r0 = _head_q(q_ref, hd, pair)
            s = jnp.dot(qm, kt_ref[0, r0:r0 + LANES, k_lo:k_hi], preferred_element_type=F32)
            m = jnp.max(s, axis=-1, keepdims=True)
            p = jnp.exp2(s - m)
            l = jnp.sum(p, axis=-1, keepdims=True)
            oh = jnp.dot(p.astype(BF16), v, preferred_element_type=F32)
            acc = jnp.where(lane_head == hd, oh * (1.0 / l), acc)
        o_ref[0] = acc.astype(BF16)

    @pl.when(t < NT_LAT)
    def _():
        run(0, TOK)

    @pl.when(t >= NT_LAT)
    def _():
        run(SEQ, TOK)


def _dense_call(q, kt, v, nq, pair, name):
    wq = q.shape[-1]
    return pl.pallas_call(
        functools.partial(_dense_kernel, pair=pair),
        grid=(BATCH, nq),
        in_specs=[pl.BlockSpec((1, TM, wq), lambda b, t: (b, t, 0)),
                  pl.BlockSpec((1, kt.shape[1], TOK), lambda b, t: (b, 0, 0)),
                  pl.BlockSpec((1, TOK, MIX_W), lambda b, t: (b, 0, 0))],
        out_specs=pl.BlockSpec((1, TM, MIX_W), lambda b, t: (b, t, 0)),
        out_shape=jax.ShapeDtypeStruct((BATCH, nq * TM, MIX_W), BF16),
        compiler_params=_params(("parallel", "arbitrary")),
        name=name,
    )(q, kt, v)


def _win_kernel(sink_ref, q_ref, kt_ref, v_ref, o_ref):
    t = pl.program_id(1)
    lane_head = lax.broadcasted_iota(jnp.int32, (TM, MIX_W), 1) // HEAD_DIM
    v_ctx = v_ref[0, SEQ:TOK, :]

    @pl.when(t < NT_LAT)
    def _():
        start = pl.multiple_of(jnp.clip(t * TM - WINDOW, 0, SEQ - WIN_KEYS), LANES)
        qpos = t * TM + lax.broadcasted_iota(jnp.int32, (TM, WIN_KEYS), 0)
        kpos = start + lax.broadcasted_iota(jnp.int32, (TM, WIN_KEYS), 1)
        valid = jnp.abs(kpos - qpos) <= WINDOW
        v_loc = v_ref[0, pl.ds(start, WIN_KEYS), :]
        acc = jnp.zeros((TM, MIX_W), F32)
        for hd in range(N_HEADS):
            qm, r0 = _head_q(q_ref, hd, True)
            s_loc = jnp.dot(qm, kt_ref[0, r0:r0 + LANES, pl.ds(start, WIN_KEYS)],
                            preferred_element_type=F32)
            s_loc = jnp.where(valid, s_loc, NEG_INF)
            s_ctx = jnp.dot(qm, kt_ref[0, r0:r0 + LANES, SEQ:TOK], preferred_element_type=F32)
            sk = sink_ref[hd]
            m = jnp.maximum(jnp.maximum(jnp.max(s_loc, axis=-1, keepdims=True),
                                        jnp.max(s_ctx, axis=-1, keepdims=True)), sk)
            p_loc = jnp.exp(s_loc - m)
            p_ctx = jnp.exp(s_ctx - m)
            l = (jnp.sum(p_loc, axis=-1, keepdims=True) + jnp.sum(p_ctx, axis=-1, keepdims=True)
                 + jnp.exp(sk - m))
            oh = (jnp.dot(p_loc.astype(BF16), v_loc, preferred_element_type=F32)
                  + jnp.dot(p_ctx.astype(BF16), v_ctx, preferred_element_type=F32))
            acc = jnp.where(lane_head == hd, oh * (1.0 / l), acc)
        o_ref[0] = acc.astype(BF16)

    @pl.when(t >= NT_LAT)
    def _():
        acc = jnp.zeros((TM, MIX_W), F32)
        for hd in range(N_HEADS):
            qm, r0 = _head_q(q_ref, hd, True)
            s = jnp.dot(qm, kt_ref[0, r0:r0 + LANES, SEQ:TOK], preferred_element_type=F32)
            sk = sink_ref[hd]
            m = jnp.maximum(jnp.max(s, axis=-1, keepdims=True), sk)
            p = jnp.exp(s - m)
            l = jnp.sum(p, axis=-1, keepdims=True) + jnp.exp(sk - m)
            oh = jnp.dot(p.astype(BF16), v_ctx, preferred_element_type=F32)
            acc = jnp.where(lane_head == hd, oh * (1.0 / l), acc)
        o_ref[0] = acc.astype(BF16)


def _win_call(sink, q, kt, v, nq):
    return pl.pallas_call(
        _win_kernel,
        grid=(BATCH, nq),
        in_specs=[pl.BlockSpec(memory_space=pltpu.SMEM),
                  pl.BlockSpec((1, TM, MIX_W), lambda b, t: (b, t, 0)),
                  pl.BlockSpec((1, MIX_W, TOK), lambda b, t: (b, 0, 0)),
                  pl.BlockSpec((1, TOK, MIX_W), lambda b, t: (b, 0, 0))],
        out_specs=pl.BlockSpec((1, TM, MIX_W), lambda b, t: (b, t, 0)),
        out_shape=jax.ShapeDtypeStruct((BATCH, nq * TM, MIX_W), BF16),
        compiler_params=_params(("parallel", "arbitrary")),
        name="window",
    )(sink, q, kt, v)


def _merge_kernel(x_ref, mod_ref, g1_ref, g2_ref, of_ref, ob_ref, rg_ref, ym_ref, yg_ref, yw_ref,
                  wg_ref, wb_ref, wo_ref, wr_ref, gmat_ref, xo_ref, h2_ref, lg_ref):
    x = x_ref[0]
    md = mod_ref[0, 0]
    hb = _modnorm(x, g1_ref[...], md[1:2], md[0:1]).astype(BF16)

    gmat = gmat_ref[...]
    o = of_ref[0] + ob_ref[0]
    dlt = o - _head_mean(o, gmat)
    var = _head_mean(dlt * dlt, gmat)
    g = rg_ref[0].astype(F32)
    y_ret = dlt * lax.rsqrt(var + EPS) * (g * _sigmoid(g))

    ys = (y_ret.astype(BF16), ym_ref[0], yg_ref[0], yw_ref[0])
    acc = jnp.zeros((TM, D_MODEL), F32)
    for i in range(4):
        gate = _sigmoid(jnp.dot(hb, wg_ref[:, i * D_MODEL:(i + 1) * D_MODEL],
                                      preferred_element_type=F32))
        acc = acc + gate * jnp.dot(ys[i], wb_ref[i], preferred_element_type=F32)
    out = jnp.dot(acc.astype(BF16), wo_ref[...], preferred_element_type=F32)
    xm = x + md[2:3] * out
    xo_ref[0] = xm
    h2 = _modnorm(xm, g2_ref[...], md[4:5], md[3:4])
    h2_ref[0] = h2
    hh, hl = _split(h2)
    wh, wl = _split(wr_ref[...])
    nt = lambda a, b: lax.dot_general(a, b, (((1,), (1,)), ((), ())), preferred_element_type=F32)
    lg_ref[0] = nt(wh, hh) + nt(wh, hl) + nt(wl, hh)


def _merge_call(xx, mod, g1, g2, of, ob, rg, ym, yg, yw, wg, wb, wo, wr, gmat, nq):
    tok = lambda w: pl.BlockSpec((1, TM, w), lambda b, t: (b, t, 0))
    return pl.pallas_call(
        _merge_kernel,
        grid=(BATCH, nq),
        in_specs=[tok(D_MODEL),
                  pl.BlockSpec((1, 1, 8, D_MODEL), lambda b, t: (b, t // NT_LAT, 0, 0)),
                  _const_spec((1, D_MODEL)), _const_spec((1, D_MODEL)),
                  tok(MIX_W), tok(MIX_W), tok(MIX_W), tok(MIX_W), tok(MIX_W), tok(MIX_W),
                  _const_spec((D_MODEL, 4 * D_MODEL)),
                  _const_spec((4, MIX_W, D_MODEL)),
                  _const_spec((D_MODEL, D_MODEL)),
                  _const_spec((N_EXPERTS, D_MODEL)),
                  _const_spec((MIX_W, MIX_W))],
        out_specs=[tok(D_MODEL), tok(D_MODEL),
                   pl.BlockSpec((1, N_EXPERTS, TM), lambda b, t: (b, 0, t))],
        out_shape=[jax.ShapeDtypeStruct((BATCH, nq * TM, D_MODEL), F32),
                   jax.ShapeDtypeStruct((BATCH, nq * TM, D_MODEL), F32),
                   jax.ShapeDtypeStruct((BATCH, N_EXPERTS, nq * TM), F32)],
        compiler_params=_params(("parallel", "parallel")),
        name="merge",
    )(xx, mod, g1, g2, of, ob, rg, ym, yg, yw, wg, wb, wo, wr, gmat)


RANK_BITS = 20
RANK_MASK = (1 << RANK_BITS) - 1


def _route_kernel(lg_ref, b_ref, tri_ref, pk_ref, gw_ref, cnt_ref, base_ref):
    i = pl.program_id(0)

    @pl.when(i == 0)
    def _():
        base_ref[...] = jnp.zeros_like(base_ref)

    s = jax.nn.sigmoid(lg_ref[0])
    sel = s + b_ref[:, 0:1]
    sub = lax.broadcasted_iota(jnp.int32, (EXPERTS_PER_GROUP, TM), 0)
    best = e1 = e2 = s1 = s2 = None
    for g in range(N_GROUPS):
        rows = slice(g * EXPERTS_PER_GROUP, (g + 1) * EXPERTS_PER_GROUP)
        blk, sb = sel[rows], s[rows]
        m1 = jnp.max(blk, axis=0, keepdims=True)
        i1 = jnp.min(jnp.where(blk == m1, sub, EXPERTS_PER_GROUP), axis=0, keepdims=True)
        hit1 = sub == i1
        blk2 = jnp.where(hit1, -jnp.inf, blk)
        m2 = jnp.max(blk2, axis=0, keepdims=True)
        i2 = jnp.min(jnp.where(blk2 == m2, sub, EXPERTS_PER_GROUP), axis=0, keepdims=True)
        hit2 = sub == i2
        score = m1 + m2
        s1g = jnp.sum(jnp.where(hit1, sb, 0.0), axis=0, keepdims=True)
        s2g = jnp.sum(jnp.where(hit2, sb, 0.0), axis=0, keepdims=True)
        e1g = g * EXPERTS_PER_GROUP + i1
        e2g = g * EXPERTS_PER_GROUP + i2
        if g == 0:
            best, e1, e2, s1, s2 = score, e1g, e2g, s1g, s2g
        else:
            better = score > best
            best = jnp.where(better, score, best)
            e1 = jnp.where(better, e1g, e1)
            e2 = jnp.where(better, e2g, e2)
            s1 = jnp.where(better, s1g, s1)
            s2 = jnp.where(better, s2g, s2)

    eid = lax.broadcasted_iota(jnp.int32, (N_EXPERTS, TM), 0)
    oh1 = eid == e1
    oh2 = eid == e2
    oh = jnp.where(oh1 | oh2, 1.0, 0.0)
    before = jnp.dot(oh.astype(BF16), tri_ref[...], preferred_element_type=F32) + base_ref[:, 0:1]
    r1 = jnp.sum(jnp.where(oh1, before, 0.0), axis=0, keepdims=True).astype(jnp.int32)
    r2 = jnp.sum(jnp.where(oh2, before, 0.0), axis=0, keepdims=True).astype(jnp.int32)
    total = base_ref[...] + jnp.sum(oh, axis=1, keepdims=True)
    base_ref[...] = total
    cnt_ref[...] = total

    pk_ref[...] = jnp.concatenate([(e1 << RANK_BITS) + r1, (e2 << RANK_BITS) + r2], axis=0)
    den = s1 + s2
    row = lax.broadcasted_iota(jnp.int32, (8, TM), 0)
    gw_ref[...] = jnp.where(row == 0, s1 / den, jnp.where(row == 1, s2 / den, 0.0))


def _route_call(lgt, b_router, nq):
    n_tiles = BATCH * nq
    n_tok = n_tiles * TM
    r = jnp.arange(TM)
    tri = (r[:, None] < r[None, :]).astype(BF16)
    bcol = jnp.broadcast_to(b_router.astype(F32)[:, None], (N_EXPERTS, LANES))
    return pl.pallas_call(
        _route_kernel,
        grid=(n_tiles,),
        in_specs=[pl.BlockSpec((1, N_EXPERTS, TM), lambda i: (i // nq, 0, i % nq)),
                  _const_spec((N_EXPERTS, LANES)),
                  _const_spec((TM, TM))],
        out_specs=[pl.BlockSpec((TOP_K, TM), lambda i: (0, i)),
                   pl.BlockSpec((8, TM), lambda i: (0, i)),
                   pl.BlockSpec((N_EXPERTS, LANES), lambda i: (0, 0))],
        out_shape=[jax.ShapeDtypeStruct((TOP_K, n_tok), jnp.int32),
                   jax.ShapeDtypeStruct((8, n_tok), F32),
                   jax.ShapeDtypeStruct((N_EXPERTS, LANES), F32)],
        scratch_shapes=[pltpu.VMEM((N_EXPERTS, LANES), F32)],
        compiler_params=_params(("arbitrary",)),
        name="route",
    )(lgt, bcol, tri)


def _slot_rows(pk, row0):
    e = pk >> RANK_BITS
    base = jnp.sum(jnp.where(e[..., None] == jnp.arange(N_EXPERTS), row0, 0), axis=-1)
    return (base + (pk & RANK_MASK)).astype(jnp.int32).reshape(-1)


def _slot_row(rows_ref, k, n, n_tok):
    return rows_ref[k * n_tok + n]


def _dispatch_kernel(rows_ref, row0_ref, nblk_ref, h_ref, xs_hbm, zbuf, hbuf, sem, zsem, *, n_tok, n_blocks):
    i = pl.program_id(0)
    n_tiles = pl.num_programs(0)

    def zero_copy(row):
        return pltpu.make_async_copy(zbuf, xs_hbm.at[pl.ds(pl.multiple_of(row, MOE_ROWS), MOE_ROWS)], zsem)

    @pl.when(i == 0)
    def _():
        zbuf[...] = jnp.zeros_like(zbuf)
        used = (row0_ref[N_EXPERTS - 1] // MOE_ROWS) + nblk_ref[N_EXPERTS - 1]

        def last_row(e):
            return row0_ref[e] + (nblk_ref[e] - 1) * MOE_ROWS

        def start_e(e, c):
            @pl.when(nblk_ref[e] > 0)
            def _():
                zero_copy(last_row(e)).start()
            return c

        def wait_e(e, c):
            @pl.when(nblk_ref[e] > 0)
            def _():
                zero_copy(last_row(e)).wait()
            return c

        lax.fori_loop(0, N_EXPERTS, start_e, 0)
        lax.fori_loop(used, n_blocks, lambda bk, c: (zero_copy(bk * MOE_ROWS).start(), c)[1], 0)
        lax.fori_loop(0, N_EXPERTS, wait_e, 0)
        lax.fori_loop(used, n_blocks, lambda bk, c: (zero_copy(bk * MOE_ROWS).wait(), c)[1], 0)

    slot = i % 2

    def wait_tile(sl):
        for _ in range(TOP_K):
            pltpu.make_async_copy(hbuf.at[sl], xs_hbm.at[pl.ds(0, TM)], sem.at[sl]).wait()

    @pl.when(i >= 2)
    def _():
        wait_tile(slot)

    hbuf[slot] = h_ref[...]

    def one(j, c):
        n = i * TM + j
        for k in range(TOP_K):
            pltpu.make_async_copy(hbuf.at[slot, pl.ds(j, 1)],
                                  xs_hbm.at[pl.ds(_slot_row(rows_ref, k, n, n_tok), 1)],
                                  sem.at[slot]).start()
        return c

    lax.fori_loop(0, TM, one, 0, unroll=4)

    @pl.when(i == n_tiles - 1)
    def _():
        @pl.when(n_tiles >= 2)
        def _():
            wait_tile(1 - slot)

        wait_tile(slot)


def _dispatch_call(pk, row0, nblk, h2):
    n_tok = h2.shape[0]
    n_blocks = TOP_K * n_tok // MOE_ROWS + N_EXPERTS
    grid_spec = pltpu.PrefetchScalarGridSpec(
        num_scalar_prefetch=3,
        grid=(n_tok // TM,),
        in_specs=[pl.BlockSpec((TM, D_MODEL), lambda i, *_: (i, 0))],
        out_specs=pl.BlockSpec(memory_space=pl.ANY),
        scratch_shapes=[pltpu.VMEM((MOE_ROWS, D_MODEL), F32),
                        pltpu.VMEM((2, TM, D_MODEL), F32),
                        pltpu.SemaphoreType.DMA((2,)),
                        pltpu.SemaphoreType.DMA(())])
    return pl.pallas_call(
        functools.partial(_dispatch_kernel, n_tok=n_tok, n_blocks=n_blocks),
        grid_spec=grid_spec,
        out_shape=jax.ShapeDtypeStruct((n_blocks * MOE_ROWS, D_MODEL), F32),
        compiler_params=_params(("arbitrary",)),
        name="dispatch",
    )(pk, row0, nblk, h2)


def _moe_kernel(row0_ref, nblk_ref, xs_hbm, w1_ref, w3_ref, w2_ref, ys_hbm,
                xbuf, ybuf, w1b, w3b, w2b, isem, osem, *, n_blocks):
    e = pl.program_id(0)
    nb = nblk_ref[e]
    b0 = row0_ref[e] // MOE_ROWS
    used = row0_ref[N_EXPERTS - 1] // MOE_ROWS + nblk_ref[N_EXPERTS - 1]

    def rows(g):
        return pl.ds(pl.multiple_of(g * MOE_ROWS, MOE_ROWS), MOE_ROWS)

    def in_copy(g):
        return pltpu.make_async_copy(xs_hbm.at[rows(g)], xbuf.at[g % MOE_SLOTS], isem.at[g % MOE_SLOTS])

    def out_copy(g):
        return pltpu.make_async_copy(ybuf.at[g % 2], ys_hbm.at[rows(g)], osem.at[g % 2])

    @pl.when(e == 0)
    def _():
        for g in range(MOE_SLOTS - 1):
            @pl.when(g < used)
            def _():
                in_copy(g).start()

    @pl.when(nb > 0)
    def _():
        w1b[...] = w1_ref[0, 0].astype(BF16)
        w3b[...] = w3_ref[0, 0].astype(BF16)
        w2b[...] = w2_ref[0, 0].astype(BF16)

        def block(r, carry):
            g = b0 + r

            @pl.when(g + MOE_SLOTS - 1 < used)
            def _():
                in_copy(g + MOE_SLOTS - 1).start()

            in_copy(g).wait()

            @pl.when(g >= 2)
            def _():
                out_copy(g - 2).wait()

            xb = xbuf[g % MOE_SLOTS].astype(BF16)
            h1 = jnp.dot(xb, w1b[...], preferred_element_type=F32)
            h3 = jnp.dot(xb, w3b[...], preferred_element_type=F32)
            hid = (h1 * _sigmoid(h1) * h3).astype(BF16)
            ybuf[g % 2] = jnp.dot(hid, w2b[...], preferred_element_type=F32)
            out_copy(g).start()
            return carry

        lax.fori_loop(0, nb, block, 0)

    @pl.when(e == N_EXPERTS - 1)
    def _():
        @pl.when(used >= 2)
        def _():
            out_copy(used - 2).wait()

        @pl.when(used >= 1)
        def _():
            out_copy(used - 1).wait()

        ybuf[0] = jnp.zeros((MOE_ROWS, D_MODEL), F32)

        def tail_copy(bk):
            return pltpu.make_async_copy(
                ybuf.at[0], ys_hbm.at[pl.ds(pl.multiple_of(bk * MOE_ROWS, MOE_ROWS), MOE_ROWS)], osem.at[0])

        lax.fori_loop(used, n_blocks, lambda bk, c: (tail_copy(bk).start(), c)[1], 0)
        lax.fori_loop(used, n_blocks, lambda bk, c: (tail_copy(bk).wait(), c)[1], 0)


def _moe_call(row0, nblk, xs, w1, w3, w2, l):
    n_blocks = xs.shape[0] // MOE_ROWS
    wspec = pl.BlockSpec((1, 1, D_MODEL, D_EXPERT), lambda e, *_: (l, e, 0, 0))
    grid_spec = pltpu.PrefetchScalarGridSpec(
        num_scalar_prefetch=2,
        grid=(N_EXPERTS,),
        in_specs=[pl.BlockSpec(memory_space=pl.ANY), wspec, wspec,
                  pl.BlockSpec((1, 1, D_EXPERT, D_MODEL), lambda e, *_: (l, e, 0, 0))],
        out_specs=pl.BlockSpec(memory_space=pl.ANY),
        scratch_shapes=[pltpu.VMEM((MOE_SLOTS, MOE_ROWS, D_MODEL), F32),
                        pltpu.VMEM((2, MOE_ROWS, D_MODEL), F32),
                        pltpu.VMEM((D_MODEL, D_EXPERT), BF16),
                        pltpu.VMEM((D_MODEL, D_EXPERT), BF16),
                        pltpu.VMEM((D_EXPERT, D_MODEL), BF16),
                        pltpu.SemaphoreType.DMA((MOE_SLOTS,)),
                        pltpu.SemaphoreType.DMA((2,))])
    return pl.pallas_call(
        functools.partial(_moe_kernel, n_blocks=n_blocks),
        grid_spec=grid_spec,
        out_shape=jax.ShapeDtypeStruct(xs.shape, F32),
        compiler_params=_params(("arbitrary",)),
        name="moe",
    )(row0, nblk, xs, w1, w3, w2)


def _combine_kernel(rows_ref, x_ref, mod_ref, gw_ref, fg_ref, ys_hbm, o_ref, ybuf, sem,
                    *, n_tok, final):
    i = pl.program_id(0)
    n_tiles = pl.num_programs(0)
    slot = i % 2

    def issue(tile, sl):
        def one(j, c):
            n = tile * TM + j
            for k in range(TOP_K):
                pltpu.make_async_copy(ys_hbm.at[pl.ds(_slot_row(rows_ref, k, n, n_tok), 1)],
                                      ybuf.at[sl, k, pl.ds(j, 1)], sem.at[sl]).start()
            return c

        lax.fori_loop(0, TM, one, 0, unroll=4)

    @pl.when(i == 0)
    def _():
        issue(0, 0)

    @pl.when(i + 1 < n_tiles)
    def _():
        issue(i + 1, 1 - slot)

    for k in range(TOP_K):
        pltpu.make_async_copy(ys_hbm.at[pl.ds(0, TM)], ybuf.at[slot, k], sem.at[slot]).wait()

    md = mod_ref[0, 0]
    gw = gw_ref[...].T
    f = gw[:, 0:1] * ybuf[slot, 0] + gw[:, 1:2] * ybuf[slot, 1]
    xn = x_ref[0] + md[5:6] * f
    o_ref[0] = _rmsnorm(xn, fg_ref[...]) if final else xn


def _combine_call(rows, xm, mod, gw, fg, ys, nq, final):
    n_tok = BATCH * nq * TM
    grid_spec = pltpu.PrefetchScalarGridSpec(
        num_scalar_prefetch=1,
        grid=(BATCH * nq,),
        in_specs=[pl.BlockSpec((1, TM, D_MODEL), lambda i, *_: (i // nq, i % nq, 0)),
                  pl.BlockSpec((1, 1, 8, D_MODEL), lambda i, *_: (i // nq, (i % nq) // NT_LAT, 0, 0)),
                  pl.BlockSpec((8, TM), lambda i, *_: (0, i)),
                  pl.BlockSpec((1, D_MODEL), lambda i, *_: (0, 0)),
                  pl.BlockSpec(memory_space=pl.ANY)],
        out_specs=pl.BlockSpec((1, TM, D_MODEL), lambda i, *_: (i // nq, i % nq, 0)),
        scratch_shapes=[pltpu.VMEM((2, TOP_K, TM, D_MODEL), F32),
                        pltpu.SemaphoreType.DMA((2,))])
    return pl.pallas_call(
        functools.partial(_combine_kernel, n_tok=n_tok, final=final),
        grid_spec=grid_spec,
        out_shape=jax.ShapeDtypeStruct((BATCH, nq * TM, D_MODEL), F32),
        compiler_params=_params(("arbitrary",)),
        name="combine",
    )(rows, xm, mod, gw, fg, ys)


def _rope_tables():
    rows = SEQ // GRID_W
    row = jnp.broadcast_to(jnp.arange(rows)[:, None], (rows, GRID_W)).reshape(-1).astype(F32)
    col = jnp.broadcast_to(jnp.arange(GRID_W)[None, :], (rows, GRID_W)).reshape(-1).astype(F32)

    def cs(rot_dim):
        n_f = rot_dim // 4
        inv = ROPE_BASE ** (-jnp.arange(n_f, dtype=F32) / n_f)
        ang = jnp.concatenate([row[:, None] * inv, col[:, None] * inv], axis=-1)
        return jnp.cos(ang), jnp.sin(ang)

    def with_ctx(c, u, d):
        one = jnp.ones((CTX_LEN, LANES), F32)
        zero = jnp.zeros((CTX_LEN, LANES), F32)
        return (jnp.concatenate([c, one]), jnp.concatenate([u, zero]), jnp.concatenate([d, zero]))

    cos, sin = cs(HEAD_DIM)
    z = jnp.zeros_like(sin)
    t64 = with_ctx(jnp.tile(jnp.concatenate([cos, cos], -1), (1, 2)),
                   jnp.tile(jnp.concatenate([-sin, z], -1), (1, 2)),
                   jnp.tile(jnp.concatenate([z, sin], -1), (1, 2)))
    cos, sin = cs(MLA_ROPE)
    z = jnp.zeros_like(sin)
    one_n = jnp.ones((SEQ, MLA_NOPE), F32)
    zero_n = jnp.zeros((SEQ, MLA_NOPE), F32)
    one_p = jnp.ones((SEQ, LANES - MLA_NOPE - MLA_ROPE), F32)
    zero_p = jnp.zeros((SEQ, LANES - MLA_NOPE - MLA_ROPE), F32)
    tm = with_ctx(jnp.concatenate([one_n, cos, cos, one_p], -1),
                  jnp.concatenate([zero_n, -sin, z, zero_p], -1),
                  jnp.concatenate([zero_n, z, sin, zero_p], -1))
    return t64 + tm


def _ret_tables(decay):
    lg = -jnp.exp(decay.astype(F32))
    idx = jnp.arange(RET_CHUNK, dtype=F32)
    diff = idx[:, None] - idx[None, :]
    fwd = diff >= 0
    bwd = diff < 0
    dm_f = jnp.where(fwd, jnp.exp(lg[0][:, None, None] * jnp.where(fwd, diff, 0.0)), 0.0)
    dm_b = jnp.where(bwd, jnp.exp(lg[1][:, None, None] * jnp.where(bwd, -diff, 0.0)), 0.0)
    dmat = jnp.stack([dm_f, dm_b])
    xi = jnp.stack([jnp.exp(lg[0][:, None] * (idx + 1.0)),
                    jnp.exp(lg[1][:, None] * (RET_CHUNK - idx))])
    zeta = jnp.stack([jnp.exp(lg[0][:, None] * (RET_CHUNK - 1.0 - idx)),
                      jnp.exp(lg[1][:, None] * idx)])
    gch = jnp.exp(lg * RET_CHUNK)
    xi_t = jnp.repeat(jnp.transpose(xi, (0, 2, 1)), HEAD_DIM, axis=2)
    zt_t = jnp.repeat(zeta, HEAD_DIM, axis=1)
    gc_t = jnp.broadcast_to(jnp.repeat(gch, HEAD_DIM, axis=1)[:, :, None], (2, MIX_W, MIX_W))
    return dmat, xi_t, zt_t, gc_t


def _in_proj_columns():
    o_mla = 4 * MIX_W
    o_gqa = o_mla + Q_LORA + KV_LORA + MLA_ROPE
    kv_w = (N_HEADS // 2) * HEAD_DIM
    o_win = o_gqa + MIX_W + 2 * kv_w
    ar = jnp.arange
    dup = jnp.concatenate([ar(HEAD_DIM), ar(HEAD_DIM), HEAD_DIM + ar(HEAD_DIM), HEAD_DIM + ar(HEAD_DIM)])

    def gqa_cols(o):
        return [o + ar(MIX_W), o + MIX_W + dup, o + MIX_W + kv_w + dup]

    return jnp.concatenate([ar(o_mla), o_mla + ar(Q_LORA), o_mla + Q_LORA + ar(KV_LORA)]
                           + gqa_cols(o_gqa) + gqa_cols(o_win)
                           + [o_mla + Q_LORA + KV_LORA + ar(MLA_ROPE)])


def _layer_weights(l, w_in, mla_w_uq, mla_w_ukv):
    cols = _in_proj_columns()
    win_p = jnp.pad(w_in[l][:, cols], ((0, 0), (0, N_IN_P - cols.shape[0]))).astype(BF16)
    uq = mla_w_uq[l].reshape(Q_LORA, N_HEADS, MLA_NOPE + MLA_ROPE)
    wuq = jnp.pad(uq, ((0, 0), (0, 0), (0, LANES - MLA_NOPE - MLA_ROPE))).reshape(Q_LORA, MLA_QK)
    ukv = mla_w_ukv[l].reshape(KV_LORA, N_HEADS, MLA_NOPE + MLA_V)
    wuk = jnp.pad(ukv[:, :, :MLA_NOPE], ((0, 0), (0, 0), (0, LANES - MLA_NOPE))).reshape(KV_LORA, MLA_QK)
    wuv = ukv[:, :, MLA_NOPE:].reshape(KV_LORA, MIX_W)
    return win_p, wuq.astype(BF16), wuk.astype(BF16), wuv.astype(BF16)


def _krope_placement():
    r = jnp.arange(LANES)[:, None]
    c = jnp.arange(MLA_QK)[None, :]
    return ((r < MLA_ROPE) & (c % LANES == MLA_NOPE + r)).astype(BF16)


def _head_block_matrix():
    r = jnp.arange(MIX_W)
    return (r[:, None] // HEAD_DIM == r[None, :] // HEAD_DIM).astype(BF16)


def kernel(x, c, ctx, c_ctx, w_ada, b_ada, norm1_g, norm2_g, w_in, w_gate, w_branch, w_out, ret_decay,
           mla_qn_g, mla_w_uq, mla_kvn_g, mla_w_ukv, gqa_qn_g, gqa_kn_g, win_sink, w_router, b_router,
           w1, w3, w2, final_norm_g):
    cvec = jnp.concatenate([c, c_ctx[None, :], jnp.zeros((7, D_MODEL), F32)], axis=0)
    ada = _ada_call(cvec, w_ada, b_ada)
    tabs = _rope_tables()
    gmat = _head_block_matrix()
    ekr = _krope_placement()
    xx = jnp.concatenate([x, ctx], axis=1)
    out = None
    for l in range(DEPTH):
        last = l == DEPTH - 1
        nq = NT_LAT if last else NT
        m = ada[l].reshape(16, 6, D_MODEL)
        m_lat = m[:BATCH]
        m_ctx = jnp.broadcast_to(m[BATCH][None], (BATCH, 6, D_MODEL))
        mod = jnp.pad(jnp.stack([m_lat, m_ctx], axis=1), ((0, 0), (0, 0), (0, 2), (0, 0)))
        g1 = norm1_g[l][None, :]
        g2 = norm2_g[l][None, :]
        win_p, wuq, wuk, wuv = _layer_weights(l, w_in, mla_w_uq, mla_w_ukv)
        (rq, rkt, rv, rg, mq, mkt, mv, gq, gkt, gv, wq, wkt, wv) = _prep_call(
            xx, mod, g1, win_p, wuq, wuk, wuv, ekr,
            mla_qn_g[l][None, :], mla_kvn_g[l][None, :],
            jnp.tile(gqa_qn_g[l], N_HEADS)[None, :], jnp.tile(gqa_kn_g[l], N_HEADS)[None, :],
            gmat, tabs)
        of, ob = _ret_call(rq, rkt, rv, *_ret_tables(ret_decay[l]))
        ym = _dense_call(mq, mkt, mv, nq, False, "mla")
        yg = _dense_call(gq, gkt, gv, nq, True, "gqa")
        yw = _win_call(win_sink[l], wq, wkt, wv, nq)
        xm, h2, lgt = _merge_call(
            xx, mod, g1, g2, of, ob, rg, ym, yg, yw,
            w_gate[l].astype(BF16), w_branch[l].astype(BF16), w_out[l].astype(BF16), w_router.T, gmat, nq)
        n_tok = BATCH * nq * TM
        pk, gw, cnt = _route_call(lgt, b_router, nq)
        nblk = (cnt[:, 0].astype(jnp.int32) + MOE_ROWS - 1) // MOE_ROWS
        row0 = (jnp.cumsum(nblk) - nblk) * MOE_ROWS
        rows = _slot_rows(pk, row0)
        xs = _dispatch_call(rows, row0, nblk, h2.reshape(n_tok, D_MODEL))
        ys = _moe_call(row0, nblk, xs, w1, w3, w2, l)
        res = _combine_call(rows, xm, mod, gw, final_norm_g[None, :], ys, nq, last)
        if last:
            out = res
        else:
            xx = res
    return out
```

```python
import functools

import jax
import jax.numpy as jnp
from jax import lax
from jax.experimental import pallas as pl
from jax.experimental.pallas import tpu as pltpu

F32 = jnp.float32
BF16 = jnp.bfloat16

D_MODEL = 1024
BATCH = 8
SEQ = 2048
DEPTH = 2
CTX_LEN = 256
TOK = SEQ + CTX_LEN
GRID_W = 64
N_HEADS = 4
HEAD_DIM = 64
MIX_W = N_HEADS * HEAD_DIM
RET_CHUNK = 128
Q_LORA = 256
KV_LORA = 128
MLA_NOPE = 64
MLA_ROPE = 32
MLA_V = 64
WINDOW = 128
N_EXPERTS = 32
N_GROUPS = 4
EXPERTS_PER_GROUP = N_EXPERTS // N_GROUPS
TOP_K = 2
D_EXPERT = 1024
ROPE_BASE = 10000.0
EPS = 1e-6
NEG_INF = -1e30

LANES = 128
TM = 256
NT = TOK // TM
NT_LAT = SEQ // TM
WIN_KEYS = TM + 2 * WINDOW
N_CHUNK = TOK // RET_CHUNK
LOG2E = 1.4426950408889634
MOE_ROWS = 128
MOE_SLOTS = 4
MOE_PIECES = 6
N_IN_P = 3072
MLA_QK = 4 * LANES

C_RET = 0
C_MQ = 1024
C_MKV = 1280
C_GQ = 1408
C_GK = 1664
C_GV = 1920
C_WQ = 2176
C_WK = 2432
C_WV = 2688
C_KR = 2944

VMEM_LIMIT = 56 * 1024 * 1024


def _params(sem, vmem=VMEM_LIMIT):
    return pltpu.CompilerParams(dimension_semantics=sem, vmem_limit_bytes=vmem)


def _const_spec(shape):
    nd = len(shape)
    return pl.BlockSpec(shape, lambda *_: (0,) * nd, pipeline_mode=pl.Buffered(1))


def _bdot(a, b):
    return jnp.dot(a.astype(BF16), b.astype(BF16), preferred_element_type=F32)


def _split(a):
    hi = a.astype(BF16)
    lo = (a - hi.astype(F32)).astype(BF16)
    return hi, lo


def _dot_split_lhs(a, b):
    hi, lo = _split(a)
    return (jnp.dot(hi, b, preferred_element_type=F32)
            + jnp.dot(lo, b, preferred_element_type=F32))


def _dot3(a, b):
    ah, al = _split(a)
    bh, bl = _split(b)
    return (jnp.dot(ah, bh, preferred_element_type=F32)
            + jnp.dot(ah, bl, preferred_element_type=F32)
            + jnp.dot(al, bh, preferred_element_type=F32))


def _modnorm(x, g, sc, sh):
    ms = jnp.mean(x * x, axis=-1, keepdims=True)
    return x * lax.rsqrt(ms + EPS) * g * (1.0 + sc) + sh


def _rmsnorm(x, g):
    ms = jnp.mean(x * x, axis=-1, keepdims=True)
    return x * lax.rsqrt(ms + EPS) * g


def _rope(x, c, s_up, s_dn, half):
    outs = []
    for j in range(x.shape[1] // LANES):
        xc = x[:, j * LANES:(j + 1) * LANES]
        outs.append(xc * c + pltpu.roll(xc, LANES - half, 1) * s_up + pltpu.roll(xc, half, 1) * s_dn)
    return outs[0] if len(outs) == 1 else jnp.concatenate(outs, axis=1)


def _sigmoid(x):
    return 0.5 * jnp.tanh(0.5 * x) + 0.5


def _head_mean(x, gmat):
    return _dot_split_lhs(x, gmat) * (1.0 / HEAD_DIM)


def _ada_kernel(c_ref, w_ref, b_ref, o_ref):
    c = c_ref[...]
    sc = c * jax.nn.sigmoid(c)
    o_ref[0] = _dot3(sc, w_ref[0]) + b_ref[0]


def _ada_call(cvec, w_ada, b_ada):
    tn = 1536
    return pl.pallas_call(
        _ada_kernel,
        grid=(DEPTH, 6 * D_MODEL // tn),
        in_specs=[
            pl.BlockSpec((16, D_MODEL), lambda l, j: (0, 0)),
            pl.BlockSpec((1, D_MODEL, tn), lambda l, j: (l, 0, j)),
            pl.BlockSpec((1, 1, tn), lambda l, j: (l, 0, j)),
        ],
        out_specs=pl.BlockSpec((1, 16, tn), lambda l, j: (l, 0, j)),
        out_shape=jax.ShapeDtypeStruct((DEPTH, 16, 6 * D_MODEL), F32),
        compiler_params=_params(("arbitrary", "arbitrary")),
        name="ada",
    )(cvec, w_ada, b_ada.reshape(DEPTH, 1, 6 * D_MODEL))


def _prep_kernel(x_ref, mod_ref, g1_ref, win_ref, wuq_ref, wuk_ref, wuv_ref, ekr_ref,
                 qng_ref, kvng_ref, gqg_ref, gkg_ref, gmat_ref,
                 c64_ref, u64_ref, d64_ref, cm_ref, um_ref, dm_ref,
                 rq_ref, rkt_ref, rv_ref, rg_ref, mq_ref, mkt_ref, mv_ref,
                 gq_ref, gkt_ref, gv_ref, wq_ref, wkt_ref, wv_ref):
    x = x_ref[0]
    md = mod_ref[0, 0]
    h = _modnorm(x, g1_ref[...], md[1:2], md[0:1])
    p = jnp.dot(h.astype(BF16), win_ref[...], preferred_element_type=F32)

    c64, u64, d64 = c64_ref[...], u64_ref[...], d64_ref[...]
    cm, um, dm = cm_ref[...], um_ref[...], dm_ref[...]
    gmat = gmat_ref[...]
    qk_scale = HEAD_DIM ** -0.5
    rope64 = lambda a: _rope(a, c64, u64, d64, HEAD_DIM // 2)
    ropem = lambda a: _rope(a, cm, um, dm, MLA_ROPE // 2)

    rq_ref[0] = (rope64(p[:, C_RET:C_RET + 256]) * qk_scale).astype(BF16)
    rkt_ref[0] = rope64(p[:, C_RET + 256:C_RET + 512]).T.astype(BF16)
    rv_ref[0] = p[:, C_RET + 512:C_RET + 768].astype(BF16)
    rg_ref[0] = p[:, C_RET + 768:C_RET + 1024].astype(BF16)

    qn = _rmsnorm(p[:, C_MQ:C_MQ + Q_LORA], qng_ref[...])
    q2 = ropem(_bdot(qn, wuq_ref[...])) * ((MLA_NOPE + MLA_ROPE) ** -0.5 * LOG2E)
    mq_ref[0] = q2.astype(BF16)
    kvn = _rmsnorm(p[:, C_MKV:C_MKV + KV_LORA], kvng_ref[...]).astype(BF16)
    k2 = (jnp.dot(kvn, wuk_ref[...], preferred_element_type=F32)
          + _dot_split_lhs(p[:, C_KR:C_KR + LANES], ekr_ref[...]))
    mkt_ref[0] = ropem(k2).T.astype(BF16)
    mv_ref[0] = jnp.dot(kvn, wuv_ref[...], preferred_element_type=F32).astype(BF16)

    gq = p[:, C_GQ:C_GQ + 256]
    gq = gq * lax.rsqrt(_head_mean(gq * gq, gmat) + EPS) * gqg_ref[...]
    gq_ref[0] = (rope64(gq) * (qk_scale * LOG2E)).astype(BF16)
    gk = p[:, C_GK:C_GK + 256]
    gk = gk * lax.rsqrt(_head_mean(gk * gk, gmat) + EPS) * gkg_ref[...]
    gkt_ref[0] = rope64(gk).T.astype(BF16)
    gv_ref[0] = p[:, C_GV:C_GV + 256].astype(BF16)

    wq_ref[0] = (rope64(p[:, C_WQ:C_WQ + 256]) * qk_scale).astype(BF16)
    wkt_ref[0] = rope64(p[:, C_WK:C_WK + 256]).T.astype(BF16)
    wv_ref[0] = p[:, C_WV:C_WV + 256].astype(BF16)


def _prep_call(xx, mod, g1, win_p, wuq, wuk, wuv, ekr, qng, kvng, gqg, gkg, gmat, tabs):
    tok = lambda w: pl.BlockSpec((1, TM, w), lambda b, t: (b, t, 0))
    tokt = lambda w: pl.BlockSpec((1, w, TM), lambda b, t: (b, 0, t))
    tab = pl.BlockSpec((TM, LANES), lambda b, t: (t, 0))
    sd = lambda w: jax.ShapeDtypeStruct((BATCH, TOK, w), BF16)
    sdt = lambda w: jax.ShapeDtypeStruct((BATCH, w, TOK), BF16)
    in_specs = [
        tok(D_MODEL),
        pl.BlockSpec((1, 1, 8, D_MODEL), lambda b, t: (b, t // NT_LAT, 0, 0)),
        _const_spec((1, D_MODEL)),
        _const_spec((D_MODEL, N_IN_P)),
        _const_spec((Q_LORA, MLA_QK)),
        _const_spec((KV_LORA, MLA_QK)),
        _const_spec((KV_LORA, MIX_W)),
        _const_spec((LANES, MLA_QK)),
        _const_spec((1, Q_LORA)),
        _const_spec((1, KV_LORA)),
        _const_spec((1, MIX_W)),
        _const_spec((1, MIX_W)),
        _const_spec((MIX_W, MIX_W)),
        tab, tab, tab, tab, tab, tab,
    ]
    out_specs = [tok(256), tokt(256), tok(256), tok(256),
                 tok(MLA_QK), tokt(MLA_QK), tok(256),
                 tok(256), tokt(256), tok(256),
                 tok(256), tokt(256), tok(256)]
    out_shape = [sd(256), sdt(256), sd(256), sd(256),
                 sd(MLA_QK), sdt(MLA_QK), sd(256),
                 sd(256), sdt(256), sd(256),
                 sd(256), sdt(256), sd(256)]
    return pl.pallas_call(
        _prep_kernel,
        grid=(BATCH, NT),
        in_specs=in_specs,
        out_specs=out_specs,
        out_shape=out_shape,
        compiler_params=_params(("parallel", "parallel")),
        name="prep",
    )(xx, mod, g1, win_p, wuq, wuk, wuv, ekr, qng, kvng, gqg, gkg, gmat, *tabs)


def _ret_kernel(qf_ref, ktf_ref, vf_ref, qb_ref, ktb_ref, vb_ref,
                dmat_ref, xi_ref, zt_ref, gc_ref, of_ref, ob_ref, sf_ref, sb_ref):
    i = pl.program_id(0)

    @pl.when(i == 0)
    def _():
        sf_ref[...] = jnp.zeros_like(sf_ref)
        sb_ref[...] = jnp.zeros_like(sb_ref)

    lane_head = lax.broadcasted_iota(jnp.int32, (RET_CHUNK, MIX_W), 1) // HEAD_DIM
    r_head = lax.broadcasted_iota(jnp.int32, (MIX_W, MIX_W), 0) // HEAD_DIM
    c_head = lax.broadcasted_iota(jnp.int32, (MIX_W, MIX_W), 1) // HEAD_DIM
    block_diag = r_head == c_head

    def one(b, d, q_ref, kt_ref, v_ref, o_ref, s_ref):
        q = q_ref[b].astype(F32)
        kt = kt_ref[b]
        v = v_ref[b]
        s_old = s_ref[b]
        o = _bdot(q * xi_ref[d], s_old)
        for hd in range(N_HEADS):
            qm = jnp.where(lane_head == hd, q, 0.0).astype(BF16)
            inner = jnp.dot(qm, kt, preferred_element_type=F32) * dmat_ref[d, hd]
            oh = jnp.dot(inner.astype(BF16), v, preferred_element_type=F32)
            o = o + jnp.where(lane_head == hd, oh, 0.0)
        o_ref[b] = o
        kz = (kt.astype(F32) * zt_ref[d]).astype(BF16)
        upd = jnp.dot(kz, v, preferred_element_type=F32)
        s_ref[b] = gc_ref[d] * s_old + jnp.where(block_diag, upd, 0.0)

    def body(b, carry):
        one(b, 0, qf_ref, ktf_ref, vf_ref, of_ref, sf_ref)
        one(b, 1, qb_ref, ktb_ref, vb_ref, ob_ref, sb_ref)
        return carry

    lax.fori_loop(0, BATCH, body, 0)


def _ret_call(rq, rkt, rv, dmat, xi, zt, gc):
    cf = lambda i: (i + SEQ // RET_CHUNK) % N_CHUNK
    cb = lambda i: N_CHUNK - 1 - i
    rows = lambda f: pl.BlockSpec((BATCH, RET_CHUNK, MIX_W), lambda i: (0, f(i), 0))
    cols = lambda f: pl.BlockSpec((BATCH, MIX_W, RET_CHUNK), lambda i: (0, 0, f(i)))
    return pl.pallas_call(
        _ret_kernel,
        grid=(N_CHUNK,),
        in_specs=[rows(cf), cols(cf), rows(cf), rows(cb), cols(cb), rows(cb),
                  _const_spec((2, N_HEADS, RET_CHUNK, RET_CHUNK)),
                  _const_spec((2, RET_CHUNK, MIX_W)),
                  _const_spec((2, MIX_W, RET_CHUNK)),
                  _const_spec((2, MIX_W, MIX_W))],
        out_specs=[rows(cf), rows(cb)],
        out_shape=[jax.ShapeDtypeStruct((BATCH, TOK, MIX_W), F32)] * 2,
        scratch_shapes=[pltpu.VMEM((BATCH, MIX_W, MIX_W), F32),
                        pltpu.VMEM((BATCH, MIX_W, MIX_W), F32)],
        compiler_params=_params(("arbitrary",)),
        name="retention",
    )(rq, rkt, rv, rq, rkt, rv, dmat, xi, zt, gc)


def _head_q(q_ref, hd, pair, rows=slice(None)):
    if not pair:
        return q_ref[0, rows, hd * LANES:(hd + 1) * LANES], hd * LANES
    c = hd // 2
    qc = q_ref[0, rows, c * LANES:(c + 1) * LANES]
    half = lax.broadcasted_iota(jnp.int32, qc.shape, 1) // HEAD_DIM
    return jnp.where(half == hd % 2, qc, jnp.zeros_like(qc)), c * LANES


def _dense_kernel(q_ref, kt_ref, v_ref, o_ref, *, pair):
    t = pl.program_id(1)
    lane_head = lax.broadcasted_iota(jnp.int32, (TM, MIX_W), 1) // HEAD_DIM

    def run(k_lo, k_hi):
        v = v_ref[0, k_lo:k_hi, :]
        acc = jnp.zeros((TM, MIX_W), F32)
        for hd in range(N_HEADS):
            qm, r0 = _head_q(q_ref, hd, pair)
            s = jnp.dot(qm, kt_ref[0, r0:r0 + LANES, k_lo:k_hi], preferred_element_type=F32)
            m = jnp.max(s, axis=-1, keepdims=True)
            p = jnp.exp2(s - m)
            l = jnp.sum(p, axis=-1, keepdims=True)
            oh = jnp.dot(p.astype(BF16), v, preferred_element_type=F32)
            acc = jnp.where(lane_head == hd, oh * (1.0 / l), acc)
        o_ref[0] = acc.astype(BF16)

    @pl.when(t < NT_LAT)
    def _():
        run(0, TOK)

    @pl.when(t >= NT_LAT)
    def _():
        run(SEQ, TOK)


def _dense_call(q, kt, v, nq, pair, name):
    wq = q.shape[-1]
    return pl.pallas_call(
        functools.partial(_dense_kernel, pair=pair),
        grid=(BATCH, nq),
        in_specs=[pl.BlockSpec((1, TM, wq), lambda b, t: (b, t, 0)),
                  pl.BlockSpec((1, kt.shape[1], TOK), lambda b, t: (b, 0, 0)),
                  pl.BlockSpec((1, TOK, MIX_W), lambda b, t: (b, 0, 0))],
        out_specs=pl.BlockSpec((1, TM, MIX_W), lambda b, t: (b, t, 0)),
        out_shape=jax.ShapeDtypeStruct((BATCH, nq * TM, MIX_W), BF16),
        compiler_params=_params(("parallel", "arbitrary")),
        name=name,
    )(q, kt, v)


def _win_kernel(sink_ref, q_ref, kt_ref, v_ref, o_ref):
    t = pl.program_id(1)
    lane_head = lax.broadcasted_iota(jnp.int32, (TM, MIX_W), 1) // HEAD_DIM
    v_ctx = v_ref[0, SEQ:TOK, :]

    @pl.when(t < NT_LAT)
    def _():
        start = pl.multiple_of(jnp.clip(t * TM - WINDOW, 0, SEQ - WIN_KEYS), LANES)
        qpos = t * TM + lax.broadcasted_iota(jnp.int32, (TM, WIN_KEYS), 0)
        kpos = start + lax.broadcasted_iota(jnp.int32, (TM, WIN_KEYS), 1)
        valid = jnp.abs(kpos - qpos) <= WINDOW
        v_loc = v_ref[0, pl.ds(start, WIN_KEYS), :]
        acc = jnp.zeros((TM, MIX_W), F32)
        for hd in range(N_HEADS):
            qm, r0 = _head_q(q_ref, hd, True)
            s_loc = jnp.dot(qm, kt_ref[0, r0:r0 + LANES, pl.ds(start, WIN_KEYS)],
                            preferred_element_type=F32)
            s_loc = jnp.where(valid, s_loc, NEG_INF)
            s_ctx = jnp.dot(qm, kt_ref[0, r0:r0 + LANES, SEQ:TOK], preferred_element_type=F32)
            sk = sink_ref[hd]
            m = jnp.maximum(jnp.maximum(jnp.max(s_loc, axis=-1, keepdims=True),
                                        jnp.max(s_ctx, axis=-1, keepdims=True)), sk)
            p_loc = jnp.exp(s_loc - m)
            p_ctx = jnp.exp(s_ctx - m)
            l = (jnp.sum(p_loc, axis=-1, keepdims=True) + jnp.sum(p_ctx, axis=-1, keepdims=True)
                 + jnp.exp(sk - m))
            oh = (jnp.dot(p_loc.astype(BF16), v_loc, preferred_element_type=F32)
                  + jnp.dot(p_ctx.astype(BF16), v_ctx, preferred_element_type=F32))
            acc = jnp.where(lane_head == hd, oh * (1.0 / l), acc)
        o_ref[0] = acc.astype(BF16)

    @pl.when(t >= NT_LAT)
    def _():
        acc = jnp.zeros((TM, MIX_W), F32)
        for hd in range(N_HEADS):
            qm, r0 = _head_q(q_ref, hd, True)
            s = jnp.dot(qm, kt_ref[0, r0:r0 + LANES, SEQ:TOK], preferred_element_type=F32)
            sk = sink_ref[hd]
            m = jnp.maximum(jnp.max(s, axis=-1, keepdims=True), sk)
            p = jnp.exp(s - m)
            l = jnp.sum(p, axis=-1, keepdims=True) + jnp.exp(sk - m)
            oh = jnp.dot(p.astype(BF16), v_ctx, preferred_element_type=F32)
            acc = jnp.where(lane_head == hd, oh * (1.0 / l), acc)
        o_ref[0] = acc.astype(BF16)


def _win_call(sink, q, kt, v, nq):
    return pl.pallas_call(
        _win_kernel,
        grid=(BATCH, nq),
        in_specs=[pl.BlockSpec(memory_space=pltpu.SMEM),
                  pl.BlockSpec((1, TM, MIX_W), lambda b, t: (b, t, 0)),
                  pl.BlockSpec((1, MIX_W, TOK), lambda b, t: (b, 0, 0)),
                  pl.BlockSpec((1, TOK, MIX_W), lambda b, t: (b, 0, 0))],
        out_specs=pl.BlockSpec((1, TM, MIX_W), lambda b, t: (b, t, 0)),
        out_shape=jax.ShapeDtypeStruct((BATCH, nq * TM, MIX_W), BF16),
        compiler_params=_params(("parallel", "arbitrary")),
        name="window",
    )(sink, q, kt, v)


def _merge_kernel(x_ref, mod_ref, g1_ref, g2_ref, of_ref, ob_ref, rg_ref, ym_ref, yg_ref, yw_ref,
                  wg_ref, wb_ref, wo_ref, wr_ref, gmat_ref, xo_ref, h2_ref, lg_ref):
    x = x_ref[0]
    md = mod_ref[0, 0]
    hb = _modnorm(x, g1_ref[...], md[1:2], md[0:1]).astype(BF16)

    gmat = gmat_ref[...]
    o = of_ref[0] + ob_ref[0]
    dlt = o - _head_mean(o, gmat)
    var = _head_mean(dlt * dlt, gmat)
    g = rg_ref[0].astype(F32)
    y_ret = dlt * lax.rsqrt(var + EPS) * (g * _sigmoid(g))

    ys = (y_ret.astype(BF16), ym_ref[0], yg_ref[0], yw_ref[0])
    acc = jnp.zeros((TM, D_MODEL), F32)
    for i in range(4):
        gate = _sigmoid(jnp.dot(hb, wg_ref[:, i * D_MODEL:(i + 1) * D_MODEL],
                                      preferred_element_type=F32))
        acc = acc + gate * jnp.dot(ys[i], wb_ref[i], preferred_element_type=F32)
    out = jnp.dot(acc.astype(BF16), wo_ref[...], preferred_element_type=F32)
    xm = x + md[2:3] * out
    xo_ref[0] = xm
    h2 = _modnorm(xm, g2_ref[...], md[4:5], md[3:4])
    h2_ref[0] = h2
    hh, hl = _split(h2)
    wh, wl = _split(wr_ref[...])
    nt = lambda a, b: lax.dot_general(a, b, (((1,), (1,)), ((), ())), preferred_element_type=F32)
    lg_ref[0] = nt(wh, hh) + nt(wh, hl) + nt(wl, hh)


def _merge_call(xx, mod, g1, g2, of, ob, rg, ym, yg, yw, wg, wb, wo, wr, gmat, nq):
    tok = lambda w: pl.BlockSpec((1, TM, w), lambda b, t: (b, t, 0))
    return pl.pallas_call(
        _merge_kernel,
        grid=(BATCH, nq),
        in_specs=[tok(D_MODEL),
                  pl.BlockSpec((1, 1, 8, D_MODEL), lambda b, t: (b, t // NT_LAT, 0, 0)),
                  _const_spec((1, D_MODEL)), _const_spec((1, D_MODEL)),
                  tok(MIX_W), tok(MIX_W), tok(MIX_W), tok(MIX_W), tok(MIX_W), tok(MIX_W),
                  _const_spec((D_MODEL, 4 * D_MODEL)),
                  _const_spec((4, MIX_W, D_MODEL)),
                  _const_spec((D_MODEL, D_MODEL)),
                  _const_spec((N_EXPERTS, D_MODEL)),
                  _const_spec((MIX_W, MIX_W))],
        out_specs=[tok(D_MODEL), tok(D_MODEL),
                   pl.BlockSpec((1, N_EXPERTS, TM), lambda b, t: (b, 0, t))],
        out_shape=[jax.ShapeDtypeStruct((BATCH, nq * TM, D_MODEL), F32),
                   jax.ShapeDtypeStruct((BATCH, nq * TM, D_MODEL), F32),
                   jax.ShapeDtypeStruct((BATCH, N_EXPERTS, nq * TM), F32)],
        compiler_params=_params(("parallel", "parallel")),
        name="merge",
    )(xx, mod, g1, g2, of, ob, rg, ym, yg, yw, wg, wb, wo, wr, gmat)


RANK_BITS = 20
RANK_MASK = (1 << RANK_BITS) - 1


def _route_kernel(lg_ref, b_ref, tri_ref, pk_ref, gw_ref, cnt_ref, base_ref):
    i = pl.program_id(0)

    @pl.when(i == 0)
    def _():
        base_ref[...] = jnp.zeros_like(base_ref)

    s = jax.nn.sigmoid(lg_ref[0])
    sel = s + b_ref[:, 0:1]
    sub = lax.broadcasted_iota(jnp.int32, (EXPERTS_PER_GROUP, TM), 0)
    best = e1 = e2 = s1 = s2 = None
    for g in range(N_GROUPS):
        rows = slice(g * EXPERTS_PER_GROUP, (g + 1) * EXPERTS_PER_GROUP)
        blk, sb = sel[rows], s[rows]
        m1 = jnp.max(blk, axis=0, keepdims=True)
        i1 = jnp.min(jnp.where(blk == m1, sub, EXPERTS_PER_GROUP), axis=0, keepdims=True)
        hit1 = sub == i1
        blk2 = jnp.where(hit1, -jnp.inf, blk)
        m2 = jnp.max(blk2, axis=0, keepdims=True)
        i2 = jnp.min(jnp.where(blk2 == m2, sub, EXPERTS_PER_GROUP), axis=0, keepdims=True)
        hit2 = sub == i2
        score = m1 + m2
        s1g = jnp.sum(jnp.where(hit1, sb, 0.0), axis=0, keepdims=True)
        s2g = jnp.sum(jnp.where(hit2, sb, 0.0), axis=0, keepdims=True)
        e1g = g * EXPERTS_PER_GROUP + i1
        e2g = g * EXPERTS_PER_GROUP + i2
        if g == 0:
            best, e1, e2, s1, s2 = score, e1g, e2g, s1g, s2g
        else:
            better = score > best
            best = jnp.where(better, score, best)
            e1 = jnp.where(better, e1g, e1)
            e2 = jnp.where(better, e2g, e2)
            s1 = jnp.where(better, s1g, s1)
            s2 = jnp.where(better, s2g, s2)

    eid = lax.broadcasted_iota(jnp.int32, (N_EXPERTS, TM), 0)
    oh1 = eid == e1
    oh2 = eid == e2
    oh = jnp.where(oh1 | oh2, 1.0, 0.0)
    before = jnp.dot(oh.astype(BF16), tri_ref[...], preferred_element_type=F32) + base_ref[:, 0:1]
    r1 = jnp.sum(jnp.where(oh1, before, 0.0), axis=0, keepdims=True).astype(jnp.int32)
    r2 = jnp.sum(jnp.where(oh2, before, 0.0), axis=0, keepdims=True).astype(jnp.int32)
    total = base_ref[...] + jnp.sum(oh, axis=1, keepdims=True)
    base_ref[...] = total
    cnt_ref[...] = total

    pk_ref[...] = jnp.concatenate([(e1 << RANK_BITS) + r1, (e2 << RANK_BITS) + r2], axis=0)
    den = s1 + s2
    row = lax.broadcasted_iota(jnp.int32, (8, TM), 0)
    gw_ref[...] = jnp.where(row == 0, s1 / den, jnp.where(row == 1, s2 / den, 0.0))


def _route_call(lgt, b_router, nq):
    n_tiles = BATCH * nq
    n_tok = n_tiles * TM
    r = jnp.arange(TM)
    tri = (r[:, None] < r[None, :]).astype(BF16)
    bcol = jnp.broadcast_to(b_router.astype(F32)[:, None], (N_EXPERTS, LANES))
    return pl.pallas_call(
        _route_kernel,
        grid=(n_tiles,),
        in_specs=[pl.BlockSpec((1, N_EXPERTS, TM), lambda i: (i // nq, 0, i % nq)),
                  _const_spec((N_EXPERTS, LANES)),
                  _const_spec((TM, TM))],
        out_specs=[pl.BlockSpec((TOP_K, TM), lambda i: (0, i)),
                   pl.BlockSpec((8, TM), lambda i: (0, i)),
                   pl.BlockSpec((N_EXPERTS, LANES), lambda i: (0, 0))],
        out_shape=[jax.ShapeDtypeStruct((TOP_K, n_tok), jnp.int32),
                   jax.ShapeDtypeStruct((8, n_tok), F32),
                   jax.ShapeDtypeStruct((N_EXPERTS, LANES), F32)],
        scratch_shapes=[pltpu.VMEM((N_EXPERTS, LANES), F32)],
        compiler_params=_params(("arbitrary",)),
        name="route",
    )(lgt, bcol, tri)


def _slot_rows(pk, row0):
    e = pk >> RANK_BITS
    base = jnp.sum(jnp.where(e[..., None] == jnp.arange(N_EXPERTS), row0, 0), axis=-1)
    return (base + (pk & RANK_MASK)).astype(jnp.int32).reshape(-1)


def _slot_row(rows_ref, k, n, n_tok):
    return rows_ref[k * n_tok + n]


def _dispatch_kernel(rows_ref, row0_ref, nblk_ref, h_ref, xs_hbm, zbuf, hbuf, sem, zsem, *, n_tok, n_blocks):
    i = pl.program_id(0)
    n_tiles = pl.num_programs(0)

    def zero_copy(row):
        return pltpu.make_async_copy(zbuf, xs_hbm.at[pl.ds(pl.multiple_of(row, MOE_ROWS), MOE_ROWS)], zsem)

    @pl.when(i == 0)
    def _():
        zbuf[...] = jnp.zeros_like(zbuf)
        used = (row0_ref[N_EXPERTS - 1] // MOE_ROWS) + nblk_ref[N_EXPERTS - 1]

        def last_row(e):
            return row0_ref[e] + (nblk_ref[e] - 1) * MOE_ROWS

        def start_e(e, c):
            @pl.when(nblk_ref[e] > 0)
            def _():
                zero_copy(last_row(e)).start()
            return c

        def wait_e(e, c):
            @pl.when(nblk_ref[e] > 0)
            def _():
                zero_copy(last_row(e)).wait()
            return c

        lax.fori_loop(0, N_EXPERTS, start_e, 0)
        lax.fori_loop(used, n_blocks, lambda bk, c: (zero_copy(bk * MOE_ROWS).start(), c)[1], 0)
        lax.fori_loop(0, N_EXPERTS, wait_e, 0)
        lax.fori_loop(used, n_blocks, lambda bk, c: (zero_copy(bk * MOE_ROWS).wait(), c)[1], 0)

    slot = i % 2

    def wait_tile(sl):
        for _ in range(TOP_K):
            pltpu.make_async_copy(hbuf.at[sl], xs_hbm.at[pl.ds(0, TM)], sem.at[sl]).wait()

    @pl.when(i >= 2)
    def _():
        wait_tile(slot)

    hbuf[slot] = h_ref[...]

    def one(j, c):
        n = i * TM + j
        for k in range(TOP_K):
            pltpu.make_async_copy(hbuf.at[slot, pl.ds(j, 1)],
                                  xs_hbm.at[pl.ds(_slot_row(rows_ref, k, n, n_tok), 1)],
                                  sem.at[slot]).start()
        return c

    lax.fori_loop(0, TM, one, 0, unroll=4)

    @pl.when(i == n_tiles - 1)
    def _():
        @pl.when(n_tiles >= 2)
        def _():
            wait_tile(1 - slot)

        wait_tile(slot)


def _dispatch_call(pk, row0, nblk, h2):
    n_tok = h2.shape[0]
    n_blocks = TOP_K * n_tok // MOE_ROWS + N_EXPERTS
    grid_spec = pltpu.PrefetchScalarGridSpec(
        num_scalar_prefetch=3,
        grid=(n_tok // TM,),
        in_specs=[pl.BlockSpec((TM, D_MODEL), lambda i, *_: (i, 0))],
        out_specs=pl.BlockSpec(memory_space=pl.ANY),
        scratch_shapes=[pltpu.VMEM((MOE_ROWS, D_MODEL), F32),
                        pltpu.VMEM((2, TM, D_MODEL), F32),
                        pltpu.SemaphoreType.DMA((2,)),
                        pltpu.SemaphoreType.DMA(())])
    return pl.pallas_call(
        functools.partial(_dispatch_kernel, n_tok=n_tok, n_blocks=n_blocks),
        grid_spec=grid_spec,
        out_shape=jax.ShapeDtypeStruct((n_blocks * MOE_ROWS, D_MODEL), F32),
        compiler_params=_params(("arbitrary",)),
        name="dispatch",
    )(pk, row0, nblk, h2)


def _moe_kernel(row0_ref, nblk_ref, xs_hbm, w1_ref, w3_ref, w2_ref, ys_hbm,
                xbuf, ybuf, w1b, w3b, w2b, isem, osem, *, n_blocks):
    e = pl.program_id(0)
    nb = nblk_ref[e]
    b0 = row0_ref[e] // MOE_ROWS
    used = row0_ref[N_EXPERTS - 1] // MOE_ROWS + nblk_ref[N_EXPERTS - 1]

    def rows(g):
        return pl.ds(pl.multiple_of(g * MOE_ROWS, MOE_ROWS), MOE_ROWS)

    def in_copy(g):
        return pltpu.make_async_copy(xs_hbm.at[rows(g)], xbuf.at[g % MOE_SLOTS], isem.at[g % MOE_SLOTS])

    def out_copy(g):
        return pltpu.make_async_copy(ybuf.at[g % 2], ys_hbm.at[rows(g)], osem.at[g % 2])

    @pl.when(e == 0)
    def _():
        for g in range(MOE_SLOTS - 1):
            @pl.when(g < used)
            def _():
                in_copy(g).start()

    @pl.when(nb > 0)
    def _():
        w1b[...] = w1_ref[0, 0].astype(BF16)
        w3b[...] = w3_ref[0, 0].astype(BF16)
        w2b[...] = w2_ref[0, 0].astype(BF16)

        def block(r, carry):
            g = b0 + r

            @pl.when(g + MOE_SLOTS - 1 < used)
            def _():
                in_copy(g + MOE_SLOTS - 1).start()

            in_copy(g).wait()

            @pl.when(g >= 2)
            def _():
                out_copy(g - 2).wait()

            xb = xbuf[g % MOE_SLOTS].astype(BF16)
            h1 = jnp.dot(xb, w1b[...], preferred_element_type=F32)
            h3 = jnp.dot(xb, w3b[...], preferred_element_type=F32)
            hid = (h1 * _sigmoid(h1) * h3).astype(BF16)
            ybuf[g % 2] = jnp.dot(hid, w2b[...], preferred_element_type=F32)
            out_copy(g).start()
            return carry

        lax.fori_loop(0, nb, block, 0)

    @pl.when(e == N_EXPERTS - 1)
    def _():
        @pl.when(used >= 2)
        def _():
            out_copy(used - 2).wait()

        @pl.when(used >= 1)
        def _():
            out_copy(used - 1).wait()

        ybuf[0] = jnp.zeros((MOE_ROWS, D_MODEL), F32)

        def tail_copy(bk):
            return pltpu.make_async_copy(
                ybuf.at[0], ys_hbm.at[pl.ds(pl.multiple_of(bk * MOE_ROWS, MOE_ROWS), MOE_ROWS)], osem.at[0])

        lax.fori_loop(used, n_blocks, lambda bk, c: (tail_copy(bk).start(), c)[1], 0)
        lax.fori_loop(used, n_blocks, lambda bk, c: (tail_copy(bk).wait(), c)[1], 0)


def _moe_call(row0, nblk, xs, w1, w3, w2, l):
    n_blocks = xs.shape[0] // MOE_ROWS
    wspec = pl.BlockSpec((1, 1, D_MODEL, D_EXPERT), lambda e, *_: (l, e, 0, 0))
    grid_spec = pltpu.PrefetchScalarGridSpec(
        num_scalar_prefetch=2,
        grid=(N_EXPERTS,),
        in_specs=[pl.BlockSpec(memory_space=pl.ANY), wspec, wspec,
                  pl.BlockSpec((1, 1, D_EXPERT, D_MODEL), lambda e, *_: (l, e, 0, 0))],
        out_specs=pl.BlockSpec(memory_space=pl.ANY),
        scratch_shapes=[pltpu.VMEM((MOE_SLOTS, MOE_ROWS, D_MODEL), F32),
                        pltpu.VMEM((2, MOE_ROWS, D_MODEL), F32),
                        pltpu.VMEM((D_MODEL, D_EXPERT), BF16),
                        pltpu.VMEM((D_MODEL, D_EXPERT), BF16),
                        pltpu.VMEM((D_EXPERT, D_MODEL), BF16),
                        pltpu.SemaphoreType.DMA((MOE_SLOTS,)),
                        pltpu.SemaphoreType.DMA((2,))])
    return pl.pallas_call(
        functools.partial(_moe_kernel, n_blocks=n_blocks),
        grid_spec=grid_spec,
        out_shape=jax.ShapeDtypeStruct(xs.shape, F32),
        compiler_params=_params(("arbitrary",)),
        name="moe",
    )(row0, nblk, xs, w1, w3, w2)


def _combine_kernel(rows_ref, x_ref, mod_ref, gw_ref, fg_ref, ys_hbm, o_ref, ybuf, sem,
                    *, n_tok, final):
    i = pl.program_id(0)
    n_tiles = pl.num_programs(0)
    slot = i % 2

    def issue(tile, sl):
        def one(j, c):
            n = tile * TM + j
            for k in range(TOP_K):
                pltpu.make_async_copy(ys_hbm.at[pl.ds(_slot_row(rows_ref, k, n, n_tok), 1)],
                                      ybuf.at[sl, k, pl.ds(j, 1)], sem.at[sl]).start()
            return c

        lax.fori_loop(0, TM, one, 0, unroll=4)

    @pl.when(i == 0)
    def _():
        issue(0, 0)

    @pl.when(i + 1 < n_tiles)
    def _():
        issue(i + 1, 1 - slot)

    for k in range(TOP_K):
        pltpu.make_async_copy(ys_hbm.at[pl.ds(0, TM)], ybuf.at[slot, k], sem.at[slot]).wait()

    md = mod_ref[0, 0]
    gw = gw_ref[...].T
    f = gw[:, 0:1] * ybuf[slot, 0] + gw[:, 1:2] * ybuf[slot, 1]
    xn = x_ref[0] + md[5:6] * f
    o_ref[0] = _rmsnorm(xn, fg_ref[...]) if final else xn


def _combine_call(rows, xm, mod, gw, fg, ys, nq, final):
    n_tok = BATCH * nq * TM
    grid_spec = pltpu.PrefetchScalarGridSpec(
        num_scalar_prefetch=1,
        grid=(BATCH * nq,),
        in_specs=[pl.BlockSpec((1, TM, D_MODEL), lambda i, *_: (i // nq, i % nq, 0)),
                  pl.BlockSpec((1, 1, 8, D_MODEL), lambda i, *_: (i // nq, (i % nq) // NT_LAT, 0, 0)),
                  pl.BlockSpec((8, TM), lambda i, *_: (0, i)),
                  pl.BlockSpec((1, D_MODEL), lambda i, *_: (0, 0)),
                  pl.BlockSpec(memory_space=pl.ANY)],
        out_specs=pl.BlockSpec((1, TM, D_MODEL), lambda i, *_: (i // nq, i % nq, 0)),
        scratch_shapes=[pltpu.VMEM((2, TOP_K, TM, D_MODEL), F32),
                        pltpu.SemaphoreType.DMA((2,))])
    return pl.pallas_call(
        functools.partial(_combine_kernel, n_tok=n_tok, final=final),
        grid_spec=grid_spec,
        out_shape=jax.ShapeDtypeStruct((BATCH, nq * TM, D_MODEL), F32),
        compiler_params=_params(("arbitrary",)),
        name="combine",
    )(rows, xm, mod, gw, fg, ys)


def _fmoe_kernel(rows_ref, row0_ref, nblk_ref, cnt_ref, h_hbm, w1_ref, w3_ref, w2_ref, ys_hbm,
                 tok, xbuf, ybuf, w1b, w3b, w2b, isem, osem, *, n_tok, n_blocks):
    e = pl.program_id(0)
    nb = nblk_ref[e]
    b0 = row0_ref[e] // MOE_ROWS
    used = row0_ref[N_EXPERTS - 1] // MOE_ROWS + nblk_ref[N_EXPERTS - 1]

    def issue_gather(b, lo=0, hi=MOE_ROWS):
        slot = b % MOE_SLOTS
        for j in range(lo, hi):
            pltpu.make_async_copy(h_hbm.at[pl.ds(tok[b * MOE_ROWS + j], 1)], xbuf.at[slot, pl.ds(j, 1)],
                                  isem.at[slot]).start()

    def wait_gather(b):
        slot = b % MOE_SLOTS
        pltpu.make_async_copy(h_hbm.at[pl.ds(0, MOE_ROWS)], xbuf.at[slot], isem.at[slot]).wait()

    def out_copy(b):
        rows = pl.ds(pl.multiple_of(b * MOE_ROWS, MOE_ROWS), MOE_ROWS)
        return pltpu.make_async_copy(ybuf.at[b % 2], ys_hbm.at[rows], osem.at[b % 2])

    @pl.when(e == 0)
    def _():
        def clear(lo, hi):
            def one(p, c):
                tok[p] = 0
                return c

            lax.fori_loop(lo, hi, one, 0)

        clear(used * MOE_ROWS, (used + MOE_SLOTS - 1) * MOE_ROWS)

        def pad_e(x, c):
            clear(row0_ref[x] + cnt_ref[x], row0_ref[x] + nblk_ref[x] * MOE_ROWS)
            return c

        lax.fori_loop(0, N_EXPERTS, pad_e, 0)

        def put(n, c):
            for k in range(TOP_K):
                tok[rows_ref[k * n_tok + n]] = n
            return c

        lax.fori_loop(0, n_tok, put, 0, unroll=8)
        for b in range(MOE_SLOTS - 1):
            issue_gather(b)

    @pl.when(nb > 0)
    def _():
        w1b[...] = w1_ref[0, 0].astype(BF16)
        w3b[...] = w3_ref[0, 0].astype(BF16)
        w2b[...] = w2_ref[0, 0].astype(BF16)

        def block(r, carry):
            g = b0 + r
            wait_gather(g)

            @pl.when(g >= 2)
            def _():
                out_copy(g - 2).wait()

            ahead = g + MOE_SLOTS - 1
            piece = MOE_ROWS // MOE_PIECES
            half = D_EXPERT // 2
            xb = xbuf[g % MOE_SLOTS].astype(BF16)
            hid = []
            for c in range(2):
                cols = slice(c * half, (c + 1) * half)
                issue_gather(ahead, (2 * c) * piece, (2 * c + 1) * piece)
                h1 = jnp.dot(xb, w1b[:, cols], preferred_element_type=F32)
                issue_gather(ahead, (2 * c + 1) * piece, (2 * c + 2) * piece)
                h3 = jnp.dot(xb, w3b[:, cols], preferred_element_type=F32)
                hid.append((h1 * _sigmoid(h1) * h3).astype(BF16))
            hid = jnp.concatenate(hid, axis=1)
            for c in range(2):
                cols = slice(c * half, (c + 1) * half)
                issue_gather(ahead, (4 + c) * piece, (5 + c) * piece if c == 0 else MOE_ROWS)
                ybuf[g % 2, :, cols] = jnp.dot(hid, w2b[:, cols], preferred_element_type=F32)
            out_copy(g).start()
            return carry

        lax.fori_loop(0, nb, block, 0)

    @pl.when(e == N_EXPERTS - 1)
    def _():
        @pl.when(used >= 2)
        def _():
            out_copy(used - 2).wait()

        @pl.when(used >= 1)
        def _():
            out_copy(used - 1).wait()

        for b in range(MOE_SLOTS - 1):
            wait_gather(used + b)

        ybuf[0] = jnp.zeros((MOE_ROWS, D_MODEL), F32)

        def tail_copy(bk):
            return pltpu.make_async_copy(
                ybuf.at[0], ys_hbm.at[pl.ds(pl.multiple_of(bk * MOE_ROWS, MOE_ROWS), MOE_ROWS)], osem.at[0])

        lax.fori_loop(used, n_blocks, lambda bk, c: (tail_copy(bk).start(), c)[1], 0)
        lax.fori_loop(used, n_blocks, lambda bk, c: (tail_copy(bk).wait(), c)[1], 0)


def _fmoe_call(rows, row0, nblk, cnt, h2, w1, w3, w2, l):
    n_tok = h2.shape[0]
    n_blocks = TOP_K * n_tok // MOE_ROWS + N_EXPERTS
    wspec = pl.BlockSpec((1, 1, D_MODEL, D_EXPERT), lambda e, *_: (l, e, 0, 0))
    grid_spec = pltpu.PrefetchScalarGridSpec(
        num_scalar_prefetch=4,
        grid=(N_EXPERTS,),
        in_specs=[pl.BlockSpec(memory_space=pl.ANY), wspec, wspec,
                  pl.BlockSpec((1, 1, D_EXPERT, D_MODEL), lambda e, *_: (l, e, 0, 0))],
        out_specs=pl.BlockSpec(memory_space=pl.ANY),
        scratch_shapes=[pltpu.SMEM(((n_blocks + MOE_SLOTS - 1) * MOE_ROWS,), jnp.int32),
                        pltpu.VMEM((MOE_SLOTS, MOE_ROWS, D_MODEL), F32),
                        pltpu.VMEM((2, MOE_ROWS, D_MODEL), F32),
                        pltpu.VMEM((D_MODEL, D_EXPERT), BF16),
                        pltpu.VMEM((D_MODEL, D_EXPERT), BF16),
                        pltpu.VMEM((D_EXPERT, D_MODEL), BF16),
                        pltpu.SemaphoreType.DMA((MOE_SLOTS,)),
                        pltpu.SemaphoreType.DMA((2,))])
    return pl.pallas_call(
        functools.partial(_fmoe_kernel, n_tok=n_tok, n_blocks=n_blocks),
        grid_spec=grid_spec,
        out_shape=jax.ShapeDtypeStruct((n_blocks * MOE_ROWS, D_MODEL), F32),
        compiler_params=_params(("arbitrary",)),
        name="moe",
    )(rows, row0, nblk, cnt, h2, w1, w3, w2)


def _fcombine_kernel(x_ref, mod_ref, gw_ref, fg_ref, y0_ref, y1_ref, o_ref, *, final):
    md = mod_ref[0, 0]
    gw = gw_ref[...].T
    f = gw[:, 0:1] * y0_ref[...] + gw[:, 1:2] * y1_ref[...]
    xn = x_ref[0] + md[5:6] * f
    o_ref[0] = _rmsnorm(xn, fg_ref[...]) if final else xn


def _fcombine_call(xm, mod, gw, fg, y2, nq, final):
    n_tiles = BATCH * nq
    return pl.pallas_call(
        functools.partial(_fcombine_kernel, final=final),
        grid=(BATCH, nq),
        in_specs=[pl.BlockSpec((1, TM, D_MODEL), lambda b, t: (b, t, 0)),
                  pl.BlockSpec((1, 1, 8, D_MODEL), lambda b, t: (b, t // NT_LAT, 0, 0)),
                  pl.BlockSpec((8, TM), lambda b, t: (0, b * nq + t)),
                  _const_spec((1, D_MODEL)),
                  pl.BlockSpec((TM, D_MODEL), lambda b, t: (b * nq + t, 0)),
                  pl.BlockSpec((TM, D_MODEL), lambda b, t: (n_tiles + b * nq + t, 0))],
        out_specs=pl.BlockSpec((1, TM, D_MODEL), lambda b, t: (b, t, 0)),
        out_shape=jax.ShapeDtypeStruct((BATCH, nq * TM, D_MODEL), F32),
        compiler_params=_params(("parallel", "parallel")),
        name="combine",
    )(xm, mod, gw, fg, y2, y2)


def _rope_tables():
    rows = SEQ // GRID_W
    row = jnp.broadcast_to(jnp.arange(rows)[:, None], (rows, GRID_W)).reshape(-1).astype(F32)
    col = jnp.broadcast_to(jnp.arange(GRID_W)[None, :], (rows, GRID_W)).reshape(-1).astype(F32)

    def cs(rot_dim):
        n_f = rot_dim // 4
        inv = ROPE_BASE ** (-jnp.arange(n_f, dtype=F32) / n_f)
        ang = jnp.concatenate([row[:, None] * inv, col[:, None] * inv], axis=-1)
        return jnp.cos(ang), jnp.sin(ang)

    def with_ctx(c, u, d):
        one = jnp.ones((CTX_LEN, LANES), F32)
        zero = jnp.zeros((CTX_LEN, LANES), F32)
        return (jnp.concatenate([c, one]), jnp.concatenate([u, zero]), jnp.concatenate([d, zero]))

    cos, sin = cs(HEAD_DIM)
    z = jnp.zeros_like(sin)
    t64 = with_ctx(jnp.tile(jnp.concatenate([cos, cos], -1), (1, 2)),
                   jnp.tile(jnp.concatenate([-sin, z], -1), (1, 2)),
                   jnp.tile(jnp.concatenate([z, sin], -1), (1, 2)))
    cos, sin = cs(MLA_ROPE)
    z = jnp.zeros_like(sin)
    one_n = jnp.ones((SEQ, MLA_NOPE), F32)
    zero_n = jnp.zeros((SEQ, MLA_NOPE), F32)
    one_p = jnp.ones((SEQ, LANES - MLA_NOPE - MLA_ROPE), F32)
    zero_p = jnp.zeros((SEQ, LANES - MLA_NOPE - MLA_ROPE), F32)
    tm = with_ctx(jnp.concatenate([one_n, cos, cos, one_p], -1),
                  jnp.concatenate([zero_n, -sin, z, zero_p], -1),
                  jnp.concatenate([zero_n, z, sin, zero_p], -1))
    return t64 + tm


def _ret_tables(decay):
    lg = -jnp.exp(decay.astype(F32))
    idx = jnp.arange(RET_CHUNK, dtype=F32)
    diff = idx[:, None] - idx[None, :]
    fwd = diff >= 0
    bwd = diff < 0
    dm_f = jnp.where(fwd, jnp.exp(lg[0][:, None, None] * jnp.where(fwd, diff, 0.0)), 0.0)
    dm_b = jnp.where(bwd, jnp.exp(lg[1][:, None, None] * jnp.where(bwd, -diff, 0.0)), 0.0)
    dmat = jnp.stack([dm_f, dm_b])
    xi = jnp.stack([jnp.exp(lg[0][:, None] * (idx + 1.0)),
                    jnp.exp(lg[1][:, None] * (RET_CHUNK - idx))])
    zeta = jnp.stack([jnp.exp(lg[0][:, None] * (RET_CHUNK - 1.0 - idx)),
                      jnp.exp(lg[1][:, None] * idx)])
    gch = jnp.exp(lg * RET_CHUNK)
    xi_t = jnp.repeat(jnp.transpose(xi, (0, 2, 1)), HEAD_DIM, axis=2)
    zt_t = jnp.repeat(zeta, HEAD_DIM, axis=1)
    gc_t = jnp.broadcast_to(jnp.repeat(gch, HEAD_DIM, axis=1)[:, :, None], (2, MIX_W, MIX_W))
    return dmat, xi_t, zt_t, gc_t


def _in_proj_columns():
    o_mla = 4 * MIX_W
    o_gqa = o_mla + Q_LORA + KV_LORA + MLA_ROPE
    kv_w = (N_HEADS // 2) * HEAD_DIM
    o_win = o_gqa + MIX_W + 2 * kv_w
    ar = jnp.arange
    dup = jnp.concatenate([ar(HEAD_DIM), ar(HEAD_DIM), HEAD_DIM + ar(HEAD_DIM), HEAD_DIM + ar(HEAD_DIM)])

    def gqa_cols(o):
        return [o + ar(MIX_W), o + MIX_W + dup, o + MIX_W + kv_w + dup]

    return jnp.concatenate([ar(o_mla), o_mla + ar(Q_LORA), o_mla + Q_LORA + ar(KV_LORA)]
                           + gqa_cols(o_gqa) + gqa_cols(o_win)
                           + [o_mla + Q_LORA + KV_LORA + ar(MLA_ROPE)])


def _layer_weights(l, w_in, mla_w_uq, mla_w_ukv):
    cols = _in_proj_columns()
    win_p = jnp.pad(w_in[l][:, cols], ((0, 0), (0, N_IN_P - cols.shape[0]))).astype(BF16)
    uq = mla_w_uq[l].reshape(Q_LORA, N_HEADS, MLA_NOPE + MLA_ROPE)
    wuq = jnp.pad(uq, ((0, 0), (0, 0), (0, LANES - MLA_NOPE - MLA_ROPE))).reshape(Q_LORA, MLA_QK)
    ukv = mla_w_ukv[l].reshape(KV_LORA, N_HEADS, MLA_NOPE + MLA_V)
    wuk = jnp.pad(ukv[:, :, :MLA_NOPE], ((0, 0), (0, 0), (0, LANES - MLA_NOPE))).reshape(KV_LORA, MLA_QK)
    wuv = ukv[:, :, MLA_NOPE:].reshape(KV_LORA, MIX_W)
    return win_p, wuq.astype(BF16), wuk.astype(BF16), wuv.astype(BF16)


def _krope_placement():
    r = jnp.arange(LANES)[:, None]
    c = jnp.arange(MLA_QK)[None, :]
    return ((r < MLA_ROPE) & (c % LANES == MLA_NOPE + r)).astype(BF16)


def _head_block_matrix():
    r = jnp.arange(MIX_W)
    return (r[:, None] // HEAD_DIM == r[None, :] // HEAD_DIM).astype(BF16)


def kernel(x, c, ctx, c_ctx, w_ada, b_ada, norm1_g, norm2_g, w_in, w_gate, w_branch, w_out, ret_decay,
           mla_qn_g, mla_w_uq, mla_kvn_g, mla_w_ukv, gqa_qn_g, gqa_kn_g, win_sink, w_router, b_router,
           w1, w3, w2, final_norm_g):
    cvec = jnp.concatenate([c, c_ctx[None, :], jnp.zeros((7, D_MODEL), F32)], axis=0)
    ada = _ada_call(cvec, w_ada, b_ada)
    tabs = _rope_tables()
    gmat = _head_block_matrix()
    ekr = _krope_placement()
    xx = jnp.concatenate([x, ctx], axis=1)
    out = None
    for l in range(DEPTH):
        last = l == DEPTH - 1
        nq = NT_LAT if last else NT
        m = ada[l].reshape(16, 6, D_MODEL)
        m_lat = m[:BATCH]
        m_ctx = jnp.broadcast_to(m[BATCH][None], (BATCH, 6, D_MODEL))
        mod = jnp.pad(jnp.stack([m_lat, m_ctx], axis=1), ((0, 0), (0, 0), (0, 2), (0, 0)))
        g1 = norm1_g[l][None, :]
        g2 = norm2_g[l][None, :]
        win_p, wuq, wuk, wuv = _layer_weights(l, w_in, mla_w_uq, mla_w_ukv)
        (rq, rkt, rv, rg, mq, mkt, mv, gq, gkt, gv, wq, wkt, wv) = _prep_call(
            xx, mod, g1, win_p, wuq, wuk, wuv, ekr,
            mla_qn_g[l][None, :], mla_kvn_g[l][None, :],
            jnp.tile(gqa_qn_g[l], N_HEADS)[None, :], jnp.tile(gqa_kn_g[l], N_HEADS)[None, :],
            gmat, tabs)
        of, ob = _ret_call(rq, rkt, rv, *_ret_tables(ret_decay[l]))
        ym = _dense_call(mq, mkt, mv, nq, False, "mla")
        yg = _dense_call(gq, gkt, gv, nq, True, "gqa")
        yw = _win_call(win_sink[l], wq, wkt, wv, nq)
        xm, h2, lgt = _merge_call(
            xx, mod, g1, g2, of, ob, rg, ym, yg, yw,
            w_gate[l].astype(BF16), w_branch[l].astype(BF16), w_out[l].astype(BF16), w_router.T, gmat, nq)
        n_tok = BATCH * nq * TM
        pk, gw, cnt = _route_call(lgt, b_router, nq)
        cnt = cnt[:, 0].astype(jnp.int32)
        nblk = (cnt + MOE_ROWS - 1) // MOE_ROWS
        row0 = (jnp.cumsum(nblk) - nblk) * MOE_ROWS
        rows = _slot_rows(pk, row0)
        ys = _fmoe_call(rows, row0, nblk, cnt, h2.reshape(n_tok, D_MODEL), w1, w3, w2, l)
        res = _combine_call(rows, xm, mod, gw, final_norm_g[None, :], ys, nq, last)
        if last:
            out = res
        else:
            xx = res
    return out
```

```python
import functools

import jax
import jax.numpy as jnp
from jax import lax
from jax.experimental import pallas as pl
from jax.experimental.pallas import tpu as pltpu

F32 = jnp.float32
BF16 = jnp.bfloat16

D_MODEL = 1024
BATCH = 8
SEQ = 2048
DEPTH = 2
CTX_LEN = 256
TOK = SEQ + CTX_LEN
GRID_W = 64
N_HEADS = 4
HEAD_DIM = 64
MIX_W = N_HEADS * HEAD_DIM
RET_CHUNK = 128
Q_LORA = 256
KV_LORA = 128
MLA_NOPE = 64
MLA_ROPE = 32
MLA_V = 64
WINDOW = 128
N_EXPERTS = 32
N_GROUPS = 4
EXPERTS_PER_GROUP = N_EXPERTS // N_GROUPS
TOP_K = 2
D_EXPERT = 1024
ROPE_BASE = 10000.0
EPS = 1e-6
NEG_INF = -1e30

LANES = 128
TM = 256
NT = TOK // TM
NT_LAT = SEQ // TM
WIN_KEYS = TM + 2 * WINDOW
N_CHUNK = TOK // RET_CHUNK
LOG2E = 1.4426950408889634
MOE_ROWS = 128
MOE_SLOTS = 6
MOE_PIECES = 6
N_IN_P = 3072
MLA_QK = 4 * LANES

C_RET = 0
C_MQ = 1024
C_MKV = 1280
C_GQ = 1408
C_GK = 1664
C_GV = 1920
C_WQ = 2176
C_WK = 2432
C_WV = 2688
C_KR = 2944

VMEM_LIMIT = 56 * 1024 * 1024


def _params(sem, vmem=VMEM_LIMIT):
    return pltpu.CompilerParams(dimension_semantics=sem, vmem_limit_bytes=vmem)


def _const_spec(shape):
    nd = len(shape)
    return pl.BlockSpec(shape, lambda *_: (0,) * nd, pipeline_mode=pl.Buffered(1))


def _bdot(a, b):
    return jnp.dot(a.astype(BF16), b.astype(BF16), preferred_element_type=F32)


def _split(a):
    hi = a.astype(BF16)
    lo = (a - hi.astype(F32)).astype(BF16)
    return hi, lo


def _dot_split_lhs(a, b):
    hi, lo = _split(a)
    return (jnp.dot(hi, b, preferred_element_type=F32)
            + jnp.dot(lo, b, preferred_element_type=F32))


def _dot3(a, b):
    ah, al = _split(a)
    bh, bl = _split(b)
    return (jnp.dot(ah, bh, preferred_element_type=F32)
            + jnp.dot(ah, bl, preferred_element_type=F32)
            + jnp.dot(al, bh, preferred_element_type=F32))


def _modnorm(x, g, sc, sh):
    ms = jnp.mean(x * x, axis=-1, keepdims=True)
    return x * lax.rsqrt(ms + EPS) * g * (1.0 + sc) + sh


def _rmsnorm(x, g):
    ms = jnp.mean(x * x, axis=-1, keepdims=True)
    return x * lax.rsqrt(ms + EPS) * g


def _rope(x, c, s_up, s_dn, half):
    outs = []
    for j in range(x.shape[1] // LANES):
        xc = x[:, j * LANES:(j + 1) * LANES]
        outs.append(xc * c + pltpu.roll(xc, LANES - half, 1) * s_up + pltpu.roll(xc, half, 1) * s_dn)
    return outs[0] if len(outs) == 1 else jnp.concatenate(outs, axis=1)


def _sigmoid(x):
    return 0.5 * jnp.tanh(0.5 * x) + 0.5


def _head_mean(x, gmat):
    return _dot_split_lhs(x, gmat) * (1.0 / HEAD_DIM)


def _ada_kernel(c_ref, w_ref, b_ref, o_ref):
    c = c_ref[...]
    sc = c * jax.nn.sigmoid(c)
    o_ref[0] = _dot3(sc, w_ref[0]) + b_ref[0]


def _ada_call(cvec, w_ada, b_ada):
    tn = 1536
    return pl.pallas_call(
        _ada_kernel,
        grid=(DEPTH, 6 * D_MODEL // tn),
        in_specs=[
            pl.BlockSpec((16, D_MODEL), lambda l, j: (0, 0)),
            pl.BlockSpec((1, D_MODEL, tn), lambda l, j: (l, 0, j)),
            pl.BlockSpec((1, 1, tn), lambda l, j: (l, 0, j)),
        ],
        out_specs=pl.BlockSpec((1, 16, tn), lambda l, j: (l, 0, j)),
        out_shape=jax.ShapeDtypeStruct((DEPTH, 16, 6 * D_MODEL), F32),
        compiler_params=_params(("arbitrary", "arbitrary")),
        name="ada",
    )(cvec, w_ada, b_ada.reshape(DEPTH, 1, 6 * D_MODEL))


def _tile_x(xl_ref, xc_ref):
    return jnp.where(pl.program_id(1) < NT_LAT, xl_ref[0], xc_ref[0])


def _x_specs(ctx_block):
    return [pl.BlockSpec((1, TM, D_MODEL), lambda b, t: (b, jnp.minimum(t, NT_LAT - 1), 0)),
            pl.BlockSpec((1, TM, D_MODEL), lambda b, t: (b, ctx_block, 0))]


def _prep_kernel(xl_ref, xc_ref, mod_ref, g1_ref, win_ref, wuq_ref, wuk_ref, wuv_ref, ekr_ref,
                 qng_ref, kvng_ref, gqg_ref, gkg_ref, gmat_ref,
                 c64_ref, u64_ref, d64_ref, cm_ref, um_ref, dm_ref,
                 rq_ref, rkt_ref, rv_ref, rg_ref, mq_ref, mkt_ref, mv_ref,
                 gq_ref, gkt_ref, gv_ref, wq_ref, wkt_ref, wv_ref):
    x = _tile_x(xl_ref, xc_ref)
    md = mod_ref[0, 0]
    h = _modnorm(x, g1_ref[...], md[1:2], md[0:1])
    p = jnp.dot(h.astype(BF16), win_ref[...], preferred_element_type=F32)

    c64, u64, d64 = c64_ref[...], u64_ref[...], d64_ref[...]
    cm, um, dm = cm_ref[...], um_ref[...], dm_ref[...]
    gmat = gmat_ref[...]
    qk_scale = HEAD_DIM ** -0.5
    rope64 = lambda a: _rope(a, c64, u64, d64, HEAD_DIM // 2)
    ropem = lambda a: _rope(a, cm, um, dm, MLA_ROPE // 2)

    rq_ref[0] = (rope64(p[:, C_RET:C_RET + 256]) * qk_scale).astype(BF16)
    rkt_ref[0] = rope64(p[:, C_RET + 256:C_RET + 512]).T.astype(BF16)
    rv_ref[0] = p[:, C_RET + 512:C_RET + 768].astype(BF16)
    rg_ref[0] = p[:, C_RET + 768:C_RET + 1024].astype(BF16)

    qn = _rmsnorm(p[:, C_MQ:C_MQ + Q_LORA], qng_ref[...])
    q2 = ropem(_bdot(qn, wuq_ref[...])) * ((MLA_NOPE + MLA_ROPE) ** -0.5 * LOG2E)
    mq_ref[0] = q2.astype(BF16)
    kvn = _rmsnorm(p[:, C_MKV:C_MKV + KV_LORA], kvng_ref[...]).astype(BF16)
    k2 = (jnp.dot(kvn, wuk_ref[...], preferred_element_type=F32)
          + _dot_split_lhs(p[:, C_KR:C_KR + LANES], ekr_ref[...]))
    mkt_ref[0] = ropem(k2).T.astype(BF16)
    mv_ref[0] = jnp.dot(kvn, wuv_ref[...], preferred_element_type=F32).astype(BF16)

    gq = p[:, C_GQ:C_GQ + 256]
    gq = gq * lax.rsqrt(_head_mean(gq * gq, gmat) + EPS) * gqg_ref[...]
    gq_ref[0] = (rope64(gq) * (qk_scale * LOG2E)).astype(BF16)
    gk = p[:, C_GK:C_GK + 256]
    gk = gk * lax.rsqrt(_head_mean(gk * gk, gmat) + EPS) * gkg_ref[...]
    gkt_ref[0] = rope64(gk).T.astype(BF16)
    gv_ref[0] = p[:, C_GV:C_GV + 256].astype(BF16)

    wq_ref[0] = (rope64(p[:, C_WQ:C_WQ + 256]) * qk_scale).astype(BF16)
    wkt_ref[0] = rope64(p[:, C_WK:C_WK + 256]).T.astype(BF16)
    wv_ref[0] = p[:, C_WV:C_WV + 256].astype(BF16)


def _prep_call(xl, xc, ctx_block, mod, g1, win_p, wuq, wuk, wuv, ekr, qng, kvng, gqg, gkg, gmat, tabs):
    tok = lambda w: pl.BlockSpec((1, TM, w), lambda b, t: (b, t, 0))
    tokt = lambda w: pl.BlockSpec((1, w, TM), lambda b, t: (b, 0, t))
    tab = pl.BlockSpec((TM, LANES), lambda b, t: (t, 0))
    sd = lambda w: jax.ShapeDtypeStruct((BATCH, TOK, w), BF16)
    sdt = lambda w: jax.ShapeDtypeStruct((BATCH, w, TOK), BF16)
    in_specs = _x_specs(ctx_block) + [
        pl.BlockSpec((1, 1, 8, D_MODEL), lambda b, t: (b, t // NT_LAT, 0, 0)),
        _const_spec((1, D_MODEL)),
        _const_spec((D_MODEL, N_IN_P)),
        _const_spec((Q_LORA, MLA_QK)),
        _const_spec((KV_LORA, MLA_QK)),
        _const_spec((KV_LORA, MIX_W)),
        _const_spec((LANES, MLA_QK)),
        _const_spec((1, Q_LORA)),
        _const_spec((1, KV_LORA)),
        _const_spec((1, MIX_W)),
        _const_spec((1, MIX_W)),
        _const_spec((MIX_W, MIX_W)),
        tab, tab, tab, tab, tab, tab,
    ]
    out_specs = [tok(256), tokt(256), tok(256), tok(256),
                 tok(MLA_QK), tokt(MLA_QK), tok(256),
                 tok(256), tokt(256), tok(256),
                 tok(256), tokt(256), tok(256)]
    out_shape = [sd(256), sdt(256), sd(256), sd(256),
                 sd(MLA_QK), sdt(MLA_QK), sd(256),
                 sd(256), sdt(256), sd(256),
                 sd(256), sdt(256), sd(256)]
    return pl.pallas_call(
        _prep_kernel,
        grid=(BATCH, NT),
        in_specs=in_specs,
        out_specs=out_specs,
        out_shape=out_shape,
        compiler_params=_params(("parallel", "parallel")),
        name="prep",
    )(xl, xc, mod, g1, win_p, wuq, wuk, wuv, ekr, qng, kvng, gqg, gkg, gmat, *tabs)


def _ret_kernel(qf_ref, ktf_ref, vf_ref, qb_ref, ktb_ref, vb_ref,
                dmat_ref, xi_ref, zt_ref, gc_ref, of_ref, ob_ref, sf_ref, sb_ref):
    i = pl.program_id(0)

    @pl.when(i == 0)
    def _():
        sf_ref[...] = jnp.zeros_like(sf_ref)
        sb_ref[...] = jnp.zeros_like(sb_ref)

    lane_head = lax.broadcasted_iota(jnp.int32, (RET_CHUNK, MIX_W), 1) // HEAD_DIM
    r_head = lax.broadcasted_iota(jnp.int32, (MIX_W, MIX_W), 0) // HEAD_DIM
    c_head = lax.broadcasted_iota(jnp.int32, (MIX_W, MIX_W), 1) // HEAD_DIM
    block_diag = r_head == c_head

    def one(b, d, q_ref, kt_ref, v_ref, o_ref, s_ref):
        q = q_ref[b].astype(F32)
        kt = kt_ref[b]
        v = v_ref[b]
        s_old = s_ref[b]
        o = _bdot(q * xi_ref[d], s_old)
        for hd in range(N_HEADS):
            qm = jnp.where(lane_head == hd, q, 0.0).astype(BF16)
            inner = jnp.dot(qm, kt, preferred_element_type=F32) * dmat_ref[d, hd]
            oh = jnp.dot(inner.astype(BF16), v, preferred_element_type=F32)
            o = o + jnp.where(lane_head == hd, oh, 0.0)
        o_ref[b] = o
        kz = (kt.astype(F32) * zt_ref[d]).astype(BF16)
        upd = jnp.dot(kz, v, preferred_element_type=F32)
        s_ref[b] = gc_ref[d] * s_old + jnp.where(block_diag, upd, 0.0)

    def body(b, carry):
        one(b, 0, qf_ref, ktf_ref, vf_ref, of_ref, sf_ref)
        one(b, 1, qb_ref, ktb_ref, vb_ref, ob_ref, sb_ref)
        return carry

    lax.fori_loop(0, BATCH, body, 0)


def _ret_call(rq, rkt, rv, dmat, xi, zt, gc):
    cf = lambda i: (i + SEQ // RET_CHUNK) % N_CHUNK
    cb = lambda i: N_CHUNK - 1 - i
    rows = lambda f: pl.BlockSpec((BATCH, RET_CHUNK, MIX_W), lambda i: (0, f(i), 0))
    cols = lambda f: pl.BlockSpec((BATCH, MIX_W, RET_CHUNK), lambda i: (0, 0, f(i)))
    return pl.pallas_call(
        _ret_kernel,
        grid=(N_CHUNK,),
        in_specs=[rows(cf), cols(cf), rows(cf), rows(cb), cols(cb), rows(cb),
                  _const_spec((2, N_HEADS, RET_CHUNK, RET_CHUNK)),
                  _const_spec((2, RET_CHUNK, MIX_W)),
                  _const_spec((2, MIX_W, RET_CHUNK)),
                  _const_spec((2, MIX_W, MIX_W))],
        out_specs=[rows(cf), rows(cb)],
        out_shape=[jax.ShapeDtypeStruct((BATCH, TOK, MIX_W), F32)] * 2,
        scratch_shapes=[pltpu.VMEM((BATCH, MIX_W, MIX_W), F32),
                        pltpu.VMEM((BATCH, MIX_W, MIX_W), F32)],
        compiler_params=_params(("arbitrary",)),
        name="retention",
    )(rq, rkt, rv, rq, rkt, rv, dmat, xi, zt, gc)


def _head_q(q_ref, hd, pair, rows=slice(None)):
    if not pair:
        return q_ref[0, rows, hd * LANES:(hd + 1) * LANES], hd * LANES
    c = hd // 2
    qc = q_ref[0, rows, c * LANES:(c + 1) * LANES]
    half = lax.broadcasted_iota(jnp.int32, qc.shape, 1) // HEAD_DIM
    return jnp.where(half == hd % 2, qc, jnp.zeros_like(qc)), c * LANES


def _dense_kernel(q_ref, kt_ref, v_ref, o_ref, *, pair):
    t = pl.program_id(1)
    lane_head = lax.broadcasted_iota(jnp.int32, (TM, MIX_W), 1) // HEAD_DIM

    def run(k_lo, k_hi):
        v = v_ref[0, k_lo:k_hi, :]
        acc = jnp.zeros((TM, MIX_W), F32)
        for hd in range(N_HEADS):
            qm, r0 = _head_q(q_ref, hd, pair)
            s = jnp.dot(qm, kt_ref[0, r0:r0 + LANES, k_lo:k_hi], preferred_element_type=F32)
            m = jnp.max(s, axis=-1, keepdims=True)
            p = jnp.exp2(s - m)
            l = jnp.sum(p, axis=-1, keepdims=True)
            oh = jnp.dot(p.astype(BF16), v, preferred_element_type=F32)
            acc = jnp.where(lane_head == hd, oh * (1.0 / l), acc)
        o_ref[0] = acc.astype(BF16)

    @pl.when(t < NT_LAT)
    def _():
        run(0, TOK)

    @pl.when(t >= NT_LAT)
    def _():
        run(SEQ, TOK)


def _dense_call(q, kt, v, nq, pair, name):
    wq = q.shape[-1]
    return pl.pallas_call(
        functools.partial(_dense_kernel, pair=pair),
        grid=(BATCH, nq),
        in_specs=[pl.BlockSpec((1, TM, wq), lambda b, t: (b, t, 0)),
                  pl.BlockSpec((1, kt.shape[1], TOK), lambda b, t: (b, 0, 0)),
                  pl.BlockSpec((1, TOK, MIX_W), lambda b, t: (b, 0, 0))],
        out_specs=pl.BlockSpec((1, TM, MIX_W), lambda b, t: (b, t, 0)),
        out_shape=jax.ShapeDtypeStruct((BATCH, nq * TM, MIX_W), BF16),
        compiler_params=_params(("parallel", "arbitrary")),
        name=name,
    )(q, kt, v)


def _win_kernel(sink_ref, q_ref, kt_ref, v_ref, o_ref):
    t = pl.program_id(1)
    lane_head = lax.broadcasted_iota(jnp.int32, (TM, MIX_W), 1) // HEAD_DIM
    v_ctx = v_ref[0, SEQ:TOK, :]

    @pl.when(t < NT_LAT)
    def _():
        start = pl.multiple_of(jnp.clip(t * TM - WINDOW, 0, SEQ - WIN_KEYS), LANES)
        qpos = t * TM + lax.broadcasted_iota(jnp.int32, (TM, WIN_KEYS), 0)
        kpos = start + lax.broadcasted_iota(jnp.int32, (TM, WIN_KEYS), 1)
        valid = jnp.abs(kpos - qpos) <= WINDOW
        v_loc = v_ref[0, pl.ds(start, WIN_KEYS), :]
        acc = jnp.zeros((TM, MIX_W), F32)
        for hd in range(N_HEADS):
            qm, r0 = _head_q(q_ref, hd, True)
            s_loc = jnp.dot(qm, kt_ref[0, r0:r0 + LANES, pl.ds(start, WIN_KEYS)],
                            preferred_element_type=F32)
            s_loc = jnp.where(valid, s_loc, NEG_INF)
            s_ctx = jnp.dot(qm, kt_ref[0, r0:r0 + LANES, SEQ:TOK], preferred_element_type=F32)
            sk = sink_ref[hd]
            m = jnp.maximum(jnp.maximum(jnp.max(s_loc, axis=-1, keepdims=True),
                                        jnp.max(s_ctx, axis=-1, keepdims=True)), sk)
            p_loc = jnp.exp(s_loc - m)
            p_ctx = jnp.exp(s_ctx - m)
            l = (jnp.sum(p_loc, axis=-1, keepdims=True) + jnp.sum(p_ctx, axis=-1, keepdims=True)
                 + jnp.exp(sk - m))
            oh = (jnp.dot(p_loc.astype(BF16), v_loc, preferred_element_type=F32)
                  + jnp.dot(p_ctx.astype(BF16), v_ctx, preferred_element_type=F32))
            acc = jnp.where(lane_head == hd, oh * (1.0 / l), acc)
        o_ref[0] = acc.astype(BF16)

    @pl.when(t >= NT_LAT)
    def _():
        acc = jnp.zeros((TM, MIX_W), F32)
        for hd in range(N_HEADS):
            qm, r0 = _head_q(q_ref, hd, True)
            s = jnp.dot(qm, kt_ref[0, r0:r0 + LANES, SEQ:TOK], preferred_element_type=F32)
            sk = sink_ref[hd]
            m = jnp.maximum(jnp.max(s, axis=-1, keepdims=True), sk)
            p = jnp.exp(s - m)
            l = jnp.sum(p, axis=-1, keepdims=True) + jnp.exp(sk - m)
            oh = jnp.dot(p.astype(BF16), v_ctx, preferred_element_type=F32)
            acc = jnp.where(lane_head == hd, oh * (1.0 / l), acc)
        o_ref[0] = acc.astype(BF16)


def _win_call(sink, q, kt, v, nq):
    return pl.pallas_call(
        _win_kernel,
        grid=(BATCH, nq),
        in_specs=[pl.BlockSpec(memory_space=pltpu.SMEM),
                  pl.BlockSpec((1, TM, MIX_W), lambda b, t: (b, t, 0)),
                  pl.BlockSpec((1, MIX_W, TOK), lambda b, t: (b, 0, 0)),
                  pl.BlockSpec((1, TOK, MIX_W), lambda b, t: (b, 0, 0))],
        out_specs=pl.BlockSpec((1, TM, MIX_W), lambda b, t: (b, t, 0)),
        out_shape=jax.ShapeDtypeStruct((BATCH, nq * TM, MIX_W), BF16),
        compiler_params=_params(("parallel", "arbitrary")),
        name="window",
    )(sink, q, kt, v)


def _merge_kernel(xl_ref, xc_ref, mod_ref, g1_ref, g2_ref, of_ref, ob_ref, rg_ref, ym_ref, yg_ref, yw_ref,
                  wg_ref, wb_ref, wo_ref, wr_ref, gmat_ref, xo_ref, h2_ref, lg_ref):
    x = _tile_x(xl_ref, xc_ref)
    md = mod_ref[0, 0]
    hb = _modnorm(x, g1_ref[...], md[1:2], md[0:1]).astype(BF16)

    gmat = gmat_ref[...]
    o = of_ref[0] + ob_ref[0]
    dlt = o - _head_mean(o, gmat)
    var = _head_mean(dlt * dlt, gmat)
    g = rg_ref[0].astype(F32)
    y_ret = dlt * lax.rsqrt(var + EPS) * (g * _sigmoid(g))

    ys = (y_ret.astype(BF16), ym_ref[0], yg_ref[0], yw_ref[0])
    acc = jnp.zeros((TM, D_MODEL), F32)
    for i in range(4):
        gate = _sigmoid(jnp.dot(hb, wg_ref[:, i * D_MODEL:(i + 1) * D_MODEL],
                                      preferred_element_type=F32))
        acc = acc + gate * jnp.dot(ys[i], wb_ref[i], preferred_element_type=F32)
    out = jnp.dot(acc.astype(BF16), wo_ref[...], preferred_element_type=F32)
    xm = x + md[2:3] * out
    xo_ref[0] = xm
    h2 = _modnorm(xm, g2_ref[...], md[4:5], md[3:4])
    h2_ref[0] = h2
    hh, hl = _split(h2)
    wh, wl = _split(wr_ref[...])
    nt = lambda a, b: lax.dot_general(a, b, (((1,), (1,)), ((), ())), preferred_element_type=F32)
    lg_ref[0] = nt(wh, hh) + nt(wh, hl) + nt(wl, hh)


def _merge_call(xl, xc, ctx_block, mod, g1, g2, of, ob, rg, ym, yg, yw, wg, wb, wo, wr, gmat, nq):
    tok = lambda w: pl.BlockSpec((1, TM, w), lambda b, t: (b, t, 0))
    return pl.pallas_call(
        _merge_kernel,
        grid=(BATCH, nq),
        in_specs=_x_specs(ctx_block) + [
                  pl.BlockSpec((1, 1, 8, D_MODEL), lambda b, t: (b, t // NT_LAT, 0, 0)),
                  _const_spec((1, D_MODEL)), _const_spec((1, D_MODEL)),
                  tok(MIX_W), tok(MIX_W), tok(MIX_W), tok(MIX_W), tok(MIX_W), tok(MIX_W),
                  _const_spec((D_MODEL, 4 * D_MODEL)),
                  _const_spec((4, MIX_W, D_MODEL)),
                  _const_spec((D_MODEL, D_MODEL)),
                  _const_spec((N_EXPERTS, D_MODEL)),
                  _const_spec((MIX_W, MIX_W))],
        out_specs=[tok(D_MODEL), tok(D_MODEL),
                   pl.BlockSpec((1, N_EXPERTS, TM), lambda b, t: (b, 0, t))],
        out_shape=[jax.ShapeDtypeStruct((BATCH, nq * TM, D_MODEL), F32),
                   jax.ShapeDtypeStruct((BATCH, nq * TM, D_MODEL), F32),
                   jax.ShapeDtypeStruct((BATCH, N_EXPERTS, nq * TM), F32)],
        compiler_params=_params(("parallel", "parallel")),
        name="merge",
    )(xl, xc, mod, g1, g2, of, ob, rg, ym, yg, yw, wg, wb, wo, wr, gmat)


RANK_BITS = 20
RANK_MASK = (1 << RANK_BITS) - 1


def _route_kernel(lg_ref, b_ref, tri_ref, pk_ref, gw_ref, cnt_ref, base_ref):
    i = pl.program_id(0)

    @pl.when(i == 0)
    def _():
        base_ref[...] = jnp.zeros_like(base_ref)

    s = jax.nn.sigmoid(lg_ref[0])
    sel = s + b_ref[:, 0:1]
    sub = lax.broadcasted_iota(jnp.int32, (EXPERTS_PER_GROUP, TM), 0)
    best = e1 = e2 = s1 = s2 = None
    for g in range(N_GROUPS):
        rows = slice(g * EXPERTS_PER_GROUP, (g + 1) * EXPERTS_PER_GROUP)
        blk, sb = sel[rows], s[rows]
        m1 = jnp.max(blk, axis=0, keepdims=True)
        i1 = jnp.min(jnp.where(blk == m1, sub, EXPERTS_PER_GROUP), axis=0, keepdims=True)
        hit1 = sub == i1
        blk2 = jnp.where(hit1, -jnp.inf, blk)
        m2 = jnp.max(blk2, axis=0, keepdims=True)
        i2 = jnp.min(jnp.where(blk2 == m2, sub, EXPERTS_PER_GROUP), axis=0, keepdims=True)
        hit2 = sub == i2
        score = m1 + m2
        s1g = jnp.sum(jnp.where(hit1, sb, 0.0), axis=0, keepdims=True)
        s2g = jnp.sum(jnp.where(hit2, sb, 0.0), axis=0, keepdims=True)
        e1g = g * EXPERTS_PER_GROUP + i1
        e2g = g * EXPERTS_PER_GROUP + i2
        if g == 0:
            best, e1, e2, s1, s2 = score, e1g, e2g, s1g, s2g
        else:
            better = score > best
            best = jnp.where(better, score, best)
            e1 = jnp.where(better, e1g, e1)
            e2 = jnp.where(better, e2g, e2)
            s1 = jnp.where(better, s1g, s1)
            s2 = jnp.where(better, s2g, s2)

    eid = lax.broadcasted_iota(jnp.int32, (N_EXPERTS, TM), 0)
    oh1 = eid == e1
    oh2 = eid == e2
    oh = jnp.where(oh1 | oh2, 1.0, 0.0)
    before = jnp.dot(oh.astype(BF16), tri_ref[...], preferred_element_type=F32) + base_ref[:, 0:1]
    r1 = jnp.sum(jnp.where(oh1, before, 0.0), axis=0, keepdims=True).astype(jnp.int32)
    r2 = jnp.sum(jnp.where(oh2, before, 0.0), axis=0, keepdims=True).astype(jnp.int32)
    total = base_ref[...] + jnp.sum(oh, axis=1, keepdims=True)
    base_ref[...] = total
    cnt_ref[...] = total

    pk_ref[...] = jnp.concatenate([(e1 << RANK_BITS) + r1, (e2 << RANK_BITS) + r2], axis=0)
    den = s1 + s2
    row = lax.broadcasted_iota(jnp.int32, (8, TM), 0)
    gw_ref[...] = jnp.where(row == 0, s1 / den, jnp.where(row == 1, s2 / den, 0.0))


def _route_call(lgt, b_router, nq):
    n_tiles = BATCH * nq
    n_tok = n_tiles * TM
    r = jnp.arange(TM)
    tri = (r[:, None] < r[None, :]).astype(BF16)
    bcol = jnp.broadcast_to(b_router.astype(F32)[:, None], (N_EXPERTS, LANES))
    return pl.pallas_call(
        _route_kernel,
        grid=(n_tiles,),
        in_specs=[pl.BlockSpec((1, N_EXPERTS, TM), lambda i: (i // nq, 0, i % nq)),
                  _const_spec((N_EXPERTS, LANES)),
                  _const_spec((TM, TM))],
        out_specs=[pl.BlockSpec((TOP_K, TM), lambda i: (0, i)),
                   pl.BlockSpec((8, TM), lambda i: (0, i)),
                   pl.BlockSpec((N_EXPERTS, LANES), lambda i: (0, 0))],
        out_shape=[jax.ShapeDtypeStruct((TOP_K, n_tok), jnp.int32),
                   jax.ShapeDtypeStruct((8, n_tok), F32),
                   jax.ShapeDtypeStruct((N_EXPERTS, LANES), F32)],
        scratch_shapes=[pltpu.VMEM((N_EXPERTS, LANES), F32)],
        compiler_params=_params(("arbitrary",)),
        name="route",
    )(lgt, bcol, tri)


def _slot_rows(pk, row0):
    e = pk >> RANK_BITS
    base = jnp.sum(jnp.where(e[..., None] == jnp.arange(N_EXPERTS), row0, 0), axis=-1)
    return (base + (pk & RANK_MASK)).astype(jnp.int32).reshape(-1)


def _slot_row(rows_ref, k, n, n_tok):
    return rows_ref[k * n_tok + n]


def _combine_kernel(rows_ref, x_ref, mod_ref, gw_ref, fg_ref, ys_hbm, o_ref, ybuf, sem,
                    *, n_tok, final):
    i = pl.program_id(0)
    n_tiles = pl.num_programs(0)
    slot = i % 2

    def issue(tile, sl):
        def one(j, c):
            n = tile * TM + j
            for k in range(TOP_K):
                pltpu.make_async_copy(ys_hbm.at[pl.ds(_slot_row(rows_ref, k, n, n_tok), 1)],
                                      ybuf.at[sl, k, pl.ds(j, 1)], sem.at[sl]).start()
            return c

        lax.fori_loop(0, TM, one, 0, unroll=4)

    @pl.when(i == 0)
    def _():
        issue(0, 0)

    @pl.when(i + 1 < n_tiles)
    def _():
        issue(i + 1, 1 - slot)

    for k in range(TOP_K):
        pltpu.make_async_copy(ys_hbm.at[pl.ds(0, TM)], ybuf.at[slot, k], sem.at[slot]).wait()

    md = mod_ref[0, 0]
    gw = gw_ref[...].T
    f = gw[:, 0:1] * ybuf[slot, 0] + gw[:, 1:2] * ybuf[slot, 1]
    xn = x_ref[0] + md[5:6] * f
    o_ref[0] = _rmsnorm(xn, fg_ref[...]) if final else xn


def _combine_call(rows, xm, mod, gw, fg, ys, nq, final):
    n_tok = BATCH * nq * TM
    grid_spec = pltpu.PrefetchScalarGridSpec(
        num_scalar_prefetch=1,
        grid=(BATCH * nq,),
        in_specs=[pl.BlockSpec((1, TM, D_MODEL), lambda i, *_: (i // nq, i % nq, 0)),
                  pl.BlockSpec((1, 1, 8, D_MODEL), lambda i, *_: (i // nq, (i % nq) // NT_LAT, 0, 0)),
                  pl.BlockSpec((8, TM), lambda i, *_: (0, i)),
                  pl.BlockSpec((1, D_MODEL), lambda i, *_: (0, 0)),
                  pl.BlockSpec(memory_space=pl.ANY)],
        out_specs=pl.BlockSpec((1, TM, D_MODEL), lambda i, *_: (i // nq, i % nq, 0)),
        scratch_shapes=[pltpu.VMEM((2, TOP_K, TM, D_MODEL), F32),
                        pltpu.SemaphoreType.DMA((2,))])
    return pl.pallas_call(
        functools.partial(_combine_kernel, n_tok=n_tok, final=final),
        grid_spec=grid_spec,
        out_shape=jax.ShapeDtypeStruct((BATCH, nq * TM, D_MODEL), F32),
        compiler_params=_params(("arbitrary",)),
        name="combine",
    )(rows, xm, mod, gw, fg, ys)


def _fmoe_kernel(rows_ref, row0_ref, nblk_ref, cnt_ref, h_hbm, w1_ref, w3_ref, w2_ref, ys_hbm,
                 tok, xbuf, ybuf, w1b, w3b, w2b, isem, osem, *, n_tok, n_blocks):
    e = pl.program_id(0)
    nb = nblk_ref[e]
    b0 = row0_ref[e] // MOE_ROWS
    used = row0_ref[N_EXPERTS - 1] // MOE_ROWS + nblk_ref[N_EXPERTS - 1]

    def issue_gather(b, lo=0, hi=MOE_ROWS):
        slot = b % MOE_SLOTS
        for j in range(lo, hi):
            pltpu.make_async_copy(h_hbm.at[pl.ds(tok[b * MOE_ROWS + j], 1)], xbuf.at[slot, pl.ds(j, 1)],
                                  isem.at[slot]).start()

    def wait_gather(b):
        slot = b % MOE_SLOTS
        pltpu.make_async_copy(h_hbm.at[pl.ds(0, MOE_ROWS)], xbuf.at[slot], isem.at[slot]).wait()

    def out_copy(b):
        rows = pl.ds(pl.multiple_of(b * MOE_ROWS, MOE_ROWS), MOE_ROWS)
        return pltpu.make_async_copy(ybuf.at[b % 2], ys_hbm.at[rows], osem.at[b % 2])

    @pl.when(e == 0)
    def _():
        def clear(lo, hi):
            def one(p, c):
                tok[p] = 0
                return c

            lax.fori_loop(lo, hi, one, 0)

        clear(used * MOE_ROWS, (used + MOE_SLOTS - 1) * MOE_ROWS)

        def pad_e(x, c):
            clear(row0_ref[x] + cnt_ref[x], row0_ref[x] + nblk_ref[x] * MOE_ROWS)
            return c

        lax.fori_loop(0, N_EXPERTS, pad_e, 0)

        def put(n, c):
            for k in range(TOP_K):
                tok[rows_ref[k * n_tok + n]] = n
            return c

        lax.fori_loop(0, n_tok, put, 0, unroll=8)
        for b in range(MOE_SLOTS - 1):
            issue_gather(b)

    @pl.when(nb > 0)
    def _():
        w1b[...] = w1_ref[0, 0].astype(BF16)
        w3b[...] = w3_ref[0, 0].astype(BF16)
        w2b[...] = w2_ref[0, 0].astype(BF16)

        def block(r, carry):
            g = b0 + r
            wait_gather(g)

            @pl.when(g >= 2)
            def _():
                out_copy(g - 2).wait()

            ahead = g + MOE_SLOTS - 1
            piece = MOE_ROWS // MOE_PIECES
            half = D_EXPERT // 2
            xb = xbuf[g % MOE_SLOTS].astype(BF16)
            hid = []
            for c in range(2):
                cols = slice(c * half, (c + 1) * half)
                issue_gather(ahead, (2 * c) * piece, (2 * c + 1) * piece)
                h1 = jnp.dot(xb, w1b[:, cols], preferred_element_type=F32)
                issue_gather(ahead, (2 * c + 1) * piece, (2 * c + 2) * piece)
                h3 = jnp.dot(xb, w3b[:, cols], preferred_element_type=F32)
                hid.append((h1 * _sigmoid(h1) * h3).astype(BF16))
            hid = jnp.concatenate(hid, axis=1)
            for c in range(2):
                cols = slice(c * half, (c + 1) * half)
                issue_gather(ahead, (4 + c) * piece, (5 + c) * piece if c == 0 else MOE_ROWS)
                ybuf[g % 2, :, cols] = jnp.dot(hid, w2b[:, cols], preferred_element_type=F32)
            out_copy(g).start()
            return carry

        lax.fori_loop(0, nb, block, 0)

    @pl.when(e == N_EXPERTS - 1)
    def _():
        @pl.when(used >= 2)
        def _():
            out_copy(used - 2).wait()

        @pl.when(used >= 1)
        def _():
            out_copy(used - 1).wait()

        for b in range(MOE_SLOTS - 1):
            wait_gather(used + b)

        ybuf[0] = jnp.zeros((MOE_ROWS, D_MODEL), F32)

        def tail_copy(bk):
            return pltpu.make_async_copy(
                ybuf.at[0], ys_hbm.at[pl.ds(pl.multiple_of(bk * MOE_ROWS, MOE_ROWS), MOE_ROWS)], osem.at[0])

        lax.fori_loop(used, n_blocks, lambda bk, c: (tail_copy(bk).start(), c)[1], 0)
        lax.fori_loop(used, n_blocks, lambda bk, c: (tail_copy(bk).wait(), c)[1], 0)


def _fmoe_call(rows, row0, nblk, cnt, h2, w1, w3, w2, l):
    n_tok = h2.shape[0]
    n_blocks = TOP_K * n_tok // MOE_ROWS + N_EXPERTS
    wspec = pl.BlockSpec((1, 1, D_MODEL, D_EXPERT), lambda e, *_: (l, e, 0, 0))
    grid_spec = pltpu.PrefetchScalarGridSpec(
        num_scalar_prefetch=4,
        grid=(N_EXPERTS,),
        in_specs=[pl.BlockSpec(memory_space=pl.ANY), wspec, wspec,
                  pl.BlockSpec((1, 1, D_EXPERT, D_MODEL), lambda e, *_: (l, e, 0, 0))],
        out_specs=pl.BlockSpec(memory_space=pl.ANY),
        scratch_shapes=[pltpu.SMEM(((n_blocks + MOE_SLOTS - 1) * MOE_ROWS,), jnp.int32),
                        pltpu.VMEM((MOE_SLOTS, MOE_ROWS, D_MODEL), F32),
                        pltpu.VMEM((2, MOE_ROWS, D_MODEL), F32),
                        pltpu.VMEM((D_MODEL, D_EXPERT), BF16),
                        pltpu.VMEM((D_MODEL, D_EXPERT), BF16),
                        pltpu.VMEM((D_EXPERT, D_MODEL), BF16),
                        pltpu.SemaphoreType.DMA((MOE_SLOTS,)),
                        pltpu.SemaphoreType.DMA((2,))])
    return pl.pallas_call(
        functools.partial(_fmoe_kernel, n_tok=n_tok, n_blocks=n_blocks),
        grid_spec=grid_spec,
        out_shape=jax.ShapeDtypeStruct((n_blocks * MOE_ROWS, D_MODEL), F32),
        compiler_params=_params(("arbitrary",)),
        name="moe",
    )(rows, row0, nblk, cnt, h2, w1, w3, w2)


def _rope_tables():
    rows = SEQ // GRID_W
    row = jnp.broadcast_to(jnp.arange(rows)[:, None], (rows, GRID_W)).reshape(-1).astype(F32)
    col = jnp.broadcast_to(jnp.arange(GRID_W)[None, :], (rows, GRID_W)).reshape(-1).astype(F32)

    def cs(rot_dim):
        n_f = rot_dim // 4
        inv = ROPE_BASE ** (-jnp.arange(n_f, dtype=F32) / n_f)
        ang = jnp.concatenate([row[:, None] * inv, col[:, None] * inv], axis=-1)
        return jnp.cos(ang), jnp.sin(ang)

    def with_ctx(c, u, d):
        one = jnp.ones((CTX_LEN, LANES), F32)
        zero = jnp.zeros((CTX_LEN, LANES), F32)
        return (jnp.concatenate([c, one]), jnp.concatenate([u, zero]), jnp.concatenate([d, zero]))

    cos, sin = cs(HEAD_DIM)
    z = jnp.zeros_like(sin)
    t64 = with_ctx(jnp.tile(jnp.concatenate([cos, cos], -1), (1, 2)),
                   jnp.tile(jnp.concatenate([-sin, z], -1), (1, 2)),
                   jnp.tile(jnp.concatenate([z, sin], -1), (1, 2)))
    cos, sin = cs(MLA_ROPE)
    z = jnp.zeros_like(sin)
    one_n = jnp.ones((SEQ, MLA_NOPE), F32)
    zero_n = jnp.zeros((SEQ, MLA_NOPE), F32)
    one_p = jnp.ones((SEQ, LANES - MLA_NOPE - MLA_ROPE), F32)
    zero_p = jnp.zeros((SEQ, LANES - MLA_NOPE - MLA_ROPE), F32)
    tm = with_ctx(jnp.concatenate([one_n, cos, cos, one_p], -1),
                  jnp.concatenate([zero_n, -sin, z, zero_p], -1),
                  jnp.concatenate([zero_n, z, sin, zero_p], -1))
    return t64 + tm


def _ret_tables(decay):
    lg = -jnp.exp(decay.astype(F32))
    idx = jnp.arange(RET_CHUNK, dtype=F32)
    diff = idx[:, None] - idx[None, :]
    fwd = diff >= 0
    bwd = diff < 0
    dm_f = jnp.where(fwd, jnp.exp(lg[0][:, None, None] * jnp.where(fwd, diff, 0.0)), 0.0)
    dm_b = jnp.where(bwd, jnp.exp(lg[1][:, None, None] * jnp.where(bwd, -diff, 0.0)), 0.0)
    dmat = jnp.stack([dm_f, dm_b])
    xi = jnp.stack([jnp.exp(lg[0][:, None] * (idx + 1.0)),
                    jnp.exp(lg[1][:, None] * (RET_CHUNK - idx))])
    zeta = jnp.stack([jnp.exp(lg[0][:, None] * (RET_CHUNK - 1.0 - idx)),
                      jnp.exp(lg[1][:, None] * idx)])
    gch = jnp.exp(lg * RET_CHUNK)
    xi_t = jnp.repeat(jnp.transpose(xi, (0, 2, 1)), HEAD_DIM, axis=2)
    zt_t = jnp.repeat(zeta, HEAD_DIM, axis=1)
    gc_t = jnp.broadcast_to(jnp.repeat(gch, HEAD_DIM, axis=1)[:, :, None], (2, MIX_W, MIX_W))
    return dmat, xi_t, zt_t, gc_t


def _in_proj_columns():
    o_mla = 4 * MIX_W
    o_gqa = o_mla + Q_LORA + KV_LORA + MLA_ROPE
    kv_w = (N_HEADS // 2) * HEAD_DIM
    o_win = o_gqa + MIX_W + 2 * kv_w
    ar = jnp.arange
    dup = jnp.concatenate([ar(HEAD_DIM), ar(HEAD_DIM), HEAD_DIM + ar(HEAD_DIM), HEAD_DIM + ar(HEAD_DIM)])

    def gqa_cols(o):
        return [o + ar(MIX_W), o + MIX_W + dup, o + MIX_W + kv_w + dup]

    return jnp.concatenate([ar(o_mla), o_mla + ar(Q_LORA), o_mla + Q_LORA + ar(KV_LORA)]
                           + gqa_cols(o_gqa) + gqa_cols(o_win)
                           + [o_mla + Q_LORA + KV_LORA + ar(MLA_ROPE)])


def _layer_weights(l, w_in, mla_w_uq, mla_w_ukv):
    cols = _in_proj_columns()
    win_p = jnp.pad(w_in[l][:, cols], ((0, 0), (0, N_IN_P - cols.shape[0]))).astype(BF16)
    uq = mla_w_uq[l].reshape(Q_LORA, N_HEADS, MLA_NOPE + MLA_ROPE)
    wuq = jnp.pad(uq, ((0, 0), (0, 0), (0, LANES - MLA_NOPE - MLA_ROPE))).reshape(Q_LORA, MLA_QK)
    ukv = mla_w_ukv[l].reshape(KV_LORA, N_HEADS, MLA_NOPE + MLA_V)
    wuk = jnp.pad(ukv[:, :, :MLA_NOPE], ((0, 0), (0, 0), (0, LANES - MLA_NOPE))).reshape(KV_LORA, MLA_QK)
    wuv = ukv[:, :, MLA_NOPE:].reshape(KV_LORA, MIX_W)
    return win_p, wuq.astype(BF16), wuk.astype(BF16), wuv.astype(BF16)


def _krope_placement():
    r = jnp.arange(LANES)[:, None]
    c = jnp.arange(MLA_QK)[None, :]
    return ((r < MLA_ROPE) & (c % LANES == MLA_NOPE + r)).astype(BF16)


def _head_block_matrix():
    r = jnp.arange(MIX_W)
    return (r[:, None] // HEAD_DIM == r[None, :] // HEAD_DIM).astype(BF16)


def kernel(x, c, ctx, c_ctx, w_ada, b_ada, norm1_g, norm2_g, w_in, w_gate, w_branch, w_out, ret_decay,
           mla_qn_g, mla_w_uq, mla_kvn_g, mla_w_ukv, gqa_qn_g, gqa_kn_g, win_sink, w_router, b_router,
           w1, w3, w2, final_norm_g):
    cvec = jnp.concatenate([c, c_ctx[None, :], jnp.zeros((7, D_MODEL), F32)], axis=0)
    ada = _ada_call(cvec, w_ada, b_ada)
    tabs = _rope_tables()
    gmat = _head_block_matrix()
    ekr = _krope_placement()
    stream = (x, ctx, 0)
    out = None
    for l in range(DEPTH):
        last = l == DEPTH - 1
        nq = NT_LAT if last else NT
        m = ada[l].reshape(16, 6, D_MODEL)
        m_lat = m[:BATCH]
        m_ctx = jnp.broadcast_to(m[BATCH][None], (BATCH, 6, D_MODEL))
        mod = jnp.pad(jnp.stack([m_lat, m_ctx], axis=1), ((0, 0), (0, 0), (0, 2), (0, 0)))
        g1 = norm1_g[l][None, :]
        g2 = norm2_g[l][None, :]
        win_p, wuq, wuk, wuv = _layer_weights(l, w_in, mla_w_uq, mla_w_ukv)
        (rq, rkt, rv, rg, mq, mkt, mv, gq, gkt, gv, wq, wkt, wv) = _prep_call(
            *stream, mod, g1, win_p, wuq, wuk, wuv, ekr,
            mla_qn_g[l][None, :], mla_kvn_g[l][None, :],
            jnp.tile(gqa_qn_g[l], N_HEADS)[None, :], jnp.tile(gqa_kn_g[l], N_HEADS)[None, :],
            gmat, tabs)
        of, ob = _ret_call(rq, rkt, rv, *_ret_tables(ret_decay[l]))
        ym = _dense_call(mq, mkt, mv, nq, False, "mla")
        yg = _dense_call(gq, gkt, gv, nq, True, "gqa")
        yw = _win_call(win_sink[l], wq, wkt, wv, nq)
        xm, h2, lgt = _merge_call(
            *stream, mod, g1, g2, of, ob, rg, ym, yg, yw,
            w_gate[l].astype(BF16), w_branch[l].astype(BF16), w_out[l].astype(BF16), w_router.T, gmat, nq)
        n_tok = BATCH * nq * TM
        pk, gw, cnt = _route_call(lgt, b_router, nq)
        cnt = cnt[:, 0].astype(jnp.int32)
        nblk = (cnt + MOE_ROWS - 1) // MOE_ROWS
        row0 = (jnp.cumsum(nblk) - nblk) * MOE_ROWS
        rows = _slot_rows(pk, row0)
        ys = _fmoe_call(rows, row0, nblk, cnt, h2.reshape(n_tok, D_MODEL), w1, w3, w2, l)
        res = _combine_call(rows, xm, mod, gw, final_norm_g[None, :], ys, nq, last)
        if last:
            out = res
        else:
            stream = (res, res, NT_LAT)
    return out
```

```python
import functools

import jax
import jax.numpy as jnp
from jax import lax
from jax.experimental import pallas as pl
from jax.experimental.pallas import tpu as pltpu

F32 = jnp.float32
BF16 = jnp.bfloat16

D_MODEL = 1024
BATCH = 8
SEQ = 2048
DEPTH = 2
CTX_LEN = 256
TOK = SEQ + CTX_LEN
GRID_W = 64
N_HEADS = 4
HEAD_DIM = 64
MIX_W = N_HEADS * HEAD_DIM
RET_CHUNK = 128
Q_LORA = 256
KV_LORA = 128
MLA_NOPE = 64
MLA_ROPE = 32
MLA_V = 64
WINDOW = 128
N_EXPERTS = 32
N_GROUPS = 4
EXPERTS_PER_GROUP = N_EXPERTS // N_GROUPS
TOP_K = 2
D_EXPERT = 1024
ROPE_BASE = 10000.0
EPS = 1e-6
NEG_INF = -1e30

LANES = 128
TM = 256
NT = TOK // TM
NT_LAT = SEQ // TM
WIN_KEYS = TM + 2 * WINDOW
N_CHUNK = TOK // RET_CHUNK
LOG2E = 1.4426950408889634
MOE_ROWS = 128
MOE_SLOTS = 8
MOE_PIECES = 6
N_IN_P = 3072
MLA_QK = 4 * LANES

C_RET = 0
C_MQ = 1024
C_MKV = 1280
C_GQ = 1408
C_GK = 1664
C_GV = 1920
C_WQ = 2176
C_WK = 2432
C_WV = 2688
C_KR = 2944

VMEM_LIMIT = 56 * 1024 * 1024


def _params(sem, vmem=VMEM_LIMIT):
    return pltpu.CompilerParams(dimension_semantics=sem, vmem_limit_bytes=vmem)


def _const_spec(shape):
    nd = len(shape)
    return pl.BlockSpec(shape, lambda *_: (0,) * nd, pipeline_mode=pl.Buffered(1))


def _bdot(a, b):
    return jnp.dot(a.astype(BF16), b.astype(BF16), preferred_element_type=F32)


def _split(a):
    hi = a.astype(BF16)
    lo = (a - hi.astype(F32)).astype(BF16)
    return hi, lo


def _dot_split_lhs(a, b):
    hi, lo = _split(a)
    return (jnp.dot(hi, b, preferred_element_type=F32)
            + jnp.dot(lo, b, preferred_element_type=F32))


def _dot3(a, b):
    ah, al = _split(a)
    bh, bl = _split(b)
    return (jnp.dot(ah, bh, preferred_element_type=F32)
            + jnp.dot(ah, bl, preferred_element_type=F32)
            + jnp.dot(al, bh, preferred_element_type=F32))


def _modnorm(x, g, sc, sh):
    ms = jnp.mean(x * x, axis=-1, keepdims=True)
    return x * lax.rsqrt(ms + EPS) * g * (1.0 + sc) + sh


def _rmsnorm(x, g):
    ms = jnp.mean(x * x, axis=-1, keepdims=True)
    return x * lax.rsqrt(ms + EPS) * g


def _rope(x, c, s_up, s_dn, half):
    outs = []
    for j in range(x.shape[1] // LANES):
        xc = x[:, j * LANES:(j + 1) * LANES]
        outs.append(xc * c + pltpu.roll(xc, LANES - half, 1) * s_up + pltpu.roll(xc, half, 1) * s_dn)
    return outs[0] if len(outs) == 1 else jnp.concatenate(outs, axis=1)


def _sigmoid(x):
    return 0.5 * jnp.tanh(0.5 * x) + 0.5


def _head_mean(x, gmat):
    return _dot_split_lhs(x, gmat) * (1.0 / HEAD_DIM)


def _ada_kernel(c_ref, w_ref, b_ref, o_ref):
    c = c_ref[...]
    sc = c * jax.nn.sigmoid(c)
    o_ref[0] = _dot3(sc, w_ref[0]) + b_ref[0]


def _ada_call(cvec, w_ada, b_ada):
    tn = 1536
    return pl.pallas_call(
        _ada_kernel,
        grid=(DEPTH, 6 * D_MODEL // tn),
        in_specs=[
            pl.BlockSpec((16, D_MODEL), lambda l, j: (0, 0)),
            pl.BlockSpec((1, D_MODEL, tn), lambda l, j: (l, 0, j)),
            pl.BlockSpec((1, 1, tn), lambda l, j: (l, 0, j)),
        ],
        out_specs=pl.BlockSpec((1, 16, tn), lambda l, j: (l, 0, j)),
        out_shape=jax.ShapeDtypeStruct((DEPTH, 16, 6 * D_MODEL), F32),
        compiler_params=_params(("arbitrary", "arbitrary")),
        name="ada",
    )(cvec, w_ada, b_ada.reshape(DEPTH, 1, 6 * D_MODEL))


def _tile_x(xl_ref, xc_ref):
    return jnp.where(pl.program_id(1) < NT_LAT, xl_ref[0], xc_ref[0])


def _x_specs(ctx_block):
    return [pl.BlockSpec((1, TM, D_MODEL), lambda b, t: (b, jnp.minimum(t, NT_LAT - 1), 0)),
            pl.BlockSpec((1, TM, D_MODEL), lambda b, t: (b, ctx_block, 0))]


def _prep_kernel(xl_ref, xc_ref, mod_ref, g1_ref, win_ref, wuq_ref, wuk_ref, wuv_ref, ekr_ref,
                 qng_ref, kvng_ref, gqg_ref, gkg_ref, gmat_ref,
                 c64_ref, u64_ref, d64_ref, cm_ref, um_ref, dm_ref,
                 rq_ref, rkt_ref, rv_ref, rg_ref, mq_ref, mkt_ref, mv_ref,
                 gq_ref, gkt_ref, gv_ref, wq_ref, wkt_ref, wv_ref):
    x = _tile_x(xl_ref, xc_ref)
    md = mod_ref[0, 0]
    h = _modnorm(x, g1_ref[...], md[1:2], md[0:1])
    p = jnp.dot(h.astype(BF16), win_ref[...], preferred_element_type=F32)

    c64, u64, d64 = c64_ref[...], u64_ref[...], d64_ref[...]
    cm, um, dm = cm_ref[...], um_ref[...], dm_ref[...]
    gmat = gmat_ref[...]
    qk_scale = HEAD_DIM ** -0.5
    rope64 = lambda a: _rope(a, c64, u64, d64, HEAD_DIM // 2)
    ropem = lambda a: _rope(a, cm, um, dm, MLA_ROPE // 2)

    rq_ref[0] = (rope64(p[:, C_RET:C_RET + 256]) * qk_scale).astype(BF16)
    rkt_ref[0] = rope64(p[:, C_RET + 256:C_RET + 512]).T.astype(BF16)
    rv_ref[0] = p[:, C_RET + 512:C_RET + 768].astype(BF16)
    rg_ref[0] = p[:, C_RET + 768:C_RET + 1024].astype(BF16)

    qn = _rmsnorm(p[:, C_MQ:C_MQ + Q_LORA], qng_ref[...])
    q2 = ropem(_bdot(qn, wuq_ref[...])) * ((MLA_NOPE + MLA_ROPE) ** -0.5 * LOG2E)
    mq_ref[0] = q2.astype(BF16)
    kvn = _rmsnorm(p[:, C_MKV:C_MKV + KV_LORA], kvng_ref[...]).astype(BF16)
    k2 = (jnp.dot(kvn, wuk_ref[...], preferred_element_type=F32)
          + _dot_split_lhs(p[:, C_KR:C_KR + LANES], ekr_ref[...]))
    mkt_ref[0] = ropem(k2).T.astype(BF16)
    mv_ref[0] = jnp.dot(kvn, wuv_ref[...], preferred_element_type=F32).astype(BF16)

    gq = p[:, C_GQ:C_GQ + 256]
    gq = gq * lax.rsqrt(_head_mean(gq * gq, gmat) + EPS) * gqg_ref[...]
    gq_ref[0] = (rope64(gq) * (qk_scale * LOG2E)).astype(BF16)
    gk = p[:, C_GK:C_GK + 256]
    gk = gk * lax.rsqrt(_head_mean(gk * gk, gmat) + EPS) * gkg_ref[...]
    gkt_ref[0] = rope64(gk).T.astype(BF16)
    gv_ref[0] = p[:, C_GV:C_GV + 256].astype(BF16)

    wq_ref[0] = (rope64(p[:, C_WQ:C_WQ + 256]) * qk_scale).astype(BF16)
    wkt_ref[0] = rope64(p[:, C_WK:C_WK + 256]).T.astype(BF16)
    wv_ref[0] = p[:, C_WV:C_WV + 256].astype(BF16)


def _prep_call(xl, xc, ctx_block, mod, g1, win_p, wuq, wuk, wuv, ekr, qng, kvng, gqg, gkg, gmat, tabs):
    tok = lambda w: pl.BlockSpec((1, TM, w), lambda b, t: (b, t, 0))
    tokt = lambda w: pl.BlockSpec((1, w, TM), lambda b, t: (b, 0, t))
    tab = pl.BlockSpec((TM, LANES), lambda b, t: (t, 0))
    sd = lambda w: jax.ShapeDtypeStruct((BATCH, TOK, w), BF16)
    sdt = lambda w: jax.ShapeDtypeStruct((BATCH, w, TOK), BF16)
    in_specs = _x_specs(ctx_block) + [
        pl.BlockSpec((1, 1, 8, D_MODEL), lambda b, t: (b, t // NT_LAT, 0, 0)),
        _const_spec((1, D_MODEL)),
        _const_spec((D_MODEL, N_IN_P)),
        _const_spec((Q_LORA, MLA_QK)),
        _const_spec((KV_LORA, MLA_QK)),
        _const_spec((KV_LORA, MIX_W)),
        _const_spec((LANES, MLA_QK)),
        _const_spec((1, Q_LORA)),
        _const_spec((1, KV_LORA)),
        _const_spec((1, MIX_W)),
        _const_spec((1, MIX_W)),
        _const_spec((MIX_W, MIX_W)),
        tab, tab, tab, tab, tab, tab,
    ]
    out_specs = [tok(256), tokt(256), tok(256), tok(256),
                 tok(MLA_QK), tokt(MLA_QK), tok(256),
                 tok(256), tokt(256), tok(256),
                 tok(256), tokt(256), tok(256)]
    out_shape = [sd(256), sdt(256), sd(256), sd(256),
                 sd(MLA_QK), sdt(MLA_QK), sd(256),
                 sd(256), sdt(256), sd(256),
                 sd(256), sdt(256), sd(256)]
    return pl.pallas_call(
        _prep_kernel,
        grid=(BATCH, NT),
        in_specs=in_specs,
        out_specs=out_specs,
        out_shape=out_shape,
        compiler_params=_params(("parallel", "parallel")),
        name="prep",
    )(xl, xc, mod, g1, win_p, wuq, wuk, wuv, ekr, qng, kvng, gqg, gkg, gmat, *tabs)


def _ret_kernel(qf_ref, ktf_ref, vf_ref, qb_ref, ktb_ref, vb_ref,
                dmat_ref, xi_ref, zt_ref, gc_ref, of_ref, ob_ref, sf_ref, sb_ref):
    i = pl.program_id(0)

    @pl.when(i == 0)
    def _():
        sf_ref[...] = jnp.zeros_like(sf_ref)
        sb_ref[...] = jnp.zeros_like(sb_ref)

    lane_head = lax.broadcasted_iota(jnp.int32, (RET_CHUNK, MIX_W), 1) // HEAD_DIM
    r_head = lax.broadcasted_iota(jnp.int32, (MIX_W, MIX_W), 0) // HEAD_DIM
    c_head = lax.broadcasted_iota(jnp.int32, (MIX_W, MIX_W), 1) // HEAD_DIM
    block_diag = r_head == c_head

    def one(b, d, q_ref, kt_ref, v_ref, o_ref, s_ref):
        q = q_ref[b].astype(F32)
        kt = kt_ref[b]
        v = v_ref[b]
        s_old = s_ref[b]
        o = _bdot(q * xi_ref[d], s_old)
        for hd in range(N_HEADS):
            qm = jnp.where(lane_head == hd, q, 0.0).astype(BF16)
            inner = jnp.dot(qm, kt, preferred_element_type=F32) * dmat_ref[d, hd]
            oh = jnp.dot(inner.astype(BF16), v, preferred_element_type=F32)
            o = o + jnp.where(lane_head == hd, oh, 0.0)
        o_ref[b] = o
        kz = (kt.astype(F32) * zt_ref[d]).astype(BF16)
        upd = jnp.dot(kz, v, preferred_element_type=F32)
        s_ref[b] = gc_ref[d] * s_old + jnp.where(block_diag, upd, 0.0)

    def body(b, carry):
        one(b, 0, qf_ref, ktf_ref, vf_ref, of_ref, sf_ref)
        one(b, 1, qb_ref, ktb_ref, vb_ref, ob_ref, sb_ref)
        return carry

    lax.fori_loop(0, BATCH, body, 0)


def _ret_call(rq, rkt, rv, dmat, xi, zt, gc):
    cf = lambda i: (i + SEQ // RET_CHUNK) % N_CHUNK
    cb = lambda i: N_CHUNK - 1 - i
    rows = lambda f: pl.BlockSpec((BATCH, RET_CHUNK, MIX_W), lambda i: (0, f(i), 0))
    cols = lambda f: pl.BlockSpec((BATCH, MIX_W, RET_CHUNK), lambda i: (0, 0, f(i)))
    return pl.pallas_call(
        _ret_kernel,
        grid=(N_CHUNK,),
        in_specs=[rows(cf), cols(cf), rows(cf), rows(cb), cols(cb), rows(cb),
                  _const_spec((2, N_HEADS, RET_CHUNK, RET_CHUNK)),
                  _const_spec((2, RET_CHUNK, MIX_W)),
                  _const_spec((2, MIX_W, RET_CHUNK)),
                  _const_spec((2, MIX_W, MIX_W))],
        out_specs=[rows(cf), rows(cb)],
        out_shape=[jax.ShapeDtypeStruct((BATCH, TOK, MIX_W), F32)] * 2,
        scratch_shapes=[pltpu.VMEM((BATCH, MIX_W, MIX_W), F32),
                        pltpu.VMEM((BATCH, MIX_W, MIX_W), F32)],
        compiler_params=_params(("arbitrary",)),
        name="retention",
    )(rq, rkt, rv, rq, rkt, rv, dmat, xi, zt, gc)


def _head_q(q_ref, hd, pair, rows=slice(None)):
    if not pair:
        return q_ref[0, rows, hd * LANES:(hd + 1) * LANES], hd * LANES
    c = hd // 2
    qc = q_ref[0, rows, c * LANES:(c + 1) * LANES]
    half = lax.broadcasted_iota(jnp.int32, qc.shape, 1) // HEAD_DIM
    return jnp.where(half == hd % 2, qc, jnp.zeros_like(qc)), c * LANES


def _dense_kernel(q_ref, kt_ref, v_ref, o_ref, *, pair):
    t = pl.program_id(1)
    lane_head = lax.broadcasted_iota(jnp.int32, (TM, MIX_W), 1) // HEAD_DIM

    def run(k_lo, k_hi):
        v = v_ref[0, k_lo:k_hi, :]
        acc = jnp.zeros((TM, MIX_W), F32)

        def scores(hd):
            qm, r0 = _head_q(q_ref, hd, pair)
            return jnp.dot(qm, kt_ref[0, r0:r0 + LANES, k_lo:k_hi], preferred_element_type=F32)

        s_next = scores(0)
        for hd in range(N_HEADS):
            s = s_next
            if hd + 1 < N_HEADS:
                s_next = scores(hd + 1)
            m = jnp.max(s, axis=-1, keepdims=True)
            p = jnp.exp2(s - m)
            l = jnp.sum(p, axis=-1, keepdims=True)
            oh = jnp.dot(p.astype(BF16), v, preferred_element_type=F32)
            acc = jnp.where(lane_head == hd, oh * (1.0 / l), acc)
        o_ref[0] = acc.astype(BF16)

    @pl.when(t < NT_LAT)
    def _():
        run(0, TOK)

    @pl.when(t >= NT_LAT)
    def _():
        run(SEQ, TOK)


def _dense_call(q, kt, v, nq, pair, name):
    wq = q.shape[-1]
    return pl.pallas_call(
        functools.partial(_dense_kernel, pair=pair),
        grid=(BATCH, nq),
        in_specs=[pl.BlockSpec((1, TM, wq), lambda b, t: (b, t, 0)),
                  pl.BlockSpec((1, kt.shape[1], TOK), lambda b, t: (b, 0, 0)),
                  pl.BlockSpec((1, TOK, MIX_W), lambda b, t: (b, 0, 0))],
        out_specs=pl.BlockSpec((1, TM, MIX_W), lambda b, t: (b, t, 0)),
        out_shape=jax.ShapeDtypeStruct((BATCH, nq * TM, MIX_W), BF16),
        compiler_params=_params(("parallel", "arbitrary")),
        name=name,
    )(q, kt, v)


def _win_kernel(sink_ref, q_ref, kt_ref, v_ref, o_ref):
    t = pl.program_id(1)
    lane_head = lax.broadcasted_iota(jnp.int32, (TM, MIX_W), 1) // HEAD_DIM
    v_ctx = v_ref[0, SEQ:TOK, :]

    @pl.when(t < NT_LAT)
    def _():
        start = pl.multiple_of(jnp.clip(t * TM - WINDOW, 0, SEQ - WIN_KEYS), LANES)
        qpos = t * TM + lax.broadcasted_iota(jnp.int32, (TM, WIN_KEYS), 0)
        kpos = start + lax.broadcasted_iota(jnp.int32, (TM, WIN_KEYS), 1)
        valid = jnp.abs(kpos - qpos) <= WINDOW
        v_loc = v_ref[0, pl.ds(start, WIN_KEYS), :]
        acc = jnp.zeros((TM, MIX_W), F32)

        def scores(hd):
            qm, r0 = _head_q(q_ref, hd, True)
            loc = jnp.dot(qm, kt_ref[0, r0:r0 + LANES, pl.ds(start, WIN_KEYS)], preferred_element_type=F32)
            ctx = jnp.dot(qm, kt_ref[0, r0:r0 + LANES, SEQ:TOK], preferred_element_type=F32)
            return jnp.where(valid, loc, NEG_INF), ctx

        s_next = scores(0)
        for hd in range(N_HEADS):
            s_loc, s_ctx = s_next
            if hd + 1 < N_HEADS:
                s_next = scores(hd + 1)
            sk = sink_ref[hd]
            m = jnp.maximum(jnp.maximum(jnp.max(s_loc, axis=-1, keepdims=True),
                                        jnp.max(s_ctx, axis=-1, keepdims=True)), sk)
            p_loc = jnp.exp(s_loc - m)
            p_ctx = jnp.exp(s_ctx - m)
            l = (jnp.sum(p_loc, axis=-1, keepdims=True) + jnp.sum(p_ctx, axis=-1, keepdims=True)
                 + jnp.exp(sk - m))
            oh = (jnp.dot(p_loc.astype(BF16), v_loc, preferred_element_type=F32)
                  + jnp.dot(p_ctx.astype(BF16), v_ctx, preferred_element_type=F32))
            acc = jnp.where(lane_head == hd, oh * (1.0 / l), acc)
        o_ref[0] = acc.astype(BF16)

    @pl.when(t >= NT_LAT)
    def _():
        acc = jnp.zeros((TM, MIX_W), F32)
        for hd in range(N_HEADS):
            qm, r0 = _head_q(q_ref, hd, True)
            s = jnp.dot(qm, kt_ref[0, r0:r0 + LANES, SEQ:TOK], preferred_element_type=F32)
            sk = sink_ref[hd]
            m = jnp.maximum(jnp.max(s, axis=-1, keepdims=True), sk)
            p = jnp.exp(s - m)
            l = jnp.sum(p, axis=-1, keepdims=True) + jnp.exp(sk - m)
            oh = jnp.dot(p.astype(BF16), v_ctx, preferred_element_type=F32)
            acc = jnp.where(lane_head == hd, oh * (1.0 / l), acc)
        o_ref[0] = acc.astype(BF16)


def _win_call(sink, q, kt, v, nq):
    return pl.pallas_call(
        _win_kernel,
        grid=(BATCH, nq),
        in_specs=[pl.BlockSpec(memory_space=pltpu.SMEM),
                  pl.BlockSpec((1, TM, MIX_W), lambda b, t: (b, t, 0)),
                  pl.BlockSpec((1, MIX_W, TOK), lambda b, t: (b, 0, 0)),
                  pl.BlockSpec((1, TOK, MIX_W), lambda b, t: (b, 0, 0))],
        out_specs=pl.BlockSpec((1, TM, MIX_W), lambda b, t: (b, t, 0)),
        out_shape=jax.ShapeDtypeStruct((BATCH, nq * TM, MIX_W), BF16),
        compiler_params=_params(("parallel", "arbitrary")),
        name="window",
    )(sink, q, kt, v)


def _merge_kernel(xl_ref, xc_ref, mod_ref, g1_ref, g2_ref, of_ref, ob_ref, rg_ref, ym_ref, yg_ref, yw_ref,
                  wg_ref, wb_ref, wo_ref, wr_ref, gmat_ref, xo_ref, h2_ref, lg_ref):
    x = _tile_x(xl_ref, xc_ref)
    md = mod_ref[0, 0]
    hb = _modnorm(x, g1_ref[...], md[1:2], md[0:1]).astype(BF16)

    gmat = gmat_ref[...]
    o = of_ref[0] + ob_ref[0]
    dlt = o - _head_mean(o, gmat)
    var = _head_mean(dlt * dlt, gmat)
    g = rg_ref[0].astype(F32)
    y_ret = dlt * lax.rsqrt(var + EPS) * (g * _sigmoid(g))

    ys = (y_ret.astype(BF16), ym_ref[0], yg_ref[0], yw_ref[0])
    acc = jnp.zeros((TM, D_MODEL), F32)
    for i in range(4):
        gate = _sigmoid(jnp.dot(hb, wg_ref[:, i * D_MODEL:(i + 1) * D_MODEL],
                                      preferred_element_type=F32))
        acc = acc + gate * jnp.dot(ys[i], wb_ref[i], preferred_element_type=F32)
    out = jnp.dot(acc.astype(BF16), wo_ref[...], preferred_element_type=F32)
    xm = x + md[2:3] * out
    xo_ref[0] = xm
    h2 = _modnorm(xm, g2_ref[...], md[4:5], md[3:4])
    h2_ref[0] = h2
    hh, hl = _split(h2)
    wh, wl = _split(wr_ref[...])
    nt = lambda a, b: lax.dot_general(a, b, (((1,), (1,)), ((), ())), preferred_element_type=F32)
    lg_ref[0] = nt(wh, hh) + nt(wh, hl) + nt(wl, hh)


def _merge_call(xl, xc, ctx_block, mod, g1, g2, of, ob, rg, ym, yg, yw, wg, wb, wo, wr, gmat, nq):
    tok = lambda w: pl.BlockSpec((1, TM, w), lambda b, t: (b, t, 0))
    return pl.pallas_call(
        _merge_kernel,
        grid=(BATCH, nq),
        in_specs=_x_specs(ctx_block) + [
                  pl.BlockSpec((1, 1, 8, D_MODEL), lambda b, t: (b, t // NT_LAT, 0, 0)),
                  _const_spec((1, D_MODEL)), _const_spec((1, D_MODEL)),
                  tok(MIX_W), tok(MIX_W), tok(MIX_W), tok(MIX_W), tok(MIX_W), tok(MIX_W),
                  _const_spec((D_MODEL, 4 * D_MODEL)),
                  _const_spec((4, MIX_W, D_MODEL)),
                  _const_spec((D_MODEL, D_MODEL)),
                  _const_spec((N_EXPERTS, D_MODEL)),
                  _const_spec((MIX_W, MIX_W))],
        out_specs=[tok(D_MODEL), tok(D_MODEL),
                   pl.BlockSpec((1, N_EXPERTS, TM), lambda b, t: (b, 0, t))],
        out_shape=[jax.ShapeDtypeStruct((BATCH, nq * TM, D_MODEL), F32),
                   jax.ShapeDtypeStruct((BATCH, nq * TM, D_MODEL), F32),
                   jax.ShapeDtypeStruct((BATCH, N_EXPERTS, nq * TM), F32)],
        compiler_params=_params(("parallel", "parallel")),
        name="merge",
    )(xl, xc, mod, g1, g2, of, ob, rg, ym, yg, yw, wg, wb, wo, wr, gmat)


RANK_BITS = 20
RANK_MASK = (1 << RANK_BITS) - 1


def _route_kernel(lg_ref, b_ref, tri_ref, pk_ref, gw_ref, cnt_ref, base_ref):
    i = pl.program_id(0)

    @pl.when(i == 0)
    def _():
        base_ref[...] = jnp.zeros_like(base_ref)

    s = jax.nn.sigmoid(lg_ref[0])
    sel = s + b_ref[:, 0:1]
    sub = lax.broadcasted_iota(jnp.int32, (EXPERTS_PER_GROUP, TM), 0)
    best = e1 = e2 = s1 = s2 = None
    for g in range(N_GROUPS):
        rows = slice(g * EXPERTS_PER_GROUP, (g + 1) * EXPERTS_PER_GROUP)
        blk, sb = sel[rows], s[rows]
        m1 = jnp.max(blk, axis=0, keepdims=True)
        i1 = jnp.min(jnp.where(blk == m1, sub, EXPERTS_PER_GROUP), axis=0, keepdims=True)
        hit1 = sub == i1
        blk2 = jnp.where(hit1, -jnp.inf, blk)
        m2 = jnp.max(blk2, axis=0, keepdims=True)
        i2 = jnp.min(jnp.where(blk2 == m2, sub, EXPERTS_PER_GROUP), axis=0, keepdims=True)
        hit2 = sub == i2
        score = m1 + m2
        s1g = jnp.sum(jnp.where(hit1, sb, 0.0), axis=0, keepdims=True)
        s2g = jnp.sum(jnp.where(hit2, sb, 0.0), axis=0, keepdims=True)
        e1g = g * EXPERTS_PER_GROUP + i1
        e2g = g * EXPERTS_PER_GROUP + i2
        if g == 0:
            best, e1, e2, s1, s2 = score, e1g, e2g, s1g, s2g
        else:
            better = score > best
            best = jnp.where(better, score, best)
            e1 = jnp.where(better, e1g, e1)
            e2 = jnp.where(better, e2g, e2)
            s1 = jnp.where(better, s1g, s1)
            s2 = jnp.where(better, s2g, s2)

    eid = lax.broadcasted_iota(jnp.int32, (N_EXPERTS, TM), 0)
    oh1 = eid == e1
    oh2 = eid == e2
    oh = jnp.where(oh1 | oh2, 1.0, 0.0)
    before = jnp.dot(oh.astype(BF16), tri_ref[...], preferred_element_type=F32) + base_ref[:, 0:1]
    r1 = jnp.sum(jnp.where(oh1, before, 0.0), axis=0, keepdims=True).astype(jnp.int32)
    r2 = jnp.sum(jnp.where(oh2, before, 0.0), axis=0, keepdims=True).astype(jnp.int32)
    total = base_ref[...] + jnp.sum(oh, axis=1, keepdims=True)
    base_ref[...] = total
    cnt_ref[...] = total

    pk_ref[...] = jnp.concatenate([(e1 << RANK_BITS) + r1, (e2 << RANK_BITS) + r2], axis=0)
    den = s1 + s2
    row = lax.broadcasted_iota(jnp.int32, (8, TM), 0)
    gw_ref[...] = jnp.where(row == 0, s1 / den, jnp.where(row == 1, s2 / den, 0.0))


def _route_call(lgt, b_router, nq):
    n_tiles = BATCH * nq
    n_tok = n_tiles * TM
    r = jnp.arange(TM)
    tri = (r[:, None] < r[None, :]).astype(BF16)
    bcol = jnp.broadcast_to(b_router.astype(F32)[:, None], (N_EXPERTS, LANES))
    return pl.pallas_call(
        _route_kernel,
        grid=(n_tiles,),
        in_specs=[pl.BlockSpec((1, N_EXPERTS, TM), lambda i: (i // nq, 0, i % nq)),
                  _const_spec((N_EXPERTS, LANES)),
                  _const_spec((TM, TM))],
        out_specs=[pl.BlockSpec((TOP_K, TM), lambda i: (0, i)),
                   pl.BlockSpec((8, TM), lambda i: (0, i)),
                   pl.BlockSpec((N_EXPERTS, LANES), lambda i: (0, 0))],
        out_shape=[jax.ShapeDtypeStruct((TOP_K, n_tok), jnp.int32),
                   jax.ShapeDtypeStruct((8, n_tok), F32),
                   jax.ShapeDtypeStruct((N_EXPERTS, LANES), F32)],
        scratch_shapes=[pltpu.VMEM((N_EXPERTS, LANES), F32)],
        compiler_params=_params(("arbitrary",)),
        name="route",
    )(lgt, bcol, tri)


def _slot_rows(pk, row0):
    e = pk >> RANK_BITS
    base = jnp.sum(jnp.where(e[..., None] == jnp.arange(N_EXPERTS), row0, 0), axis=-1)
    return (base + (pk & RANK_MASK)).astype(jnp.int32).reshape(-1)


def _slot_row(rows_ref, k, n, n_tok):
    return rows_ref[k * n_tok + n]


def _combine_kernel(rows_ref, x_ref, mod_ref, gw_ref, fg_ref, ys_hbm, o_ref, ybuf, sem,
                    *, n_tok, final):
    i = pl.program_id(0)
    n_tiles = pl.num_programs(0)
    slot = i % 2

    def issue(tile, sl):
        def one(j, c):
            n = tile * TM + j
            for k in range(TOP_K):
                pltpu.make_async_copy(ys_hbm.at[pl.ds(_slot_row(rows_ref, k, n, n_tok), 1)],
                                      ybuf.at[sl, k, pl.ds(j, 1)], sem.at[sl]).start()
            return c

        lax.fori_loop(0, TM, one, 0, unroll=4)

    @pl.when(i == 0)
    def _():
        issue(0, 0)

    @pl.when(i + 1 < n_tiles)
    def _():
        issue(i + 1, 1 - slot)

    for k in range(TOP_K):
        pltpu.make_async_copy(ys_hbm.at[pl.ds(0, TM)], ybuf.at[slot, k], sem.at[slot]).wait()

    md = mod_ref[0, 0]
    gw = gw_ref[...].T
    f = gw[:, 0:1] * ybuf[slot, 0] + gw[:, 1:2] * ybuf[slot, 1]
    xn = x_ref[0] + md[5:6] * f
    o_ref[0] = _rmsnorm(xn, fg_ref[...]) if final else xn


def _combine_call(rows, xm, mod, gw, fg, ys, nq, final):
    n_tok = BATCH * nq * TM
    grid_spec = pltpu.PrefetchScalarGridSpec(
        num_scalar_prefetch=1,
        grid=(BATCH * nq,),
        in_specs=[pl.BlockSpec((1, TM, D_MODEL), lambda i, *_: (i // nq, i % nq, 0)),
                  pl.BlockSpec((1, 1, 8, D_MODEL), lambda i, *_: (i // nq, (i % nq) // NT_LAT, 0, 0)),
                  pl.BlockSpec((8, TM), lambda i, *_: (0, i)),
                  pl.BlockSpec((1, D_MODEL), lambda i, *_: (0, 0)),
                  pl.BlockSpec(memory_space=pl.ANY)],
        out_specs=pl.BlockSpec((1, TM, D_MODEL), lambda i, *_: (i // nq, i % nq, 0)),
        scratch_shapes=[pltpu.VMEM((2, TOP_K, TM, D_MODEL), F32),
                        pltpu.SemaphoreType.DMA((2,))])
    return pl.pallas_call(
        functools.partial(_combine_kernel, n_tok=n_tok, final=final),
        grid_spec=grid_spec,
        out_shape=jax.ShapeDtypeStruct((BATCH, nq * TM, D_MODEL), F32),
        compiler_params=_params(("arbitrary",)),
        name="combine",
    )(rows, xm, mod, gw, fg, ys)


def _fmoe_kernel(rows_ref, row0_ref, nblk_ref, cnt_ref, h_hbm, w1_ref, w3_ref, w2_ref, ys_hbm,
                 tok, xbuf, ybuf, w1b, w3b, w2b, isem, osem, *, n_tok, n_blocks):
    e = pl.program_id(0)
    nb = nblk_ref[e]
    b0 = row0_ref[e] // MOE_ROWS
    used = row0_ref[N_EXPERTS - 1] // MOE_ROWS + nblk_ref[N_EXPERTS - 1]

    def issue_gather(b, lo=0, hi=MOE_ROWS):
        slot = b % MOE_SLOTS
        for j in range(lo, hi):
            pltpu.make_async_copy(h_hbm.at[pl.ds(tok[b * MOE_ROWS + j], 1)], xbuf.at[slot, pl.ds(j, 1)],
                                  isem.at[slot]).start()

    def wait_gather(b):
        slot = b % MOE_SLOTS
        pltpu.make_async_copy(h_hbm.at[pl.ds(0, MOE_ROWS)], xbuf.at[slot], isem.at[slot]).wait()

    def out_copy(b):
        rows = pl.ds(pl.multiple_of(b * MOE_ROWS, MOE_ROWS), MOE_ROWS)
        return pltpu.make_async_copy(ybuf.at[b % 2], ys_hbm.at[rows], osem.at[b % 2])

    @pl.when(e == 0)
    def _():
        def clear(lo, hi):
            def one(p, c):
                tok[p] = 0
                return c

            lax.fori_loop(lo, hi, one, 0)

        clear(used * MOE_ROWS, (used + MOE_SLOTS - 1) * MOE_ROWS)

        def pad_e(x, c):
            clear(row0_ref[x] + cnt_ref[x], row0_ref[x] + nblk_ref[x] * MOE_ROWS)
            return c

        lax.fori_loop(0, N_EXPERTS, pad_e, 0)

        def put(n, c):
            for k in range(TOP_K):
                tok[rows_ref[k * n_tok + n]] = n
            return c

        lax.fori_loop(0, n_tok, put, 0, unroll=8)

        def prime(idx, c):
            b = idx // MOE_ROWS
            pltpu.make_async_copy(h_hbm.at[pl.ds(tok[idx], 1)], xbuf.at[b, pl.ds(idx % MOE_ROWS, 1)],
                                  isem.at[b]).start()
            return c

        lax.fori_loop(0, (MOE_SLOTS - 1) * MOE_ROWS, prime, 0)

    @pl.when(nb > 0)
    def _():
        w1b[...] = w1_ref[0, 0].astype(BF16)
        w3b[...] = w3_ref[0, 0].astype(BF16)
        w2b[...] = w2_ref[0, 0].astype(BF16)

        def block(r, carry):
            g = b0 + r
            wait_gather(g)

            @pl.when(g >= 2)
            def _():
                out_copy(g - 2).wait()

            ahead = g + MOE_SLOTS - 1
            piece = MOE_ROWS // MOE_PIECES
            half = D_EXPERT // 2
            xb = xbuf[g % MOE_SLOTS].astype(BF16)
            hid = []
            for c in range(2):
                cols = slice(c * half, (c + 1) * half)
                issue_gather(ahead, (2 * c) * piece, (2 * c + 1) * piece)
                h1 = jnp.dot(xb, w1b[:, cols], preferred_element_type=F32)
                issue_gather(ahead, (2 * c + 1) * piece, (2 * c + 2) * piece)
                h3 = jnp.dot(xb, w3b[:, cols], preferred_element_type=F32)
                hid.append((h1 * _sigmoid(h1) * h3).astype(BF16))
            hid = jnp.concatenate(hid, axis=1)
            for c in range(2):
                cols = slice(c * half, (c + 1) * half)
                issue_gather(ahead, (4 + c) * piece, (5 + c) * piece if c == 0 else MOE_ROWS)
                ybuf[g % 2, :, cols] = jnp.dot(hid, w2b[:, cols], preferred_element_type=F32)
            out_copy(g).start()
            return carry

        lax.fori_loop(0, nb, block, 0)

    @pl.when(e == N_EXPERTS - 1)
    def _():
        @pl.when(used >= 2)
        def _():
            out_copy(used - 2).wait()

        @pl.when(used >= 1)
        def _():
            out_copy(used - 1).wait()

        for b in range(MOE_SLOTS - 1):
            wait_gather(used + b)

        ybuf[0] = jnp.zeros((MOE_ROWS, D_MODEL), F32)

        def tail_copy(bk):
            return pltpu.make_async_copy(
                ybuf.at[0], ys_hbm.at[pl.ds(pl.multiple_of(bk * MOE_ROWS, MOE_ROWS), MOE_ROWS)], osem.at[0])

        lax.fori_loop(used, n_blocks, lambda bk, c: (tail_copy(bk).start(), c)[1], 0)
        lax.fori_loop(used, n_blocks, lambda bk, c: (tail_copy(bk).wait(), c)[1], 0)


def _fmoe_call(rows, row0, nblk, cnt, h2, w1, w3, w2, l):
    n_tok = h2.shape[0]
    n_blocks = TOP_K * n_tok // MOE_ROWS + N_EXPERTS
    wspec = pl.BlockSpec((1, 1, D_MODEL, D_EXPERT), lambda e, *_: (l, e, 0, 0))
    grid_spec = pltpu.PrefetchScalarGridSpec(
        num_scalar_prefetch=4,
        grid=(N_EXPERTS,),
        in_specs=[pl.BlockSpec(memory_space=pl.ANY), wspec, wspec,
                  pl.BlockSpec((1, 1, D_EXPERT, D_MODEL), lambda e, *_: (l, e, 0, 0))],
        out_specs=pl.BlockSpec(memory_space=pl.ANY),
        scratch_shapes=[pltpu.SMEM(((n_blocks + MOE_SLOTS - 1) * MOE_ROWS,), jnp.int32),
                        pltpu.VMEM((MOE_SLOTS, MOE_ROWS, D_MODEL), F32),
                        pltpu.VMEM((2, MOE_ROWS, D_MODEL), F32),
                        pltpu.VMEM((D_MODEL, D_EXPERT), BF16),
                        pltpu.VMEM((D_MODEL, D_EXPERT), BF16),
                        pltpu.VMEM((D_EXPERT, D_MODEL), BF16),
                        pltpu.SemaphoreType.DMA((MOE_SLOTS,)),
                        pltpu.SemaphoreType.DMA((2,))])
    return pl.pallas_call(
        functools.partial(_fmoe_kernel, n_tok=n_tok, n_blocks=n_blocks),
        grid_spec=grid_spec,
        out_shape=jax.ShapeDtypeStruct((n_blocks * MOE_ROWS, D_MODEL), F32),
        compiler_params=_params(("arbitrary",)),
        name="moe",
    )(rows, row0, nblk, cnt, h2, w1, w3, w2)


def _rope_tables():
    rows = SEQ // GRID_W
    row = jnp.broadcast_to(jnp.arange(rows)[:, None], (rows, GRID_W)).reshape(-1).astype(F32)
    col = jnp.broadcast_to(jnp.arange(GRID_W)[None, :], (rows, GRID_W)).reshape(-1).astype(F32)

    def cs(rot_dim):
        n_f = rot_dim // 4
        inv = ROPE_BASE ** (-jnp.arange(n_f, dtype=F32) / n_f)
        ang = jnp.concatenate([row[:, None] * inv, col[:, None] * inv], axis=-1)
        return jnp.cos(ang), jnp.sin(ang)

    def with_ctx(c, u, d):
        one = jnp.ones((CTX_LEN, LANES), F32)
        zero = jnp.zeros((CTX_LEN, LANES), F32)
        return (jnp.concatenate([c, one]), jnp.concatenate([u, zero]), jnp.concatenate([d, zero]))

    cos, sin = cs(HEAD_DIM)
    z = jnp.zeros_like(sin)
    t64 = with_ctx(jnp.tile(jnp.concatenate([cos, cos], -1), (1, 2)),
                   jnp.tile(jnp.concatenate([-sin, z], -1), (1, 2)),
                   jnp.tile(jnp.concatenate([z, sin], -1), (1, 2)))
    cos, sin = cs(MLA_ROPE)
    z = jnp.zeros_like(sin)
    one_n = jnp.ones((SEQ, MLA_NOPE), F32)
    zero_n = jnp.zeros((SEQ, MLA_NOPE), F32)
    one_p = jnp.ones((SEQ, LANES - MLA_NOPE - MLA_ROPE), F32)
    zero_p = jnp.zeros((SEQ, LANES - MLA_NOPE - MLA_ROPE), F32)
    tm = with_ctx(jnp.concatenate([one_n, cos, cos, one_p], -1),
                  jnp.concatenate([zero_n, -sin, z, zero_p], -1),
                  jnp.concatenate([zero_n, z, sin, zero_p], -1))
    return t64 + tm


def _ret_tables(decay):
    lg = -jnp.exp(decay.astype(F32))
    idx = jnp.arange(RET_CHUNK, dtype=F32)
    diff = idx[:, None] - idx[None, :]
    fwd = diff >= 0
    bwd = diff < 0
    dm_f = jnp.where(fwd, jnp.exp(lg[0][:, None, None] * jnp.where(fwd, diff, 0.0)), 0.0)
    dm_b = jnp.where(bwd, jnp.exp(lg[1][:, None, None] * jnp.where(bwd, -diff, 0.0)), 0.0)
    dmat = jnp.stack([dm_f, dm_b])
    xi = jnp.stack([jnp.exp(lg[0][:, None] * (idx + 1.0)),
                    jnp.exp(lg[1][:, None] * (RET_CHUNK - idx))])
    zeta = jnp.stack([jnp.exp(lg[0][:, None] * (RET_CHUNK - 1.0 - idx)),
                      jnp.exp(lg[1][:, None] * idx)])
    gch = jnp.exp(lg * RET_CHUNK)
    xi_t = jnp.repeat(jnp.transpose(xi, (0, 2, 1)), HEAD_DIM, axis=2)
    zt_t = jnp.repeat(zeta, HEAD_DIM, axis=1)
    gc_t = jnp.broadcast_to(jnp.repeat(gch, HEAD_DIM, axis=1)[:, :, None], (2, MIX_W, MIX_W))
    return dmat, xi_t, zt_t, gc_t


def _in_proj_columns():
    o_mla = 4 * MIX_W
    o_gqa = o_mla + Q_LORA + KV_LORA + MLA_ROPE
    kv_w = (N_HEADS // 2) * HEAD_DIM
    o_win = o_gqa + MIX_W + 2 * kv_w
    ar = jnp.arange
    dup = jnp.concatenate([ar(HEAD_DIM), ar(HEAD_DIM), HEAD_DIM + ar(HEAD_DIM), HEAD_DIM + ar(HEAD_DIM)])

    def gqa_cols(o):
        return [o + ar(MIX_W), o + MIX_W + dup, o + MIX_W + kv_w + dup]

    return jnp.concatenate([ar(o_mla), o_mla + ar(Q_LORA), o_mla + Q_LORA + ar(KV_LORA)]
                           + gqa_cols(o_gqa) + gqa_cols(o_win)
                           + [o_mla + Q_LORA + KV_LORA + ar(MLA_ROPE)])


def _layer_weights(l, w_in, mla_w_uq, mla_w_ukv):
    cols = _in_proj_columns()
    win_p = jnp.pad(w_in[l][:, cols], ((0, 0), (0, N_IN_P - cols.shape[0]))).astype(BF16)
    uq = mla_w_uq[l].reshape(Q_LORA, N_HEADS, MLA_NOPE + MLA_ROPE)
    wuq = jnp.pad(uq, ((0, 0), (0, 0), (0, LANES - MLA_NOPE - MLA_ROPE))).reshape(Q_LORA, MLA_QK)
    ukv = mla_w_ukv[l].reshape(KV_LORA, N_HEADS, MLA_NOPE + MLA_V)
    wuk = jnp.pad(ukv[:, :, :MLA_NOPE], ((0, 0), (0, 0), (0, LANES - MLA_NOPE))).reshape(KV_LORA, MLA_QK)
    wuv = ukv[:, :, MLA_NOPE:].reshape(KV_LORA, MIX_W)
    return win_p, wuq.astype(BF16), wuk.astype(BF16), wuv.astype(BF16)


def _krope_placement():
    r = jnp.arange(LANES)[:, None]
    c = jnp.arange(MLA_QK)[None, :]
    return ((r < MLA_ROPE) & (c % LANES == MLA_NOPE + r)).astype(BF16)


def _head_block_matrix():
    r = jnp.arange(MIX_W)
    return (r[:, None] // HEAD_DIM == r[None, :] // HEAD_DIM).astype(BF16)


def kernel(x, c, ctx, c_ctx, w_ada, b_ada, norm1_g, norm2_g, w_in, w_gate, w_branch, w_out, ret_decay,
           mla_qn_g, mla_w_uq, mla_kvn_g, mla_w_ukv, gqa_qn_g, gqa_kn_g, win_sink, w_router, b_router,
           w1, w3, w2, final_norm_g):
    cvec = jnp.concatenate([c, c_ctx[None, :], jnp.zeros((7, D_MODEL), F32)], axis=0)
    ada = _ada_call(cvec, w_ada, b_ada)
    tabs = _rope_tables()
    gmat = _head_block_matrix()
    ekr = _krope_placement()
    stream = (x, ctx, 0)
    out = None
    for l in range(DEPTH):
        last = l == DEPTH - 1
        nq = NT_LAT if last else NT
        m = ada[l].reshape(16, 6, D_MODEL)
        m_lat = m[:BATCH]
        m_ctx = jnp.broadcast_to(m[BATCH][None], (BATCH, 6, D_MODEL))
        mod = jnp.pad(jnp.stack([m_lat, m_ctx], axis=1), ((0, 0), (0, 0), (0, 2), (0, 0)))
        g1 = norm1_g[l][None, :]
        g2 = norm2_g[l][None, :]
        win_p, wuq, wuk, wuv = _layer_weights(l, w_in, mla_w_uq, mla_w_ukv)
        (rq, rkt, rv, rg, mq, mkt, mv, gq, gkt, gv, wq, wkt, wv) = _prep_call(
            *stream, mod, g1, win_p, wuq, wuk, wuv, ekr,
            mla_qn_g[l][None, :], mla_kvn_g[l][None, :],
            jnp.tile(gqa_qn_g[l], N_HEADS)[None, :], jnp.tile(gqa_kn_g[l], N_HEADS)[None, :],
            gmat, tabs)
        of, ob = _ret_call(rq, rkt, rv, *_ret_tables(ret_decay[l]))
        ym = _dense_call(mq, mkt, mv, nq, False, "mla")
        yg = _dense_call(gq, gkt, gv, nq, True, "gqa")
        yw = _win_call(win_sink[l], wq, wkt, wv, nq)
        xm, h2, lgt = _merge_call(
            *stream, mod, g1, g2, of, ob, rg, ym, yg, yw,
            w_gate[l].astype(BF16), w_branch[l].astype(BF16), w_out[l].astype(BF16), w_router.T, gmat, nq)
        n_tok = BATCH * nq * TM
        pk, gw, cnt = _route_call(lgt, b_router, nq)
        cnt = cnt[:, 0].astype(jnp.int32)
        nblk = (cnt + MOE_ROWS - 1) // MOE_ROWS
        row0 = (jnp.cumsum(nblk) - nblk) * MOE_ROWS
        rows = _slot_rows(pk, row0)
        ys = _fmoe_call(rows, row0, nblk, cnt, h2.reshape(n_tok, D_MODEL), w1, w3, w2, l)
        res = _combine_call(rows, xm, mod, gw, final_norm_g[None, :], ys, nq, last)
        if last:
            out = res
        else:
            stream = (res, res, NT_LAT)
    return out
```

```python
import functools

import jax
import jax.numpy as jnp
from jax import lax
from jax.experimental import pallas as pl
from jax.experimental.pallas import tpu as pltpu

F32 = jnp.float32
BF16 = jnp.bfloat16

D_MODEL = 1024
BATCH = 8
SEQ = 2048
DEPTH = 2
CTX_LEN = 256
TOK = SEQ + CTX_LEN
GRID_W = 64
N_HEADS = 4
HEAD_DIM = 64
MIX_W = N_HEADS * HEAD_DIM
RET_CHUNK = 128
Q_LORA = 256
KV_LORA = 128
MLA_NOPE = 64
MLA_ROPE = 32
MLA_V = 64
WINDOW = 128
N_EXPERTS = 32
N_GROUPS = 4
EXPERTS_PER_GROUP = N_EXPERTS // N_GROUPS
TOP_K = 2
D_EXPERT = 1024
ROPE_BASE = 10000.0
EPS = 1e-6
NEG_INF = -1e30

LANES = 128
TM = 256
NT = TOK // TM
NT_LAT = SEQ // TM
WIN_KEYS = TM + 2 * WINDOW
N_CHUNK = TOK // RET_CHUNK
LOG2E = 1.4426950408889634
PAIR = 2
MOE_ROWS = 128
MOE_SLOTS = 8
MOE_PIECES = 6
N_IN_P = 3072
MLA_QK = 4 * LANES

C_RET = 0
C_MQ = 1024
C_MKV = 1280
C_GQ = 1408
C_GK = 1664
C_GV = 1920
C_WQ = 2176
C_WK = 2432
C_WV = 2688
C_KR = 2944

VMEM_LIMIT = 56 * 1024 * 1024


def _params(sem, vmem=VMEM_LIMIT):
    return pltpu.CompilerParams(dimension_semantics=sem, vmem_limit_bytes=vmem)


def _const_spec(shape):
    nd = len(shape)
    return pl.BlockSpec(shape, lambda *_: (0,) * nd, pipeline_mode=pl.Buffered(1))


def _bdot(a, b):
    return jnp.dot(a.astype(BF16), b.astype(BF16), preferred_element_type=F32)


def _split(a):
    hi = a.astype(BF16)
    lo = (a - hi.astype(F32)).astype(BF16)
    return hi, lo


def _dot_split_lhs(a, b):
    hi, lo = _split(a)
    return (jnp.dot(hi, b, preferred_element_type=F32)
            + jnp.dot(lo, b, preferred_element_type=F32))


def _dot3(a, b):
    ah, al = _split(a)
    bh, bl = _split(b)
    return (jnp.dot(ah, bh, preferred_element_type=F32)
            + jnp.dot(ah, bl, preferred_element_type=F32)
            + jnp.dot(al, bh, preferred_element_type=F32))


def _modnorm(x, g, sc, sh):
    ms = jnp.mean(x * x, axis=-1, keepdims=True)
    return x * lax.rsqrt(ms + EPS) * g * (1.0 + sc) + sh


def _rmsnorm(x, g):
    ms = jnp.mean(x * x, axis=-1, keepdims=True)
    return x * lax.rsqrt(ms + EPS) * g


def _rope(x, c, s_up, s_dn, half):
    outs = []
    for j in range(x.shape[1] // LANES):
        xc = x[:, j * LANES:(j + 1) * LANES]
        outs.append(xc * c + pltpu.roll(xc, LANES - half, 1) * s_up + pltpu.roll(xc, half, 1) * s_dn)
    return outs[0] if len(outs) == 1 else jnp.concatenate(outs, axis=1)


def _sigmoid(x):
    return 0.5 * jnp.tanh(0.5 * x) + 0.5


def _head_mean(x, gmat):
    return _dot_split_lhs(x, gmat) * (1.0 / HEAD_DIM)


def _ada_kernel(c_ref, w_ref, b_ref, o_ref):
    c = c_ref[...]
    sc = c * jax.nn.sigmoid(c)
    o_ref[0] = _dot3(sc, w_ref[0]) + b_ref[0]


def _ada_call(cvec, w_ada, b_ada):
    tn = 1536
    return pl.pallas_call(
        _ada_kernel,
        grid=(DEPTH, 6 * D_MODEL // tn),
        in_specs=[
            pl.BlockSpec((16, D_MODEL), lambda l, j: (0, 0)),
            pl.BlockSpec((1, D_MODEL, tn), lambda l, j: (l, 0, j)),
            pl.BlockSpec((1, 1, tn), lambda l, j: (l, 0, j)),
        ],
        out_specs=pl.BlockSpec((1, 16, tn), lambda l, j: (l, 0, j)),
        out_shape=jax.ShapeDtypeStruct((DEPTH, 16, 6 * D_MODEL), F32),
        compiler_params=_params(("arbitrary", "arbitrary")),
        name="ada",
    )(cvec, w_ada, b_ada.reshape(DEPTH, 1, 6 * D_MODEL))


def _tile_x(xl_ref, xc_ref, pair=1):
    latent = pl.program_id(1) < NT_LAT
    return [jnp.where(latent, xl_ref[p], xc_ref[p]) for p in range(pair)]


def _x_specs(ctx_block, pair=1):
    return [pl.BlockSpec((pair, TM, D_MODEL), lambda b, t: (b, jnp.minimum(t, NT_LAT - 1), 0)),
            pl.BlockSpec((pair, TM, D_MODEL), lambda b, t: (b, ctx_block, 0))]


def _prep_kernel(xl_ref, xc_ref, mod_ref, g1_ref, win_ref, wuq_ref, wuk_ref, wuv_ref, ekr_ref,
                 qng_ref, kvng_ref, gqg_ref, gkg_ref, gmat_ref,
                 c64_ref, u64_ref, d64_ref, cm_ref, um_ref, dm_ref,
                 rq_ref, rkt_ref, rv_ref, rg_ref, mq_ref, mkt_ref, mv_ref,
                 gq_ref, gkt_ref, gv_ref, wq_ref, wkt_ref, wv_ref):
    x = _tile_x(xl_ref, xc_ref)[0]
    md = mod_ref[0, 0]
    h = _modnorm(x, g1_ref[...], md[1:2], md[0:1])
    p = jnp.dot(h.astype(BF16), win_ref[...], preferred_element_type=F32)

    c64, u64, d64 = c64_ref[...], u64_ref[...], d64_ref[...]
    cm, um, dm = cm_ref[...], um_ref[...], dm_ref[...]
    gmat = gmat_ref[...]
    qk_scale = HEAD_DIM ** -0.5
    rope64 = lambda a: _rope(a, c64, u64, d64, HEAD_DIM // 2)
    ropem = lambda a: _rope(a, cm, um, dm, MLA_ROPE // 2)

    rq_ref[0] = (rope64(p[:, C_RET:C_RET + 256]) * qk_scale).astype(BF16)
    rkt_ref[0] = rope64(p[:, C_RET + 256:C_RET + 512]).T.astype(BF16)
    rv_ref[0] = p[:, C_RET + 512:C_RET + 768].astype(BF16)
    rg_ref[0] = p[:, C_RET + 768:C_RET + 1024].astype(BF16)

    qn = _rmsnorm(p[:, C_MQ:C_MQ + Q_LORA], qng_ref[...])
    q2 = ropem(_bdot(qn, wuq_ref[...])) * ((MLA_NOPE + MLA_ROPE) ** -0.5 * LOG2E)
    mq_ref[0] = q2.astype(BF16)
    kvn = _rmsnorm(p[:, C_MKV:C_MKV + KV_LORA], kvng_ref[...]).astype(BF16)
    k2 = (jnp.dot(kvn, wuk_ref[...], preferred_element_type=F32)
          + _dot_split_lhs(p[:, C_KR:C_KR + LANES], ekr_ref[...]))
    mkt_ref[0] = ropem(k2).T.astype(BF16)
    mv_ref[0] = jnp.dot(kvn, wuv_ref[...], preferred_element_type=F32).astype(BF16)

    gq = p[:, C_GQ:C_GQ + 256]
    gq = gq * lax.rsqrt(_head_mean(gq * gq, gmat) + EPS) * gqg_ref[...]
    gq_ref[0] = (rope64(gq) * (qk_scale * LOG2E)).astype(BF16)
    gk = p[:, C_GK:C_GK + 256]
    gk = gk * lax.rsqrt(_head_mean(gk * gk, gmat) + EPS) * gkg_ref[...]
    gkt_ref[0] = rope64(gk).T.astype(BF16)
    gv_ref[0] = p[:, C_GV:C_GV + 256].astype(BF16)

    wq_ref[0] = (rope64(p[:, C_WQ:C_WQ + 256]) * qk_scale).astype(BF16)
    wkt_ref[0] = rope64(p[:, C_WK:C_WK + 256]).T.astype(BF16)
    wv_ref[0] = p[:, C_WV:C_WV + 256].astype(BF16)


def _prep_call(xl, xc, ctx_block, mod, g1, win_p, wuq, wuk, wuv, ekr, qng, kvng, gqg, gkg, gmat, tabs):
    tok = lambda w: pl.BlockSpec((1, TM, w), lambda b, t: (b, t, 0))
    tokt = lambda w: pl.BlockSpec((1, w, TM), lambda b, t: (b, 0, t))
    tab = pl.BlockSpec((TM, LANES), lambda b, t: (t, 0))
    sd = lambda w: jax.ShapeDtypeStruct((BATCH, TOK, w), BF16)
    sdt = lambda w: jax.ShapeDtypeStruct((BATCH, w, TOK), BF16)
    in_specs = _x_specs(ctx_block) + [
        pl.BlockSpec((1, 1, 8, D_MODEL), lambda b, t: (b, t // NT_LAT, 0, 0)),
        _const_spec((1, D_MODEL)),
        _const_spec((D_MODEL, N_IN_P)),
        _const_spec((Q_LORA, MLA_QK)),
        _const_spec((KV_LORA, MLA_QK)),
        _const_spec((KV_LORA, MIX_W)),
        _const_spec((LANES, MLA_QK)),
        _const_spec((1, Q_LORA)),
        _const_spec((1, KV_LORA)),
        _const_spec((1, MIX_W)),
        _const_spec((1, MIX_W)),
        _const_spec((MIX_W, MIX_W)),
        tab, tab, tab, tab, tab, tab,
    ]
    out_specs = [tok(256), tokt(256), tok(256), tok(256),
                 tok(MLA_QK), tokt(MLA_QK), tok(256),
                 tok(256), tokt(256), tok(256),
                 tok(256), tokt(256), tok(256)]
    out_shape = [sd(256), sdt(256), sd(256), sd(256),
                 sd(MLA_QK), sdt(MLA_QK), sd(256),
                 sd(256), sdt(256), sd(256),
                 sd(256), sdt(256), sd(256)]
    return pl.pallas_call(
        _prep_kernel,
        grid=(BATCH, NT),
        in_specs=in_specs,
        out_specs=out_specs,
        out_shape=out_shape,
        compiler_params=_params(("parallel", "parallel")),
        name="prep",
    )(xl, xc, mod, g1, win_p, wuq, wuk, wuv, ekr, qng, kvng, gqg, gkg, gmat, *tabs)


def _ret_kernel(qf_ref, ktf_ref, vf_ref, qb_ref, ktb_ref, vb_ref,
                dmat_ref, xi_ref, zt_ref, gc_ref, of_ref, ob_ref, sf_ref, sb_ref):
    i = pl.program_id(0)

    @pl.when(i == 0)
    def _():
        sf_ref[...] = jnp.zeros_like(sf_ref)
        sb_ref[...] = jnp.zeros_like(sb_ref)

    lane_head = lax.broadcasted_iota(jnp.int32, (RET_CHUNK, MIX_W), 1) // HEAD_DIM
    r_head = lax.broadcasted_iota(jnp.int32, (MIX_W, MIX_W), 0) // HEAD_DIM
    c_head = lax.broadcasted_iota(jnp.int32, (MIX_W, MIX_W), 1) // HEAD_DIM
    block_diag = r_head == c_head

    dirs = ((qf_ref, ktf_ref, vf_ref, of_ref, sf_ref), (qb_ref, ktb_ref, vb_ref, ob_ref, sb_ref))

    def body(b, carry):
        q = [r[0][b].astype(F32) for r in dirs]
        kt = [r[1][b] for r in dirs]
        v = [r[2][b] for r in dirs]
        s_old = [r[4][b] for r in dirs]
        o = [_bdot(q[d] * xi_ref[d], s_old[d]) for d in range(2)]
        inner = {}
        for hd in range(N_HEADS):
            for d in range(2):
                qm = jnp.where(lane_head == hd, q[d], 0.0).astype(BF16)
                inner[d, hd] = (jnp.dot(qm, kt[d], preferred_element_type=F32)
                                * dmat_ref[d, hd]).astype(BF16)
        upd = [jnp.dot((kt[d].astype(F32) * zt_ref[d]).astype(BF16), v[d], preferred_element_type=F32)
               for d in range(2)]
        for hd in range(N_HEADS):
            for d in range(2):
                oh = jnp.dot(inner[d, hd], v[d], preferred_element_type=F32)
                o[d] = o[d] + jnp.where(lane_head == hd, oh, 0.0)
        for d in range(2):
            dirs[d][3][b] = o[d]
            dirs[d][4][b] = gc_ref[d] * s_old[d] + jnp.where(block_diag, upd[d], 0.0)
        return carry

    lax.fori_loop(0, BATCH, body, 0)


def _ret_call(rq, rkt, rv, dmat, xi, zt, gc):
    cf = lambda i: (i + SEQ // RET_CHUNK) % N_CHUNK
    cb = lambda i: N_CHUNK - 1 - i
    rows = lambda f: pl.BlockSpec((BATCH, RET_CHUNK, MIX_W), lambda i: (0, f(i), 0))
    cols = lambda f: pl.BlockSpec((BATCH, MIX_W, RET_CHUNK), lambda i: (0, 0, f(i)))
    return pl.pallas_call(
        _ret_kernel,
        grid=(N_CHUNK,),
        in_specs=[rows(cf), cols(cf), rows(cf), rows(cb), cols(cb), rows(cb),
                  _const_spec((2, N_HEADS, RET_CHUNK, RET_CHUNK)),
                  _const_spec((2, RET_CHUNK, MIX_W)),
                  _const_spec((2, MIX_W, RET_CHUNK)),
                  _const_spec((2, MIX_W, MIX_W))],
        out_specs=[rows(cf), rows(cb)],
        out_shape=[jax.ShapeDtypeStruct((BATCH, TOK, MIX_W), F32)] * 2,
        scratch_shapes=[pltpu.VMEM((BATCH, MIX_W, MIX_W), F32),
                        pltpu.VMEM((BATCH, MIX_W, MIX_W), F32)],
        compiler_params=_params(("arbitrary",)),
        name="retention",
    )(rq, rkt, rv, rq, rkt, rv, dmat, xi, zt, gc)


def _head_q(q_ref, hd, pair, rows=slice(None)):
    if not pair:
        return q_ref[0, rows, hd * LANES:(hd + 1) * LANES], hd * LANES
    c = hd // 2
    qc = q_ref[0, rows, c * LANES:(c + 1) * LANES]
    half = lax.broadcasted_iota(jnp.int32, qc.shape, 1) // HEAD_DIM
    return jnp.where(half == hd % 2, qc, jnp.zeros_like(qc)), c * LANES


def _dense_kernel(q_ref, kt_ref, v_ref, o_ref, *, pair):
    t = pl.program_id(1)
    lane_head = lax.broadcasted_iota(jnp.int32, (TM, MIX_W), 1) // HEAD_DIM

    def run(k_lo, k_hi):
        v = v_ref[0, k_lo:k_hi, :]
        acc = jnp.zeros((TM, MIX_W), F32)

        def scores(hd):
            qm, r0 = _head_q(q_ref, hd, pair)
            return jnp.dot(qm, kt_ref[0, r0:r0 + LANES, k_lo:k_hi], preferred_element_type=F32)

        s_next = scores(0)
        for hd in range(N_HEADS):
            s = s_next
            if hd + 1 < N_HEADS:
                s_next = scores(hd + 1)
            m = jnp.max(s, axis=-1, keepdims=True)
            p = jnp.exp2(s - m)
            l = jnp.sum(p, axis=-1, keepdims=True)
            oh = jnp.dot(p.astype(BF16), v, preferred_element_type=F32)
            acc = jnp.where(lane_head == hd, oh * (1.0 / l), acc)
        o_ref[0] = acc.astype(BF16)

    @pl.when(t < NT_LAT)
    def _():
        run(0, TOK)

    @pl.when(t >= NT_LAT)
    def _():
        run(SEQ, TOK)


def _dense_call(q, kt, v, nq, pair, name):
    wq = q.shape[-1]
    return pl.pallas_call(
        functools.partial(_dense_kernel, pair=pair),
        grid=(BATCH, nq),
        in_specs=[pl.BlockSpec((1, TM, wq), lambda b, t: (b, t, 0)),
                  pl.BlockSpec((1, kt.shape[1], TOK), lambda b, t: (b, 0, 0)),
                  pl.BlockSpec((1, TOK, MIX_W), lambda b, t: (b, 0, 0))],
        out_specs=pl.BlockSpec((1, TM, MIX_W), lambda b, t: (b, t, 0)),
        out_shape=jax.ShapeDtypeStruct((BATCH, nq * TM, MIX_W), BF16),
        compiler_params=_params(("parallel", "arbitrary")),
        name=name,
    )(q, kt, v)


def _win_kernel(sink_ref, q_ref, kt_ref, v_ref, o_ref):
    t = pl.program_id(1)
    lane_head = lax.broadcasted_iota(jnp.int32, (TM, MIX_W), 1) // HEAD_DIM
    v_ctx = v_ref[0, SEQ:TOK, :]

    @pl.when(t < NT_LAT)
    def _():
        start = pl.multiple_of(jnp.clip(t * TM - WINDOW, 0, SEQ - WIN_KEYS), LANES)
        qpos = t * TM + lax.broadcasted_iota(jnp.int32, (TM, WIN_KEYS), 0)
        kpos = start + lax.broadcasted_iota(jnp.int32, (TM, WIN_KEYS), 1)
        valid = jnp.abs(kpos - qpos) <= WINDOW
        v_loc = v_ref[0, pl.ds(start, WIN_KEYS), :]
        acc = jnp.zeros((TM, MIX_W), F32)

        def scores(hd):
            qm, r0 = _head_q(q_ref, hd, True)
            loc = jnp.dot(qm, kt_ref[0, r0:r0 + LANES, pl.ds(start, WIN_KEYS)], preferred_element_type=F32)
            ctx = jnp.dot(qm, kt_ref[0, r0:r0 + LANES, SEQ:TOK], preferred_element_type=F32)
            return jnp.where(valid, loc, NEG_INF), ctx

        s_next = scores(0)
        for hd in range(N_HEADS):
            s_loc, s_ctx = s_next
            if hd + 1 < N_HEADS:
                s_next = scores(hd + 1)
            sk = sink_ref[hd]
            m = jnp.maximum(jnp.maximum(jnp.max(s_loc, axis=-1, keepdims=True),
                                        jnp.max(s_ctx, axis=-1, keepdims=True)), sk)
            p_loc = jnp.exp(s_loc - m)
            p_ctx = jnp.exp(s_ctx - m)
            l = (jnp.sum(p_loc, axis=-1, keepdims=True) + jnp.sum(p_ctx, axis=-1, keepdims=True)
                 + jnp.exp(sk - m))
            oh = (jnp.dot(p_loc.astype(BF16), v_loc, preferred_element_type=F32)
                  + jnp.dot(p_ctx.astype(BF16), v_ctx, preferred_element_type=F32))
            acc = jnp.where(lane_head == hd, oh * (1.0 / l), acc)
        o_ref[0] = acc.astype(BF16)

    @pl.when(t >= NT_LAT)
    def _():
        acc = jnp.zeros((TM, MIX_W), F32)
        for hd in range(N_HEADS):
            qm, r0 = _head_q(q_ref, hd, True)
            s = jnp.dot(qm, kt_ref[0, r0:r0 + LANES, SEQ:TOK], preferred_element_type=F32)
            sk = sink_ref[hd]
            m = jnp.maximum(jnp.max(s, axis=-1, keepdims=True), sk)
            p = jnp.exp(s - m)
            l = jnp.sum(p, axis=-1, keepdims=True) + jnp.exp(sk - m)
            oh = jnp.dot(p.astype(BF16), v_ctx, preferred_element_type=F32)
            acc = jnp.where(lane_head == hd, oh * (1.0 / l), acc)
        o_ref[0] = acc.astype(BF16)


def _win_call(sink, q, kt, v, nq):
    return pl.pallas_call(
        _win_kernel,
        grid=(BATCH, nq),
        in_specs=[pl.BlockSpec(memory_space=pltpu.SMEM),
                  pl.BlockSpec((1, TM, MIX_W), lambda b, t: (b, t, 0)),
                  pl.BlockSpec((1, MIX_W, TOK), lambda b, t: (b, 0, 0)),
                  pl.BlockSpec((1, TOK, MIX_W), lambda b, t: (b, 0, 0))],
        out_specs=pl.BlockSpec((1, TM, MIX_W), lambda b, t: (b, t, 0)),
        out_shape=jax.ShapeDtypeStruct((BATCH, nq * TM, MIX_W), BF16),
        compiler_params=_params(("parallel", "arbitrary")),
        name="window",
    )(sink, q, kt, v)


def _merge_kernel(xl_ref, xc_ref, mod_ref, g1_ref, g2_ref, of_ref, ob_ref, rg_ref, ym_ref, yg_ref, yw_ref,
                  wg_ref, wb_ref, wo_ref, wr_ref, gmat_ref, xo_ref, h2_ref, lg_ref):
    xs = _tile_x(xl_ref, xc_ref, PAIR)
    mds = [mod_ref[p, 0] for p in range(PAIR)]
    stack = lambda f: jnp.concatenate([f(p) for p in range(PAIR)], axis=0)
    flat = lambda ref: ref[...].reshape(PAIR * TM, ref.shape[-1])
    hb = stack(lambda p: _modnorm(xs[p], g1_ref[...], mds[p][1:2], mds[p][0:1])).astype(BF16)

    gmat = gmat_ref[...]
    o = flat(of_ref) + flat(ob_ref)
    dlt = o - _head_mean(o, gmat)
    var = _head_mean(dlt * dlt, gmat)
    g = flat(rg_ref).astype(F32)
    y_ret = dlt * lax.rsqrt(var + EPS) * (g * _sigmoid(g))

    ys = (y_ret.astype(BF16), flat(ym_ref), flat(yg_ref), flat(yw_ref))
    acc = jnp.zeros((PAIR * TM, D_MODEL), F32)
    for i in range(4):
        gate = _sigmoid(jnp.dot(hb, wg_ref[:, i * D_MODEL:(i + 1) * D_MODEL],
                                      preferred_element_type=F32))
        acc = acc + gate * jnp.dot(ys[i], wb_ref[i], preferred_element_type=F32)
    out = jnp.dot(acc.astype(BF16), wo_ref[...], preferred_element_type=F32)
    wh, wl = _split(wr_ref[...])
    nt = lambda a, b: lax.dot_general(a, b, (((1,), (1,)), ((), ())), preferred_element_type=F32)
    for p in range(PAIR):
        xm = xs[p] + mds[p][2:3] * out[p * TM:(p + 1) * TM]
        xo_ref[p] = xm
        h2 = _modnorm(xm, g2_ref[...], mds[p][4:5], mds[p][3:4])
        h2_ref[p] = h2
        hh, hl = _split(h2)
        lg_ref[p] = nt(wh, hh) + nt(wh, hl) + nt(wl, hh)


def _merge_call(xl, xc, ctx_block, mod, g1, g2, of, ob, rg, ym, yg, yw, wg, wb, wo, wr, gmat, nq):
    tok = lambda w: pl.BlockSpec((PAIR, TM, w), lambda b, t: (b, t, 0))
    return pl.pallas_call(
        _merge_kernel,
        grid=(BATCH // PAIR, nq),
        in_specs=_x_specs(ctx_block, PAIR) + [
                  pl.BlockSpec((PAIR, 1, 8, D_MODEL), lambda b, t: (b, t // NT_LAT, 0, 0)),
                  _const_spec((1, D_MODEL)), _const_spec((1, D_MODEL)),
                  tok(MIX_W), tok(MIX_W), tok(MIX_W), tok(MIX_W), tok(MIX_W), tok(MIX_W),
                  _const_spec((D_MODEL, 4 * D_MODEL)),
                  _const_spec((4, MIX_W, D_MODEL)),
                  _const_spec((D_MODEL, D_MODEL)),
                  _const_spec((N_EXPERTS, D_MODEL)),
                  _const_spec((MIX_W, MIX_W))],
        out_specs=[tok(D_MODEL), tok(D_MODEL),
                   pl.BlockSpec((PAIR, N_EXPERTS, TM), lambda b, t: (b, 0, t))],
        out_shape=[jax.ShapeDtypeStruct((BATCH, nq * TM, D_MODEL), F32),
                   jax.ShapeDtypeStruct((BATCH, nq * TM, D_MODEL), F32),
                   jax.ShapeDtypeStruct((BATCH, N_EXPERTS, nq * TM), F32)],
        compiler_params=_params(("parallel", "parallel")),
        name="merge",
    )(xl, xc, mod, g1, g2, of, ob, rg, ym, yg, yw, wg, wb, wo, wr, gmat)


RANK_BITS = 20
RANK_MASK = (1 << RANK_BITS) - 1


def _route_kernel(lg_ref, b_ref, tri_ref, pk_ref, gw_ref, cnt_ref, base_ref):
    i = pl.program_id(0)

    @pl.when(i == 0)
    def _():
        base_ref[...] = jnp.zeros_like(base_ref)

    s = jax.nn.sigmoid(lg_ref[0])
    sel = s + b_ref[:, 0:1]
    sub = lax.broadcasted_iota(jnp.int32, (EXPERTS_PER_GROUP, TM), 0)
    best = e1 = e2 = s1 = s2 = None
    for g in range(N_GROUPS):
        rows = slice(g * EXPERTS_PER_GROUP, (g + 1) * EXPERTS_PER_GROUP)
        blk, sb = sel[rows], s[rows]
        m1 = jnp.max(blk, axis=0, keepdims=True)
        i1 = jnp.min(jnp.where(blk == m1, sub, EXPERTS_PER_GROUP), axis=0, keepdims=True)
        hit1 = sub == i1
        blk2 = jnp.where(hit1, -jnp.inf, blk)
        m2 = jnp.max(blk2, axis=0, keepdims=True)
        i2 = jnp.min(jnp.where(blk2 == m2, sub, EXPERTS_PER_GROUP), axis=0, keepdims=True)
        hit2 = sub == i2
        score = m1 + m2
        s1g = jnp.sum(jnp.where(hit1, sb, 0.0), axis=0, keepdims=True)
        s2g = jnp.sum(jnp.where(hit2, sb, 0.0), axis=0, keepdims=True)
        e1g = g * EXPERTS_PER_GROUP + i1
        e2g = g * EXPERTS_PER_GROUP + i2
        if g == 0:
            best, e1, e2, s1, s2 = score, e1g, e2g, s1g, s2g
        else:
            better = score > best
            best = jnp.where(better, score, best)
            e1 = jnp.where(better, e1g, e1)
            e2 = jnp.where(better, e2g, e2)
            s1 = jnp.where(better, s1g, s1)
            s2 = jnp.where(better, s2g, s2)

    eid = lax.broadcasted_iota(jnp.int32, (N_EXPERTS, TM), 0)
    oh1 = eid == e1
    oh2 = eid == e2
    oh = jnp.where(oh1 | oh2, 1.0, 0.0)
    before = jnp.dot(oh.astype(BF16), tri_ref[...], preferred_element_type=F32) + base_ref[:, 0:1]
    r1 = jnp.sum(jnp.where(oh1, before, 0.0), axis=0, keepdims=True).astype(jnp.int32)
    r2 = jnp.sum(jnp.where(oh2, before, 0.0), axis=0, keepdims=True).astype(jnp.int32)
    total = base_ref[...] + jnp.sum(oh, axis=1, keepdims=True)
    base_ref[...] = total
    cnt_ref[...] = total

    pk_ref[...] = jnp.concatenate([(e1 << RANK_BITS) + r1, (e2 << RANK_BITS) + r2], axis=0)
    den = s1 + s2
    row = lax.broadcasted_iota(jnp.int32, (8, TM), 0)
    gw_ref[...] = jnp.where(row == 0, s1 / den, jnp.where(row == 1, s2 / den, 0.0))


def _route_call(lgt, b_router, nq):
    n_tiles = BATCH * nq
    n_tok = n_tiles * TM
    r = jnp.arange(TM)
    tri = (r[:, None] < r[None, :]).astype(BF16)
    bcol = jnp.broadcast_to(b_router.astype(F32)[:, None], (N_EXPERTS, LANES))
    return pl.pallas_call(
        _route_kernel,
        grid=(n_tiles,),
        in_specs=[pl.BlockSpec((1, N_EXPERTS, TM), lambda i: (i // nq, 0, i % nq)),
                  _const_spec((N_EXPERTS, LANES)),
                  _const_spec((TM, TM))],
        out_specs=[pl.BlockSpec((TOP_K, TM), lambda i: (0, i)),
                   pl.BlockSpec((8, TM), lambda i: (0, i)),
                   pl.BlockSpec((N_EXPERTS, LANES), lambda i: (0, 0))],
        out_shape=[jax.ShapeDtypeStruct((TOP_K, n_tok), jnp.int32),
                   jax.ShapeDtypeStruct((8, n_tok), F32),
                   jax.ShapeDtypeStruct((N_EXPERTS, LANES), F32)],
        scratch_shapes=[pltpu.VMEM((N_EXPERTS, LANES), F32)],
        compiler_params=_params(("arbitrary",)),
        name="route",
    )(lgt, bcol, tri)


def _slot_rows(pk, row0):
    e = pk >> RANK_BITS
    base = jnp.sum(jnp.where(e[..., None] == jnp.arange(N_EXPERTS), row0, 0), axis=-1)
    return (base + (pk & RANK_MASK)).astype(jnp.int32).reshape(-1)


def _slot_row(rows_ref, k, n, n_tok):
    return rows_ref[k * n_tok + n]


def _combine_kernel(rows_ref, x_ref, mod_ref, gw_ref, fg_ref, ys_hbm, o_ref, ybuf, sem,
                    *, n_tok, final):
    i = pl.program_id(0)
    n_tiles = pl.num_programs(0)
    slot = i % 2

    def issue(tile, sl):
        def one(j, c):
            n = tile * TM + j
            for k in range(TOP_K):
                pltpu.make_async_copy(ys_hbm.at[pl.ds(_slot_row(rows_ref, k, n, n_tok), 1)],
                                      ybuf.at[sl, k, pl.ds(j, 1)], sem.at[sl]).start()
            return c

        lax.fori_loop(0, TM, one, 0, unroll=4)

    @pl.when(i == 0)
    def _():
        issue(0, 0)

    @pl.when(i + 1 < n_tiles)
    def _():
        issue(i + 1, 1 - slot)

    for k in range(TOP_K):
        pltpu.make_async_copy(ys_hbm.at[pl.ds(0, TM)], ybuf.at[slot, k], sem.at[slot]).wait()

    md = mod_ref[0, 0]
    gw = gw_ref[...].T
    f = gw[:, 0:1] * ybuf[slot, 0] + gw[:, 1:2] * ybuf[slot, 1]
    xn = x_ref[0] + md[5:6] * f
    o_ref[0] = _rmsnorm(xn, fg_ref[...]) if final else xn


def _combine_call(rows, xm, mod, gw, fg, ys, nq, final):
    n_tok = BATCH * nq * TM
    grid_spec = pltpu.PrefetchScalarGridSpec(
        num_scalar_prefetch=1,
        grid=(BATCH * nq,),
        in_specs=[pl.BlockSpec((1, TM, D_MODEL), lambda i, *_: (i // nq, i % nq, 0)),
                  pl.BlockSpec((1, 1, 8, D_MODEL), lambda i, *_: (i // nq, (i % nq) // NT_LAT, 0, 0)),
                  pl.BlockSpec((8, TM), lambda i, *_: (0, i)),
                  pl.BlockSpec((1, D_MODEL), lambda i, *_: (0, 0)),
                  pl.BlockSpec(memory_space=pl.ANY)],
        out_specs=pl.BlockSpec((1, TM, D_MODEL), lambda i, *_: (i // nq, i % nq, 0)),
        scratch_shapes=[pltpu.VMEM((2, TOP_K, TM, D_MODEL), F32),
                        pltpu.SemaphoreType.DMA((2,))])
    return pl.pallas_call(
        functools.partial(_combine_kernel, n_tok=n_tok, final=final),
        grid_spec=grid_spec,
        out_shape=jax.ShapeDtypeStruct((BATCH, nq * TM, D_MODEL), F32),
        compiler_params=_params(("arbitrary",)),
        name="combine",
    )(rows, xm, mod, gw, fg, ys)


def _fmoe_kernel(rows_ref, row0_ref, nblk_ref, cnt_ref, h_hbm, w1_ref, w3_ref, w2_ref, ys_hbm,
                 tok, xbuf, ybuf, w1b, w3b, w2b, isem, osem, *, n_tok, n_blocks):
    e = pl.program_id(0)
    nb = nblk_ref[e]
    b0 = row0_ref[e] // MOE_ROWS
    used = row0_ref[N_EXPERTS - 1] // MOE_ROWS + nblk_ref[N_EXPERTS - 1]

    def issue_gather(b, lo=0, hi=MOE_ROWS):
        slot = b % MOE_SLOTS
        for j in range(lo, hi):
            pltpu.make_async_copy(h_hbm.at[pl.ds(tok[b * MOE_ROWS + j], 1)], xbuf.at[slot, pl.ds(j, 1)],
                                  isem.at[slot]).start()

    def wait_gather(b):
        slot = b % MOE_SLOTS
        pltpu.make_async_copy(h_hbm.at[pl.ds(0, MOE_ROWS)], xbuf.at[slot], isem.at[slot]).wait()

    def out_copy(b):
        rows = pl.ds(pl.multiple_of(b * MOE_ROWS, MOE_ROWS), MOE_ROWS)
        return pltpu.make_async_copy(ybuf.at[b % 2], ys_hbm.at[rows], osem.at[b % 2])

    @pl.when(e == 0)
    def _():
        def clear(lo, hi):
            def one(p, c):
                tok[p] = 0
                return c

            lax.fori_loop(lo, hi, one, 0)

        clear(used * MOE_ROWS, (used + MOE_SLOTS - 1) * MOE_ROWS)

        def pad_e(x, c):
            clear(row0_ref[x] + cnt_ref[x], row0_ref[x] + nblk_ref[x] * MOE_ROWS)
            return c

        lax.fori_loop(0, N_EXPERTS, pad_e, 0)

        def put(n, c):
            for k in range(TOP_K):
                tok[rows_ref[k * n_tok + n]] = n
            return c

        lax.fori_loop(0, n_tok, put, 0, unroll=8)

        def prime(idx, c):
            b = idx // MOE_ROWS
            pltpu.make_async_copy(h_hbm.at[pl.ds(tok[idx], 1)], xbuf.at[b, pl.ds(idx % MOE_ROWS, 1)],
                                  isem.at[b]).start()
            return c

        lax.fori_loop(0, (MOE_SLOTS - 1) * MOE_ROWS, prime, 0)

    @pl.when(nb > 0)
    def _():
        w1b[...] = w1_ref[0, 0].astype(BF16)
        w3b[...] = w3_ref[0, 0].astype(BF16)
        w2b[...] = w2_ref[0, 0].astype(BF16)

        def block(r, carry):
            g = b0 + r
            wait_gather(g)

            @pl.when(g >= 2)
            def _():
                out_copy(g - 2).wait()

            ahead = g + MOE_SLOTS - 1
            piece = MOE_ROWS // MOE_PIECES
            half = D_EXPERT // 2
            xb = xbuf[g % MOE_SLOTS].astype(BF16)
            hid = []
            for c in range(2):
                cols = slice(c * half, (c + 1) * half)
                issue_gather(ahead, (2 * c) * piece, (2 * c + 1) * piece)
                h1 = jnp.dot(xb, w1b[:, cols], preferred_element_type=F32)
                issue_gather(ahead, (2 * c + 1) * piece, (2 * c + 2) * piece)
                h3 = jnp.dot(xb, w3b[:, cols], preferred_element_type=F32)
                hid.append((h1 * _sigmoid(h1) * h3).astype(BF16))
            hid = jnp.concatenate(hid, axis=1)
            for c in range(2):
                cols = slice(c * half, (c + 1) * half)
                issue_gather(ahead, (4 + c) * piece, (5 + c) * piece if c == 0 else MOE_ROWS)
                ybuf[g % 2, :, cols] = jnp.dot(hid, w2b[:, cols], preferred_element_type=F32)
            out_copy(g).start()
            return carry

        lax.fori_loop(0, nb, block, 0)

    @pl.when(e == N_EXPERTS - 1)
    def _():
        @pl.when(used >= 2)
        def _():
            out_copy(used - 2).wait()

        @pl.when(used >= 1)
        def _():
            out_copy(used - 1).wait()

        for b in range(MOE_SLOTS - 1):
            wait_gather(used + b)

        ybuf[0] = jnp.zeros((MOE_ROWS, D_MODEL), F32)

        def tail_copy(bk):
            return pltpu.make_async_copy(
                ybuf.at[0], ys_hbm.at[pl.ds(pl.multiple_of(bk * MOE_ROWS, MOE_ROWS), MOE_ROWS)], osem.at[0])

        lax.fori_loop(used, n_blocks, lambda bk, c: (tail_copy(bk).start(), c)[1], 0)
        lax.fori_loop(used, n_blocks, lambda bk, c: (tail_copy(bk).wait(), c)[1], 0)


def _fmoe_call(rows, row0, nblk, cnt, h2, w1, w3, w2, l):
    n_tok = h2.shape[0]
    n_blocks = TOP_K * n_tok // MOE_ROWS + N_EXPERTS
    wspec = pl.BlockSpec((1, 1, D_MODEL, D_EXPERT), lambda e, *_: (l, e, 0, 0))
    grid_spec = pltpu.PrefetchScalarGridSpec(
        num_scalar_prefetch=4,
        grid=(N_EXPERTS,),
        in_specs=[pl.BlockSpec(memory_space=pl.ANY), wspec, wspec,
                  pl.BlockSpec((1, 1, D_EXPERT, D_MODEL), lambda e, *_: (l, e, 0, 0))],
        out_specs=pl.BlockSpec(memory_space=pl.ANY),
        scratch_shapes=[pltpu.SMEM(((n_blocks + MOE_SLOTS - 1) * MOE_ROWS,), jnp.int32),
                        pltpu.VMEM((MOE_SLOTS, MOE_ROWS, D_MODEL), F32),
                        pltpu.VMEM((2, MOE_ROWS, D_MODEL), F32),
                        pltpu.VMEM((D_MODEL, D_EXPERT), BF16),
                        pltpu.VMEM((D_MODEL, D_EXPERT), BF16),
                        pltpu.VMEM((D_EXPERT, D_MODEL), BF16),
                        pltpu.SemaphoreType.DMA((MOE_SLOTS,)),
                        pltpu.SemaphoreType.DMA((2,))])
    return pl.pallas_call(
        functools.partial(_fmoe_kernel, n_tok=n_tok, n_blocks=n_blocks),
        grid_spec=grid_spec,
        out_shape=jax.ShapeDtypeStruct((n_blocks * MOE_ROWS, D_MODEL), F32),
        compiler_params=_params(("arbitrary",)),
        name="moe",
    )(rows, row0, nblk, cnt, h2, w1, w3, w2)


def _rope_tables():
    rows = SEQ // GRID_W
    row = jnp.broadcast_to(jnp.arange(rows)[:, None], (rows, GRID_W)).reshape(-1).astype(F32)
    col = jnp.broadcast_to(jnp.arange(GRID_W)[None, :], (rows, GRID_W)).reshape(-1).astype(F32)

    def cs(rot_dim):
        n_f = rot_dim // 4
        inv = ROPE_BASE ** (-jnp.arange(n_f, dtype=F32) / n_f)
        ang = jnp.concatenate([row[:, None] * inv, col[:, None] * inv], axis=-1)
        return jnp.cos(ang), jnp.sin(ang)

    def with_ctx(c, u, d):
        one = jnp.ones((CTX_LEN, LANES), F32)
        zero = jnp.zeros((CTX_LEN, LANES), F32)
        return (jnp.concatenate([c, one]), jnp.concatenate([u, zero]), jnp.concatenate([d, zero]))

    cos, sin = cs(HEAD_DIM)
    z = jnp.zeros_like(sin)
    t64 = with_ctx(jnp.tile(jnp.concatenate([cos, cos], -1), (1, 2)),
                   jnp.tile(jnp.concatenate([-sin, z], -1), (1, 2)),
                   jnp.tile(jnp.concatenate([z, sin], -1), (1, 2)))
    cos, sin = cs(MLA_ROPE)
    z = jnp.zeros_like(sin)
    one_n = jnp.ones((SEQ, MLA_NOPE), F32)
    zero_n = jnp.zeros((SEQ, MLA_NOPE), F32)
    one_p = jnp.ones((SEQ, LANES - MLA_NOPE - MLA_ROPE), F32)
    zero_p = jnp.zeros((SEQ, LANES - MLA_NOPE - MLA_ROPE), F32)
    tm = with_ctx(jnp.concatenate([one_n, cos, cos, one_p], -1),
                  jnp.concatenate([zero_n, -sin, z, zero_p], -1),
                  jnp.concatenate([zero_n, z, sin, zero_p], -1))
    return t64 + tm


def _ret_tables(decay):
    lg = -jnp.exp(decay.astype(F32))
    idx = jnp.arange(RET_CHUNK, dtype=F32)
    diff = idx[:, None] - idx[None, :]
    fwd = diff >= 0
    bwd = diff < 0
    dm_f = jnp.where(fwd, jnp.exp(lg[0][:, None, None] * jnp.where(fwd, diff, 0.0)), 0.0)
    dm_b = jnp.where(bwd, jnp.exp(lg[1][:, None, None] * jnp.where(bwd, -diff, 0.0)), 0.0)
    dmat = jnp.stack([dm_f, dm_b])
    xi = jnp.stack([jnp.exp(lg[0][:, None] * (idx + 1.0)),
                    jnp.exp(lg[1][:, None] * (RET_CHUNK - idx))])
    zeta = jnp.stack([jnp.exp(lg[0][:, None] * (RET_CHUNK - 1.0 - idx)),
                      jnp.exp(lg[1][:, None] * idx)])
    gch = jnp.exp(lg * RET_CHUNK)
    xi_t = jnp.repeat(jnp.transpose(xi, (0, 2, 1)), HEAD_DIM, axis=2)
    zt_t = jnp.repeat(zeta, HEAD_DIM, axis=1)
    gc_t = jnp.broadcast_to(jnp.repeat(gch, HEAD_DIM, axis=1)[:, :, None], (2, MIX_W, MIX_W))
    return dmat, xi_t, zt_t, gc_t


def _in_proj_columns():
    o_mla = 4 * MIX_W
    o_gqa = o_mla + Q_LORA + KV_LORA + MLA_ROPE
    kv_w = (N_HEADS // 2) * HEAD_DIM
    o_win = o_gqa + MIX_W + 2 * kv_w
    ar = jnp.arange
    dup = jnp.concatenate([ar(HEAD_DIM), ar(HEAD_DIM), HEAD_DIM + ar(HEAD_DIM), HEAD_DIM + ar(HEAD_DIM)])

    def gqa_cols(o):
        return [o + ar(MIX_W), o + MIX_W + dup, o + MIX_W + kv_w + dup]

    return jnp.concatenate([ar(o_mla), o_mla + ar(Q_LORA), o_mla + Q_LORA + ar(KV_LORA)]
                           + gqa_cols(o_gqa) + gqa_cols(o_win)
                           + [o_mla + Q_LORA + KV_LORA + ar(MLA_ROPE)])


def _layer_weights(l, w_in, mla_w_uq, mla_w_ukv):
    cols = _in_proj_columns()
    win_p = jnp.pad(w_in[l][:, cols], ((0, 0), (0, N_IN_P - cols.shape[0]))).astype(BF16)
    uq = mla_w_uq[l].reshape(Q_LORA, N_HEADS, MLA_NOPE + MLA_ROPE)
    wuq = jnp.pad(uq, ((0, 0), (0, 0), (0, LANES - MLA_NOPE - MLA_ROPE))).reshape(Q_LORA, MLA_QK)
    ukv = mla_w_ukv[l].reshape(KV_LORA, N_HEADS, MLA_NOPE + MLA_V)
    wuk = jnp.pad(ukv[:, :, :MLA_NOPE], ((0, 0), (0, 0), (0, LANES - MLA_NOPE))).reshape(KV_LORA, MLA_QK)
    wuv = ukv[:, :, MLA_NOPE:].reshape(KV_LORA, MIX_W)
    return win_p, wuq.astype(BF16), wuk.astype(BF16), wuv.astype(BF16)


def _krope_placement():
    r = jnp.arange(LANES)[:, None]
    c = jnp.arange(MLA_QK)[None, :]
    return ((r < MLA_ROPE) & (c % LANES == MLA_NOPE + r)).astype(BF16)


def _head_block_matrix():
    r = jnp.arange(MIX_W)
    return (r[:, None] // HEAD_DIM == r[None, :] // HEAD_DIM).astype(BF16)


def kernel(x, c, ctx, c_ctx, w_ada, b_ada, norm1_g, norm2_g, w_in, w_gate, w_branch, w_out, ret_decay,
           mla_qn_g, mla_w_uq, mla_kvn_g, mla_w_ukv, gqa_qn_g, gqa_kn_g, win_sink, w_router, b_router,
           w1, w3, w2, final_norm_g):
    cvec = jnp.concatenate([c, c_ctx[None, :], jnp.zeros((7, D_MODEL), F32)], axis=0)
    ada = _ada_call(cvec, w_ada, b_ada)
    tabs = _rope_tables()
    gmat = _head_block_matrix()
    ekr = _krope_placement()
    stream = (x, ctx, 0)
    out = None
    for l in range(DEPTH):
        last = l == DEPTH - 1
        nq = NT_LAT if last else NT
        m = ada[l].reshape(16, 6, D_MODEL)
        m_lat = m[:BATCH]
        m_ctx = jnp.broadcast_to(m[BATCH][None], (BATCH, 6, D_MODEL))
        mod = jnp.pad(jnp.stack([m_lat, m_ctx], axis=1), ((0, 0), (0, 0), (0, 2), (0, 0)))
        g1 = norm1_g[l][None, :]
        g2 = norm2_g[l][None, :]
        win_p, wuq, wuk, wuv = _layer_weights(l, w_in, mla_w_uq, mla_w_ukv)
        (rq, rkt, rv, rg, mq, mkt, mv, gq, gkt, gv, wq, wkt, wv) = _prep_call(
            *stream, mod, g1, win_p, wuq, wuk, wuv, ekr,
            mla_qn_g[l][None, :], mla_kvn_g[l][None, :],
            jnp.tile(gqa_qn_g[l], N_HEADS)[None, :], jnp.tile(gqa_kn_g[l], N_HEADS)[None, :],
            gmat, tabs)
        of, ob = _ret_call(rq, rkt, rv, *_ret_tables(ret_decay[l]))
        ym = _dense_call(mq, mkt, mv, nq, False, "mla")
        yg = _dense_call(gq, gkt, gv, nq, True, "gqa")
        yw = _win_call(win_sink[l], wq, wkt, wv, nq)
        xm, h2, lgt = _merge_call(
            *stream, mod, g1, g2, of, ob, rg, ym, yg, yw,
            w_gate[l].astype(BF16), w_branch[l].astype(BF16), w_out[l].astype(BF16), w_router.T, gmat, nq)
        n_tok = BATCH * nq * TM
        pk, gw, cnt = _route_call(lgt, b_router, nq)
        cnt = cnt[:, 0].astype(jnp.int32)
        nblk = (cnt + MOE_ROWS - 1) // MOE_ROWS
        row0 = (jnp.cumsum(nblk) - nblk) * MOE_ROWS
        rows = _slot_rows(pk, row0)
        ys = _fmoe_call(rows, row0, nblk, cnt, h2.reshape(n_tok, D_MODEL), w1, w3, w2, l)
        res = _combine_call(rows, xm, mod, gw, final_norm_g[None, :], ys, nq, last)
        if last:
            out = res
        else:
            stream = (res, res, NT_LAT)
    return out
```

```python
import functools

import jax
import jax.numpy as jnp
from jax import lax
from jax.experimental import pallas as pl
from jax.experimental.pallas import tpu as pltpu

F32 = jnp.float32
BF16 = jnp.bfloat16

D_MODEL = 1024
BATCH = 8
SEQ = 2048
DEPTH = 2
CTX_LEN = 256
TOK = SEQ + CTX_LEN
GRID_W = 64
N_HEADS = 4
HEAD_DIM = 64
MIX_W = N_HEADS * HEAD_DIM
RET_CHUNK = 128
Q_LORA = 256
KV_LORA = 128
MLA_NOPE = 64
MLA_ROPE = 32
MLA_V = 64
WINDOW = 128
N_EXPERTS = 32
N_GROUPS = 4
EXPERTS_PER_GROUP = N_EXPERTS // N_GROUPS
TOP_K = 2
D_EXPERT = 1024
ROPE_BASE = 10000.0
EPS = 1e-6
NEG_INF = -1e30

LANES = 128
TM = 256
NT = TOK // TM
NT_LAT = SEQ // TM
WIN_KEYS = TM + 2 * WINDOW
N_CHUNK = TOK // RET_CHUNK
LOG2E = 1.4426950408889634
PAIR = 2
MOE_ROWS = 128
MOE_SLOTS = 8
MOE_PIECES = 6
N_IN_P = 3072
MLA_QK = 4 * LANES

C_RET = 0
C_MQ = 1024
C_MKV = 1280
C_GQ = 1408
C_GK = 1664
C_GV = 1920
C_WQ = 2176
C_WK = 2432
C_WV = 2688
C_KR = 2944

VMEM_LIMIT = 56 * 1024 * 1024


def _params(sem, vmem=VMEM_LIMIT):
    return pltpu.CompilerParams(dimension_semantics=sem, vmem_limit_bytes=vmem)


def _const_spec(shape):
    nd = len(shape)
    return pl.BlockSpec(shape, lambda *_: (0,) * nd, pipeline_mode=pl.Buffered(1))


def _bdot(a, b):
    return jnp.dot(a.astype(BF16), b.astype(BF16), preferred_element_type=F32)


def _split(a):
    hi = a.astype(BF16)
    lo = (a - hi.astype(F32)).astype(BF16)
    return hi, lo


def _dot_split_lhs(a, b):
    hi, lo = _split(a)
    return (jnp.dot(hi, b, preferred_element_type=F32)
            + jnp.dot(lo, b, preferred_element_type=F32))


def _dot3(a, b):
    ah, al = _split(a)
    bh, bl = _split(b)
    return (jnp.dot(ah, bh, preferred_element_type=F32)
            + jnp.dot(ah, bl, preferred_element_type=F32)
            + jnp.dot(al, bh, preferred_element_type=F32))


def _modnorm(x, g, sc, sh):
    ms = jnp.mean(x * x, axis=-1, keepdims=True)
    return x * lax.rsqrt(ms + EPS) * g * (1.0 + sc) + sh


def _rmsnorm(x, g):
    ms = jnp.mean(x * x, axis=-1, keepdims=True)
    return x * lax.rsqrt(ms + EPS) * g


def _rope(x, c, s_up, s_dn, half):
    outs = []
    for j in range(x.shape[1] // LANES):
        xc = x[:, j * LANES:(j + 1) * LANES]
        outs.append(xc * c + pltpu.roll(xc, LANES - half, 1) * s_up + pltpu.roll(xc, half, 1) * s_dn)
    return outs[0] if len(outs) == 1 else jnp.concatenate(outs, axis=1)


def _sigmoid(x):
    return 0.5 * jnp.tanh(0.5 * x) + 0.5


def _head_mean(x, gmat):
    return _dot_split_lhs(x, gmat) * (1.0 / HEAD_DIM)


def _ada_kernel(c_ref, w_ref, b_ref, o_ref):
    c = c_ref[...]
    sc = c * jax.nn.sigmoid(c)
    o_ref[0] = _dot3(sc, w_ref[0]) + b_ref[0]


def _ada_call(cvec, w_ada, b_ada):
    tn = 1536
    return pl.pallas_call(
        _ada_kernel,
        grid=(DEPTH, 6 * D_MODEL // tn),
        in_specs=[
            pl.BlockSpec((16, D_MODEL), lambda l, j: (0, 0)),
            pl.BlockSpec((1, D_MODEL, tn), lambda l, j: (l, 0, j)),
            pl.BlockSpec((1, 1, tn), lambda l, j: (l, 0, j)),
        ],
        out_specs=pl.BlockSpec((1, 16, tn), lambda l, j: (l, 0, j)),
        out_shape=jax.ShapeDtypeStruct((DEPTH, 16, 6 * D_MODEL), F32),
        compiler_params=_params(("arbitrary", "arbitrary")),
        name="ada",
    )(cvec, w_ada, b_ada.reshape(DEPTH, 1, 6 * D_MODEL))


def _tile_x(xl_ref, xc_ref, pair=1):
    latent = pl.program_id(1) < NT_LAT
    return [jnp.where(latent, xl_ref[p], xc_ref[p]) for p in range(pair)]


def _x_specs(ctx_block, pair=1):
    return [pl.BlockSpec((pair, TM, D_MODEL), lambda b, t: (b, jnp.minimum(t, NT_LAT - 1), 0)),
            pl.BlockSpec((pair, TM, D_MODEL), lambda b, t: (b, ctx_block, 0))]


def _prep_kernel(xl_ref, xc_ref, mod_ref, g1_ref, win_ref, wuq_ref, wuk_ref, wuv_ref, ekr_ref,
                 qng_ref, kvng_ref, gqg_ref, gkg_ref, gmat_ref,
                 c64_ref, u64_ref, d64_ref, cm_ref, um_ref, dm_ref,
                 rq_ref, rkt_ref, rv_ref, rg_ref, mq_ref, mkt_ref, mv_ref,
                 gq_ref, gkt_ref, gv_ref, wq_ref, wkt_ref, wv_ref):
    xs = _tile_x(xl_ref, xc_ref, PAIR)

    def project(i):
        md = mod_ref[i, 0]
        h = _modnorm(xs[i], g1_ref[...], md[1:2], md[0:1])
        return jnp.dot(h.astype(BF16), win_ref[...], preferred_element_type=F32)

    refs = (rq_ref, rkt_ref, rv_ref, rg_ref, mq_ref, mkt_ref, mv_ref,
            gq_ref, gkt_ref, gv_ref, wq_ref, wkt_ref, wv_ref)
    p_next = project(0)
    for i in range(PAIR):
        p = p_next
        if i + 1 < PAIR:
            p_next = project(i + 1)
        _prep_mixers(p, i, wuq_ref, wuk_ref, wuv_ref, ekr_ref,
                     qng_ref, kvng_ref, gqg_ref, gkg_ref, gmat_ref,
                     c64_ref, u64_ref, d64_ref, cm_ref, um_ref, dm_ref, *refs)


def _prep_mixers(p, i, wuq_ref, wuk_ref, wuv_ref, ekr_ref, qng_ref, kvng_ref, gqg_ref, gkg_ref, gmat_ref,
                 c64_ref, u64_ref, d64_ref, cm_ref, um_ref, dm_ref,
                 rq_ref, rkt_ref, rv_ref, rg_ref, mq_ref, mkt_ref, mv_ref,
                 gq_ref, gkt_ref, gv_ref, wq_ref, wkt_ref, wv_ref):
    c64, u64, d64 = c64_ref[...], u64_ref[...], d64_ref[...]
    cm, um, dm = cm_ref[...], um_ref[...], dm_ref[...]
    gmat = gmat_ref[...]
    qk_scale = HEAD_DIM ** -0.5
    rope64 = lambda a: _rope(a, c64, u64, d64, HEAD_DIM // 2)
    ropem = lambda a: _rope(a, cm, um, dm, MLA_ROPE // 2)

    rq_ref[i] = (rope64(p[:, C_RET:C_RET + 256]) * qk_scale).astype(BF16)
    rkt_ref[i] = rope64(p[:, C_RET + 256:C_RET + 512]).T.astype(BF16)
    rv_ref[i] = p[:, C_RET + 512:C_RET + 768].astype(BF16)
    rg_ref[i] = p[:, C_RET + 768:C_RET + 1024].astype(BF16)

    qn = _rmsnorm(p[:, C_MQ:C_MQ + Q_LORA], qng_ref[...])
    q2 = ropem(_bdot(qn, wuq_ref[...])) * ((MLA_NOPE + MLA_ROPE) ** -0.5 * LOG2E)
    mq_ref[i] = q2.astype(BF16)
    kvn = _rmsnorm(p[:, C_MKV:C_MKV + KV_LORA], kvng_ref[...]).astype(BF16)
    k2 = (jnp.dot(kvn, wuk_ref[...], preferred_element_type=F32)
          + _dot_split_lhs(p[:, C_KR:C_KR + LANES], ekr_ref[...]))
    mkt_ref[i] = ropem(k2).T.astype(BF16)
    mv_ref[i] = jnp.dot(kvn, wuv_ref[...], preferred_element_type=F32).astype(BF16)

    gq = p[:, C_GQ:C_GQ + 256]
    gq = gq * lax.rsqrt(_head_mean(gq * gq, gmat) + EPS) * gqg_ref[...]
    gq_ref[i] = (rope64(gq) * (qk_scale * LOG2E)).astype(BF16)
    gk = p[:, C_GK:C_GK + 256]
    gk = gk * lax.rsqrt(_head_mean(gk * gk, gmat) + EPS) * gkg_ref[...]
    gkt_ref[i] = rope64(gk).T.astype(BF16)
    gv_ref[i] = p[:, C_GV:C_GV + 256].astype(BF16)

    wq_ref[i] = (rope64(p[:, C_WQ:C_WQ + 256]) * qk_scale).astype(BF16)
    wkt_ref[i] = rope64(p[:, C_WK:C_WK + 256]).T.astype(BF16)
    wv_ref[i] = p[:, C_WV:C_WV + 256].astype(BF16)


def _prep_call(xl, xc, ctx_block, mod, g1, win_p, wuq, wuk, wuv, ekr, qng, kvng, gqg, gkg, gmat, tabs):
    tok = lambda w: pl.BlockSpec((PAIR, TM, w), lambda b, t: (b, t, 0))
    tokt = lambda w: pl.BlockSpec((PAIR, w, TM), lambda b, t: (b, 0, t))
    tab = pl.BlockSpec((TM, LANES), lambda b, t: (t, 0))
    sd = lambda w: jax.ShapeDtypeStruct((BATCH, TOK, w), BF16)
    sdt = lambda w: jax.ShapeDtypeStruct((BATCH, w, TOK), BF16)
    in_specs = _x_specs(ctx_block, PAIR) + [
        pl.BlockSpec((PAIR, 1, 8, D_MODEL), lambda b, t: (b, t // NT_LAT, 0, 0)),
        _const_spec((1, D_MODEL)),
        _const_spec((D_MODEL, N_IN_P)),
        _const_spec((Q_LORA, MLA_QK)),
        _const_spec((KV_LORA, MLA_QK)),
        _const_spec((KV_LORA, MIX_W)),
        _const_spec((LANES, MLA_QK)),
        _const_spec((1, Q_LORA)),
        _const_spec((1, KV_LORA)),
        _const_spec((1, MIX_W)),
        _const_spec((1, MIX_W)),
        _const_spec((MIX_W, MIX_W)),
        tab, tab, tab, tab, tab, tab,
    ]
    out_specs = [tok(256), tokt(256), tok(256), tok(256),
                 tok(MLA_QK), tokt(MLA_QK), tok(256),
                 tok(256), tokt(256), tok(256),
                 tok(256), tokt(256), tok(256)]
    out_shape = [sd(256), sdt(256), sd(256), sd(256),
                 sd(MLA_QK), sdt(MLA_QK), sd(256),
                 sd(256), sdt(256), sd(256),
                 sd(256), sdt(256), sd(256)]
    return pl.pallas_call(
        _prep_kernel,
        grid=(BATCH // PAIR, NT),
        in_specs=in_specs,
        out_specs=out_specs,
        out_shape=out_shape,
        compiler_params=_params(("parallel", "parallel")),
        name="prep",
    )(xl, xc, mod, g1, win_p, wuq, wuk, wuv, ekr, qng, kvng, gqg, gkg, gmat, *tabs)


def _ret_kernel(qf_ref, ktf_ref, vf_ref, qb_ref, ktb_ref, vb_ref,
                dmat_ref, xi_ref, zt_ref, gc_ref, of_ref, ob_ref, sf_ref, sb_ref):
    i = pl.program_id(0)

    @pl.when(i == 0)
    def _():
        sf_ref[...] = jnp.zeros_like(sf_ref)
        sb_ref[...] = jnp.zeros_like(sb_ref)

    lane_head = lax.broadcasted_iota(jnp.int32, (RET_CHUNK, MIX_W), 1) // HEAD_DIM
    r_head = lax.broadcasted_iota(jnp.int32, (MIX_W, MIX_W), 0) // HEAD_DIM
    c_head = lax.broadcasted_iota(jnp.int32, (MIX_W, MIX_W), 1) // HEAD_DIM
    block_diag = r_head == c_head

    dirs = ((qf_ref, ktf_ref, vf_ref, of_ref, sf_ref), (qb_ref, ktb_ref, vb_ref, ob_ref, sb_ref))

    def body(b, carry):
        q = [r[0][b].astype(F32) for r in dirs]
        kt = [r[1][b] for r in dirs]
        v = [r[2][b] for r in dirs]
        s_old = [r[4][b] for r in dirs]
        o = [_bdot(q[d] * xi_ref[d], s_old[d]) for d in range(2)]
        inner = {}
        for hd in range(N_HEADS):
            for d in range(2):
                qm = jnp.where(lane_head == hd, q[d], 0.0).astype(BF16)
                inner[d, hd] = (jnp.dot(qm, kt[d], preferred_element_type=F32)
                                * dmat_ref[d, hd]).astype(BF16)
        upd = [jnp.dot((kt[d].astype(F32) * zt_ref[d]).astype(BF16), v[d], preferred_element_type=F32)
               for d in range(2)]
        for hd in range(N_HEADS):
            for d in range(2):
                oh = jnp.dot(inner[d, hd], v[d], preferred_element_type=F32)
                o[d] = o[d] + jnp.where(lane_head == hd, oh, 0.0)
        for d in range(2):
            dirs[d][3][b] = o[d]
            dirs[d][4][b] = gc_ref[d] * s_old[d] + jnp.where(block_diag, upd[d], 0.0)
        return carry

    lax.fori_loop(0, BATCH, body, 0)


def _ret_call(rq, rkt, rv, dmat, xi, zt, gc):
    cf = lambda i: (i + SEQ // RET_CHUNK) % N_CHUNK
    cb = lambda i: N_CHUNK - 1 - i
    rows = lambda f: pl.BlockSpec((BATCH, RET_CHUNK, MIX_W), lambda i: (0, f(i), 0))
    cols = lambda f: pl.BlockSpec((BATCH, MIX_W, RET_CHUNK), lambda i: (0, 0, f(i)))
    return pl.pallas_call(
        _ret_kernel,
        grid=(N_CHUNK,),
        in_specs=[rows(cf), cols(cf), rows(cf), rows(cb), cols(cb), rows(cb),
                  _const_spec((2, N_HEADS, RET_CHUNK, RET_CHUNK)),
                  _const_spec((2, RET_CHUNK, MIX_W)),
                  _const_spec((2, MIX_W, RET_CHUNK)),
                  _const_spec((2, MIX_W, MIX_W))],
        out_specs=[rows(cf), rows(cb)],
        out_shape=[jax.ShapeDtypeStruct((BATCH, TOK, MIX_W), F32)] * 2,
        scratch_shapes=[pltpu.VMEM((BATCH, MIX_W, MIX_W), F32),
                        pltpu.VMEM((BATCH, MIX_W, MIX_W), F32)],
        compiler_params=_params(("arbitrary",)),
        name="retention",
    )(rq, rkt, rv, rq, rkt, rv, dmat, xi, zt, gc)


def _head_q(q_ref, hd, pair, i=0):
    if not pair:
        return q_ref[i, :, hd * LANES:(hd + 1) * LANES], hd * LANES
    c = hd // 2
    qc = q_ref[i, :, c * LANES:(c + 1) * LANES]
    half = lax.broadcasted_iota(jnp.int32, qc.shape, 1) // HEAD_DIM
    return jnp.where(half == hd % 2, qc, jnp.zeros_like(qc)), c * LANES


def _dense_kernel(q_ref, kt_ref, v_ref, o_ref, *, pair):
    t = pl.program_id(1)
    lane_head = lax.broadcasted_iota(jnp.int32, (TM, MIX_W), 1) // HEAD_DIM

    def run(k_lo, k_hi):
        def scores(job):
            i, hd = job
            qm, r0 = _head_q(q_ref, hd, pair, i)
            return jnp.dot(qm, kt_ref[i, r0:r0 + LANES, k_lo:k_hi], preferred_element_type=F32)

        jobs = [(i, hd) for i in range(PAIR) for hd in range(N_HEADS)]
        s_next = scores(jobs[0])
        acc = None
        for n, (i, hd) in enumerate(jobs):
            s = s_next
            if n + 1 < len(jobs):
                s_next = scores(jobs[n + 1])
            if hd == 0:
                acc = jnp.zeros((TM, MIX_W), F32)
            m = jnp.max(s, axis=-1, keepdims=True)
            p = jnp.exp2(s - m)
            l = jnp.sum(p, axis=-1, keepdims=True)
            oh = jnp.dot(p.astype(BF16), v_ref[i, k_lo:k_hi, :], preferred_element_type=F32)
            acc = jnp.where(lane_head == hd, oh * (1.0 / l), acc)
            if hd == N_HEADS - 1:
                o_ref[i] = acc.astype(BF16)

    @pl.when(t < NT_LAT)
    def _():
        run(0, TOK)

    @pl.when(t >= NT_LAT)
    def _():
        run(SEQ, TOK)


def _dense_call(q, kt, v, nq, pair, name):
    wq = q.shape[-1]
    return pl.pallas_call(
        functools.partial(_dense_kernel, pair=pair),
        grid=(BATCH // PAIR, nq),
        in_specs=[pl.BlockSpec((PAIR, TM, wq), lambda b, t: (b, t, 0)),
                  pl.BlockSpec((PAIR, kt.shape[1], TOK), lambda b, t: (b, 0, 0)),
                  pl.BlockSpec((PAIR, TOK, MIX_W), lambda b, t: (b, 0, 0))],
        out_specs=pl.BlockSpec((PAIR, TM, MIX_W), lambda b, t: (b, t, 0)),
        out_shape=jax.ShapeDtypeStruct((BATCH, nq * TM, MIX_W), BF16),
        compiler_params=_params(("parallel", "arbitrary")),
        name=name,
    )(q, kt, v)


def _win_kernel(sink_ref, q_ref, kt_ref, v_ref, o_ref):
    t = pl.program_id(1)
    lane_head = lax.broadcasted_iota(jnp.int32, (TM, MIX_W), 1) // HEAD_DIM
    v_ctx = v_ref[0, SEQ:TOK, :]

    @pl.when(t < NT_LAT)
    def _():
        start = pl.multiple_of(jnp.clip(t * TM - WINDOW, 0, SEQ - WIN_KEYS), LANES)
        qpos = t * TM + lax.broadcasted_iota(jnp.int32, (TM, WIN_KEYS), 0)
        kpos = start + lax.broadcasted_iota(jnp.int32, (TM, WIN_KEYS), 1)
        valid = jnp.abs(kpos - qpos) <= WINDOW
        v_loc = v_ref[0, pl.ds(start, WIN_KEYS), :]
        acc = jnp.zeros((TM, MIX_W), F32)

        def scores(hd):
            qm, r0 = _head_q(q_ref, hd, True)
            loc = jnp.dot(qm, kt_ref[0, r0:r0 + LANES, pl.ds(start, WIN_KEYS)], preferred_element_type=F32)
            ctx = jnp.dot(qm, kt_ref[0, r0:r0 + LANES, SEQ:TOK], preferred_element_type=F32)
            return jnp.where(valid, loc, NEG_INF), ctx

        s_next = scores(0)
        for hd in range(N_HEADS):
            s_loc, s_ctx = s_next
            if hd + 1 < N_HEADS:
                s_next = scores(hd + 1)
            sk = sink_ref[hd]
            m = jnp.maximum(jnp.maximum(jnp.max(s_loc, axis=-1, keepdims=True),
                                        jnp.max(s_ctx, axis=-1, keepdims=True)), sk)
            p_loc = jnp.exp(s_loc - m)
            p_ctx = jnp.exp(s_ctx - m)
            l = (jnp.sum(p_loc, axis=-1, keepdims=True) + jnp.sum(p_ctx, axis=-1, keepdims=True)
                 + jnp.exp(sk - m))
            oh = (jnp.dot(p_loc.astype(BF16), v_loc, preferred_element_type=F32)
                  + jnp.dot(p_ctx.astype(BF16), v_ctx, preferred_element_type=F32))
            acc = jnp.where(lane_head == hd, oh * (1.0 / l), acc)
        o_ref[0] = acc.astype(BF16)

    @pl.when(t >= NT_LAT)
    def _():
        acc = jnp.zeros((TM, MIX_W), F32)
        for hd in range(N_HEADS):
            qm, r0 = _head_q(q_ref, hd, True)
            s = jnp.dot(qm, kt_ref[0, r0:r0 + LANES, SEQ:TOK], preferred_element_type=F32)
            sk = sink_ref[hd]
            m = jnp.maximum(jnp.max(s, axis=-1, keepdims=True), sk)
            p = jnp.exp(s - m)
            l = jnp.sum(p, axis=-1, keepdims=True) + jnp.exp(sk - m)
            oh = jnp.dot(p.astype(BF16), v_ctx, preferred_element_type=F32)
            acc = jnp.where(lane_head == hd, oh * (1.0 / l), acc)
        o_ref[0] = acc.astype(BF16)


def _win_call(sink, q, kt, v, nq):
    return pl.pallas_call(
        _win_kernel,
        grid=(BATCH, nq),
        in_specs=[pl.BlockSpec(memory_space=pltpu.SMEM),
                  pl.BlockSpec((1, TM, MIX_W), lambda b, t: (b, t, 0)),
                  pl.BlockSpec((1, MIX_W, TOK), lambda b, t: (b, 0, 0)),
                  pl.BlockSpec((1, TOK, MIX_W), lambda b, t: (b, 0, 0))],
        out_specs=pl.BlockSpec((1, TM, MIX_W), lambda b, t: (b, t, 0)),
        out_shape=jax.ShapeDtypeStruct((BATCH, nq * TM, MIX_W), BF16),
        compiler_params=_params(("parallel", "arbitrary")),
        name="window",
    )(sink, q, kt, v)


def _merge_kernel(xl_ref, xc_ref, mod_ref, g1_ref, g2_ref, of_ref, ob_ref, rg_ref, ym_ref, yg_ref, yw_ref,
                  wg_ref, wb_ref, wo_ref, wr_ref, gmat_ref, xo_ref, h2_ref, lg_ref):
    xs = _tile_x(xl_ref, xc_ref, PAIR)
    mds = [mod_ref[p, 0] for p in range(PAIR)]
    stack = lambda f: jnp.concatenate([f(p) for p in range(PAIR)], axis=0)
    flat = lambda ref: ref[...].reshape(PAIR * TM, ref.shape[-1])
    hb = stack(lambda p: _modnorm(xs[p], g1_ref[...], mds[p][1:2], mds[p][0:1])).astype(BF16)

    gmat = gmat_ref[...]
    o = flat(of_ref) + flat(ob_ref)
    dlt = o - _head_mean(o, gmat)
    var = _head_mean(dlt * dlt, gmat)
    g = flat(rg_ref).astype(F32)
    y_ret = dlt * lax.rsqrt(var + EPS) * (g * _sigmoid(g))

    ys = (y_ret.astype(BF16), flat(ym_ref), flat(yg_ref), flat(yw_ref))
    acc = jnp.zeros((PAIR * TM, D_MODEL), F32)
    for i in range(4):
        gate = _sigmoid(jnp.dot(hb, wg_ref[:, i * D_MODEL:(i + 1) * D_MODEL],
                                      preferred_element_type=F32))
        acc = acc + gate * jnp.dot(ys[i], wb_ref[i], preferred_element_type=F32)
    out = jnp.dot(acc.astype(BF16), wo_ref[...], preferred_element_type=F32)
    wh, wl = _split(wr_ref[...])
    nt = lambda a, b: lax.dot_general(a, b, (((1,), (1,)), ((), ())), preferred_element_type=F32)
    for p in range(PAIR):
        xm = xs[p] + mds[p][2:3] * out[p * TM:(p + 1) * TM]
        xo_ref[p] = xm
        h2 = _modnorm(xm, g2_ref[...], mds[p][4:5], mds[p][3:4])
        h2_ref[p] = h2
        hh, hl = _split(h2)
        lg_ref[p] = nt(wh, hh) + nt(wh, hl) + nt(wl, hh)


def _merge_call(xl, xc, ctx_block, mod, g1, g2, of, ob, rg, ym, yg, yw, wg, wb, wo, wr, gmat, nq):
    tok = lambda w: pl.BlockSpec((PAIR, TM, w), lambda b, t: (b, t, 0))
    return pl.pallas_call(
        _merge_kernel,
        grid=(BATCH // PAIR, nq),
        in_specs=_x_specs(ctx_block, PAIR) + [
                  pl.BlockSpec((PAIR, 1, 8, D_MODEL), lambda b, t: (b, t // NT_LAT, 0, 0)),
                  _const_spec((1, D_MODEL)), _const_spec((1, D_MODEL)),
                  tok(MIX_W), tok(MIX_W), tok(MIX_W), tok(MIX_W), tok(MIX_W), tok(MIX_W),
                  _const_spec((D_MODEL, 4 * D_MODEL)),
                  _const_spec((4, MIX_W, D_MODEL)),
                  _const_spec((D_MODEL, D_MODEL)),
                  _const_spec((N_EXPERTS, D_MODEL)),
                  _const_spec((MIX_W, MIX_W))],
        out_specs=[tok(D_MODEL), tok(D_MODEL),
                   pl.BlockSpec((PAIR, N_EXPERTS, TM), lambda b, t: (b, 0, t))],
        out_shape=[jax.ShapeDtypeStruct((BATCH, nq * TM, D_MODEL), F32),
                   jax.ShapeDtypeStruct((BATCH, nq * TM, D_MODEL), F32),
                   jax.ShapeDtypeStruct((BATCH, N_EXPERTS, nq * TM), F32)],
        compiler_params=_params(("parallel", "parallel")),
        name="merge",
    )(xl, xc, mod, g1, g2, of, ob, rg, ym, yg, yw, wg, wb, wo, wr, gmat)


RANK_BITS = 20
RANK_MASK = (1 << RANK_BITS) - 1


def _route_kernel(lg_ref, b_ref, tri_ref, pk_ref, gw_ref, cnt_ref, base_ref):
    i = pl.program_id(0)

    @pl.when(i == 0)
    def _():
        base_ref[...] = jnp.zeros_like(base_ref)

    s = jax.nn.sigmoid(lg_ref[0])
    sel = s + b_ref[:, 0:1]
    sub = lax.broadcasted_iota(jnp.int32, (EXPERTS_PER_GROUP, TM), 0)
    best = e1 = e2 = s1 = s2 = None
    for g in range(N_GROUPS):
        rows = slice(g * EXPERTS_PER_GROUP, (g + 1) * EXPERTS_PER_GROUP)
        blk, sb = sel[rows], s[rows]
        m1 = jnp.max(blk, axis=0, keepdims=True)
        i1 = jnp.min(jnp.where(blk == m1, sub, EXPERTS_PER_GROUP), axis=0, keepdims=True)
        hit1 = sub == i1
        blk2 = jnp.where(hit1, -jnp.inf, blk)
        m2 = jnp.max(blk2, axis=0, keepdims=True)
        i2 = jnp.min(jnp.where(blk2 == m2, sub, EXPERTS_PER_GROUP), axis=0, keepdims=True)
        hit2 = sub == i2
        score = m1 + m2
        s1g = jnp.sum(jnp.where(hit1, sb, 0.0), axis=0, keepdims=True)
        s2g = jnp.sum(jnp.where(hit2, sb, 0.0), axis=0, keepdims=True)
        e1g = g * EXPERTS_PER_GROUP + i1
        e2g = g * EXPERTS_PER_GROUP + i2
        if g == 0:
            best, e1, e2, s1, s2 = score, e1g, e2g, s1g, s2g
        else:
            better = score > best
            best = jnp.where(better, score, best)
            e1 = jnp.where(better, e1g, e1)
            e2 = jnp.where(better, e2g, e2)
            s1 = jnp.where(better, s1g, s1)
            s2 = jnp.where(better, s2g, s2)

    eid = lax.broadcasted_iota(jnp.int32, (N_EXPERTS, TM), 0)
    oh1 = eid == e1
    oh2 = eid == e2
    oh = jnp.where(oh1 | oh2, 1.0, 0.0)
    before = jnp.dot(oh.astype(BF16), tri_ref[...], preferred_element_type=F32) + base_ref[:, 0:1]
    r1 = jnp.sum(jnp.where(oh1, before, 0.0), axis=0, keepdims=True).astype(jnp.int32)
    r2 = jnp.sum(jnp.where(oh2, before, 0.0), axis=0, keepdims=True).astype(jnp.int32)
    total = base_ref[...] + jnp.sum(oh, axis=1, keepdims=True)
    base_ref[...] = total
    cnt_ref[...] = total

    pk_ref[...] = jnp.concatenate([(e1 << RANK_BITS) + r1, (e2 << RANK_BITS) + r2], axis=0)
    den = s1 + s2
    row = lax.broadcasted_iota(jnp.int32, (8, TM), 0)
    gw_ref[...] = jnp.where(row == 0, s1 / den, jnp.where(row == 1, s2 / den, 0.0))


def _route_call(lgt, b_router, nq):
    n_tiles = BATCH * nq
    n_tok = n_tiles * TM
    r = jnp.arange(TM)
    tri = (r[:, None] < r[None, :]).astype(BF16)
    bcol = jnp.broadcast_to(b_router.astype(F32)[:, None], (N_EXPERTS, LANES))
    return pl.pallas_call(
        _route_kernel,
        grid=(n_tiles,),
        in_specs=[pl.BlockSpec((1, N_EXPERTS, TM), lambda i: (i // nq, 0, i % nq)),
                  _const_spec((N_EXPERTS, LANES)),
                  _const_spec((TM, TM))],
        out_specs=[pl.BlockSpec((TOP_K, TM), lambda i: (0, i)),
                   pl.BlockSpec((8, TM), lambda i: (0, i)),
                   pl.BlockSpec((N_EXPERTS, LANES), lambda i: (0, 0))],
        out_shape=[jax.ShapeDtypeStruct((TOP_K, n_tok), jnp.int32),
                   jax.ShapeDtypeStruct((8, n_tok), F32),
                   jax.ShapeDtypeStruct((N_EXPERTS, LANES), F32)],
        scratch_shapes=[pltpu.VMEM((N_EXPERTS, LANES), F32)],
        compiler_params=_params(("arbitrary",)),
        name="route",
    )(lgt, bcol, tri)


def _slot_rows(pk, row0):
    e = pk >> RANK_BITS
    base = jnp.sum(jnp.where(e[..., None] == jnp.arange(N_EXPERTS), row0, 0), axis=-1)
    return (base + (pk & RANK_MASK)).astype(jnp.int32).reshape(-1)


def _slot_row(rows_ref, k, n, n_tok):
    return rows_ref[k * n_tok + n]


def _combine_kernel(rows_ref, x_ref, mod_ref, gw_ref, fg_ref, ys_hbm, o_ref, ybuf, sem,
                    *, n_tok, final):
    i = pl.program_id(0)
    n_tiles = pl.num_programs(0)
    slot = i % 2

    def issue(tile, sl):
        def one(j, c):
            n = tile * TM + j
            for k in range(TOP_K):
                pltpu.make_async_copy(ys_hbm.at[pl.ds(_slot_row(rows_ref, k, n, n_tok), 1)],
                                      ybuf.at[sl, k, pl.ds(j, 1)], sem.at[sl]).start()
            return c

        lax.fori_loop(0, TM, one, 0, unroll=4)

    @pl.when(i == 0)
    def _():
        issue(0, 0)

    @pl.when(i + 1 < n_tiles)
    def _():
        issue(i + 1, 1 - slot)

    for k in range(TOP_K):
        pltpu.make_async_copy(ys_hbm.at[pl.ds(0, TM)], ybuf.at[slot, k], sem.at[slot]).wait()

    md = mod_ref[0, 0]
    gw = gw_ref[...].T
    f = gw[:, 0:1] * ybuf[slot, 0] + gw[:, 1:2] * ybuf[slot, 1]
    xn = x_ref[0] + md[5:6] * f
    o_ref[0] = _rmsnorm(xn, fg_ref[...]) if final else xn


def _combine_call(rows, xm, mod, gw, fg, ys, nq, final):
    n_tok = BATCH * nq * TM
    grid_spec = pltpu.PrefetchScalarGridSpec(
        num_scalar_prefetch=1,
        grid=(BATCH * nq,),
        in_specs=[pl.BlockSpec((1, TM, D_MODEL), lambda i, *_: (i // nq, i % nq, 0)),
                  pl.BlockSpec((1, 1, 8, D_MODEL), lambda i, *_: (i // nq, (i % nq) // NT_LAT, 0, 0)),
                  pl.BlockSpec((8, TM), lambda i, *_: (0, i)),
                  pl.BlockSpec((1, D_MODEL), lambda i, *_: (0, 0)),
                  pl.BlockSpec(memory_space=pl.ANY)],
        out_specs=pl.BlockSpec((1, TM, D_MODEL), lambda i, *_: (i // nq, i % nq, 0)),
        scratch_shapes=[pltpu.VMEM((2, TOP_K, TM, D_MODEL), F32),
                        pltpu.SemaphoreType.DMA((2,))])
    return pl.pallas_call(
        functools.partial(_combine_kernel, n_tok=n_tok, final=final),
        grid_spec=grid_spec,
        out_shape=jax.ShapeDtypeStruct((BATCH, nq * TM, D_MODEL), F32),
        compiler_params=_params(("arbitrary",)),
        name="combine",
    )(rows, xm, mod, gw, fg, ys)


def _fmoe_kernel(rows_ref, row0_ref, nblk_ref, cnt_ref, h_hbm, w1_ref, w3_ref, w2_ref, ys_hbm,
                 tok, xbuf, ybuf, w1b, w3b, w2b, isem, osem, *, n_tok, n_blocks):
    e = pl.program_id(0)
    nb = nblk_ref[e]
    b0 = row0_ref[e] // MOE_ROWS
    used = row0_ref[N_EXPERTS - 1] // MOE_ROWS + nblk_ref[N_EXPERTS - 1]

    def issue_gather(b, lo=0, hi=MOE_ROWS):
        slot = b % MOE_SLOTS
        for j in range(lo, hi):
            pltpu.make_async_copy(h_hbm.at[pl.ds(tok[b * MOE_ROWS + j], 1)], xbuf.at[slot, pl.ds(j, 1)],
                                  isem.at[slot]).start()

    def wait_gather(b):
        slot = b % MOE_SLOTS
        pltpu.make_async_copy(h_hbm.at[pl.ds(0, MOE_ROWS)], xbuf.at[slot], isem.at[slot]).wait()

    def out_copy(b):
        rows = pl.ds(pl.multiple_of(b * MOE_ROWS, MOE_ROWS), MOE_ROWS)
        return pltpu.make_async_copy(ybuf.at[b % 2], ys_hbm.at[rows], osem.at[b % 2])

    @pl.when(e == 0)
    def _():
        def clear(lo, hi):
            def one(p, c):
                tok[p] = 0
                return c

            lax.fori_loop(lo, hi, one, 0)

        clear(used * MOE_ROWS, (used + MOE_SLOTS - 1) * MOE_ROWS)

        def pad_e(x, c):
            clear(row0_ref[x] + cnt_ref[x], row0_ref[x] + nblk_ref[x] * MOE_ROWS)
            return c

        lax.fori_loop(0, N_EXPERTS, pad_e, 0)

        def put(n, c):
            for k in range(TOP_K):
                tok[rows_ref[k * n_tok + n]] = n
            return c

        lax.fori_loop(0, n_tok, put, 0, unroll=8)

        def prime(idx, c):
            b = idx // MOE_ROWS
            pltpu.make_async_copy(h_hbm.at[pl.ds(tok[idx], 1)], xbuf.at[b, pl.ds(idx % MOE_ROWS, 1)],
                                  isem.at[b]).start()
            return c

        lax.fori_loop(0, (MOE_SLOTS - 1) * MOE_ROWS, prime, 0)

    @pl.when(nb > 0)
    def _():
        w1b[...] = w1_ref[0, 0].astype(BF16)
        w3b[...] = w3_ref[0, 0].astype(BF16)
        w2b[...] = w2_ref[0, 0].astype(BF16)

        def block(r, carry):
            g = b0 + r
            wait_gather(g)

            @pl.when(g >= 2)
            def _():
                out_copy(g - 2).wait()

            ahead = g + MOE_SLOTS - 1
            piece = MOE_ROWS // MOE_PIECES
            half = D_EXPERT // 2
            xb = xbuf[g % MOE_SLOTS].astype(BF16)
            hid = []
            for c in range(2):
                cols = slice(c * half, (c + 1) * half)
                issue_gather(ahead, (2 * c) * piece, (2 * c + 1) * piece)
                h1 = jnp.dot(xb, w1b[:, cols], preferred_element_type=F32)
                issue_gather(ahead, (2 * c + 1) * piece, (2 * c + 2) * piece)
                h3 = jnp.dot(xb, w3b[:, cols], preferred_element_type=F32)
                hid.append((h1 * _sigmoid(h1) * h3).astype(BF16))
            hid = jnp.concatenate(hid, axis=1)
            for c in range(2):
                cols = slice(c * half, (c + 1) * half)
                issue_gather(ahead, (4 + c) * piece, (5 + c) * piece if c == 0 else MOE_ROWS)
                ybuf[g % 2, :, cols] = jnp.dot(hid, w2b[:, cols], preferred_element_type=F32)
            out_copy(g).start()
            return carry

        lax.fori_loop(0, nb, block, 0)

    @pl.when(e == N_EXPERTS - 1)
    def _():
        @pl.when(used >= 2)
        def _():
            out_copy(used - 2).wait()

        @pl.when(used >= 1)
        def _():
            out_copy(used - 1).wait()

        for b in range(MOE_SLOTS - 1):
            wait_gather(used + b)

        ybuf[0] = jnp.zeros((MOE_ROWS, D_MODEL), F32)

        def tail_copy(bk):
            return pltpu.make_async_copy(
                ybuf.at[0], ys_hbm.at[pl.ds(pl.multiple_of(bk * MOE_ROWS, MOE_ROWS), MOE_ROWS)], osem.at[0])

        lax.fori_loop(used, n_blocks, lambda bk, c: (tail_copy(bk).start(), c)[1], 0)
        lax.fori_loop(used, n_blocks, lambda bk, c: (tail_copy(bk).wait(), c)[1], 0)


def _fmoe_call(rows, row0, nblk, cnt, h2, w1, w3, w2, l):
    n_tok = h2.shape[0]
    n_blocks = TOP_K * n_tok // MOE_ROWS + N_EXPERTS
    wspec = pl.BlockSpec((1, 1, D_MODEL, D_EXPERT), lambda e, *_: (l, e, 0, 0))
    grid_spec = pltpu.PrefetchScalarGridSpec(
        num_scalar_prefetch=4,
        grid=(N_EXPERTS,),
        in_specs=[pl.BlockSpec(memory_space=pl.ANY), wspec, wspec,
                  pl.BlockSpec((1, 1, D_EXPERT, D_MODEL), lambda e, *_: (l, e, 0, 0))],
        out_specs=pl.BlockSpec(memory_space=pl.ANY),
        scratch_shapes=[pltpu.SMEM(((n_blocks + MOE_SLOTS - 1) * MOE_ROWS,), jnp.int32),
                        pltpu.VMEM((MOE_SLOTS, MOE_ROWS, D_MODEL), F32),
                        pltpu.VMEM((2, MOE_ROWS, D_MODEL), F32),
                        pltpu.VMEM((D_MODEL, D_EXPERT), BF16),
                        pltpu.VMEM((D_MODEL, D_EXPERT), BF16),
                        pltpu.VMEM((D_EXPERT, D_MODEL), BF16),
                        pltpu.SemaphoreType.DMA((MOE_SLOTS,)),
                        pltpu.SemaphoreType.DMA((2,))])
    return pl.pallas_call(
        functools.partial(_fmoe_kernel, n_tok=n_tok, n_blocks=n_blocks),
        grid_spec=grid_spec,
        out_shape=jax.ShapeDtypeStruct((n_blocks * MOE_ROWS, D_MODEL), F32),
        compiler_params=_params(("arbitrary",)),
        name="moe",
    )(rows, row0, nblk, cnt, h2, w1, w3, w2)


def _rope_tables():
    rows = SEQ // GRID_W
    row = jnp.broadcast_to(jnp.arange(rows)[:, None], (rows, GRID_W)).reshape(-1).astype(F32)
    col = jnp.broadcast_to(jnp.arange(GRID_W)[None, :], (rows, GRID_W)).reshape(-1).astype(F32)

    def cs(rot_dim):
        n_f = rot_dim // 4
        inv = ROPE_BASE ** (-jnp.arange(n_f, dtype=F32) / n_f)
        ang = jnp.concatenate([row[:, None] * inv, col[:, None] * inv], axis=-1)
        return jnp.cos(ang), jnp.sin(ang)

    def with_ctx(c, u, d):
        one = jnp.ones((CTX_LEN, LANES), F32)
        zero = jnp.zeros((CTX_LEN, LANES), F32)
        return (jnp.concatenate([c, one]), jnp.concatenate([u, zero]), jnp.concatenate([d, zero]))

    cos, sin = cs(HEAD_DIM)
    z = jnp.zeros_like(sin)
    t64 = with_ctx(jnp.tile(jnp.concatenate([cos, cos], -1), (1, 2)),
                   jnp.tile(jnp.concatenate([-sin, z], -1), (1, 2)),
                   jnp.tile(jnp.concatenate([z, sin], -1), (1, 2)))
    cos, sin = cs(MLA_ROPE)
    z = jnp.zeros_like(sin)
    one_n = jnp.ones((SEQ, MLA_NOPE), F32)
    zero_n = jnp.zeros((SEQ, MLA_NOPE), F32)
    one_p = jnp.ones((SEQ, LANES - MLA_NOPE - MLA_ROPE), F32)
    zero_p = jnp.zeros((SEQ, LANES - MLA_NOPE - MLA_ROPE), F32)
    tm = with_ctx(jnp.concatenate([one_n, cos, cos, one_p], -1),
                  jnp.concatenate([zero_n, -sin, z, zero_p], -1),
                  jnp.concatenate([zero_n, z, sin, zero_p], -1))
    return t64 + tm


def _ret_tables(decay):
    lg = -jnp.exp(decay.astype(F32))
    idx = jnp.arange(RET_CHUNK, dtype=F32)
    diff = idx[:, None] - idx[None, :]
    fwd = diff >= 0
    bwd = diff < 0
    dm_f = jnp.where(fwd, jnp.exp(lg[0][:, None, None] * jnp.where(fwd, diff, 0.0)), 0.0)
    dm_b = jnp.where(bwd, jnp.exp(lg[1][:, None, None] * jnp.where(bwd, -diff, 0.0)), 0.0)
    dmat = jnp.stack([dm_f, dm_b])
    xi = jnp.stack([jnp.exp(lg[0][:, None] * (idx + 1.0)),
                    jnp.exp(lg[1][:, None] * (RET_CHUNK - idx))])
    zeta = jnp.stack([jnp.exp(lg[0][:, None] * (RET_CHUNK - 1.0 - idx)),
                      jnp.exp(lg[1][:, None] * idx)])
    gch = jnp.exp(lg * RET_CHUNK)
    xi_t = jnp.repeat(jnp.transpose(xi, (0, 2, 1)), HEAD_DIM, axis=2)
    zt_t = jnp.repeat(zeta, HEAD_DIM, axis=1)
    gc_t = jnp.broadcast_to(jnp.repeat(gch, HEAD_DIM, axis=1)[:, :, None], (2, MIX_W, MIX_W))
    return dmat, xi_t, zt_t, gc_t


def _in_proj_columns():
    o_mla = 4 * MIX_W
    o_gqa = o_mla + Q_LORA + KV_LORA + MLA_ROPE
    kv_w = (N_HEADS // 2) * HEAD_DIM
    o_win = o_gqa + MIX_W + 2 * kv_w
    ar = jnp.arange
    dup = jnp.concatenate([ar(HEAD_DIM), ar(HEAD_DIM), HEAD_DIM + ar(HEAD_DIM), HEAD_DIM + ar(HEAD_DIM)])

    def gqa_cols(o):
        return [o + ar(MIX_W), o + MIX_W + dup, o + MIX_W + kv_w + dup]

    return jnp.concatenate([ar(o_mla), o_mla + ar(Q_LORA), o_mla + Q_LORA + ar(KV_LORA)]
                           + gqa_cols(o_gqa) + gqa_cols(o_win)
                           + [o_mla + Q_LORA + KV_LORA + ar(MLA_ROPE)])


def _layer_weights(l, w_in, mla_w_uq, mla_w_ukv):
    cols = _in_proj_columns()
    win_p = jnp.pad(w_in[l][:, cols], ((0, 0), (0, N_IN_P - cols.shape[0]))).astype(BF16)
    uq = mla_w_uq[l].reshape(Q_LORA, N_HEADS, MLA_NOPE + MLA_ROPE)
    wuq = jnp.pad(uq, ((0, 0), (0, 0), (0, LANES - MLA_NOPE - MLA_ROPE))).reshape(Q_LORA, MLA_QK)
    ukv = mla_w_ukv[l].reshape(KV_LORA, N_HEADS, MLA_NOPE + MLA_V)
    wuk = jnp.pad(ukv[:, :, :MLA_NOPE], ((0, 0), (0, 0), (0, LANES - MLA_NOPE))).reshape(KV_LORA, MLA_QK)
    wuv = ukv[:, :, MLA_NOPE:].reshape(KV_LORA, MIX_W)
    return win_p, wuq.astype(BF16), wuk.astype(BF16), wuv.astype(BF16)


def _krope_placement():
    r = jnp.arange(LANES)[:, None]
    c = jnp.arange(MLA_QK)[None, :]
    return ((r < MLA_ROPE) & (c % LANES == MLA_NOPE + r)).astype(BF16)


def _head_block_matrix():
    r = jnp.arange(MIX_W)
    return (r[:, None] // HEAD_DIM == r[None, :] // HEAD_DIM).astype(BF16)


def kernel(x, c, ctx, c_ctx, w_ada, b_ada, norm1_g, norm2_g, w_in, w_gate, w_branch, w_out, ret_decay,
           mla_qn_g, mla_w_uq, mla_kvn_g, mla_w_ukv, gqa_qn_g, gqa_kn_g, win_sink, w_router, b_router,
           w1, w3, w2, final_norm_g):
    cvec = jnp.concatenate([c, c_ctx[None, :], jnp.zeros((7, D_MODEL), F32)], axis=0)
    ada = _ada_call(cvec, w_ada, b_ada)
    tabs = _rope_tables()
    gmat = _head_block_matrix()
    ekr = _krope_placement()
    stream = (x, ctx, 0)
    out = None
    for l in range(DEPTH):
        last = l == DEPTH - 1
        nq = NT_LAT if last else NT
        m = ada[l].reshape(16, 6, D_MODEL)
        m_lat = m[:BATCH]
        m_ctx = jnp.broadcast_to(m[BATCH][None], (BATCH, 6, D_MODEL))
        mod = jnp.pad(jnp.stack([m_lat, m_ctx], axis=1), ((0, 0), (0, 0), (0, 2), (0, 0)))
        g1 = norm1_g[l][None, :]
        g2 = norm2_g[l][None, :]
        win_p, wuq, wuk, wuv = _layer_weights(l, w_in, mla_w_uq, mla_w_ukv)
        (rq, rkt, rv, rg, mq, mkt, mv, gq, gkt, gv, wq, wkt, wv) = _prep_call(
            *stream, mod, g1, win_p, wuq, wuk, wuv, ekr,
            mla_qn_g[l][None, :], mla_kvn_g[l][None, :],
            jnp.tile(gqa_qn_g[l], N_HEADS)[None, :], jnp.tile(gqa_kn_g[l], N_HEADS)[None, :],
            gmat, tabs)
        of, ob = _ret_call(rq, rkt, rv, *_ret_tables(ret_decay[l]))
        ym = _dense_call(mq, mkt, mv, nq, False, "mla")
        yg = _dense_call(gq, gkt, gv, nq, True, "gqa")
        yw = _win_call(win_sink[l], wq, wkt, wv, nq)
        xm, h2, lgt = _merge_call(
            *stream, mod, g1, g2, of, ob, rg, ym, yg, yw,
            w_gate[l].astype(BF16), w_branch[l].astype(BF16), w_out[l].astype(BF16), w_router.T, gmat, nq)
        n_tok = BATCH * nq * TM
        pk, gw, cnt = _route_call(lgt, b_router, nq)
        cnt = cnt[:, 0].astype(jnp.int32)
        nblk = (cnt + MOE_ROWS - 1) // MOE_ROWS
        row0 = (jnp.cumsum(nblk) - nblk) * MOE_ROWS
        rows = _slot_rows(pk, row0)
        ys = _fmoe_call(rows, row0, nblk, cnt, h2.reshape(n_tok, D_MODEL), w1, w3, w2, l)
        res = _combine_call(rows, xm, mod, gw, final_norm_g[None, :], ys, nq, last)
        if last:
            out = res
        else:
            stream = (res, res, NT_LAT)
    return out
```

```python
import functools

import jax
import jax.numpy as jnp
from jax import lax
from jax.experimental import pallas as pl
from jax.experimental.pallas import tpu as pltpu

F32 = jnp.float32
BF16 = jnp.bfloat16

D_MODEL = 1024
BATCH = 8
SEQ = 2048
DEPTH = 2
CTX_LEN = 256
TOK = SEQ + CTX_LEN
GRID_W = 64
N_HEADS = 4
HEAD_DIM = 64
MIX_W = N_HEADS * HEAD_DIM
RET_CHUNK = 128
Q_LORA = 256
KV_LORA = 128
MLA_NOPE = 64
MLA_ROPE = 32
MLA_V = 64
WINDOW = 128
N_EXPERTS = 32
N_GROUPS = 4
EXPERTS_PER_GROUP = N_EXPERTS // N_GROUPS
TOP_K = 2
D_EXPERT = 1024
ROPE_BASE = 10000.0
EPS = 1e-6
NEG_INF = -1e30

LANES = 128
TM = 256
NT = TOK // TM
NT_LAT = SEQ // TM
WIN_KEYS = TM + 2 * WINDOW
N_CHUNK = TOK // RET_CHUNK
LOG2E = 1.4426950408889634
PAIR = 2
COMBINE_ROWS = 32
MOE_ROWS = 128
MOE_SLOTS = 8
MOE_PIECES = 6
N_IN_P = 3072
MLA_QK = 4 * LANES

C_RET = 0
C_MQ = 1024
C_MKV = 1280
C_GQ = 1408
C_GK = 1664
C_GV = 1920
C_WQ = 2176
C_WK = 2432
C_WV = 2688
C_KR = 2944

VMEM_LIMIT = 56 * 1024 * 1024


def _params(sem, vmem=VMEM_LIMIT):
    return pltpu.CompilerParams(dimension_semantics=sem, vmem_limit_bytes=vmem)


def _const_spec(shape):
    nd = len(shape)
    return pl.BlockSpec(shape, lambda *_: (0,) * nd, pipeline_mode=pl.Buffered(1))


def _bdot(a, b):
    return jnp.dot(a.astype(BF16), b.astype(BF16), preferred_element_type=F32)


def _split(a):
    hi = a.astype(BF16)
    lo = (a - hi.astype(F32)).astype(BF16)
    return hi, lo


def _dot_split_lhs(a, b):
    hi, lo = _split(a)
    return (jnp.dot(hi, b, preferred_element_type=F32)
            + jnp.dot(lo, b, preferred_element_type=F32))


def _dot3(a, b):
    ah, al = _split(a)
    bh, bl = _split(b)
    return (jnp.dot(ah, bh, preferred_element_type=F32)
            + jnp.dot(ah, bl, preferred_element_type=F32)
            + jnp.dot(al, bh, preferred_element_type=F32))


def _modnorm(x, g, sc, sh):
    ms = jnp.mean(x * x, axis=-1, keepdims=True)
    return x * lax.rsqrt(ms + EPS) * g * (1.0 + sc) + sh


def _rmsnorm(x, g):
    ms = jnp.mean(x * x, axis=-1, keepdims=True)
    return x * lax.rsqrt(ms + EPS) * g


def _rope(x, c, s_up, s_dn, half):
    outs = []
    for j in range(x.shape[1] // LANES):
        xc = x[:, j * LANES:(j + 1) * LANES]
        outs.append(xc * c + pltpu.roll(xc, LANES - half, 1) * s_up + pltpu.roll(xc, half, 1) * s_dn)
    return outs[0] if len(outs) == 1 else jnp.concatenate(outs, axis=1)


def _sigmoid(x):
    return 0.5 * jnp.tanh(0.5 * x) + 0.5


def _head_mean(x, gmat):
    return _dot_split_lhs(x, gmat) * (1.0 / HEAD_DIM)


def _ada_kernel(c_ref, w_ref, b_ref, o_ref):
    c = c_ref[...]
    sc = c * jax.nn.sigmoid(c)
    o_ref[0] = _dot3(sc, w_ref[0]) + b_ref[0]


def _ada_call(cvec, w_ada, b_ada):
    tn = 1536
    return pl.pallas_call(
        _ada_kernel,
        grid=(DEPTH, 6 * D_MODEL // tn),
        in_specs=[
            pl.BlockSpec((16, D_MODEL), lambda l, j: (0, 0)),
            pl.BlockSpec((1, D_MODEL, tn), lambda l, j: (l, 0, j)),
            pl.BlockSpec((1, 1, tn), lambda l, j: (l, 0, j)),
        ],
        out_specs=pl.BlockSpec((1, 16, tn), lambda l, j: (l, 0, j)),
        out_shape=jax.ShapeDtypeStruct((DEPTH, 16, 6 * D_MODEL), F32),
        compiler_params=_params(("arbitrary", "arbitrary")),
        name="ada",
    )(cvec, w_ada, b_ada.reshape(DEPTH, 1, 6 * D_MODEL))


def _tile_x(xl_ref, xc_ref, pair=1):
    latent = pl.program_id(1) < NT_LAT
    return [jnp.where(latent, xl_ref[p], xc_ref[p]) for p in range(pair)]


def _x_specs(ctx_block, pair=1):
    return [pl.BlockSpec((pair, TM, D_MODEL), lambda b, t: (b, jnp.minimum(t, NT_LAT - 1), 0)),
            pl.BlockSpec((pair, TM, D_MODEL), lambda b, t: (b, ctx_block, 0))]


def _prep_kernel(xl_ref, xc_ref, mod_ref, g1_ref, win_ref, wuq_ref, wuk_ref, wuv_ref, ekr_ref,
                 qng_ref, kvng_ref, gqg_ref, gkg_ref, gmat_ref,
                 c64_ref, u64_ref, d64_ref, cm_ref, um_ref, dm_ref,
                 rq_ref, rkt_ref, rv_ref, rg_ref, mq_ref, mkt_ref, mv_ref,
                 gq_ref, gkt_ref, gv_ref, wq_ref, wkt_ref, wv_ref):
    xs = _tile_x(xl_ref, xc_ref, PAIR)

    def project(i):
        md = mod_ref[i, 0]
        h = _modnorm(xs[i], g1_ref[...], md[1:2], md[0:1])
        return jnp.dot(h.astype(BF16), win_ref[...], preferred_element_type=F32)

    refs = (rq_ref, rkt_ref, rv_ref, rg_ref, mq_ref, mkt_ref, mv_ref,
            gq_ref, gkt_ref, gv_ref, wq_ref, wkt_ref, wv_ref)
    p_next = project(0)
    for i in range(PAIR):
        p = p_next
        if i + 1 < PAIR:
            p_next = project(i + 1)
        _prep_mixers(p, i, wuq_ref, wuk_ref, wuv_ref, ekr_ref,
                     qng_ref, kvng_ref, gqg_ref, gkg_ref, gmat_ref,
                     c64_ref, u64_ref, d64_ref, cm_ref, um_ref, dm_ref, *refs)


def _prep_mixers(p, i, wuq_ref, wuk_ref, wuv_ref, ekr_ref, qng_ref, kvng_ref, gqg_ref, gkg_ref, gmat_ref,
                 c64_ref, u64_ref, d64_ref, cm_ref, um_ref, dm_ref,
                 rq_ref, rkt_ref, rv_ref, rg_ref, mq_ref, mkt_ref, mv_ref,
                 gq_ref, gkt_ref, gv_ref, wq_ref, wkt_ref, wv_ref):
    c64, u64, d64 = c64_ref[...], u64_ref[...], d64_ref[...]
    cm, um, dm = cm_ref[...], um_ref[...], dm_ref[...]
    gmat = gmat_ref[...]
    qk_scale = HEAD_DIM ** -0.5
    rope64 = lambda a: _rope(a, c64, u64, d64, HEAD_DIM // 2)
    ropem = lambda a: _rope(a, cm, um, dm, MLA_ROPE // 2)

    rq_ref[i] = (rope64(p[:, C_RET:C_RET + 256]) * qk_scale).astype(BF16)
    rkt_ref[i] = rope64(p[:, C_RET + 256:C_RET + 512]).T.astype(BF16)
    rv_ref[i] = p[:, C_RET + 512:C_RET + 768].astype(BF16)
    rg_ref[i] = p[:, C_RET + 768:C_RET + 1024].astype(BF16)

    qn = _rmsnorm(p[:, C_MQ:C_MQ + Q_LORA], qng_ref[...])
    q2 = ropem(_bdot(qn, wuq_ref[...])) * ((MLA_NOPE + MLA_ROPE) ** -0.5 * LOG2E)
    mq_ref[i] = q2.astype(BF16)
    kvn = _rmsnorm(p[:, C_MKV:C_MKV + KV_LORA], kvng_ref[...]).astype(BF16)
    k2 = (jnp.dot(kvn, wuk_ref[...], preferred_element_type=F32)
          + _dot_split_lhs(p[:, C_KR:C_KR + LANES], ekr_ref[...]))
    mkt_ref[i] = ropem(k2).T.astype(BF16)
    mv_ref[i] = jnp.dot(kvn, wuv_ref[...], preferred_element_type=F32).astype(BF16)

    gq = p[:, C_GQ:C_GQ + 256]
    gq = gq * lax.rsqrt(_head_mean(gq * gq, gmat) + EPS) * gqg_ref[...]
    gq_ref[i] = (rope64(gq) * (qk_scale * LOG2E)).astype(BF16)
    gk = p[:, C_GK:C_GK + 256]
    gk = gk * lax.rsqrt(_head_mean(gk * gk, gmat) + EPS) * gkg_ref[...]
    gkt_ref[i] = rope64(gk).T.astype(BF16)
    gv_ref[i] = p[:, C_GV:C_GV + 256].astype(BF16)

    wq_ref[i] = (rope64(p[:, C_WQ:C_WQ + 256]) * qk_scale).astype(BF16)
    wkt_ref[i] = rope64(p[:, C_WK:C_WK + 256]).T.astype(BF16)
    wv_ref[i] = p[:, C_WV:C_WV + 256].astype(BF16)


def _prep_call(xl, xc, ctx_block, mod, g1, win_p, wuq, wuk, wuv, ekr, qng, kvng, gqg, gkg, gmat, tabs):
    tok = lambda w: pl.BlockSpec((PAIR, TM, w), lambda b, t: (b, t, 0))
    tokt = lambda w: pl.BlockSpec((PAIR, w, TM), lambda b, t: (b, 0, t))
    tab = pl.BlockSpec((TM, LANES), lambda b, t: (t, 0))
    sd = lambda w: jax.ShapeDtypeStruct((BATCH, TOK, w), BF16)
    sdt = lambda w: jax.ShapeDtypeStruct((BATCH, w, TOK), BF16)
    in_specs = _x_specs(ctx_block, PAIR) + [
        pl.BlockSpec((PAIR, 1, 8, D_MODEL), lambda b, t: (b, t // NT_LAT, 0, 0)),
        _const_spec((1, D_MODEL)),
        _const_spec((D_MODEL, N_IN_P)),
        _const_spec((Q_LORA, MLA_QK)),
        _const_spec((KV_LORA, MLA_QK)),
        _const_spec((KV_LORA, MIX_W)),
        _const_spec((LANES, MLA_QK)),
        _const_spec((1, Q_LORA)),
        _const_spec((1, KV_LORA)),
        _const_spec((1, MIX_W)),
        _const_spec((1, MIX_W)),
        _const_spec((MIX_W, MIX_W)),
        tab, tab, tab, tab, tab, tab,
    ]
    out_specs = [tok(256), tokt(256), tok(256), tok(256),
                 tok(MLA_QK), tokt(MLA_QK), tok(256),
                 tok(256), tokt(256), tok(256),
                 tok(256), tokt(256), tok(256)]
    out_shape = [sd(256), sdt(256), sd(256), sd(256),
                 sd(MLA_QK), sdt(MLA_QK), sd(256),
                 sd(256), sdt(256), sd(256),
                 sd(256), sdt(256), sd(256)]
    return pl.pallas_call(
        _prep_kernel,
        grid=(BATCH // PAIR, NT),
        in_specs=in_specs,
        out_specs=out_specs,
        out_shape=out_shape,
        compiler_params=_params(("parallel", "parallel")),
        name="prep",
    )(xl, xc, mod, g1, win_p, wuq, wuk, wuv, ekr, qng, kvng, gqg, gkg, gmat, *tabs)


def _ret_kernel(qf_ref, ktf_ref, vf_ref, qb_ref, ktb_ref, vb_ref,
                dmat_ref, xi_ref, zt_ref, gc_ref, of_ref, ob_ref, sf_ref, sb_ref):
    i = pl.program_id(0)

    @pl.when(i == 0)
    def _():
        sf_ref[...] = jnp.zeros_like(sf_ref)
        sb_ref[...] = jnp.zeros_like(sb_ref)

    lane_head = lax.broadcasted_iota(jnp.int32, (RET_CHUNK, MIX_W), 1) // HEAD_DIM
    r_head = lax.broadcasted_iota(jnp.int32, (MIX_W, MIX_W), 0) // HEAD_DIM
    c_head = lax.broadcasted_iota(jnp.int32, (MIX_W, MIX_W), 1) // HEAD_DIM
    block_diag = r_head == c_head

    dirs = ((qf_ref, ktf_ref, vf_ref, of_ref, sf_ref), (qb_ref, ktb_ref, vb_ref, ob_ref, sb_ref))

    def body(b, carry):
        q = [r[0][b].astype(F32) for r in dirs]
        kt = [r[1][b] for r in dirs]
        v = [r[2][b] for r in dirs]
        s_old = [r[4][b] for r in dirs]
        o = [_bdot(q[d] * xi_ref[d], s_old[d]) for d in range(2)]
        inner = {}
        for hd in range(N_HEADS):
            for d in range(2):
                qm = jnp.where(lane_head == hd, q[d], 0.0).astype(BF16)
                inner[d, hd] = (jnp.dot(qm, kt[d], preferred_element_type=F32)
                                * dmat_ref[d, hd]).astype(BF16)
        upd = [jnp.dot((kt[d].astype(F32) * zt_ref[d]).astype(BF16), v[d], preferred_element_type=F32)
               for d in range(2)]
        for hd in range(N_HEADS):
            for d in range(2):
                oh = jnp.dot(inner[d, hd], v[d], preferred_element_type=F32)
                o[d] = o[d] + jnp.where(lane_head == hd, oh, 0.0)
        for d in range(2):
            dirs[d][3][b] = o[d]
            dirs[d][4][b] = gc_ref[d] * s_old[d] + jnp.where(block_diag, upd[d], 0.0)
        return carry

    lax.fori_loop(0, BATCH, body, 0)


def _ret_call(rq, rkt, rv, dmat, xi, zt, gc):
    cf = lambda i: (i + SEQ // RET_CHUNK) % N_CHUNK
    cb = lambda i: N_CHUNK - 1 - i
    rows = lambda f: pl.BlockSpec((BATCH, RET_CHUNK, MIX_W), lambda i: (0, f(i), 0))
    cols = lambda f: pl.BlockSpec((BATCH, MIX_W, RET_CHUNK), lambda i: (0, 0, f(i)))
    return pl.pallas_call(
        _ret_kernel,
        grid=(N_CHUNK,),
        in_specs=[rows(cf), cols(cf), rows(cf), rows(cb), cols(cb), rows(cb),
                  _const_spec((2, N_HEADS, RET_CHUNK, RET_CHUNK)),
                  _const_spec((2, RET_CHUNK, MIX_W)),
                  _const_spec((2, MIX_W, RET_CHUNK)),
                  _const_spec((2, MIX_W, MIX_W))],
        out_specs=[rows(cf), rows(cb)],
        out_shape=[jax.ShapeDtypeStruct((BATCH, TOK, MIX_W), F32)] * 2,
        scratch_shapes=[pltpu.VMEM((BATCH, MIX_W, MIX_W), F32),
                        pltpu.VMEM((BATCH, MIX_W, MIX_W), F32)],
        compiler_params=_params(("arbitrary",)),
        name="retention",
    )(rq, rkt, rv, rq, rkt, rv, dmat, xi, zt, gc)


def _head_q(q_ref, hd, pair, i=0):
    if not pair:
        return q_ref[i, :, hd * LANES:(hd + 1) * LANES], hd * LANES
    c = hd // 2
    qc = q_ref[i, :, c * LANES:(c + 1) * LANES]
    half = lax.broadcasted_iota(jnp.int32, qc.shape, 1) // HEAD_DIM
    return jnp.where(half == hd % 2, qc, jnp.zeros_like(qc)), c * LANES


def _dense_kernel(q_ref, kt_ref, v_ref, o_ref, *, pair):
    t = pl.program_id(1)
    lane_head = lax.broadcasted_iota(jnp.int32, (TM, MIX_W), 1) // HEAD_DIM

    def run(k_lo, k_hi):
        def scores(job):
            i, hd = job
            qm, r0 = _head_q(q_ref, hd, pair, i)
            return jnp.dot(qm, kt_ref[i, r0:r0 + LANES, k_lo:k_hi], preferred_element_type=F32)

        jobs = [(i, hd) for i in range(PAIR) for hd in range(N_HEADS)]
        s_next = scores(jobs[0])
        acc = None
        for n, (i, hd) in enumerate(jobs):
            s = s_next
            if n + 1 < len(jobs):
                s_next = scores(jobs[n + 1])
            if hd == 0:
                acc = jnp.zeros((TM, MIX_W), F32)
            m = jnp.max(s, axis=-1, keepdims=True)
            p = jnp.exp2(s - m)
            l = jnp.sum(p, axis=-1, keepdims=True)
            oh = jnp.dot(p.astype(BF16), v_ref[i, k_lo:k_hi, :], preferred_element_type=F32)
            acc = jnp.where(lane_head == hd, oh * (1.0 / l), acc)
            if hd == N_HEADS - 1:
                o_ref[i] = acc.astype(BF16)

    @pl.when(t < NT_LAT)
    def _():
        run(0, TOK)

    @pl.when(t >= NT_LAT)
    def _():
        run(SEQ, TOK)


def _dense_call(q, kt, v, nq, pair, name):
    wq = q.shape[-1]
    return pl.pallas_call(
        functools.partial(_dense_kernel, pair=pair),
        grid=(BATCH // PAIR, nq),
        in_specs=[pl.BlockSpec((PAIR, TM, wq), lambda b, t: (b, t, 0)),
                  pl.BlockSpec((PAIR, kt.shape[1], TOK), lambda b, t: (b, 0, 0)),
                  pl.BlockSpec((PAIR, TOK, MIX_W), lambda b, t: (b, 0, 0))],
        out_specs=pl.BlockSpec((PAIR, TM, MIX_W), lambda b, t: (b, t, 0)),
        out_shape=jax.ShapeDtypeStruct((BATCH, nq * TM, MIX_W), BF16),
        compiler_params=_params(("parallel", "arbitrary")),
        name=name,
    )(q, kt, v)


def _win_kernel(sink_ref, q_ref, kt_ref, v_ref, o_ref):
    t = pl.program_id(1)
    lane_head = lax.broadcasted_iota(jnp.int32, (TM, MIX_W), 1) // HEAD_DIM
    v_ctx = v_ref[0, SEQ:TOK, :]

    @pl.when(t < NT_LAT)
    def _():
        start = pl.multiple_of(jnp.clip(t * TM - WINDOW, 0, SEQ - WIN_KEYS), LANES)
        qpos = t * TM + lax.broadcasted_iota(jnp.int32, (TM, WIN_KEYS), 0)
        kpos = start + lax.broadcasted_iota(jnp.int32, (TM, WIN_KEYS), 1)
        valid = jnp.abs(kpos - qpos) <= WINDOW
        v_loc = v_ref[0, pl.ds(start, WIN_KEYS), :]
        acc = jnp.zeros((TM, MIX_W), F32)

        def scores(hd):
            qm, r0 = _head_q(q_ref, hd, True)
            loc = jnp.dot(qm, kt_ref[0, r0:r0 + LANES, pl.ds(start, WIN_KEYS)], preferred_element_type=F32)
            ctx = jnp.dot(qm, kt_ref[0, r0:r0 + LANES, SEQ:TOK], preferred_element_type=F32)
            return jnp.where(valid, loc, NEG_INF), ctx

        s_next = scores(0)
        for hd in range(N_HEADS):
            s_loc, s_ctx = s_next
            if hd + 1 < N_HEADS:
                s_next = scores(hd + 1)
            sk = sink_ref[hd]
            m = jnp.maximum(jnp.maximum(jnp.max(s_loc, axis=-1, keepdims=True),
                                        jnp.max(s_ctx, axis=-1, keepdims=True)), sk)
            p_loc = jnp.exp(s_loc - m)
            p_ctx = jnp.exp(s_ctx - m)
            l = (jnp.sum(p_loc, axis=-1, keepdims=True) + jnp.sum(p_ctx, axis=-1, keepdims=True)
                 + jnp.exp(sk - m))
            oh = (jnp.dot(p_loc.astype(BF16), v_loc, preferred_element_type=F32)
                  + jnp.dot(p_ctx.astype(BF16), v_ctx, preferred_element_type=F32))
            acc = jnp.where(lane_head == hd, oh * (1.0 / l), acc)
        o_ref[0] = acc.astype(BF16)

    @pl.when(t >= NT_LAT)
    def _():
        acc = jnp.zeros((TM, MIX_W), F32)
        for hd in range(N_HEADS):
            qm, r0 = _head_q(q_ref, hd, True)
            s = jnp.dot(qm, kt_ref[0, r0:r0 + LANES, SEQ:TOK], preferred_element_type=F32)
            sk = sink_ref[hd]
            m = jnp.maximum(jnp.max(s, axis=-1, keepdims=True), sk)
            p = jnp.exp(s - m)
            l = jnp.sum(p, axis=-1, keepdims=True) + jnp.exp(sk - m)
            oh = jnp.dot(p.astype(BF16), v_ctx, preferred_element_type=F32)
            acc = jnp.where(lane_head == hd, oh * (1.0 / l), acc)
        o_ref[0] = acc.astype(BF16)


def _win_call(sink, q, kt, v, nq):
    return pl.pallas_call(
        _win_kernel,
        grid=(BATCH, nq),
        in_specs=[pl.BlockSpec(memory_space=pltpu.SMEM),
                  pl.BlockSpec((1, TM, MIX_W), lambda b, t: (b, t, 0)),
                  pl.BlockSpec((1, MIX_W, TOK), lambda b, t: (b, 0, 0)),
                  pl.BlockSpec((1, TOK, MIX_W), lambda b, t: (b, 0, 0))],
        out_specs=pl.BlockSpec((1, TM, MIX_W), lambda b, t: (b, t, 0)),
        out_shape=jax.ShapeDtypeStruct((BATCH, nq * TM, MIX_W), BF16),
        compiler_params=_params(("parallel", "arbitrary")),
        name="window",
    )(sink, q, kt, v)


def _merge_kernel(xl_ref, xc_ref, mod_ref, g1_ref, g2_ref, of_ref, ob_ref, rg_ref, ym_ref, yg_ref, yw_ref,
                  wg_ref, wb_ref, wo_ref, wr_ref, gmat_ref, xo_ref, h2_ref, lg_ref):
    xs = _tile_x(xl_ref, xc_ref, PAIR)
    mds = [mod_ref[p, 0] for p in range(PAIR)]
    stack = lambda f: jnp.concatenate([f(p) for p in range(PAIR)], axis=0)
    flat = lambda ref: ref[...].reshape(PAIR * TM, ref.shape[-1])
    hb = stack(lambda p: _modnorm(xs[p], g1_ref[...], mds[p][1:2], mds[p][0:1])).astype(BF16)

    gmat = gmat_ref[...]
    o = flat(of_ref) + flat(ob_ref)
    dlt = o - _head_mean(o, gmat)
    var = _head_mean(dlt * dlt, gmat)
    g = flat(rg_ref).astype(F32)
    y_ret = dlt * lax.rsqrt(var + EPS) * (g * _sigmoid(g))

    ys = (y_ret.astype(BF16), flat(ym_ref), flat(yg_ref), flat(yw_ref))
    acc = jnp.zeros((PAIR * TM, D_MODEL), F32)
    for i in range(4):
        gate = _sigmoid(jnp.dot(hb, wg_ref[:, i * D_MODEL:(i + 1) * D_MODEL],
                                      preferred_element_type=F32))
        acc = acc + gate * jnp.dot(ys[i], wb_ref[i], preferred_element_type=F32)
    out = jnp.dot(acc.astype(BF16), wo_ref[...], preferred_element_type=F32)
    wh, wl = _split(wr_ref[...])
    nt = lambda a, b: lax.dot_general(a, b, (((1,), (1,)), ((), ())), preferred_element_type=F32)
    for p in range(PAIR):
        xm = xs[p] + mds[p][2:3] * out[p * TM:(p + 1) * TM]
        xo_ref[p] = xm
        h2 = _modnorm(xm, g2_ref[...], mds[p][4:5], mds[p][3:4])
        h2_ref[p] = h2
        hh, hl = _split(h2)
        lg_ref[p] = nt(wh, hh) + nt(wh, hl) + nt(wl, hh)


def _merge_call(xl, xc, ctx_block, mod, g1, g2, of, ob, rg, ym, yg, yw, wg, wb, wo, wr, gmat, nq):
    tok = lambda w: pl.BlockSpec((PAIR, TM, w), lambda b, t: (b, t, 0))
    return pl.pallas_call(
        _merge_kernel,
        grid=(BATCH // PAIR, nq),
        in_specs=_x_specs(ctx_block, PAIR) + [
                  pl.BlockSpec((PAIR, 1, 8, D_MODEL), lambda b, t: (b, t // NT_LAT, 0, 0)),
                  _const_spec((1, D_MODEL)), _const_spec((1, D_MODEL)),
                  tok(MIX_W), tok(MIX_W), tok(MIX_W), tok(MIX_W), tok(MIX_W), tok(MIX_W),
                  _const_spec((D_MODEL, 4 * D_MODEL)),
                  _const_spec((4, MIX_W, D_MODEL)),
                  _const_spec((D_MODEL, D_MODEL)),
                  _const_spec((N_EXPERTS, D_MODEL)),
                  _const_spec((MIX_W, MIX_W))],
        out_specs=[tok(D_MODEL), tok(D_MODEL),
                   pl.BlockSpec((PAIR, N_EXPERTS, TM), lambda b, t: (b, 0, t))],
        out_shape=[jax.ShapeDtypeStruct((BATCH, nq * TM, D_MODEL), F32),
                   jax.ShapeDtypeStruct((BATCH, nq * TM, D_MODEL), F32),
                   jax.ShapeDtypeStruct((BATCH, N_EXPERTS, nq * TM), F32)],
        compiler_params=_params(("parallel", "parallel")),
        name="merge",
    )(xl, xc, mod, g1, g2, of, ob, rg, ym, yg, yw, wg, wb, wo, wr, gmat)


RANK_BITS = 20
RANK_MASK = (1 << RANK_BITS) - 1


def _route_kernel(lg_ref, b_ref, tri_ref, pk_ref, gw_ref, cnt_ref, base_ref):
    i = pl.program_id(0)

    @pl.when(i == 0)
    def _():
        base_ref[...] = jnp.zeros_like(base_ref)

    s = jax.nn.sigmoid(lg_ref[0])
    sel = s + b_ref[:, 0:1]
    sub = lax.broadcasted_iota(jnp.int32, (EXPERTS_PER_GROUP, TM), 0)
    best = e1 = e2 = s1 = s2 = None
    for g in range(N_GROUPS):
        rows = slice(g * EXPERTS_PER_GROUP, (g + 1) * EXPERTS_PER_GROUP)
        blk, sb = sel[rows], s[rows]
        m1 = jnp.max(blk, axis=0, keepdims=True)
        i1 = jnp.min(jnp.where(blk == m1, sub, EXPERTS_PER_GROUP), axis=0, keepdims=True)
        hit1 = sub == i1
        blk2 = jnp.where(hit1, -jnp.inf, blk)
        m2 = jnp.max(blk2, axis=0, keepdims=True)
        i2 = jnp.min(jnp.where(blk2 == m2, sub, EXPERTS_PER_GROUP), axis=0, keepdims=True)
        hit2 = sub == i2
        score = m1 + m2
        s1g = jnp.sum(jnp.where(hit1, sb, 0.0), axis=0, keepdims=True)
        s2g = jnp.sum(jnp.where(hit2, sb, 0.0), axis=0, keepdims=True)
        e1g = g * EXPERTS_PER_GROUP + i1
        e2g = g * EXPERTS_PER_GROUP + i2
        if g == 0:
            best, e1, e2, s1, s2 = score, e1g, e2g, s1g, s2g
        else:
            better = score > best
            best = jnp.where(better, score, best)
            e1 = jnp.where(better, e1g, e1)
            e2 = jnp.where(better, e2g, e2)
            s1 = jnp.where(better, s1g, s1)
            s2 = jnp.where(better, s2g, s2)

    eid = lax.broadcasted_iota(jnp.int32, (N_EXPERTS, TM), 0)
    oh1 = eid == e1
    oh2 = eid == e2
    oh = jnp.where(oh1 | oh2, 1.0, 0.0)
    before = jnp.dot(oh.astype(BF16), tri_ref[...], preferred_element_type=F32) + base_ref[:, 0:1]
    r1 = jnp.sum(jnp.where(oh1, before, 0.0), axis=0, keepdims=True).astype(jnp.int32)
    r2 = jnp.sum(jnp.where(oh2, before, 0.0), axis=0, keepdims=True).astype(jnp.int32)
    total = base_ref[...] + jnp.sum(oh, axis=1, keepdims=True)
    base_ref[...] = total
    cnt_ref[...] = total

    pk_ref[...] = jnp.concatenate([(e1 << RANK_BITS) + r1, (e2 << RANK_BITS) + r2], axis=0)
    den = s1 + s2
    row = lax.broadcasted_iota(jnp.int32, (8, TM), 0)
    gw_ref[...] = jnp.where(row == 0, s1 / den, jnp.where(row == 1, s2 / den, 0.0))


def _route_call(lgt, b_router, nq):
    n_tiles = BATCH * nq
    n_tok = n_tiles * TM
    r = jnp.arange(TM)
    tri = (r[:, None] < r[None, :]).astype(BF16)
    bcol = jnp.broadcast_to(b_router.astype(F32)[:, None], (N_EXPERTS, LANES))
    return pl.pallas_call(
        _route_kernel,
        grid=(n_tiles,),
        in_specs=[pl.BlockSpec((1, N_EXPERTS, TM), lambda i: (i // nq, 0, i % nq)),
                  _const_spec((N_EXPERTS, LANES)),
                  _const_spec((TM, TM))],
        out_specs=[pl.BlockSpec((TOP_K, TM), lambda i: (0, i)),
                   pl.BlockSpec((8, TM), lambda i: (0, i)),
                   pl.BlockSpec((N_EXPERTS, LANES), lambda i: (0, 0))],
        out_shape=[jax.ShapeDtypeStruct((TOP_K, n_tok), jnp.int32),
                   jax.ShapeDtypeStruct((8, n_tok), F32),
                   jax.ShapeDtypeStruct((N_EXPERTS, LANES), F32)],
        scratch_shapes=[pltpu.VMEM((N_EXPERTS, LANES), F32)],
        compiler_params=_params(("arbitrary",)),
        name="route",
    )(lgt, bcol, tri)


def _slot_rows(pk, row0):
    e = pk >> RANK_BITS
    base = jnp.sum(jnp.where(e[..., None] == jnp.arange(N_EXPERTS), row0, 0), axis=-1)
    return (base + (pk & RANK_MASK)).astype(jnp.int32).reshape(-1)


def _slot_row(rows_ref, k, n, n_tok):
    return rows_ref[k * n_tok + n]


def _combine_kernel(rows_ref, x_ref, mod_ref, gw_ref, fg_ref, ys_hbm, o_ref, ybuf, sem,
                    *, n_tok, final):
    i = pl.program_id(0)
    n_tiles = pl.num_programs(0)
    slot = i % 2

    def start_row(tile, sl, j):
        n = tile * TM + j
        for k in range(TOP_K):
            pltpu.make_async_copy(ys_hbm.at[pl.ds(_slot_row(rows_ref, k, n, n_tok), 1)],
                                  ybuf.at[sl, k, pl.ds(j, 1)], sem.at[sl]).start()

    @pl.when(i == 0)
    def _():
        lax.fori_loop(0, TM, lambda j, c: (start_row(0, 0, j), c)[1], 0, unroll=4)

    for k in range(TOP_K):
        pltpu.make_async_copy(ys_hbm.at[pl.ds(0, TM)], ybuf.at[slot, k], sem.at[slot]).wait()

    md = mod_ref[0, 0]
    gw = gw_ref[...].T

    def finish(rb):
        r = slice(rb * COMBINE_ROWS, (rb + 1) * COMBINE_ROWS)
        f = gw[r, 0:1] * ybuf[slot, 0, r, :] + gw[r, 1:2] * ybuf[slot, 1, r, :]
        xn = x_ref[0, r, :] + md[5:6] * f
        o_ref[0, r, :] = _rmsnorm(xn, fg_ref[...]) if final else xn

    @pl.when(i + 1 < n_tiles)
    def _():
        for rb in range(TM // COMBINE_ROWS):
            for j in range(rb * COMBINE_ROWS, (rb + 1) * COMBINE_ROWS):
                start_row(i + 1, 1 - slot, j)
            finish(rb)

    @pl.when(i + 1 == n_tiles)
    def _():
        for rb in range(TM // COMBINE_ROWS):
            finish(rb)


def _combine_call(rows, xm, mod, gw, fg, ys, nq, final):
    n_tok = BATCH * nq * TM
    grid_spec = pltpu.PrefetchScalarGridSpec(
        num_scalar_prefetch=1,
        grid=(BATCH * nq,),
        in_specs=[pl.BlockSpec((1, TM, D_MODEL), lambda i, *_: (i // nq, i % nq, 0)),
                  pl.BlockSpec((1, 1, 8, D_MODEL), lambda i, *_: (i // nq, (i % nq) // NT_LAT, 0, 0)),
                  pl.BlockSpec((8, TM), lambda i, *_: (0, i)),
                  pl.BlockSpec((1, D_MODEL), lambda i, *_: (0, 0)),
                  pl.BlockSpec(memory_space=pl.ANY)],
        out_specs=pl.BlockSpec((1, TM, D_MODEL), lambda i, *_: (i // nq, i % nq, 0)),
        scratch_shapes=[pltpu.VMEM((2, TOP_K, TM, D_MODEL), F32),
                        pltpu.SemaphoreType.DMA((2,))])
    return pl.pallas_call(
        functools.partial(_combine_kernel, n_tok=n_tok, final=final),
        grid_spec=grid_spec,
        out_shape=jax.ShapeDtypeStruct((BATCH, nq * TM, D_MODEL), F32),
        compiler_params=_params(("arbitrary",)),
        name="combine",
    )(rows, xm, mod, gw, fg, ys)


def _fmoe_kernel(rows_ref, row0_ref, nblk_ref, cnt_ref, h_hbm, w1_ref, w3_ref, w2_ref, ys_hbm,
                 tok, xbuf, ybuf, w1b, w3b, w2b, isem, osem, *, n_tok, n_blocks):
    e = pl.program_id(0)
    nb = nblk_ref[e]
    b0 = row0_ref[e] // MOE_ROWS
    used = row0_ref[N_EXPERTS - 1] // MOE_ROWS + nblk_ref[N_EXPERTS - 1]

    def issue_gather(b, lo=0, hi=MOE_ROWS):
        slot = b % MOE_SLOTS
        for j in range(lo, hi):
            pltpu.make_async_copy(h_hbm.at[pl.ds(tok[b * MOE_ROWS + j], 1)], xbuf.at[slot, pl.ds(j, 1)],
                                  isem.at[slot]).start()

    def wait_gather(b):
        slot = b % MOE_SLOTS
        pltpu.make_async_copy(h_hbm.at[pl.ds(0, MOE_ROWS)], xbuf.at[slot], isem.at[slot]).wait()

    def out_copy(b):
        rows = pl.ds(pl.multiple_of(b * MOE_ROWS, MOE_ROWS), MOE_ROWS)
        return pltpu.make_async_copy(ybuf.at[b % 2], ys_hbm.at[rows], osem.at[b % 2])

    @pl.when(e == 0)
    def _():
        def clear(lo, hi):
            def one(p, c):
                tok[p] = 0
                return c

            lax.fori_loop(lo, hi, one, 0)

        clear(used * MOE_ROWS, (used + MOE_SLOTS - 1) * MOE_ROWS)

        def pad_e(x, c):
            clear(row0_ref[x] + cnt_ref[x], row0_ref[x] + nblk_ref[x] * MOE_ROWS)
            return c

        lax.fori_loop(0, N_EXPERTS, pad_e, 0)

        def put(n, c):
            for k in range(TOP_K):
                tok[rows_ref[k * n_tok + n]] = n
            return c

        lax.fori_loop(0, n_tok, put, 0, unroll=8)

        def prime(idx, c):
            b = idx // MOE_ROWS
            pltpu.make_async_copy(h_hbm.at[pl.ds(tok[idx], 1)], xbuf.at[b, pl.ds(idx % MOE_ROWS, 1)],
                                  isem.at[b]).start()
            return c

        lax.fori_loop(0, (MOE_SLOTS - 1) * MOE_ROWS, prime, 0)

    @pl.when(nb > 0)
    def _():
        w1b[...] = w1_ref[0, 0].astype(BF16)
        w3b[...] = w3_ref[0, 0].astype(BF16)
        w2b[...] = w2_ref[0, 0].astype(BF16)

        def block(r, carry):
            g = b0 + r
            wait_gather(g)

            @pl.when(g >= 2)
            def _():
                out_copy(g - 2).wait()

            ahead = g + MOE_SLOTS - 1
            piece = MOE_ROWS // MOE_PIECES
            half = D_EXPERT // 2
            xb = xbuf[g % MOE_SLOTS].astype(BF16)
            hid = []
            for c in range(2):
                cols = slice(c * half, (c + 1) * half)
                issue_gather(ahead, (2 * c) * piece, (2 * c + 1) * piece)
                h1 = jnp.dot(xb, w1b[:, cols], preferred_element_type=F32)
                issue_gather(ahead, (2 * c + 1) * piece, (2 * c + 2) * piece)
                h3 = jnp.dot(xb, w3b[:, cols], preferred_element_type=F32)
                hid.append((h1 * _sigmoid(h1) * h3).astype(BF16))
            hid = jnp.concatenate(hid, axis=1)
            for c in range(2):
                cols = slice(c * half, (c + 1) * half)
                issue_gather(ahead, (4 + c) * piece, (5 + c) * piece if c == 0 else MOE_ROWS)
                ybuf[g % 2, :, cols] = jnp.dot(hid, w2b[:, cols], preferred_element_type=F32)
            out_copy(g).start()
            return carry

        lax.fori_loop(0, nb, block, 0)

    @pl.when(e == N_EXPERTS - 1)
    def _():
        @pl.when(used >= 2)
        def _():
            out_copy(used - 2).wait()

        @pl.when(used >= 1)
        def _():
            out_copy(used - 1).wait()

        for b in range(MOE_SLOTS - 1):
            wait_gather(used + b)

        ybuf[0] = jnp.zeros((MOE_ROWS, D_MODEL), F32)

        def tail_copy(bk):
            return pltpu.make_async_copy(
                ybuf.at[0], ys_hbm.at[pl.ds(pl.multiple_of(bk * MOE_ROWS, MOE_ROWS), MOE_ROWS)], osem.at[0])

        lax.fori_loop(used, n_blocks, lambda bk, c: (tail_copy(bk).start(), c)[1], 0)
        lax.fori_loop(used, n_blocks, lambda bk, c: (tail_copy(bk).wait(), c)[1], 0)


def _fmoe_call(rows, row0, nblk, cnt, h2, w1, w3, w2, l):
    n_tok = h2.shape[0]
    n_blocks = TOP_K * n_tok // MOE_ROWS + N_EXPERTS
    wspec = pl.BlockSpec((1, 1, D_MODEL, D_EXPERT), lambda e, *_: (l, e, 0, 0))
    grid_spec = pltpu.PrefetchScalarGridSpec(
        num_scalar_prefetch=4,
        grid=(N_EXPERTS,),
        in_specs=[pl.BlockSpec(memory_space=pl.ANY), wspec, wspec,
                  pl.BlockSpec((1, 1, D_EXPERT, D_MODEL), lambda e, *_: (l, e, 0, 0))],
        out_specs=pl.BlockSpec(memory_space=pl.ANY),
        scratch_shapes=[pltpu.SMEM(((n_blocks + MOE_SLOTS - 1) * MOE_ROWS,), jnp.int32),
                        pltpu.VMEM((MOE_SLOTS, MOE_ROWS, D_MODEL), F32),
                        pltpu.VMEM((2, MOE_ROWS, D_MODEL), F32),
                        pltpu.VMEM((D_MODEL, D_EXPERT), BF16),
                        pltpu.VMEM((D_MODEL, D_EXPERT), BF16),
                        pltpu.VMEM((D_EXPERT, D_MODEL), BF16),
                        pltpu.SemaphoreType.DMA((MOE_SLOTS,)),
                        pltpu.SemaphoreType.DMA((2,))])
    return pl.pallas_call(
        functools.partial(_fmoe_kernel, n_tok=n_tok, n_blocks=n_blocks),
        grid_spec=grid_spec,
        out_shape=jax.ShapeDtypeStruct((n_blocks * MOE_ROWS, D_MODEL), F32),
        compiler_params=_params(("arbitrary",)),
        name="moe",
    )(rows, row0, nblk, cnt, h2, w1, w3, w2)


def _rope_tables():
    rows = SEQ // GRID_W
    row = jnp.broadcast_to(jnp.arange(rows)[:, None], (rows, GRID_W)).reshape(-1).astype(F32)
    col = jnp.broadcast_to(jnp.arange(GRID_W)[None, :], (rows, GRID_W)).reshape(-1).astype(F32)

    def cs(rot_dim):
        n_f = rot_dim // 4
        inv = ROPE_BASE ** (-jnp.arange(n_f, dtype=F32) / n_f)
        ang = jnp.concatenate([row[:, None] * inv, col[:, None] * inv], axis=-1)
        return jnp.cos(ang), jnp.sin(ang)

    def with_ctx(c, u, d):
        one = jnp.ones((CTX_LEN, LANES), F32)
        zero = jnp.zeros((CTX_LEN, LANES), F32)
        return (jnp.concatenate([c, one]), jnp.concatenate([u, zero]), jnp.concatenate([d, zero]))

    cos, sin = cs(HEAD_DIM)
    z = jnp.zeros_like(sin)
    t64 = with_ctx(jnp.tile(jnp.concatenate([cos, cos], -1), (1, 2)),
                   jnp.tile(jnp.concatenate([-sin, z], -1), (1, 2)),
                   jnp.tile(jnp.concatenate([z, sin], -1), (1, 2)))
    cos, sin = cs(MLA_ROPE)
    z = jnp.zeros_like(sin)
    one_n = jnp.ones((SEQ, MLA_NOPE), F32)
    zero_n = jnp.zeros((SEQ, MLA_NOPE), F32)
    one_p = jnp.ones((SEQ, LANES - MLA_NOPE - MLA_ROPE), F32)
    zero_p = jnp.zeros((SEQ, LANES - MLA_NOPE - MLA_ROPE), F32)
    tm = with_ctx(jnp.concatenate([one_n, cos, cos, one_p], -1),
                  jnp.concatenate([zero_n, -sin, z, zero_p], -1),
                  jnp.concatenate([zero_n, z, sin, zero_p], -1))
    return t64 + tm


def _ret_tables(decay):
    lg = -jnp.exp(decay.astype(F32))
    idx = jnp.arange(RET_CHUNK, dtype=F32)
    diff = idx[:, None] - idx[None, :]
    fwd = diff >= 0
    bwd = diff < 0
    dm_f = jnp.where(fwd, jnp.exp(lg[0][:, None, None] * jnp.where(fwd, diff, 0.0)), 0.0)
    dm_b = jnp.where(bwd, jnp.exp(lg[1][:, None, None] * jnp.where(bwd, -diff, 0.0)), 0.0)
    dmat = jnp.stack([dm_f, dm_b])
    xi = jnp.stack([jnp.exp(lg[0][:, None] * (idx + 1.0)),
                    jnp.exp(lg[1][:, None] * (RET_CHUNK - idx))])
    zeta = jnp.stack([jnp.exp(lg[0][:, None] * (RET_CHUNK - 1.0 - idx)),
                      jnp.exp(lg[1][:, None] * idx)])
    gch = jnp.exp(lg * RET_CHUNK)
    xi_t = jnp.repeat(jnp.transpose(xi, (0, 2, 1)), HEAD_DIM, axis=2)
    zt_t = jnp.repeat(zeta, HEAD_DIM, axis=1)
    gc_t = jnp.broadcast_to(jnp.repeat(gch, HEAD_DIM, axis=1)[:, :, None], (2, MIX_W, MIX_W))
    return dmat, xi_t, zt_t, gc_t


def _in_proj_columns():
    o_mla = 4 * MIX_W
    o_gqa = o_mla + Q_LORA + KV_LORA + MLA_ROPE
    kv_w = (N_HEADS // 2) * HEAD_DIM
    o_win = o_gqa + MIX_W + 2 * kv_w
    ar = jnp.arange
    dup = jnp.concatenate([ar(HEAD_DIM), ar(HEAD_DIM), HEAD_DIM + ar(HEAD_DIM), HEAD_DIM + ar(HEAD_DIM)])

    def gqa_cols(o):
        return [o + ar(MIX_W), o + MIX_W + dup, o + MIX_W + kv_w + dup]

    return jnp.concatenate([ar(o_mla), o_mla + ar(Q_LORA), o_mla + Q_LORA + ar(KV_LORA)]
                           + gqa_cols(o_gqa) + gqa_cols(o_win)
                           + [o_mla + Q_LORA + KV_LORA + ar(MLA_ROPE)])


def _layer_weights(l, w_in, mla_w_uq, mla_w_ukv):
    cols = _in_proj_columns()
    win_p = jnp.pad(w_in[l][:, cols], ((0, 0), (0, N_IN_P - cols.shape[0]))).astype(BF16)
    uq = mla_w_uq[l].reshape(Q_LORA, N_HEADS, MLA_NOPE + MLA_ROPE)
    wuq = jnp.pad(uq, ((0, 0), (0, 0), (0, LANES - MLA_NOPE - MLA_ROPE))).reshape(Q_LORA, MLA_QK)
    ukv = mla_w_ukv[l].reshape(KV_LORA, N_HEADS, MLA_NOPE + MLA_V)
    wuk = jnp.pad(ukv[:, :, :MLA_NOPE], ((0, 0), (0, 0), (0, LANES - MLA_NOPE))).reshape(KV_LORA, MLA_QK)
    wuv = ukv[:, :, MLA_NOPE:].reshape(KV_LORA, MIX_W)
    return win_p, wuq.astype(BF16), wuk.astype(BF16), wuv.astype(BF16)


def _krope_placement():
    r = jnp.arange(LANES)[:, None]
    c = jnp.arange(MLA_QK)[None, :]
    return ((r < MLA_ROPE) & (c % LANES == MLA_NOPE + r)).astype(BF16)


def _head_block_matrix():
    r = jnp.arange(MIX_W)
    return (r[:, None] // HEAD_DIM == r[None, :] // HEAD_DIM).astype(BF16)


def kernel(x, c, ctx, c_ctx, w_ada, b_ada, norm1_g, norm2_g, w_in, w_gate, w_branch, w_out, ret_decay,
           mla_qn_g, mla_w_uq, mla_kvn_g, mla_w_ukv, gqa_qn_g, gqa_kn_g, win_sink, w_router, b_router,
           w1, w3, w2, final_norm_g):
    cvec = jnp.concatenate([c, c_ctx[None, :], jnp.zeros((7, D_MODEL), F32)], axis=0)
    ada = _ada_call(cvec, w_ada, b_ada)
    tabs = _rope_tables()
    gmat = _head_block_matrix()
    ekr = _krope_placement()
    stream = (x, ctx, 0)
    out = None
    for l in range(DEPTH):
        last = l == DEPTH - 1
        nq = NT_LAT if last else NT
        m = ada[l].reshape(16, 6, D_MODEL)
        m_lat = m[:BATCH]
        m_ctx = jnp.broadcast_to(m[BATCH][None], (BATCH, 6, D_MODEL))
        mod = jnp.pad(jnp.stack([m_lat, m_ctx], axis=1), ((0, 0), (0, 0), (0, 2), (0, 0)))
        g1 = norm1_g[l][None, :]
        g2 = norm2_g[l][None, :]
        win_p, wuq, wuk, wuv = _layer_weights(l, w_in, mla_w_uq, mla_w_ukv)
        (rq, rkt, rv, rg, mq, mkt, mv, gq, gkt, gv, wq, wkt, wv) = _prep_call(
            *stream, mod, g1, win_p, wuq, wuk, wuv, ekr,
            mla_qn_g[l][None, :], mla_kvn_g[l][None, :],
            jnp.tile(gqa_qn_g[l], N_HEADS)[None, :], jnp.tile(gqa_kn_g[l], N_HEADS)[None, :],
            gmat, tabs)
        of, ob = _ret_call(rq, rkt, rv, *_ret_tables(ret_decay[l]))
        ym = _dense_call(mq, mkt, mv, nq, False, "mla")
        yg = _dense_call(gq, gkt, gv, nq, True, "gqa")
        yw = _win_call(win_sink[l], wq, wkt, wv, nq)
        xm, h2, lgt = _merge_call(
            *stream, mod, g1, g2, of, ob, rg, ym, yg, yw,
            w_gate[l].astype(BF16), w_branch[l].astype(BF16), w_out[l].astype(BF16), w_router.T, gmat, nq)
        n_tok = BATCH * nq * TM
        pk, gw, cnt = _route_call(lgt, b_router, nq)
        cnt = cnt[:, 0].astype(jnp.int32)
        nblk = (cnt + MOE_ROWS - 1) // MOE_ROWS
        row0 = (jnp.cumsum(nblk) - nblk) * MOE_ROWS
        rows = _slot_rows(pk, row0)
        ys = _fmoe_call(rows, row0, nblk, cnt, h2.reshape(n_tok, D_MODEL), w1, w3, w2, l)
        res = _combine_call(rows, xm, mod, gw, final_norm_g[None, :], ys, nq, last)
        if last:
            out = res
        else:
            stream = (res, res, NT_LAT)
    return out
```

```python
import functools

import jax
import jax.numpy as jnp
from jax import lax
from jax.experimental import pallas as pl
from jax.experimental.pallas import tpu as pltpu

F32 = jnp.float32
BF16 = jnp.bfloat16

D_MODEL = 1024
BATCH = 8
SEQ = 2048
DEPTH = 2
CTX_LEN = 256
TOK = SEQ + CTX_LEN
GRID_W = 64
N_HEADS = 4
HEAD_DIM = 64
MIX_W = N_HEADS * HEAD_DIM
RET_CHUNK = 128
Q_LORA = 256
KV_LORA = 128
MLA_NOPE = 64
MLA_ROPE = 32
MLA_V = 64
WINDOW = 128
N_EXPERTS = 32
N_GROUPS = 4
EXPERTS_PER_GROUP = N_EXPERTS // N_GROUPS
TOP_K = 2
D_EXPERT = 1024
ROPE_BASE = 10000.0
EPS = 1e-6
NEG_INF = -1e30

LANES = 128
TM = 256
NT = TOK // TM
NT_LAT = SEQ // TM
WIN_KEYS = TM + 2 * WINDOW
N_CHUNK = TOK // RET_CHUNK
LOG2E = 1.4426950408889634
PAIR = 2
COMBINE_ROWS = 32
ATT_PAIR = 4
MOE_ROWS = 256
MOE_SLOTS = 8
MOE_PIECES = 6
N_IN_P = 3072
MLA_QK = 4 * LANES

C_RET = 0
C_MQ = 1024
C_MKV = 1280
C_GQ = 1408
C_GK = 1664
C_GV = 1920
C_WQ = 2176
C_WK = 2432
C_WV = 2688
C_KR = 2944

VMEM_LIMIT = 56 * 1024 * 1024


def _params(sem, vmem=VMEM_LIMIT):
    return pltpu.CompilerParams(dimension_semantics=sem, vmem_limit_bytes=vmem)


def _const_spec(shape):
    nd = len(shape)
    return pl.BlockSpec(shape, lambda *_: (0,) * nd, pipeline_mode=pl.Buffered(1))


def _bdot(a, b):
    return jnp.dot(a.astype(BF16), b.astype(BF16), preferred_element_type=F32)


def _split(a):
    hi = a.astype(BF16)
    lo = (a - hi.astype(F32)).astype(BF16)
    return hi, lo


def _dot_split_lhs(a, b):
    hi, lo = _split(a)
    return (jnp.dot(hi, b, preferred_element_type=F32)
            + jnp.dot(lo, b, preferred_element_type=F32))


def _dot3(a, b):
    ah, al = _split(a)
    bh, bl = _split(b)
    return (jnp.dot(ah, bh, preferred_element_type=F32)
            + jnp.dot(ah, bl, preferred_element_type=F32)
            + jnp.dot(al, bh, preferred_element_type=F32))


def _modnorm(x, g, sc, sh):
    ms = jnp.mean(x * x, axis=-1, keepdims=True)
    return x * lax.rsqrt(ms + EPS) * g * (1.0 + sc) + sh


def _rmsnorm(x, g):
    ms = jnp.mean(x * x, axis=-1, keepdims=True)
    return x * lax.rsqrt(ms + EPS) * g


def _rope(x, c, s_up, s_dn, half):
    outs = []
    for j in range(x.shape[1] // LANES):
        xc = x[:, j * LANES:(j + 1) * LANES]
        outs.append(xc * c + pltpu.roll(xc, LANES - half, 1) * s_up + pltpu.roll(xc, half, 1) * s_dn)
    return outs[0] if len(outs) == 1 else jnp.concatenate(outs, axis=1)


def _sigmoid(x):
    return 0.5 * jnp.tanh(0.5 * x) + 0.5


def _head_mean(x, gmat):
    return _dot_split_lhs(x, gmat) * (1.0 / HEAD_DIM)


def _ada_kernel(c_ref, w_ref, b_ref, o_ref):
    c = c_ref[...]
    sc = c * jax.nn.sigmoid(c)
    o_ref[0] = _dot3(sc, w_ref[0]) + b_ref[0]


def _ada_call(cvec, w_ada, b_ada):
    tn = 1536
    return pl.pallas_call(
        _ada_kernel,
        grid=(DEPTH, 6 * D_MODEL // tn),
        in_specs=[
            pl.BlockSpec((16, D_MODEL), lambda l, j: (0, 0)),
            pl.BlockSpec((1, D_MODEL, tn), lambda l, j: (l, 0, j)),
            pl.BlockSpec((1, 1, tn), lambda l, j: (l, 0, j)),
        ],
        out_specs=pl.BlockSpec((1, 16, tn), lambda l, j: (l, 0, j)),
        out_shape=jax.ShapeDtypeStruct((DEPTH, 16, 6 * D_MODEL), F32),
        compiler_params=_params(("arbitrary", "arbitrary")),
        name="ada",
    )(cvec, w_ada, b_ada.reshape(DEPTH, 1, 6 * D_MODEL))


def _tile_x(xl_ref, xc_ref, pair=1):
    latent = pl.program_id(1) < NT_LAT
    return [jnp.where(latent, xl_ref[p], xc_ref[p]) for p in range(pair)]


def _x_specs(ctx_block, pair=1):
    return [pl.BlockSpec((pair, TM, D_MODEL), lambda b, t: (b, jnp.minimum(t, NT_LAT - 1), 0)),
            pl.BlockSpec((pair, TM, D_MODEL), lambda b, t: (b, ctx_block, 0))]


def _prep_kernel(xl_ref, xc_ref, mod_ref, g1_ref, win_ref, wuq_ref, wuk_ref, wuv_ref, ekr_ref,
                 qng_ref, kvng_ref, gqg_ref, gkg_ref, gmat_ref,
                 c64_ref, u64_ref, d64_ref, cm_ref, um_ref, dm_ref,
                 rq_ref, rkt_ref, rv_ref, rg_ref, mq_ref, mkt_ref, mv_ref,
                 gq_ref, gkt_ref, gv_ref, wq_ref, wkt_ref, wv_ref):
    xs = _tile_x(xl_ref, xc_ref, PAIR)

    def project(i):
        md = mod_ref[i, 0]
        h = _modnorm(xs[i], g1_ref[...], md[1:2], md[0:1])
        return jnp.dot(h.astype(BF16), win_ref[...], preferred_element_type=F32)

    refs = (rq_ref, rkt_ref, rv_ref, rg_ref, mq_ref, mkt_ref, mv_ref,
            gq_ref, gkt_ref, gv_ref, wq_ref, wkt_ref, wv_ref)
    p_next = project(0)
    for i in range(PAIR):
        p = p_next
        if i + 1 < PAIR:
            p_next = project(i + 1)
        _prep_mixers(p, i, wuq_ref, wuk_ref, wuv_ref, ekr_ref,
                     qng_ref, kvng_ref, gqg_ref, gkg_ref, gmat_ref,
                     c64_ref, u64_ref, d64_ref, cm_ref, um_ref, dm_ref, *refs)


def _prep_mixers(p, i, wuq_ref, wuk_ref, wuv_ref, ekr_ref, qng_ref, kvng_ref, gqg_ref, gkg_ref, gmat_ref,
                 c64_ref, u64_ref, d64_ref, cm_ref, um_ref, dm_ref,
                 rq_ref, rkt_ref, rv_ref, rg_ref, mq_ref, mkt_ref, mv_ref,
                 gq_ref, gkt_ref, gv_ref, wq_ref, wkt_ref, wv_ref):
    c64, u64, d64 = c64_ref[...], u64_ref[...], d64_ref[...]
    cm, um, dm = cm_ref[...], um_ref[...], dm_ref[...]
    gmat = gmat_ref[...]
    qk_scale = HEAD_DIM ** -0.5
    rope64 = lambda a: _rope(a, c64, u64, d64, HEAD_DIM // 2)
    ropem = lambda a: _rope(a, cm, um, dm, MLA_ROPE // 2)

    rq_ref[i] = (rope64(p[:, C_RET:C_RET + 256]) * qk_scale).astype(BF16)
    rkt_ref[i] = rope64(p[:, C_RET + 256:C_RET + 512]).T.astype(BF16)
    rv_ref[i] = p[:, C_RET + 512:C_RET + 768].astype(BF16)
    rg_ref[i] = p[:, C_RET + 768:C_RET + 1024].astype(BF16)

    qn = _rmsnorm(p[:, C_MQ:C_MQ + Q_LORA], qng_ref[...])
    q2 = ropem(_bdot(qn, wuq_ref[...])) * ((MLA_NOPE + MLA_ROPE) ** -0.5 * LOG2E)
    mq_ref[i] = q2.astype(BF16)
    kvn = _rmsnorm(p[:, C_MKV:C_MKV + KV_LORA], kvng_ref[...]).astype(BF16)
    k2 = (jnp.dot(kvn, wuk_ref[...], preferred_element_type=F32)
          + _dot_split_lhs(p[:, C_KR:C_KR + LANES], ekr_ref[...]))
    mkt_ref[i] = ropem(k2).T.astype(BF16)
    mv_ref[i] = jnp.dot(kvn, wuv_ref[...], preferred_element_type=F32).astype(BF16)

    gq = p[:, C_GQ:C_GQ + 256]
    gq = gq * lax.rsqrt(_head_mean(gq * gq, gmat) + EPS) * gqg_ref[...]
    gq_ref[i] = (rope64(gq) * (qk_scale * LOG2E)).astype(BF16)
    gk = p[:, C_GK:C_GK + 256]
    gk = gk * lax.rsqrt(_head_mean(gk * gk, gmat) + EPS) * gkg_ref[...]
    gkt_ref[i] = rope64(gk).T.astype(BF16)
    gv_ref[i] = p[:, C_GV:C_GV + 256].astype(BF16)

    wq_ref[i] = (rope64(p[:, C_WQ:C_WQ + 256]) * qk_scale).astype(BF16)
    wkt_ref[i] = rope64(p[:, C_WK:C_WK + 256]).T.astype(BF16)
    wv_ref[i] = p[:, C_WV:C_WV + 256].astype(BF16)


def _prep_call(xl, xc, ctx_block, mod, g1, win_p, wuq, wuk, wuv, ekr, qng, kvng, gqg, gkg, gmat, tabs):
    tok = lambda w: pl.BlockSpec((PAIR, TM, w), lambda b, t: (b, t, 0))
    tokt = lambda w: pl.BlockSpec((PAIR, w, TM), lambda b, t: (b, 0, t))
    tab = pl.BlockSpec((TM, LANES), lambda b, t: (t, 0))
    sd = lambda w: jax.ShapeDtypeStruct((BATCH, TOK, w), BF16)
    sdt = lambda w: jax.ShapeDtypeStruct((BATCH, w, TOK), BF16)
    in_specs = _x_specs(ctx_block, PAIR) + [
        pl.BlockSpec((PAIR, 1, 8, D_MODEL), lambda b, t: (b, t // NT_LAT, 0, 0)),
        _const_spec((1, D_MODEL)),
        _const_spec((D_MODEL, N_IN_P)),
        _const_spec((Q_LORA, MLA_QK)),
        _const_spec((KV_LORA, MLA_QK)),
        _const_spec((KV_LORA, MIX_W)),
        _const_spec((LANES, MLA_QK)),
        _const_spec((1, Q_LORA)),
        _const_spec((1, KV_LORA)),
        _const_spec((1, MIX_W)),
        _const_spec((1, MIX_W)),
        _const_spec((MIX_W, MIX_W)),
        tab, tab, tab, tab, tab, tab,
    ]
    out_specs = [tok(256), tokt(256), tok(256), tok(256),
                 tok(MLA_QK), tokt(MLA_QK), tok(256),
                 tok(256), tokt(256), tok(256),
                 tok(256), tokt(256), tok(256)]
    out_shape = [sd(256), sdt(256), sd(256), sd(256),
                 sd(MLA_QK), sdt(MLA_QK), sd(256),
                 sd(256), sdt(256), sd(256),
                 sd(256), sdt(256), sd(256)]
    return pl.pallas_call(
        _prep_kernel,
        grid=(BATCH // PAIR, NT),
        in_specs=in_specs,
        out_specs=out_specs,
        out_shape=out_shape,
        compiler_params=_params(("parallel", "parallel")),
        name="prep",
    )(xl, xc, mod, g1, win_p, wuq, wuk, wuv, ekr, qng, kvng, gqg, gkg, gmat, *tabs)


def _ret_kernel(qf_ref, ktf_ref, vf_ref, qb_ref, ktb_ref, vb_ref,
                dmat_ref, xi_ref, zt_ref, gc_ref, of_ref, ob_ref, sf_ref, sb_ref):
    i = pl.program_id(0)

    @pl.when(i == 0)
    def _():
        sf_ref[...] = jnp.zeros_like(sf_ref)
        sb_ref[...] = jnp.zeros_like(sb_ref)

    lane_head = lax.broadcasted_iota(jnp.int32, (RET_CHUNK, MIX_W), 1) // HEAD_DIM
    r_head = lax.broadcasted_iota(jnp.int32, (MIX_W, MIX_W), 0) // HEAD_DIM
    c_head = lax.broadcasted_iota(jnp.int32, (MIX_W, MIX_W), 1) // HEAD_DIM
    block_diag = r_head == c_head

    dirs = ((qf_ref, ktf_ref, vf_ref, of_ref, sf_ref), (qb_ref, ktb_ref, vb_ref, ob_ref, sb_ref))

    def body(b, carry):
        q = [r[0][b].astype(F32) for r in dirs]
        kt = [r[1][b] for r in dirs]
        v = [r[2][b] for r in dirs]
        s_old = [r[4][b] for r in dirs]
        o = [_bdot(q[d] * xi_ref[d], s_old[d]) for d in range(2)]
        inner = {}
        for hd in range(N_HEADS):
            for d in range(2):
                qm = jnp.where(lane_head == hd, q[d], 0.0).astype(BF16)
                inner[d, hd] = (jnp.dot(qm, kt[d], preferred_element_type=F32)
                                * dmat_ref[d, hd]).astype(BF16)
        upd = [jnp.dot((kt[d].astype(F32) * zt_ref[d]).astype(BF16), v[d], preferred_element_type=F32)
               for d in range(2)]
        for hd in range(N_HEADS):
            for d in range(2):
                oh = jnp.dot(inner[d, hd], v[d], preferred_element_type=F32)
                o[d] = o[d] + jnp.where(lane_head == hd, oh, 0.0)
        for d in range(2):
            dirs[d][3][b] = o[d]
            dirs[d][4][b] = gc_ref[d] * s_old[d] + jnp.where(block_diag, upd[d], 0.0)
        return carry

    lax.fori_loop(0, BATCH, body, 0)


def _ret_call(rq, rkt, rv, dmat, xi, zt, gc):
    cf = lambda i: (i + SEQ // RET_CHUNK) % N_CHUNK
    cb = lambda i: N_CHUNK - 1 - i
    rows = lambda f: pl.BlockSpec((BATCH, RET_CHUNK, MIX_W), lambda i: (0, f(i), 0))
    cols = lambda f: pl.BlockSpec((BATCH, MIX_W, RET_CHUNK), lambda i: (0, 0, f(i)))
    return pl.pallas_call(
        _ret_kernel,
        grid=(N_CHUNK,),
        in_specs=[rows(cf), cols(cf), rows(cf), rows(cb), cols(cb), rows(cb),
                  _const_spec((2, N_HEADS, RET_CHUNK, RET_CHUNK)),
                  _const_spec((2, RET_CHUNK, MIX_W)),
                  _const_spec((2, MIX_W, RET_CHUNK)),
                  _const_spec((2, MIX_W, MIX_W))],
        out_specs=[rows(cf), rows(cb)],
        out_shape=[jax.ShapeDtypeStruct((BATCH, TOK, MIX_W), F32)] * 2,
        scratch_shapes=[pltpu.VMEM((BATCH, MIX_W, MIX_W), F32),
                        pltpu.VMEM((BATCH, MIX_W, MIX_W), F32)],
        compiler_params=_params(("arbitrary",)),
        name="retention",
    )(rq, rkt, rv, rq, rkt, rv, dmat, xi, zt, gc)


def _head_q(q_ref, hd, pair, i=0):
    if not pair:
        return q_ref[i, :, hd * LANES:(hd + 1) * LANES], hd * LANES
    c = hd // 2
    qc = q_ref[i, :, c * LANES:(c + 1) * LANES]
    half = lax.broadcasted_iota(jnp.int32, qc.shape, 1) // HEAD_DIM
    return jnp.where(half == hd % 2, qc, jnp.zeros_like(qc)), c * LANES


def _dense_kernel(q_ref, kt_ref, v_ref, o_ref, *, pair):
    t = pl.program_id(1)
    lane_head = lax.broadcasted_iota(jnp.int32, (TM, MIX_W), 1) // HEAD_DIM

    def run(k_lo, k_hi):
        def scores(job):
            i, hd = job
            qm, r0 = _head_q(q_ref, hd, pair, i)
            return jnp.dot(qm, kt_ref[i, r0:r0 + LANES, k_lo:k_hi], preferred_element_type=F32)

        jobs = [(i, hd) for i in range(ATT_PAIR) for hd in range(N_HEADS)]
        s_next = scores(jobs[0])
        acc = None
        for n, (i, hd) in enumerate(jobs):
            s = s_next
            if n + 1 < len(jobs):
                s_next = scores(jobs[n + 1])
            if hd == 0:
                acc = jnp.zeros((TM, MIX_W), F32)
            m = jnp.max(s, axis=-1, keepdims=True)
            p = jnp.exp2(s - m)
            l = jnp.sum(p, axis=-1, keepdims=True)
            oh = jnp.dot(p.astype(BF16), v_ref[i, k_lo:k_hi, :], preferred_element_type=F32)
            acc = jnp.where(lane_head == hd, oh * (1.0 / l), acc)
            if hd == N_HEADS - 1:
                o_ref[i] = acc.astype(BF16)

    @pl.when(t < NT_LAT)
    def _():
        run(0, TOK)

    @pl.when(t >= NT_LAT)
    def _():
        run(SEQ, TOK)


def _dense_call(q, kt, v, nq, pair, name):
    wq = q.shape[-1]
    return pl.pallas_call(
        functools.partial(_dense_kernel, pair=pair),
        grid=(BATCH // ATT_PAIR, nq),
        in_specs=[pl.BlockSpec((ATT_PAIR, TM, wq), lambda b, t: (b, t, 0)),
                  pl.BlockSpec((ATT_PAIR, kt.shape[1], TOK), lambda b, t: (b, 0, 0)),
                  pl.BlockSpec((ATT_PAIR, TOK, MIX_W), lambda b, t: (b, 0, 0))],
        out_specs=pl.BlockSpec((ATT_PAIR, TM, MIX_W), lambda b, t: (b, t, 0)),
        out_shape=jax.ShapeDtypeStruct((BATCH, nq * TM, MIX_W), BF16),
        compiler_params=_params(("parallel", "arbitrary")),
        name=name,
    )(q, kt, v)


def _win_kernel(sink_ref, q_ref, kt_ref, v_ref, o_ref):
    t = pl.program_id(1)
    lane_head = lax.broadcasted_iota(jnp.int32, (TM, MIX_W), 1) // HEAD_DIM
    jobs = [(i, hd) for i in range(ATT_PAIR) for hd in range(N_HEADS)]

    @pl.when(t < NT_LAT)
    def _():
        start = pl.multiple_of(jnp.clip(t * TM - WINDOW, 0, SEQ - WIN_KEYS), LANES)
        qpos = t * TM + lax.broadcasted_iota(jnp.int32, (TM, WIN_KEYS), 0)
        kpos = start + lax.broadcasted_iota(jnp.int32, (TM, WIN_KEYS), 1)
        valid = jnp.abs(kpos - qpos) <= WINDOW

        def scores(job):
            i, hd = job
            qm, r0 = _head_q(q_ref, hd, True, i)
            loc = jnp.dot(qm, kt_ref[i, r0:r0 + LANES, pl.ds(start, WIN_KEYS)], preferred_element_type=F32)
            ctx = jnp.dot(qm, kt_ref[i, r0:r0 + LANES, SEQ:TOK], preferred_element_type=F32)
            return jnp.where(valid, loc, NEG_INF), ctx

        s_next = scores(jobs[0])
        acc = None
        for n, (i, hd) in enumerate(jobs):
            s_loc, s_ctx = s_next
            if n + 1 < len(jobs):
                s_next = scores(jobs[n + 1])
            if hd == 0:
                acc = jnp.zeros((TM, MIX_W), F32)
            v_loc = v_ref[i, pl.ds(start, WIN_KEYS), :]
            v_ctx = v_ref[i, SEQ:TOK, :]
            sk = sink_ref[hd]
            m = jnp.maximum(jnp.maximum(jnp.max(s_loc, axis=-1, keepdims=True),
                                        jnp.max(s_ctx, axis=-1, keepdims=True)), sk)
            p_loc = jnp.exp(s_loc - m)
            p_ctx = jnp.exp(s_ctx - m)
            l = (jnp.sum(p_loc, axis=-1, keepdims=True) + jnp.sum(p_ctx, axis=-1, keepdims=True)
                 + jnp.exp(sk - m))
            oh = (jnp.dot(p_loc.astype(BF16), v_loc, preferred_element_type=F32)
                  + jnp.dot(p_ctx.astype(BF16), v_ctx, preferred_element_type=F32))
            acc = jnp.where(lane_head == hd, oh * (1.0 / l), acc)
            if hd == N_HEADS - 1:
                o_ref[i] = acc.astype(BF16)

    @pl.when(t >= NT_LAT)
    def _():
        acc = None
        for i, hd in jobs:
            if hd == 0:
                acc = jnp.zeros((TM, MIX_W), F32)
            qm, r0 = _head_q(q_ref, hd, True, i)
            s = jnp.dot(qm, kt_ref[i, r0:r0 + LANES, SEQ:TOK], preferred_element_type=F32)
            sk = sink_ref[hd]
            m = jnp.maximum(jnp.max(s, axis=-1, keepdims=True), sk)
            p = jnp.exp(s - m)
            l = jnp.sum(p, axis=-1, keepdims=True) + jnp.exp(sk - m)
            oh = jnp.dot(p.astype(BF16), v_ref[i, SEQ:TOK, :], preferred_element_type=F32)
            acc = jnp.where(lane_head == hd, oh * (1.0 / l), acc)
            if hd == N_HEADS - 1:
                o_ref[i] = acc.astype(BF16)


def _win_call(sink, q, kt, v, nq):
    return pl.pallas_call(
        _win_kernel,
        grid=(BATCH // ATT_PAIR, nq),
        in_specs=[pl.BlockSpec(memory_space=pltpu.SMEM),
                  pl.BlockSpec((ATT_PAIR, TM, MIX_W), lambda b, t: (b, t, 0)),
                  pl.BlockSpec((ATT_PAIR, MIX_W, TOK), lambda b, t: (b, 0, 0)),
                  pl.BlockSpec((ATT_PAIR, TOK, MIX_W), lambda b, t: (b, 0, 0))],
        out_specs=pl.BlockSpec((ATT_PAIR, TM, MIX_W), lambda b, t: (b, t, 0)),
        out_shape=jax.ShapeDtypeStruct((BATCH, nq * TM, MIX_W), BF16),
        compiler_params=_params(("parallel", "arbitrary")),
        name="window",
    )(sink, q, kt, v)


def _merge_kernel(xl_ref, xc_ref, mod_ref, g1_ref, g2_ref, of_ref, ob_ref, rg_ref, ym_ref, yg_ref, yw_ref,
                  wg_ref, wb_ref, wo_ref, wr_ref, gmat_ref, xo_ref, h2_ref, lg_ref):
    xs = _tile_x(xl_ref, xc_ref, PAIR)
    mds = [mod_ref[p, 0] for p in range(PAIR)]
    stack = lambda f: jnp.concatenate([f(p) for p in range(PAIR)], axis=0)
    flat = lambda ref: ref[...].reshape(PAIR * TM, ref.shape[-1])
    hb = stack(lambda p: _modnorm(xs[p], g1_ref[...], mds[p][1:2], mds[p][0:1])).astype(BF16)

    gmat = gmat_ref[...]
    o = flat(of_ref) + flat(ob_ref)
    dlt = o - _head_mean(o, gmat)
    var = _head_mean(dlt * dlt, gmat)
    g = flat(rg_ref).astype(F32)
    y_ret = dlt * lax.rsqrt(var + EPS) * (g * _sigmoid(g))

    ys = (y_ret.astype(BF16), flat(ym_ref), flat(yg_ref), flat(yw_ref))
    acc = jnp.zeros((PAIR * TM, D_MODEL), F32)
    for i in range(4):
        gate = _sigmoid(jnp.dot(hb, wg_ref[:, i * D_MODEL:(i + 1) * D_MODEL],
                                      preferred_element_type=F32))
        acc = acc + gate * jnp.dot(ys[i], wb_ref[i], preferred_element_type=F32)
    out = jnp.dot(acc.astype(BF16), wo_ref[...], preferred_element_type=F32)
    wh, wl = _split(wr_ref[...])
    nt = lambda a, b: lax.dot_general(a, b, (((1,), (1,)), ((), ())), preferred_element_type=F32)
    for p in range(PAIR):
        xm = xs[p] + mds[p][2:3] * out[p * TM:(p + 1) * TM]
        xo_ref[p] = xm
        h2 = _modnorm(xm, g2_ref[...], mds[p][4:5], mds[p][3:4])
        h2_ref[p] = h2
        hh, hl = _split(h2)
        lg_ref[p] = nt(wh, hh) + nt(wh, hl) + nt(wl, hh)


def _merge_call(xl, xc, ctx_block, mod, g1, g2, of, ob, rg, ym, yg, yw, wg, wb, wo, wr, gmat, nq):
    tok = lambda w: pl.BlockSpec((PAIR, TM, w), lambda b, t: (b, t, 0))
    return pl.pallas_call(
        _merge_kernel,
        grid=(BATCH // PAIR, nq),
        in_specs=_x_specs(ctx_block, PAIR) + [
                  pl.BlockSpec((PAIR, 1, 8, D_MODEL), lambda b, t: (b, t // NT_LAT, 0, 0)),
                  _const_spec((1, D_MODEL)), _const_spec((1, D_MODEL)),
                  tok(MIX_W), tok(MIX_W), tok(MIX_W), tok(MIX_W), tok(MIX_W), tok(MIX_W),
                  _const_spec((D_MODEL, 4 * D_MODEL)),
                  _const_spec((4, MIX_W, D_MODEL)),
                  _const_spec((D_MODEL, D_MODEL)),
                  _const_spec((N_EXPERTS, D_MODEL)),
                  _const_spec((MIX_W, MIX_W))],
        out_specs=[tok(D_MODEL), tok(D_MODEL),
                   pl.BlockSpec((PAIR, N_EXPERTS, TM), lambda b, t: (b, 0, t))],
        out_shape=[jax.ShapeDtypeStruct((BATCH, nq * TM, D_MODEL), F32),
                   jax.ShapeDtypeStruct((BATCH, nq * TM, D_MODEL), F32),
                   jax.ShapeDtypeStruct((BATCH, N_EXPERTS, nq * TM), F32)],
        compiler_params=_params(("parallel", "parallel")),
        name="merge",
    )(xl, xc, mod, g1, g2, of, ob, rg, ym, yg, yw, wg, wb, wo, wr, gmat)


RANK_BITS = 20
RANK_MASK = (1 << RANK_BITS) - 1


def _route_kernel(lg_ref, b_ref, tri_ref, pk_ref, gw_ref, cnt_ref, base_ref):
    i = pl.program_id(0)

    @pl.when(i == 0)
    def _():
        base_ref[...] = jnp.zeros_like(base_ref)

    s = jax.nn.sigmoid(lg_ref[0])
    sel = s + b_ref[:, 0:1]
    sub = lax.broadcasted_iota(jnp.int32, (EXPERTS_PER_GROUP, TM), 0)
    best = e1 = e2 = s1 = s2 = None
    for g in range(N_GROUPS):
        rows = slice(g * EXPERTS_PER_GROUP, (g + 1) * EXPERTS_PER_GROUP)
        blk, sb = sel[rows], s[rows]
        m1 = jnp.max(blk, axis=0, keepdims=True)
        i1 = jnp.min(jnp.where(blk == m1, sub, EXPERTS_PER_GROUP), axis=0, keepdims=True)
        hit1 = sub == i1
        blk2 = jnp.where(hit1, -jnp.inf, blk)
        m2 = jnp.max(blk2, axis=0, keepdims=True)
        i2 = jnp.min(jnp.where(blk2 == m2, sub, EXPERTS_PER_GROUP), axis=0, keepdims=True)
        hit2 = sub == i2
        score = m1 + m2
        s1g = jnp.sum(jnp.where(hit1, sb, 0.0), axis=0, keepdims=True)
        s2g = jnp.sum(jnp.where(hit2, sb, 0.0), axis=0, keepdims=True)
        e1g = g * EXPERTS_PER_GROUP + i1
        e2g = g * EXPERTS_PER_GROUP + i2
        if g == 0:
            best, e1, e2, s1, s2 = score, e1g, e2g, s1g, s2g
        else:
            better = score > best
            best = jnp.where(better, score, best)
            e1 = jnp.where(better, e1g, e1)
            e2 = jnp.where(better, e2g, e2)
            s1 = jnp.where(better, s1g, s1)
            s2 = jnp.where(better, s2g, s2)

    eid = lax.broadcasted_iota(jnp.int32, (N_EXPERTS, TM), 0)
    oh1 = eid == e1
    oh2 = eid == e2
    oh = jnp.where(oh1 | oh2, 1.0, 0.0)
    before = jnp.dot(oh.astype(BF16), tri_ref[...], preferred_element_type=F32) + base_ref[:, 0:1]
    r1 = jnp.sum(jnp.where(oh1, before, 0.0), axis=0, keepdims=True).astype(jnp.int32)
    r2 = jnp.sum(jnp.where(oh2, before, 0.0), axis=0, keepdims=True).astype(jnp.int32)
    total = base_ref[...] + jnp.sum(oh, axis=1, keepdims=True)
    base_ref[...] = total
    cnt_ref[...] = total

    pk_ref[...] = jnp.concatenate([(e1 << RANK_BITS) + r1, (e2 << RANK_BITS) + r2], axis=0)
    den = s1 + s2
    row = lax.broadcasted_iota(jnp.int32, (8, TM), 0)
    gw_ref[...] = jnp.where(row == 0, s1 / den, jnp.where(row == 1, s2 / den, 0.0))


def _route_call(lgt, b_router, nq):
    n_tiles = BATCH * nq
    n_tok = n_tiles * TM
    r = jnp.arange(TM)
    tri = (r[:, None] < r[None, :]).astype(BF16)
    bcol = jnp.broadcast_to(b_router.astype(F32)[:, None], (N_EXPERTS, LANES))
    return pl.pallas_call(
        _route_kernel,
        grid=(n_tiles,),
        in_specs=[pl.BlockSpec((1, N_EXPERTS, TM), lambda i: (i // nq, 0, i % nq)),
                  _const_spec((N_EXPERTS, LANES)),
                  _const_spec((TM, TM))],
        out_specs=[pl.BlockSpec((TOP_K, TM), lambda i: (0, i)),
                   pl.BlockSpec((8, TM), lambda i: (0, i)),
                   pl.BlockSpec((N_EXPERTS, LANES), lambda i: (0, 0))],
        out_shape=[jax.ShapeDtypeStruct((TOP_K, n_tok), jnp.int32),
                   jax.ShapeDtypeStruct((8, n_tok), F32),
                   jax.ShapeDtypeStruct((N_EXPERTS, LANES), F32)],
        scratch_shapes=[pltpu.VMEM((N_EXPERTS, LANES), F32)],
        compiler_params=_params(("arbitrary",)),
        name="route",
    )(lgt, bcol, tri)


def _slot_rows(pk, row0):
    e = pk >> RANK_BITS
    base = jnp.sum(jnp.where(e[..., None] == jnp.arange(N_EXPERTS), row0, 0), axis=-1)
    return (base + (pk & RANK_MASK)).astype(jnp.int32).reshape(-1)


def _slot_row(rows_ref, k, n, n_tok):
    return rows_ref[k * n_tok + n]


def _combine_kernel(rows_ref, x_ref, mod_ref, gw_ref, fg_ref, ys_hbm, o_ref, ybuf, sem,
                    *, n_tok, final):
    i = pl.program_id(0)
    n_tiles = pl.num_programs(0)
    slot = i % 2

    def start_row(tile, sl, j):
        n = tile * TM + j
        for k in range(TOP_K):
            pltpu.make_async_copy(ys_hbm.at[pl.ds(_slot_row(rows_ref, k, n, n_tok), 1)],
                                  ybuf.at[sl, k, pl.ds(j, 1)], sem.at[sl]).start()

    @pl.when(i == 0)
    def _():
        lax.fori_loop(0, TM, lambda j, c: (start_row(0, 0, j), c)[1], 0, unroll=4)

    for k in range(TOP_K):
        pltpu.make_async_copy(ys_hbm.at[pl.ds(0, TM)], ybuf.at[slot, k], sem.at[slot]).wait()

    md = mod_ref[0, 0]
    gw = gw_ref[...].T

    def finish(rb):
        r = slice(rb * COMBINE_ROWS, (rb + 1) * COMBINE_ROWS)
        f = gw[r, 0:1] * ybuf[slot, 0, r, :] + gw[r, 1:2] * ybuf[slot, 1, r, :]
        xn = x_ref[0, r, :] + md[5:6] * f
        o_ref[0, r, :] = _rmsnorm(xn, fg_ref[...]) if final else xn

    @pl.when(i + 1 < n_tiles)
    def _():
        for rb in range(TM // COMBINE_ROWS):
            for j in range(rb * COMBINE_ROWS, (rb + 1) * COMBINE_ROWS):
                start_row(i + 1, 1 - slot, j)
            finish(rb)

    @pl.when(i + 1 == n_tiles)
    def _():
        for rb in range(TM // COMBINE_ROWS):
            finish(rb)


def _combine_call(rows, xm, mod, gw, fg, ys, nq, final):
    n_tok = BATCH * nq * TM
    grid_spec = pltpu.PrefetchScalarGridSpec(
        num_scalar_prefetch=1,
        grid=(BATCH * nq,),
        in_specs=[pl.BlockSpec((1, TM, D_MODEL), lambda i, *_: (i // nq, i % nq, 0)),
                  pl.BlockSpec((1, 1, 8, D_MODEL), lambda i, *_: (i // nq, (i % nq) // NT_LAT, 0, 0)),
                  pl.BlockSpec((8, TM), lambda i, *_: (0, i)),
                  pl.BlockSpec((1, D_MODEL), lambda i, *_: (0, 0)),
                  pl.BlockSpec(memory_space=pl.ANY)],
        out_specs=pl.BlockSpec((1, TM, D_MODEL), lambda i, *_: (i // nq, i % nq, 0)),
        scratch_shapes=[pltpu.VMEM((2, TOP_K, TM, D_MODEL), F32),
                        pltpu.SemaphoreType.DMA((2,))])
    return pl.pallas_call(
        functools.partial(_combine_kernel, n_tok=n_tok, final=final),
        grid_spec=grid_spec,
        out_shape=jax.ShapeDtypeStruct((BATCH, nq * TM, D_MODEL), F32),
        compiler_params=_params(("arbitrary",)),
        name="combine",
    )(rows, xm, mod, gw, fg, ys)


def _fmoe_kernel(rows_ref, row0_ref, nblk_ref, cnt_ref, h_hbm, w1_ref, w3_ref, w2_ref, ys_hbm,
                 tok, xbuf, ybuf, w1b, w3b, w2b, isem, osem, *, n_tok, n_blocks):
    e = pl.program_id(0)
    nb = nblk_ref[e]
    b0 = row0_ref[e] // MOE_ROWS
    used = row0_ref[N_EXPERTS - 1] // MOE_ROWS + nblk_ref[N_EXPERTS - 1]

    def issue_gather(b, lo=0, hi=MOE_ROWS):
        slot = b % MOE_SLOTS
        for j in range(lo, hi):
            pltpu.make_async_copy(h_hbm.at[pl.ds(tok[b * MOE_ROWS + j], 1)], xbuf.at[slot, pl.ds(j, 1)],
                                  isem.at[slot]).start()

    def wait_gather(b):
        slot = b % MOE_SLOTS
        pltpu.make_async_copy(h_hbm.at[pl.ds(0, MOE_ROWS)], xbuf.at[slot], isem.at[slot]).wait()

    def out_copy(b):
        rows = pl.ds(pl.multiple_of(b * MOE_ROWS, MOE_ROWS), MOE_ROWS)
        return pltpu.make_async_copy(ybuf.at[b % 2], ys_hbm.at[rows], osem.at[b % 2])

    @pl.when(e == 0)
    def _():
        def clear(lo, hi):
            def one(p, c):
                tok[p] = 0
                return c

            lax.fori_loop(lo, hi, one, 0)

        clear(used * MOE_ROWS, (used + MOE_SLOTS - 1) * MOE_ROWS)

        def pad_e(x, c):
            clear(row0_ref[x] + cnt_ref[x], row0_ref[x] + nblk_ref[x] * MOE_ROWS)
            return c

        lax.fori_loop(0, N_EXPERTS, pad_e, 0)

        def put(n, c):
            for k in range(TOP_K):
                tok[rows_ref[k * n_tok + n]] = n
            return c

        lax.fori_loop(0, n_tok, put, 0, unroll=8)

        def prime(idx, c):
            b = idx // MOE_ROWS
            pltpu.make_async_copy(h_hbm.at[pl.ds(tok[idx], 1)], xbuf.at[b, pl.ds(idx % MOE_ROWS, 1)],
                                  isem.at[b]).start()
            return c

        lax.fori_loop(0, (MOE_SLOTS - 1) * MOE_ROWS, prime, 0)

    @pl.when(nb > 0)
    def _():
        w1b[...] = w1_ref[0, 0].astype(BF16)
        w3b[...] = w3_ref[0, 0].astype(BF16)
        w2b[...] = w2_ref[0, 0].astype(BF16)

        def block(r, carry):
            g = b0 + r
            wait_gather(g)

            @pl.when(g >= 2)
            def _():
                out_copy(g - 2).wait()

            ahead = g + MOE_SLOTS - 1
            piece = MOE_ROWS // MOE_PIECES
            half = D_EXPERT // 2
            xb = xbuf[g % MOE_SLOTS].astype(BF16)
            hid = []
            for c in range(2):
                cols = slice(c * half, (c + 1) * half)
                issue_gather(ahead, (2 * c) * piece, (2 * c + 1) * piece)
                h1 = jnp.dot(xb, w1b[:, cols], preferred_element_type=F32)
                issue_gather(ahead, (2 * c + 1) * piece, (2 * c + 2) * piece)
                h3 = jnp.dot(xb, w3b[:, cols], preferred_element_type=F32)
                hid.append((h1 * _sigmoid(h1) * h3).astype(BF16))
            hid = jnp.concatenate(hid, axis=1)
            for c in range(2):
                cols = slice(c * half, (c + 1) * half)
                issue_gather(ahead, (4 + c) * piece, (5 + c) * piece if c == 0 else MOE_ROWS)
                ybuf[g % 2, :, cols] = jnp.dot(hid, w2b[:, cols], preferred_element_type=F32)
            out_copy(g).start()
            return carry

        lax.fori_loop(0, nb, block, 0)

    @pl.when(e == N_EXPERTS - 1)
    def _():
        @pl.when(used >= 2)
        def _():
            out_copy(used - 2).wait()

        @pl.when(used >= 1)
        def _():
            out_copy(used - 1).wait()

        for b in range(MOE_SLOTS - 1):
            wait_gather(used + b)

        ybuf[0] = jnp.zeros((MOE_ROWS, D_MODEL), F32)

        def tail_copy(bk):
            return pltpu.make_async_copy(
                ybuf.at[0], ys_hbm.at[pl.ds(pl.multiple_of(bk * MOE_ROWS, MOE_ROWS), MOE_ROWS)], osem.at[0])

        lax.fori_loop(used, n_blocks, lambda bk, c: (tail_copy(bk).start(), c)[1], 0)
        lax.fori_loop(used, n_blocks, lambda bk, c: (tail_copy(bk).wait(), c)[1], 0)


def _fmoe_call(rows, row0, nblk, cnt, h2, w1, w3, w2, l):
    n_tok = h2.shape[0]
    n_blocks = TOP_K * n_tok // MOE_ROWS + N_EXPERTS
    wspec = pl.BlockSpec((1, 1, D_MODEL, D_EXPERT), lambda e, *_: (l, e, 0, 0))
    grid_spec = pltpu.PrefetchScalarGridSpec(
        num_scalar_prefetch=4,
        grid=(N_EXPERTS,),
        in_specs=[pl.BlockSpec(memory_space=pl.ANY), wspec, wspec,
                  pl.BlockSpec((1, 1, D_EXPERT, D_MODEL), lambda e, *_: (l, e, 0, 0))],
        out_specs=pl.BlockSpec(memory_space=pl.ANY),
        scratch_shapes=[pltpu.SMEM(((n_blocks + MOE_SLOTS - 1) * MOE_ROWS,), jnp.int32),
                        pltpu.VMEM((MOE_SLOTS, MOE_ROWS, D_MODEL), F32),
                        pltpu.VMEM((2, MOE_ROWS, D_MODEL), F32),
                        pltpu.VMEM((D_MODEL, D_EXPERT), BF16),
                        pltpu.VMEM((D_MODEL, D_EXPERT), BF16),
                        pltpu.VMEM((D_EXPERT, D_MODEL), BF16),
                        pltpu.SemaphoreType.DMA((MOE_SLOTS,)),
                        pltpu.SemaphoreType.DMA((2,))])
    return pl.pallas_call(
        functools.partial(_fmoe_kernel, n_tok=n_tok, n_blocks=n_blocks),
        grid_spec=grid_spec,
        out_shape=jax.ShapeDtypeStruct((n_blocks * MOE_ROWS, D_MODEL), F32),
        compiler_params=_params(("arbitrary",)),
        name="moe",
    )(rows, row0, nblk, cnt, h2, w1, w3, w2)


def _rope_tables():
    rows = SEQ // GRID_W
    row = jnp.broadcast_to(jnp.arange(rows)[:, None], (rows, GRID_W)).reshape(-1).astype(F32)
    col = jnp.broadcast_to(jnp.arange(GRID_W)[None, :], (rows, GRID_W)).reshape(-1).astype(F32)

    def cs(rot_dim):
        n_f = rot_dim // 4
        inv = ROPE_BASE ** (-jnp.arange(n_f, dtype=F32) / n_f)
        ang = jnp.concatenate([row[:, None] * inv, col[:, None] * inv], axis=-1)
        return jnp.cos(ang), jnp.sin(ang)

    def with_ctx(c, u, d):
        one = jnp.ones((CTX_LEN, LANES), F32)
        zero = jnp.zeros((CTX_LEN, LANES), F32)
        return (jnp.concatenate([c, one]), jnp.concatenate([u, zero]), jnp.concatenate([d, zero]))

    cos, sin = cs(HEAD_DIM)
    z = jnp.zeros_like(sin)
    t64 = with_ctx(jnp.tile(jnp.concatenate([cos, cos], -1), (1, 2)),
                   jnp.tile(jnp.concatenate([-sin, z], -1), (1, 2)),
                   jnp.tile(jnp.concatenate([z, sin], -1), (1, 2)))
    cos, sin = cs(MLA_ROPE)
    z = jnp.zeros_like(sin)
    one_n = jnp.ones((SEQ, MLA_NOPE), F32)
    zero_n = jnp.zeros((SEQ, MLA_NOPE), F32)
    one_p = jnp.ones((SEQ, LANES - MLA_NOPE - MLA_ROPE), F32)
    zero_p = jnp.zeros((SEQ, LANES - MLA_NOPE - MLA_ROPE), F32)
    tm = with_ctx(jnp.concatenate([one_n, cos, cos, one_p], -1),
                  jnp.concatenate([zero_n, -sin, z, zero_p], -1),
                  jnp.concatenate([zero_n, z, sin, zero_p], -1))
    return t64 + tm


def _ret_tables(decay):
    lg = -jnp.exp(decay.astype(F32))
    idx = jnp.arange(RET_CHUNK, dtype=F32)
    diff = idx[:, None] - idx[None, :]
    fwd = diff >= 0
    bwd = diff < 0
    dm_f = jnp.where(fwd, jnp.exp(lg[0][:, None, None] * jnp.where(fwd, diff, 0.0)), 0.0)
    dm_b = jnp.where(bwd, jnp.exp(lg[1][:, None, None] * jnp.where(bwd, -diff, 0.0)), 0.0)
    dmat = jnp.stack([dm_f, dm_b])
    xi = jnp.stack([jnp.exp(lg[0][:, None] * (idx + 1.0)),
                    jnp.exp(lg[1][:, None] * (RET_CHUNK - idx))])
    zeta = jnp.stack([jnp.exp(lg[0][:, None] * (RET_CHUNK - 1.0 - idx)),
                      jnp.exp(lg[1][:, None] * idx)])
    gch = jnp.exp(lg * RET_CHUNK)
    xi_t = jnp.repeat(jnp.transpose(xi, (0, 2, 1)), HEAD_DIM, axis=2)
    zt_t = jnp.repeat(zeta, HEAD_DIM, axis=1)
    gc_t = jnp.broadcast_to(jnp.repeat(gch, HEAD_DIM, axis=1)[:, :, None], (2, MIX_W, MIX_W))
    return dmat, xi_t, zt_t, gc_t


def _in_proj_columns():
    o_mla = 4 * MIX_W
    o_gqa = o_mla + Q_LORA + KV_LORA + MLA_ROPE
    kv_w = (N_HEADS // 2) * HEAD_DIM
    o_win = o_gqa + MIX_W + 2 * kv_w
    ar = jnp.arange
    dup = jnp.concatenate([ar(HEAD_DIM), ar(HEAD_DIM), HEAD_DIM + ar(HEAD_DIM), HEAD_DIM + ar(HEAD_DIM)])

    def gqa_cols(o):
        return [o + ar(MIX_W), o + MIX_W + dup, o + MIX_W + kv_w + dup]

    return jnp.concatenate([ar(o_mla), o_mla + ar(Q_LORA), o_mla + Q_LORA + ar(KV_LORA)]
                           + gqa_cols(o_gqa) + gqa_cols(o_win)
                           + [o_mla + Q_LORA + KV_LORA + ar(MLA_ROPE)])


def _layer_weights(l, w_in, mla_w_uq, mla_w_ukv):
    cols = _in_proj_columns()
    win_p = jnp.pad(w_in[l][:, cols], ((0, 0), (0, N_IN_P - cols.shape[0]))).astype(BF16)
    uq = mla_w_uq[l].reshape(Q_LORA, N_HEADS, MLA_NOPE + MLA_ROPE)
    wuq = jnp.pad(uq, ((0, 0), (0, 0), (0, LANES - MLA_NOPE - MLA_ROPE))).reshape(Q_LORA, MLA_QK)
    ukv = mla_w_ukv[l].reshape(KV_LORA, N_HEADS, MLA_NOPE + MLA_V)
    wuk = jnp.pad(ukv[:, :, :MLA_NOPE], ((0, 0), (0, 0), (0, LANES - MLA_NOPE))).reshape(KV_LORA, MLA_QK)
    wuv = ukv[:, :, MLA_NOPE:].reshape(KV_LORA, MIX_W)
    return win_p, wuq.astype(BF16), wuk.astype(BF16), wuv.astype(BF16)


def _krope_placement():
    r = jnp.arange(LANES)[:, None]
    c = jnp.arange(MLA_QK)[None, :]
    return ((r < MLA_ROPE) & (c % LANES == MLA_NOPE + r)).astype(BF16)


def _head_block_matrix():
    r = jnp.arange(MIX_W)
    return (r[:, None] // HEAD_DIM == r[None, :] // HEAD_DIM).astype(BF16)


def kernel(x, c, ctx, c_ctx, w_ada, b_ada, norm1_g, norm2_g, w_in, w_gate, w_branch, w_out, ret_decay,
           mla_qn_g, mla_w_uq, mla_kvn_g, mla_w_ukv, gqa_qn_g, gqa_kn_g, win_sink, w_router, b_router,
           w1, w3, w2, final_norm_g):
    cvec = jnp.concatenate([c, c_ctx[None, :], jnp.zeros((7, D_MODEL), F32)], axis=0)
    ada = _ada_call(cvec, w_ada, b_ada)
    tabs = _rope_tables()
    gmat = _head_block_matrix()
    ekr = _krope_placement()
    stream = (x, ctx, 0)
    out = None
    for l in range(DEPTH):
        last = l == DEPTH - 1
        nq = NT_LAT if last else NT
        m = ada[l].reshape(16, 6, D_MODEL)
        m_lat = m[:BATCH]
        m_ctx = jnp.broadcast_to(m[BATCH][None], (BATCH, 6, D_MODEL))
        mod = jnp.pad(jnp.stack([m_lat, m_ctx], axis=1), ((0, 0), (0, 0), (0, 2), (0, 0)))
        g1 = norm1_g[l][None, :]
        g2 = norm2_g[l][None, :]
        win_p, wuq, wuk, wuv = _layer_weights(l, w_in, mla_w_uq, mla_w_ukv)
        (rq, rkt, rv, rg, mq, mkt, mv, gq, gkt, gv, wq, wkt, wv) = _prep_call(
            *stream, mod, g1, win_p, wuq, wuk, wuv, ekr,
            mla_qn_g[l][None, :], mla_kvn_g[l][None, :],
            jnp.tile(gqa_qn_g[l], N_HEADS)[None, :], jnp.tile(gqa_kn_g[l], N_HEADS)[None, :],
            gmat, tabs)
        of, ob = _ret_call(rq, rkt, rv, *_ret_tables(ret_decay[l]))
        ym = _dense_call(mq, mkt, mv, nq, False, "mla")
        yg = _dense_call(gq, gkt, gv, nq, True, "gqa")
        yw = _win_call(win_sink[l], wq, wkt, wv, nq)
        xm, h2, lgt = _merge_call(
            *stream, mod, g1, g2, of, ob, rg, ym, yg, yw,
            w_gate[l].astype(BF16), w_branch[l].astype(BF16), w_out[l].astype(BF16), w_router.T, gmat, nq)
        n_tok = BATCH * nq * TM
        pk, gw, cnt = _route_call(lgt, b_router, nq)
        cnt = cnt[:, 0].astype(jnp.int32)
        nblk = (cnt + MOE_ROWS - 1) // MOE_ROWS
        row0 = (jnp.cumsum(nblk) - nblk) * MOE_ROWS
        rows = _slot_rows(pk, row0)
        ys = _fmoe_call(rows, row0, nblk, cnt, h2.reshape(n_tok, D_MODEL), w1, w3, w2, l)
        res = _combine_call(rows, xm, mod, gw, final_norm_g[None, :], ys, nq, last)
        if last:
            out = res
        else:
            stream = (res, res, NT_LAT)
    return out
```
